```python
import math
import jax, jax.numpy as jnp
from jax import lax
import numpy as np

D_MODEL = 1024
BATCH = 8
SEQ = 4096
DEPTH = 2

CHUNK = 64
MIX_WIDTH = D_MODEL
ATT_HEADS = 8
HEAD_DIM = 64
ATT_WIDTH = ATT_HEADS * HEAD_DIM
CONV_CH = MIX_WIDTH - ATT_WIDTH
CONV_WIDTH = 31
D_FF = 4 * D_MODEL
QBLK = 128
N_IN = 3 * ATT_WIDTH + ATT_HEADS + 2 * CONV_CH
EPS = 1e-6

kernel_name = "fox_conformer_hybrid_trunk"


def rms_norm(x, g):
    xf = x.astype(jnp.float32)
    y = xf * lax.rsqrt(jnp.mean(xf * xf, axis=-1, keepdims=True) + EPS)
    return (y * g.astype(jnp.float32)).astype(x.dtype)


def layer_norm(x, g, b):
    xf = x.astype(jnp.float32)
    mu = jnp.mean(xf, axis=-1, keepdims=True)
    var = jnp.mean(jnp.square(xf - mu), axis=-1, keepdims=True)
    y = (xf - mu) * lax.rsqrt(var + EPS)
    return (y * g.astype(jnp.float32) + b.astype(jnp.float32)).astype(x.dtype)


def forgetting_attention(q, k, v, logf):
    S = q.shape[2]
    scale = 1.0 / math.sqrt(q.shape[-1])
    c = jnp.cumsum(logf.astype(jnp.float32), axis=-1)
    outs = []
    for blk in range(S // QBLK):
        q0, q1 = blk * QBLK, (blk + 1) * QBLK
        qb = q[:, :, q0:q1]
        kb = k[:, :, :q1]
        vb = v[:, :, :q1]
        s = (jnp.einsum('bhqd,bhkd->bhqk', qb, kb).astype(jnp.float32) * scale
             + (c[:, :, q0:q1, None] - c[:, :, None, :q1]))
        mask = jnp.arange(q0, q1)[:, None] >= jnp.arange(q1)[None, :]
        s = jnp.where(mask, s, -jnp.inf)
        p = jax.nn.softmax(s, axis=-1)
        outs.append(jnp.einsum('bhqk,bhkd->bhqd', p.astype(vb.dtype), vb))
    return jnp.concatenate(outs, axis=2)


def causal_depthwise_conv(x, w, b):
    W, C = w.shape
    y = lax.conv_general_dilated(
        x, w.reshape(W, 1, C).astype(x.dtype), window_strides=(1,), padding=[(W - 1, 0)],
        dimension_numbers=('NWC', 'WIO', 'NWC'), feature_group_count=C)
    return y + b.astype(x.dtype)


def hybrid_mixer(x, norm_g, w_in, b_f, q_norm_g, k_norm_g, conv_w, conv_b, conv_ln_g, conv_ln_b, w_o):
    B, S, _ = x.shape
    u = rms_norm(x, norm_g)
    proj = jnp.einsum('bsd,dn->bsn', u, w_in)
    o1 = ATT_WIDTH; o2 = 2 * ATT_WIDTH; o3 = 3 * ATT_WIDTH; o4 = o3 + ATT_HEADS
    q = proj[..., :o1].reshape(B, S, ATT_HEADS, HEAD_DIM)
    k = proj[..., o1:o2].reshape(B, S, ATT_HEADS, HEAD_DIM)
    v = proj[..., o2:o3].reshape(B, S, ATT_HEADS, HEAD_DIM)
    f_logit = proj[..., o3:o4]
    glu_in = proj[..., o4:]

    q = rms_norm(q, q_norm_g).transpose(0, 2, 1, 3)
    k = rms_norm(k, k_norm_g).transpose(0, 2, 1, 3)
    v = v.transpose(0, 2, 1, 3)
    logf = jax.nn.log_sigmoid(f_logit.astype(jnp.float32) + b_f.astype(jnp.float32)).transpose(0, 2, 1)
    att = forgetting_attention(q, k, v, logf).transpose(0, 2, 1, 3).reshape(B, S, ATT_WIDTH)

    a, g = jnp.split(glu_in, 2, axis=-1)
    h = a * jax.nn.sigmoid(g)
    h = causal_depthwise_conv(h, conv_w, conv_b)
    h = layer_norm(h, conv_ln_g, conv_ln_b)
    h = jax.nn.silu(h)

    mixed = jnp.concatenate([att, h.astype(att.dtype)], axis=-1)
    return x + jnp.einsum('bsm,md->bsd', mixed, w_o)


def sq_relu_mlp(x, norm_g, w1, w2):
    u = rms_norm(x, norm_g)
    h = jnp.square(jax.nn.relu(jnp.einsum('bsd,df->bsf', u, w1)))
    return x + jnp.einsum('bsf,fd->bsd', h, w2)


def _fwd_setup_inputs(seed: int = 0) -> dict:
    key = jax.random.key(seed)
    ks = jax.random.split(key, 16)
    f32 = jnp.float32
    nrm = lambda k, shape, s: jax.random.normal(k, shape, f32) * s
    return {
        "x": jax.random.normal(ks[0], (BATCH, SEQ, D_MODEL), f32),
        "norm1_g": 1.0 + nrm(ks[1], (DEPTH, D_MODEL), 0.02),
        "w_in": nrm(ks[2], (DEPTH, D_MODEL, N_IN), D_MODEL ** -0.5),
        "b_f": 3.0 + nrm(ks[3], (DEPTH, ATT_HEADS), 0.1),
        "q_norm_g": 1.0 + nrm(ks[4], (DEPTH, HEAD_DIM), 0.02),
        "k_norm_g": 1.0 + nrm(ks[5], (DEPTH, HEAD_DIM), 0.02),
        "conv_w": nrm(ks[6], (DEPTH, CONV_WIDTH, CONV_CH), CONV_WIDTH ** -0.5),
        "conv_b": nrm(ks[7], (DEPTH, CONV_CH), 0.01),
        "conv_ln_g": 1.0 + nrm(ks[8], (DEPTH, CONV_CH), 0.02),
        "conv_ln_b": nrm(ks[9], (DEPTH, CONV_CH), 0.01),
        "w_o": nrm(ks[10], (DEPTH, MIX_WIDTH, D_MODEL), MIX_WIDTH ** -0.5),
        "norm2_g": 1.0 + nrm(ks[11], (DEPTH, D_MODEL), 0.02),
        "w_mlp_in": nrm(ks[12], (DEPTH, D_MODEL, D_FF), D_MODEL ** -0.5),
        "w_mlp_out": nrm(ks[13], (DEPTH, D_FF, D_MODEL), D_FF ** -0.5),
    }


def _fwd_reference(x, norm1_g, w_in, b_f, q_norm_g, k_norm_g, conv_w, conv_b, conv_ln_g, conv_ln_b,
              w_o, norm2_g, w_mlp_in, w_mlp_out):
    for l in range(DEPTH):
        x = hybrid_mixer(x, norm1_g[l], w_in[l], b_f[l], q_norm_g[l], k_norm_g[l], conv_w[l],
                         conv_b[l], conv_ln_g[l], conv_ln_b[l], w_o[l])
        x = sq_relu_mlp(x, norm2_g[l], w_mlp_in[l], w_mlp_out[l])
    return x


import jax as _jax
import jax.numpy as _jnp

TWIN_FORMAT = 'train_step'
FWD_PARAMS = ['x', 'norm1_g', 'w_in', 'b_f', 'q_norm_g', 'k_norm_g', 'conv_w', 'conv_b', 'conv_ln_g', 'conv_ln_b', 'w_o', 'norm2_g', 'w_mlp_in', 'w_mlp_out']
TWIN_WEIGHTS = ['norm1_g', 'w_in', 'b_f', 'q_norm_g', 'k_norm_g', 'conv_w', 'conv_b', 'conv_ln_g', 'conv_ln_b', 'w_o', 'norm2_g', 'w_mlp_in', 'w_mlp_out']
TWIN_DIFF_INPUT = 'x'
TWIN_INPUTS = ['x', 'norm1_g', 'w_in', 'b_f', 'q_norm_g', 'k_norm_g', 'conv_w', 'conv_b', 'conv_ln_g', 'conv_ln_b', 'w_o', 'norm2_g', 'w_mlp_in', 'w_mlp_out', 'loss_target', 'm_norm1_g', 'm_w_in', 'm_b_f', 'm_q_norm_g', 'm_k_norm_g', 'm_conv_w', 'm_conv_b', 'm_conv_ln_g', 'm_conv_ln_b', 'm_w_o', 'm_norm2_g', 'm_w_mlp_in', 'm_w_mlp_out', 'v_norm1_g', 'v_w_in', 'v_b_f', 'v_q_norm_g', 'v_k_norm_g', 'v_conv_w', 'v_conv_b', 'v_conv_ln_g', 'v_conv_ln_b', 'v_w_o', 'v_norm2_g', 'v_w_mlp_in', 'v_w_mlp_out']
TWIN_OUTPUTS = ['loss', 'grad_x', 'grad_norm1_g', 'grad_w_in', 'grad_b_f', 'grad_q_norm_g', 'grad_k_norm_g', 'grad_conv_w', 'grad_conv_b', 'grad_conv_ln_g', 'grad_conv_ln_b', 'grad_w_o', 'grad_norm2_g', 'grad_w_mlp_in', 'grad_w_mlp_out', 'delta_norm1_g', 'delta_w_in', 'delta_b_f', 'delta_q_norm_g', 'delta_k_norm_g', 'delta_conv_w', 'delta_conv_b', 'delta_conv_ln_g', 'delta_conv_ln_b', 'delta_w_o', 'delta_norm2_g', 'delta_w_mlp_in', 'delta_w_mlp_out', 'new_m_norm1_g', 'new_m_w_in', 'new_m_b_f', 'new_m_q_norm_g', 'new_m_k_norm_g', 'new_m_conv_w', 'new_m_conv_b', 'new_m_conv_ln_g', 'new_m_conv_ln_b', 'new_m_w_o', 'new_m_norm2_g', 'new_m_w_mlp_in', 'new_m_w_mlp_out', 'new_v_norm1_g', 'new_v_w_in', 'new_v_b_f', 'new_v_q_norm_g', 'new_v_k_norm_g', 'new_v_conv_w', 'new_v_conv_b', 'new_v_conv_ln_g', 'new_v_conv_ln_b', 'new_v_w_o', 'new_v_norm2_g', 'new_v_w_mlp_in', 'new_v_w_mlp_out']
TWIN_LEAF_KINDS = {'loss': 'loss', 'grad_x': 'grad_x', 'grad_norm1_g': 'grad_w', 'grad_w_in': 'grad_w', 'grad_b_f': 'grad_w', 'grad_q_norm_g': 'grad_w', 'grad_k_norm_g': 'grad_w', 'grad_conv_w': 'grad_w', 'grad_conv_b': 'grad_w', 'grad_conv_ln_g': 'grad_w', 'grad_conv_ln_b': 'grad_w', 'grad_w_o': 'grad_w', 'grad_norm2_g': 'grad_w', 'grad_w_mlp_in': 'grad_w', 'grad_w_mlp_out': 'grad_w', 'delta_norm1_g': 'delta_w', 'delta_w_in': 'delta_w', 'delta_b_f': 'delta_w', 'delta_q_norm_g': 'delta_w', 'delta_k_norm_g': 'delta_w', 'delta_conv_w': 'delta_w', 'delta_conv_b': 'delta_w', 'delta_conv_ln_g': 'delta_w', 'delta_conv_ln_b': 'delta_w', 'delta_w_o': 'delta_w', 'delta_norm2_g': 'delta_w', 'delta_w_mlp_in': 'delta_w', 'delta_w_mlp_out': 'delta_w', 'new_m_norm1_g': 'new_m', 'new_m_w_in': 'new_m', 'new_m_b_f': 'new_m', 'new_m_q_norm_g': 'new_m', 'new_m_k_norm_g': 'new_m', 'new_m_conv_w': 'new_m', 'new_m_conv_b': 'new_m', 'new_m_conv_ln_g': 'new_m', 'new_m_conv_ln_b': 'new_m', 'new_m_w_o': 'new_m', 'new_m_norm2_g': 'new_m', 'new_m_w_mlp_in': 'new_m', 'new_m_w_mlp_out': 'new_m', 'new_v_norm1_g': 'new_v', 'new_v_w_in': 'new_v', 'new_v_b_f': 'new_v', 'new_v_q_norm_g': 'new_v', 'new_v_k_norm_g': 'new_v', 'new_v_conv_w': 'new_v', 'new_v_conv_b': 'new_v', 'new_v_conv_ln_g': 'new_v', 'new_v_conv_ln_b': 'new_v', 'new_v_w_o': 'new_v', 'new_v_norm2_g': 'new_v', 'new_v_w_mlp_in': 'new_v', 'new_v_w_mlp_out': 'new_v'}


def _forward(args):
    return _fwd_reference(*[args[k] for k in FWD_PARAMS])


def _output_shape():
    out = _jax.eval_shape(lambda: _forward(_fwd_setup_inputs(0)))
    return out.shape, out.dtype

N_MICROBATCH = 1
ADAM_LR = 0.001
ADAM_B1 = 0.9
ADAM_B2 = 0.999
ADAM_EPS = 1e-08
ADAM_WD = 0.01
ADAM_STEP = 10
PER_EXAMPLE_BATCH_AXIS = {'x': 0, 'loss_target': 0}
SHARED_INPUTS = []
_WEIGHT_DTYPES = {'norm1_g': _jnp.float32, 'w_in': _jnp.float32, 'b_f': _jnp.float32, 'q_norm_g': _jnp.float32, 'k_norm_g': _jnp.float32, 'conv_w': _jnp.float32, 'conv_b': _jnp.float32, 'conv_ln_g': _jnp.float32, 'conv_ln_b': _jnp.float32, 'w_o': _jnp.float32, 'norm2_g': _jnp.float32, 'w_mlp_in': _jnp.float32, 'w_mlp_out': _jnp.float32}
MOMENT_SCALE = {'norm1_g': 6.373832e+00, 'w_in': 3.615179e+00, 'b_f': 5.595150e+01, 'q_norm_g': 9.470431e+00, 'k_norm_g': 9.489543e+00, 'conv_w': 4.354530e+00, 'conv_b': 3.923990e+01, 'conv_ln_g': 2.224589e+01, 'conv_ln_b': 2.482889e+01, 'w_o': 8.767209e+00, 'norm2_g': 9.718867e+01, 'w_mlp_in': 4.905697e+00, 'w_mlp_out': 1.785275e+01}


def _to_microbatches(a, axis):
    t = _jnp.moveaxis(a, axis, 0)
    t = t.reshape((N_MICROBATCH, t.shape[0] // N_MICROBATCH) + t.shape[1:])
    return _jnp.moveaxis(t, 1, axis + 1)


def setup_inputs(seed: int = 0) -> dict:
    inp = _fwd_setup_inputs(seed)
    key = _jax.random.fold_in(_jax.random.key(seed), 7919)
    shape, _ = _output_shape()
    out = dict(inp)
    out["loss_target"] = _jax.random.normal(_jax.random.fold_in(key, 0), shape, _jnp.float32)
    for i, name in enumerate(TWIN_WEIGHTS):
        w = inp[name].astype(_jnp.float32)
        if MOMENT_SCALE is None:
            s = _jnp.sqrt(_jnp.mean(_jnp.square(w)) + 1e-30)
        else:
            s = MOMENT_SCALE[name]
        km, kv = _jax.random.split(_jax.random.fold_in(key, i + 1))
        out[name] = w
        out["m_" + name] = s * _jax.random.normal(km, w.shape, _jnp.float32)
        out["v_" + name] = (s * s) * _jax.random.uniform(kv, w.shape, _jnp.float32, 0.5, 1.5)
    if N_MICROBATCH > 1:
        for name, axis in PER_EXAMPLE_BATCH_AXIS.items():
            out[name] = _to_microbatches(out[name], axis)
    return {'x': out['x'], 'norm1_g': out['norm1_g'], 'w_in': out['w_in'], 'b_f': out['b_f'], 'q_norm_g': out['q_norm_g'], 'k_norm_g': out['k_norm_g'], 'conv_w': out['conv_w'], 'conv_b': out['conv_b'], 'conv_ln_g': out['conv_ln_g'], 'conv_ln_b': out['conv_ln_b'], 'w_o': out['w_o'], 'norm2_g': out['norm2_g'], 'w_mlp_in': out['w_mlp_in'], 'w_mlp_out': out['w_mlp_out'], 'loss_target': out['loss_target'], 'm_norm1_g': out['m_norm1_g'], 'm_w_in': out['m_w_in'], 'm_b_f': out['m_b_f'], 'm_q_norm_g': out['m_q_norm_g'], 'm_k_norm_g': out['m_k_norm_g'], 'm_conv_w': out['m_conv_w'], 'm_conv_b': out['m_conv_b'], 'm_conv_ln_g': out['m_conv_ln_g'], 'm_conv_ln_b': out['m_conv_ln_b'], 'm_w_o': out['m_w_o'], 'm_norm2_g': out['m_norm2_g'], 'm_w_mlp_in': out['m_w_mlp_in'], 'm_w_mlp_out': out['m_w_mlp_out'], 'v_norm1_g': out['v_norm1_g'], 'v_w_in': out['v_w_in'], 'v_b_f': out['v_b_f'], 'v_q_norm_g': out['v_q_norm_g'], 'v_k_norm_g': out['v_k_norm_g'], 'v_conv_w': out['v_conv_w'], 'v_conv_b': out['v_conv_b'], 'v_conv_ln_g': out['v_conv_ln_g'], 'v_conv_ln_b': out['v_conv_ln_b'], 'v_w_o': out['v_w_o'], 'v_norm2_g': out['v_norm2_g'], 'v_w_mlp_in': out['v_w_mlp_in'], 'v_w_mlp_out': out['v_w_mlp_out']}


def _loss(weights, diff, rest, loss_target):
    with _jax.named_scope("forward"):
        args = {**rest, TWIN_DIFF_INPUT: diff, **{k: w.astype(_WEIGHT_DTYPES[k]) for k, w in weights.items()}}
        y = _forward(args)
    with _jax.named_scope("loss_head"):
        err = _jnp.square(y.astype(_jnp.float32) - loss_target)
        return 0.5 * _jnp.sum(_jnp.mean(err, axis=-1)) if err.ndim else 0.5 * err


def _adamw(w, g, m, v):
    m = ADAM_B1 * m + (1.0 - ADAM_B1) * g
    v = ADAM_B2 * v + (1.0 - ADAM_B2) * _jnp.square(g)
    m_hat = m / (1.0 - ADAM_B1 ** ADAM_STEP)
    v_hat = v / (1.0 - ADAM_B2 ** ADAM_STEP)
    delta = -ADAM_LR * (m_hat / (_jnp.sqrt(v_hat) + ADAM_EPS) + ADAM_WD * w)
    return delta, m, v


def reference(x, norm1_g, w_in, b_f, q_norm_g, k_norm_g, conv_w, conv_b, conv_ln_g, conv_ln_b, w_o, norm2_g, w_mlp_in, w_mlp_out, loss_target, m_norm1_g, m_w_in, m_b_f, m_q_norm_g, m_k_norm_g, m_conv_w, m_conv_b, m_conv_ln_g, m_conv_ln_b, m_w_o, m_norm2_g, m_w_mlp_in, m_w_mlp_out, v_norm1_g, v_w_in, v_b_f, v_q_norm_g, v_k_norm_g, v_conv_w, v_conv_b, v_conv_ln_g, v_conv_ln_b, v_w_o, v_norm2_g, v_w_mlp_in, v_w_mlp_out):
    given = dict(x=x, norm1_g=norm1_g, w_in=w_in, b_f=b_f, q_norm_g=q_norm_g, k_norm_g=k_norm_g, conv_w=conv_w, conv_b=conv_b, conv_ln_g=conv_ln_g, conv_ln_b=conv_ln_b, w_o=w_o, norm2_g=norm2_g, w_mlp_in=w_mlp_in, w_mlp_out=w_mlp_out, loss_target=loss_target, m_norm1_g=m_norm1_g, m_w_in=m_w_in, m_b_f=m_b_f, m_q_norm_g=m_q_norm_g, m_k_norm_g=m_k_norm_g, m_conv_w=m_conv_w, m_conv_b=m_conv_b, m_conv_ln_g=m_conv_ln_g, m_conv_ln_b=m_conv_ln_b, m_w_o=m_w_o, m_norm2_g=m_norm2_g, m_w_mlp_in=m_w_mlp_in, m_w_mlp_out=m_w_mlp_out, v_norm1_g=v_norm1_g, v_w_in=v_w_in, v_b_f=v_b_f, v_q_norm_g=v_q_norm_g, v_k_norm_g=v_k_norm_g, v_conv_w=v_conv_w, v_conv_b=v_conv_b, v_conv_ln_g=v_conv_ln_g, v_conv_ln_b=v_conv_ln_b, v_w_o=v_w_o, v_norm2_g=v_norm2_g, v_w_mlp_in=v_w_mlp_in, v_w_mlp_out=v_w_mlp_out)
    weights = {n: given[n] for n in TWIN_WEIGHTS}
    shared = {n: given[n] for n in SHARED_INPUTS}
    per_example = {n: given[n] for n in ['x']}
    grad_fn = _jax.value_and_grad(_loss, argnums=(0, 1))

    def one_microbatch(ex, loss_target):
        ex = dict(ex)
        diff = ex.pop(TWIN_DIFF_INPUT)
        return grad_fn(weights, diff, {**shared, **ex}, loss_target)

    if N_MICROBATCH == 1:
        loss, (grad_w, grad_x) = one_microbatch(per_example, given["loss_target"])
    else:
        def body(carry, xs):
            loss_sum, grad_sum = carry
            l_k, (gw_k, gx_k) = one_microbatch(xs[0], xs[1])
            with _jax.named_scope("update"):
                return (loss_sum + l_k, _jax.tree.map(_jnp.add, grad_sum, gw_k)), gx_k

        init = (_jnp.zeros((), _jnp.float32), _jax.tree.map(_jnp.zeros_like, weights))
        (loss, grad_w), grad_x = _jax.lax.scan(body, init, (per_example, given["loss_target"]))
    with _jax.named_scope("update"):
        delta_w, new_m, new_v = {}, {}, {}
        for n in TWIN_WEIGHTS:
            delta_w[n], new_m[n], new_v[n] = _adamw(weights[n], grad_w[n], given["m_" + n], given["v_" + n])
    return (loss, grad_x, *[grad_w[n] for n in TWIN_WEIGHTS], *[delta_w[n] for n in TWIN_WEIGHTS],
            *[new_m[n] for n in TWIN_WEIGHTS], *[new_v[n] for n in TWIN_WEIGHTS])
```

```python
import functools

import jax
import jax.numpy as jnp
from jax import lax
from jax.experimental import pallas as pl
from jax.experimental.pallas import tpu as pltpu

F32 = jnp.float32
BF16 = jnp.bfloat16

EPS = 1e-6
HEAD_DIM = 64
LANES = 128
PAIR = 2 * LANES
N_DEV = 8
CONV_TAPS = 31
CONV_PAD = 32
NEG = -1e30

ADAM_LR = 0.001
ADAM_B1 = 0.9
ADAM_B2 = 0.999
ADAM_EPS = 1e-08
ADAM_WD = 0.01
ADAM_STEP = 10

TM = 512
TQ = 512
CONV_ROWS = 128
FLAT_ROWS = 1024
MESH = pl.DeviceIdType.MESH


def _params(*sem):
    return pltpu.CompilerParams(dimension_semantics=sem, vmem_limit_bytes=56 * 1024 * 1024)


def _split3(x):
    hi = x.astype(BF16)
    r1 = x - hi.astype(F32)
    mid = r1.astype(BF16)
    lo = (r1 - mid.astype(F32)).astype(BF16)
    return hi, mid, lo


def _dot(a, b):
    return jnp.dot(a, b, preferred_element_type=F32)


def _dot_nt(a, b):
    return lax.dot_general(a, b, (((1,), (1,)), ((), ())), preferred_element_type=F32)


def _dot_tn(a, b):
    return lax.dot_general(a, b, (((0,), (0,)), ((), ())), preferred_element_type=F32)


def _dot3(x, mat):
    hi, mid, lo = _split3(x)
    return _dot(hi, mat) + _dot(mid, mat) + _dot(lo, mat)


def _dot3_r(mat, x):
    hi, mid, lo = _split3(x)
    return _dot(mat, hi) + _dot(mat, mid) + _dot(mat, lo)


def _iota(shape, dim):
    return lax.broadcasted_iota(jnp.int32, shape, dim)


def _sigmoid(x):
    return 1.0 / (1.0 + jnp.exp(-x))


def _matmul(a, b, *, mode, out_dtypes, name, epilogue=None, extras=(), tm=TM, tn=1024, tk=1024):
    if mode == "nn":
        (m, k), (k2, n) = a.shape, b.shape
    elif mode == "nt":
        (m, k), (n, k2) = a.shape, b.shape
    else:
        (k, m), (k2, n) = a.shape, b.shape
    assert k == k2, (name, a.shape, b.shape)
    tm, tn, tk = min(tm, m), min(tn, n), min(tk, k)
    assert m % tm == 0 and n % tn == 0 and k % tk == 0, (name, m, n, k, tm, tn, tk)
    nk = k // tk
    if mode == "tn":
        a_spec = pl.BlockSpec((tk, tm), lambda i, j, kk: (kk, i))
    else:
        a_spec = pl.BlockSpec((tm, tk), lambda i, j, kk: (i, kk))
    if mode == "nt":
        b_spec = pl.BlockSpec((tn, tk), lambda i, j, kk: (j, kk))
    else:
        b_spec = pl.BlockSpec((tk, tn), lambda i, j, kk: (kk, j))
    dot = {"nn": _dot, "nt": _dot_nt, "tn": _dot_tn}[mode]
    tile = pl.BlockSpec((tm, tn), lambda i, j, kk: (i, j))
    n_ex, n_out = len(extras), len(out_dtypes)

    def body(a_ref, b_ref, *rest):
        ex_refs, out_refs = rest[:n_ex], rest[n_ex:n_ex + n_out]
        part = dot(a_ref[...].astype(BF16), b_ref[...].astype(BF16))

        def finish(acc):
            res = epilogue(acc, *[e[...] for e in ex_refs]) if epilogue is not None else (acc,)
            for o_ref, r in zip(out_refs, res):
                o_ref[...] = r.astype(o_ref.dtype)

        if nk == 1:
            finish(part)
        else:
            acc_ref = rest[-1]
            kk = pl.program_id(2)

            @pl.when(kk == 0)
            def _():
                acc_ref[...] = part

            @pl.when(kk > 0)
            def _():
                acc_ref[...] += part

            @pl.when(kk == nk - 1)
            def _():
                finish(acc_ref[...])

    outs = pl.pallas_call(
        body,
        name=name,
        grid=(m // tm, n // tn, nk),
        in_specs=[a_spec, b_spec] + [tile] * n_ex,
        out_specs=[tile] * n_out,
        out_shape=[jax.ShapeDtypeStruct((m, n), dt) for dt in out_dtypes],
        scratch_shapes=[pltpu.VMEM((tm, tn), F32)] if nk > 1 else [],
        compiler_params=_params("parallel", "parallel", "arbitrary"),
    )(a, b, *extras)
    return outs if n_out > 1 else outs[0]


def _rms_fwd(x, g, name):
    s, d = x.shape
    ts = min(TM, s)

    def body(x_ref, g_ref, u_ref):
        xv = x_ref[...]
        y = xv * lax.rsqrt(jnp.mean(xv * xv, axis=-1, keepdims=True) + EPS)
        u_ref[...] = (y * g_ref[...]).astype(BF16)

    return pl.pallas_call(
        body, name=name, grid=(s // ts,),
        in_specs=[pl.BlockSpec((ts, d), lambda i: (i, 0)), pl.BlockSpec((1, d), lambda i: (0, 0))],
        out_specs=pl.BlockSpec((ts, d), lambda i: (i, 0)),
        out_shape=jax.ShapeDtypeStruct((s, d), BF16),
        compiler_params=_params("parallel"),
    )(x, g)


def _rms_bwd(x, g, du, dres, name):
    s, d = x.shape
    ts = min(TM, s)

    def body(x_ref, g_ref, du_ref, dres_ref, dx_ref, dg_ref):
        @pl.when(pl.program_id(0) == 0)
        def _():
            dg_ref[...] = jnp.zeros_like(dg_ref)

        xv, duv = x_ref[...], du_ref[...]
        r = lax.rsqrt(jnp.mean(xv * xv, axis=-1, keepdims=True) + EPS)
        xh = xv * r
        dxh = duv * g_ref[...]
        dx_ref[...] = dres_ref[...] + r * (dxh - xh * jnp.mean(dxh * xh, axis=-1, keepdims=True))
        dg_ref[...] += jnp.sum(duv * xh, axis=0, keepdims=True)

    row = pl.BlockSpec((ts, d), lambda i: (i, 0))
    vec = pl.BlockSpec((1, d), lambda i: (0, 0))
    return pl.pallas_call(
        body, name=name, grid=(s // ts,),
        in_specs=[row, vec, row, row], out_specs=[row, vec],
        out_shape=[jax.ShapeDtypeStruct((s, d), F32), jax.ShapeDtypeStruct((1, d), F32)],
        compiler_params=_params("arbitrary"),
    )(x, g, du, dres)


def _ln_silu_fwd(y, g, b, name):
    s, c = y.shape
    ts = min(TM, s)

    def body(y_ref, g_ref, b_ref, h_ref):
        yv = y_ref[...]
        mu = jnp.mean(yv, axis=-1, keepdims=True)
        yc = yv - mu
        z = yc * lax.rsqrt(jnp.mean(yc * yc, axis=-1, keepdims=True) + EPS) * g_ref[...] + b_ref[...]
        h_ref[...] = (z * _sigmoid(z)).astype(BF16)

    row = pl.BlockSpec((ts, c), lambda i: (i, 0))
    vec = pl.BlockSpec((1, c), lambda i: (0, 0))
    return pl.pallas_call(
        body, name=name, grid=(s // ts,), in_specs=[row, vec, vec], out_specs=row,
        out_shape=jax.ShapeDtypeStruct((s, c), BF16), compiler_params=_params("parallel"),
    )(y, g, b)


def _ln_silu_bwd(y, g, b, dmixed, name):
    s, c = y.shape
    ts = min(TM, s)

    def body(y_ref, g_ref, b_ref, dh_ref, dy_ref, dg_ref, db_ref):
        @pl.when(pl.program_id(0) == 0)
        def _():
            dg_ref[...] = jnp.zeros_like(dg_ref)
            db_ref[...] = jnp.zeros_like(db_ref)

        yv = y_ref[...]
        mu = jnp.mean(yv, axis=-1, keepdims=True)
        yc = yv - mu
        r = lax.rsqrt(jnp.mean(yc * yc, axis=-1, keepdims=True) + EPS)
        yh = yc * r
        z = yh * g_ref[...] + b_ref[...]
        sg = _sigmoid(z)
        dz = dh_ref[...] * (sg * (1.0 + z * (1.0 - sg)))
        dg_ref[...] += jnp.sum(dz * yh, axis=0, keepdims=True)
        db_ref[...] += jnp.sum(dz, axis=0, keepdims=True)
        dyh = dz * g_ref[...]
        dy_ref[...] = r * (dyh - jnp.mean(dyh, axis=-1, keepdims=True) - yh * jnp.mean(dyh * yh, axis=-1, keepdims=True))

    row = pl.BlockSpec((ts, c), lambda i: (i, 0))
    vec = pl.BlockSpec((1, c), lambda i: (0, 0))
    return pl.pallas_call(
        body, name=name, grid=(s // ts,),
        in_specs=[row, vec, vec, pl.BlockSpec((ts, c), lambda i: (i, 1))], out_specs=[row, vec, vec],
        out_shape=[jax.ShapeDtypeStruct((s, c), F32), jax.ShapeDtypeStruct((1, c), F32), jax.ShapeDtypeStruct((1, c), F32)],
        compiler_params=_params("arbitrary"),
    )(y, g, b, dmixed)


def _loss_grad(y, target, name):
    s, d = y.shape
    ts = min(TM, s)

    def body(y_ref, t_ref, dy_ref, sq_ref):
        @pl.when(pl.program_id(0) == 0)
        def _():
            sq_ref[...] = jnp.zeros_like(sq_ref)

        err = y_ref[...] - t_ref[...]
        dy_ref[...] = err * (1.0 / d)
        sq_ref[...] += jnp.sum(err * err, axis=0, keepdims=True)

    row = pl.BlockSpec((ts, d), lambda i: (i, 0))
    vec = pl.BlockSpec((1, d), lambda i: (0, 0))
    return pl.pallas_call(
        body, name=name, grid=(s // ts,), in_specs=[row, row], out_specs=[row, vec],
        out_shape=[jax.ShapeDtypeStruct((s, d), F32), jax.ShapeDtypeStruct((1, d), F32)],
        compiler_params=_params("arbitrary"),
    )(y, target)


def _head_masks():
    lane2 = _iota((1, PAIR), 1)
    lane1 = _iota((1, LANES), 1)
    qa = (lane2 < HEAD_DIM) | ((lane2 >= LANES) & (lane2 < LANES + 3))
    qb = ((lane2 >= HEAD_DIM) & (lane2 < LANES)) | ((lane2 >= LANES + 3) & (lane2 < LANES + 6))
    return (qa, qb), (lane1 < HEAD_DIM, lane1 >= HEAD_DIM)


def _group_matrix(width):
    shift = HEAD_DIM.bit_length() - 1
    return ((_iota((width, width), 0) >> shift) == (_iota((width, width), 1) >> shift)).astype(BF16)


def _prep_fwd(proj, gq, gk, bf, n_heads, name):
    s = proj.shape[0]
    aw = n_heads * HEAD_DIM
    n_pairs = n_heads // 2
    ts = min(TM, s)
    f_col = (proj.shape[1] - LANES) // LANES

    def body(q_ref, k_ref, v_ref, f_ref, gq_ref, gk_ref, bf_ref, qa_ref, ka_ref, vb_ref, carry_ref):
        @pl.when(pl.program_id(0) == 0)
        def _():
            carry_ref[...] = jnp.zeros_like(carry_ref)

        gmat = _group_matrix(aw)

        def head_norm(xv, g):
            ms = _dot3(xv * xv, gmat) * (1.0 / HEAD_DIM)
            return xv * lax.rsqrt(ms + EPS) * g

        qn = head_norm(q_ref[...], gq_ref[...]) * (HEAD_DIM ** -0.5)
        kn = head_norm(k_ref[...], gk_ref[...])
        z = f_ref[...] + bf_ref[...]
        logf = jnp.minimum(z, 0.0) - jnp.log(1.0 + jnp.exp(-jnp.abs(z)))
        tri = (_iota((ts, ts), 0) >= _iota((ts, ts), 1)).astype(BF16)
        c = _dot3_r(tri, logf) + carry_ref[...]
        carry_ref[...] = c[ts - 1:ts, :]
        terms = _split3(-c)
        row, col = _iota((LANES, LANES), 0), _iota((LANES, LANES), 1)
        ones = jnp.where(_iota((ts, LANES), 1) < 6, 1.0, 0.0).astype(BF16)
        for p in range(n_pairs):
            extra = jnp.zeros((ts, LANES), F32)
            for t, term in enumerate(terms):
                sel = ((row == 2 * p) & (col == t)) | ((row == 2 * p + 1) & (col == 3 + t))
                extra += _dot(term, sel.astype(BF16))
            lo, hi = p * PAIR, p * PAIR + LANES
            ka_ref[:, lo:hi] = kn[:, p * LANES:(p + 1) * LANES].astype(BF16)
            ka_ref[:, hi:hi + LANES] = extra.astype(BF16)
            qa_ref[:, lo:hi] = qn[:, p * LANES:(p + 1) * LANES].astype(BF16)
            qa_ref[:, hi:hi + LANES] = ones
        vb_ref[...] = v_ref[...].astype(BF16)

    blk = lambda j: pl.BlockSpec((ts, aw), lambda i: (i, j))
    vec = lambda w: pl.BlockSpec((1, w), lambda i: (0, 0))
    return pl.pallas_call(
        body, name=name, grid=(s // ts,),
        in_specs=[blk(0), blk(1), blk(2), pl.BlockSpec((ts, LANES), lambda i: (i, f_col)), vec(aw), vec(aw), vec(LANES)],
        out_specs=[pl.BlockSpec((ts, n_pairs * PAIR), lambda i: (i, 0)), pl.BlockSpec((ts, n_pairs * PAIR), lambda i: (i, 0)), blk(0)],
        out_shape=[jax.ShapeDtypeStruct((s, n_pairs * PAIR), BF16), jax.ShapeDtypeStruct((s, n_pairs * PAIR), BF16),
                   jax.ShapeDtypeStruct((s, aw), BF16)],
        scratch_shapes=[pltpu.VMEM((1, LANES), F32)],
        compiler_params=_params("arbitrary"),
    )(proj, proj, proj, proj, gq, gk, bf)


def _prep_bwd(proj, dq, dka, drow, dcol, gq, gk, bf, n_heads, name):
    s = proj.shape[0]
    aw = n_heads * HEAD_DIM
    n_pairs = n_heads // 2
    ts = min(TM, s)
    nt = s // ts
    f_col = (proj.shape[1] - LANES) // LANES
    shift = HEAD_DIM.bit_length() - 1

    def body(q_ref, k_ref, f_ref, dq_ref, dka_ref, drow_ref, dcol_ref, gq_ref, gk_ref, bf_ref,
             dpq_ref, dpk_ref, dpf_ref, dgq_ref, dgk_ref, dbf_ref, carry_ref):
        @pl.when(pl.program_id(0) == 0)
        def _():
            carry_ref[...] = jnp.zeros_like(carry_ref)
            dgq_ref[...] = jnp.zeros_like(dgq_ref)
            dgk_ref[...] = jnp.zeros_like(dgk_ref)
            dbf_ref[...] = jnp.zeros_like(dbf_ref)

        gmat = _group_matrix(aw)

        def head_norm_bwd(xv, g, dn):
            r = lax.rsqrt(_dot3(xv * xv, gmat) * (1.0 / HEAD_DIM) + EPS)
            xh = xv * r
            dxh = dn * g
            dx = r * (dxh - xh * (_dot3(dxh * xh, gmat) * (1.0 / HEAD_DIM)))
            return dx, jnp.sum(dn * xh, axis=0, keepdims=True)

        dkav = dka_ref[...]
        dx, dg = head_norm_bwd(q_ref[...], gq_ref[...], dq_ref[...] * (HEAD_DIM ** -0.5))
        dpq_ref[...] = dx.astype(BF16)
        dgq_ref[...] += dg
        dkn = jnp.concatenate([dkav[:, p * PAIR:p * PAIR + LANES] for p in range(n_pairs)], axis=1)
        dx, dg = head_norm_bwd(k_ref[...], gk_ref[...], dkn)
        dpk_ref[...] = dx.astype(BF16)
        dgk_ref[...] += dg

        pick = (_iota((aw, LANES), 0) == (_iota((aw, LANES), 1) << shift)).astype(BF16)
        dc = _dot3(drow_ref[...], pick)
        r16, c16 = _iota((16, LANES), 0), _iota((16, LANES), 1)
        for p in range(n_pairs):
            place = ((r16 < 2) & (c16 == 2 * p + r16)).astype(BF16)
            for term in _split3(dcol_ref[p]):
                dc -= _dot_tn(term, place)
        triu = (_iota((ts, ts), 0) <= _iota((ts, ts), 1)).astype(BF16)
        dlogf = _dot3_r(triu, dc) + carry_ref[...]
        carry_ref[...] = dlogf[0:1, :]
        z = f_ref[...] + bf_ref[...]
        dz = dlogf * (1.0 / (1.0 + jnp.exp(z)))
        dpf_ref[...] = dz.astype(BF16)
        dbf_ref[...] += jnp.sum(dz, axis=0, keepdims=True)

    rev = lambda w, j: pl.BlockSpec((ts, w), lambda i: (nt - 1 - i, j))
    vec = lambda w: pl.BlockSpec((1, w), lambda i: (0, 0))
    return pl.pallas_call(
        body, name=name, grid=(nt,),
        in_specs=[rev(aw, 0), rev(aw, 1), rev(LANES, f_col), rev(aw, 0), rev(n_pairs * PAIR, 0), rev(aw, 0),
                  pl.BlockSpec((n_pairs, 16, ts), lambda i: (0, 0, nt - 1 - i)), vec(aw), vec(aw), vec(LANES)],
        out_specs=[rev(aw, 0), rev(aw, 0), rev(LANES, 0), vec(aw), vec(aw), vec(LANES)],
        out_shape=[jax.ShapeDtypeStruct((s, aw), BF16), jax.ShapeDtypeStruct((s, aw), BF16), jax.ShapeDtypeStruct((s, LANES), BF16),
                   jax.ShapeDtypeStruct((1, aw), F32), jax.ShapeDtypeStruct((1, aw), F32), jax.ShapeDtypeStruct((1, LANES), F32)],
        scratch_shapes=[pltpu.VMEM((1, LANES), F32)],
        compiler_params=_params("arbitrary"),
    )(proj, proj, proj, dq, dka, drow, dcol, gq, gk, bf)


def _attn_fwd(qa, ka, vb, n_heads, name):
    s = qa.shape[0]
    aw = n_heads * HEAD_DIM
    n_pairs = n_heads // 2
    tq = min(TQ, s)

    def body(q_ref, k_ref, v_ref, o_ref, lse_ref):
        i = pl.program_id(1)
        qmasks, omasks = _head_masks()
        qv = q_ref[...]
        causal = _iota((tq, tq), 1) <= _iota((tq, tq), 0)
        res = []
        for h in range(2):
            qh = jnp.where(qmasks[h], qv, jnp.zeros_like(qv))

            def step(j, carry, masked, qh=qh):
                m, l, acc = carry
                off = pl.multiple_of(j * tq, tq)
                sc = _dot_nt(qh, k_ref[pl.ds(off, tq), :])
                if masked:
                    sc = jnp.where(causal, sc, NEG)
                m_new = jnp.maximum(m, jnp.max(sc, axis=1, keepdims=True))
                p = jnp.exp(sc - m_new)
                alpha = jnp.exp(m - m_new)
                l = alpha * l + jnp.sum(p, axis=1, keepdims=True)
                acc = alpha * acc + _dot(p.astype(BF16), v_ref[pl.ds(off, tq), :])
                return m_new, l, acc

            init = (jnp.full((tq, 1), NEG, F32), jnp.zeros((tq, 1), F32), jnp.zeros((tq, LANES), F32))
            carry = lax.fori_loop(0, i, functools.partial(step, masked=False), init)
            m, l, acc = step(i, carry, True)
            res.append((acc * (1.0 / l), m + jnp.log(l)))
        o_ref[...] = jnp.where(omasks[0], res[0][0], res[1][0])
        lse_ref[...] = jnp.where(omasks[0], res[0][1], res[1][1])

    return pl.pallas_call(
        body, name=name, grid=(n_pairs, s // tq),
        in_specs=[pl.BlockSpec((tq, PAIR), lambda p, i: (i, p)), pl.BlockSpec((s, PAIR), lambda p, i: (0, p)),
                  pl.BlockSpec((s, LANES), lambda p, i: (0, p))],
        out_specs=[pl.BlockSpec((tq, LANES), lambda p, i: (i, p))] * 2,
        out_shape=[jax.ShapeDtypeStruct((s, aw), F32)] * 2,
        compiler_params=_params("parallel", "parallel"),
    )(qa, ka, vb)


def _attn_bwd(qa, ka, vb, o, lse, dmixed, n_heads, name):
    s = qa.shape[0]
    aw = n_heads * HEAD_DIM
    n_pairs = n_heads // 2
    tq = min(TQ, s)
    nq = s // tq

    def body(q_ref, k_ref, v_ref, o_ref, lse_ref, do_ref, dq_ref, dka_ref, dv_ref, drow_ref, dcol_ref, delta_ref):
        j = pl.program_id(1)
        qmasks, omasks = _head_masks()

        @pl.when(j == 0)
        def _():
            dq_ref[...] = jnp.zeros_like(dq_ref)
            drow_ref[...] = jnp.zeros_like(drow_ref)
            for c in range(nq):
                rows = slice(c * tq, (c + 1) * tq)
                prod = do_ref[rows, :] * o_ref[rows, :]
                da = jnp.sum(jnp.where(omasks[0], prod, 0.0), axis=1, keepdims=True)
                db = jnp.sum(jnp.where(omasks[1], prod, 0.0), axis=1, keepdims=True)
                delta_ref[rows, :] = jnp.where(omasks[0], da, db)

        dka_ref[...] = jnp.zeros_like(dka_ref)
        dv_ref[...] = jnp.zeros_like(dv_ref)
        dcol_ref[...] = jnp.zeros_like(dcol_ref)
        kv = k_ref[...]
        kk = kv[:, :LANES]
        vv = v_ref[...]
        causal = _iota((tq, tq), 1) <= _iota((tq, tq), 0)

        def step(i, masked):
            off = pl.multiple_of(i * tq, tq)
            qv = q_ref[pl.ds(off, tq), :]
            dov = do_ref[pl.ds(off, tq), :]
            lsev = lse_ref[pl.ds(off, tq), :]
            dlv = delta_ref[pl.ds(off, tq), :]
            for h in range(2):
                qh = jnp.where(qmasks[h], qv, jnp.zeros_like(qv))
                doh = jnp.where(omasks[h], dov, 0.0).astype(BF16)
                lane = h * HEAD_DIM
                sc = _dot_nt(qh, kv)
                if masked:
                    sc = jnp.where(causal, sc, NEG)
                p = jnp.exp(sc - lsev[:, lane:lane + 1])
                dv_ref[...] += _dot_tn(p.astype(BF16), doh)
                dp = _dot_nt(doh, vv)
                dsf = p * (dp - dlv[:, lane:lane + 1])
                drow_ref[pl.ds(off, tq), :] += jnp.where(omasks[h], jnp.sum(dsf, axis=1, keepdims=True), 0.0)
                dcol_ref[0, h:h + 1, :] += jnp.sum(dsf, axis=0, keepdims=True)
                ds = dsf.astype(BF16)
                dka_ref[...] += _dot_tn(ds, qh)
                dq_ref[pl.ds(off, tq), :] += jnp.where(omasks[h], _dot(ds, kk), 0.0)

        step(j, True)

        def loop_body(i, carry):
            step(i, False)
            return carry

        lax.fori_loop(j + 1, nq, loop_body, 0)

    full = lambda w: pl.BlockSpec((s, w), lambda p, j: (0, p))
    blk = lambda w: pl.BlockSpec((tq, w), lambda p, j: (j, p))
    return pl.pallas_call(
        body, name=name, grid=(n_pairs, nq),
        in_specs=[full(PAIR), blk(PAIR), blk(LANES), full(LANES), full(LANES), full(LANES)],
        out_specs=[full(LANES), blk(PAIR), blk(LANES), full(LANES), pl.BlockSpec((1, 16, tq), lambda p, j: (p, 0, j))],
        out_shape=[jax.ShapeDtypeStruct((s, aw), F32), jax.ShapeDtypeStruct((s, n_pairs * PAIR), F32),
                   jax.ShapeDtypeStruct((s, aw), F32), jax.ShapeDtypeStruct((s, aw), F32),
                   jax.ShapeDtypeStruct((n_pairs, 16, s), F32)],
        scratch_shapes=[pltpu.VMEM((s, LANES), F32)],
        compiler_params=_params("parallel", "arbitrary"),
    )(qa, ka, vb, o, lse, dmixed)


def _conv_fwd(proj, w32, bias, n_ch, a_col, g_col, name):
    s = proj.shape[0]
    rows = min(CONV_ROWS, s)

    def body(a_ref, g_ref, w_ref, b_ref, y_ref, pad_ref):
        pad_ref[0:CONV_PAD, :] = jnp.zeros((CONV_PAD, LANES), F32)
        pad_ref[CONV_PAD:CONV_PAD + s, :] = a_ref[...] * _sigmoid(g_ref[...])
        wv = w_ref[...]
        for c in range(s // rows):
            acc = jnp.broadcast_to(b_ref[...], (rows, LANES))
            for t in range(CONV_TAPS):
                start = c * rows + CONV_PAD - (CONV_TAPS - 1) + t
                acc = acc + wv[t:t + 1, :] * pad_ref[start:start + rows, :]
            y_ref[c * rows:(c + 1) * rows, :] = acc

    col = lambda j0: pl.BlockSpec((s, LANES), lambda c: (0, j0 + c))
    return pl.pallas_call(
        body, name=name, grid=(n_ch // LANES,),
        in_specs=[col(a_col), col(g_col), pl.BlockSpec((CONV_PAD, LANES), lambda c: (0, c)), pl.BlockSpec((1, LANES), lambda c: (0, c))],
        out_specs=pl.BlockSpec((s, LANES), lambda c: (0, c)),
        out_shape=jax.ShapeDtypeStruct((s, n_ch), F32),
        scratch_shapes=[pltpu.VMEM((s + CONV_PAD, LANES), F32)],
        compiler_params=_params("parallel"),
    )(proj, proj, w32, bias)


def _conv_bwd(proj, w32, dy, n_ch, a_col, g_col, name):
    s = proj.shape[0]
    rows = min(CONV_ROWS, s)
    sub = 8

    def fold(x):
        acc = x[0:sub, :]
        for r in range(1, rows // sub):
            acc = acc + x[r * sub:(r + 1) * sub, :]
        return acc

    def body(a_ref, g_ref, w_ref, dy_ref, da_ref, dg_ref, dw_ref, padh_ref, padd_ref):
        sg = _sigmoid(g_ref[...])
        padh_ref[0:CONV_PAD, :] = jnp.zeros((CONV_PAD, LANES), F32)
        padh_ref[CONV_PAD:CONV_PAD + s, :] = a_ref[...] * sg
        padd_ref[0:s, :] = dy_ref[...]
        padd_ref[s:s + CONV_PAD, :] = jnp.zeros((CONV_PAD, LANES), F32)
        wv = w_ref[...]
        dw = [jnp.zeros((sub, LANES), F32) for _ in range(CONV_TAPS + 1)]
        for c in range(s // rows):
            r0 = c * rows
            acc = jnp.zeros((rows, LANES), F32)
            dyc = dy_ref[r0:r0 + rows, :]
            for t in range(CONV_TAPS):
                back = r0 + (CONV_TAPS - 1) - t
                acc = acc + wv[t:t + 1, :] * padd_ref[back:back + rows, :]
                start = r0 + CONV_PAD - (CONV_TAPS - 1) + t
                dw[t] = dw[t] + fold(dyc * padh_ref[start:start + rows, :])
            dw[CONV_TAPS] = dw[CONV_TAPS] + fold(dyc)
            av = a_ref[r0:r0 + rows, :]
            sgc = _sigmoid(g_ref[r0:r0 + rows, :])
            da_ref[r0:r0 + rows, :] = (acc * sgc).astype(BF16)
            dg_ref[r0:r0 + rows, :] = (acc * av * sgc * (1.0 - sgc)).astype(BF16)
        for t in range(CONV_TAPS + 1):
            dw_ref[t:t + 1, :] = jnp.sum(dw[t], axis=0, keepdims=True)

    col = lambda j0: pl.BlockSpec((s, LANES), lambda c: (0, j0 + c))
    wspec = pl.BlockSpec((CONV_PAD, LANES), lambda c: (0, c))
    return pl.pallas_call(
        body, name=name, grid=(n_ch // LANES,),
        in_specs=[col(a_col), col(g_col), wspec, col(0)],
        out_specs=[col(0), col(0), wspec],
        out_shape=[jax.ShapeDtypeStruct((s, n_ch), BF16), jax.ShapeDtypeStruct((s, n_ch), BF16),
                   jax.ShapeDtypeStruct((CONV_PAD, n_ch), F32)],
        scratch_shapes=[pltpu.VMEM((s + CONV_PAD, LANES), F32), pltpu.VMEM((s + CONV_PAD, LANES), F32)],
        compiler_params=_params("parallel"),
    )(proj, proj, w32, dy)


def _my_place():
    return lax.axis_index("x"), lax.axis_index("y"), lax.axis_index("c")


def _all_gather(shard, name):
    r, w = shard.shape

    def body(x_ref, out_ref, send_sems, recv_sems, local_sem):
        x, y, c = _my_place()
        me, sibling = (x, y, c), (x, y, 1 - c)
        chips = [(1 - x, y), (x, 1 - y), (1 - x, 1 - y)]

        def slot(px, py, pc):
            return out_ref.at[4 * px + 2 * py + pc]

        def copy(k, block, to, src=None):
            return pltpu.make_async_remote_copy(
                src_ref=slot(*block) if src is None else src, dst_ref=slot(*block),
                send_sem=send_sems.at[k], recv_sem=recv_sems.at[k], device_id=to, device_id_type=MESH)

        mine = pltpu.make_async_copy(x_ref, slot(*me), local_sem)
        mine.start()
        first = [copy(0, me, sibling, src=x_ref)]
        first += [copy(1 + j, me, (*chip, c), src=x_ref) for j, chip in enumerate(chips)]
        for cp in first:
            cp.start()
        passed = [copy(4 + j, (*chip, c), sibling) for j, chip in enumerate(chips)]
        for j, chip in enumerate(chips):
            copy(1 + j, (*chip, c), me).wait_recv()
            passed[j].start()
        copy(0, sibling, me).wait_recv()
        for j, chip in enumerate(chips):
            copy(4 + j, (*chip, 1 - c), me).wait_recv()
        for cp in first + passed:
            cp.wait_send()
        mine.wait()

    return pl.pallas_call(
        body, name=name,
        out_shape=jax.ShapeDtypeStruct((N_DEV, r, w), shard.dtype),
        in_specs=[pl.BlockSpec(memory_space=pl.ANY)], out_specs=pl.BlockSpec(memory_space=pl.ANY),
        scratch_shapes=[pltpu.SemaphoreType.DMA((7,)), pltpu.SemaphoreType.DMA((7,)), pltpu.SemaphoreType.DMA],
    )(shard)


def _flip(place, k):
    x, y, c = place
    return (1 - x if k & 4 else x, 1 - y if k & 2 else y, 1 - c if k & 1 else c)


def _scatter_slabs(slabs, name):
    _, r, w = slabs.shape

    def body(src_ref, out_ref, send_sems, recv_sems):
        place = _my_place()
        copies = []
        for k in range(1, N_DEV):
            px, py, pc = _flip(place, k)
            copies.append(pltpu.make_async_remote_copy(
                src_ref=src_ref.at[4 * px + 2 * py + pc], dst_ref=out_ref.at[k - 1],
                send_sem=send_sems.at[k - 1], recv_sem=recv_sems.at[k - 1], device_id=(px, py, pc), device_id_type=MESH))
        for cp in copies:
            cp.start()
        for cp in copies:
            cp.wait()

    return pl.pallas_call(
        body, name=name,
        out_shape=jax.ShapeDtypeStruct((N_DEV - 1, r, w), slabs.dtype),
        in_specs=[pl.BlockSpec(memory_space=pl.ANY)], out_specs=pl.BlockSpec(memory_space=pl.ANY),
        scratch_shapes=[pltpu.SemaphoreType.DMA((7,)), pltpu.SemaphoreType.DMA((7,))],
    )(slabs)


def _all_reduce_small(g, name):
    r, w = g.shape

    def body(g_ref, out_ref, buf_ref, send_sems, recv_sems):
        place = _my_place()
        me = 4 * place[0] + 2 * place[1] + place[2]
        buf_ref[me] = g_ref[...]
        copies = []
        for k in range(1, N_DEV):
            copies.append(pltpu.make_async_remote_copy(
                src_ref=g_ref, dst_ref=buf_ref.at[me],
                send_sem=send_sems.at[k - 1], recv_sem=recv_sems.at[k - 1], device_id=_flip(place, k), device_id_type=MESH))
        for cp in copies:
            cp.start()
        for cp in copies:
            cp.wait()
        acc = buf_ref[0]
        for d in range(1, N_DEV):
            acc = acc + buf_ref[d]
        out_ref[...] = acc

    return pl.pallas_call(
        body, name=name,
        out_shape=jax.ShapeDtypeStruct((r, w), F32),
        in_specs=[pl.BlockSpec(memory_space=pltpu.VMEM)], out_specs=pl.BlockSpec(memory_space=pltpu.VMEM),
        scratch_shapes=[pltpu.VMEM((N_DEV, r, w), F32), pltpu.SemaphoreType.DMA((7,)), pltpu.SemaphoreType.DMA((7,))],
    )(g)


def _adamw(w, m, v, g_own, recv, name):
    rows = w.shape[0]
    tr = min(FLAT_ROWS, rows)
    assert rows % tr == 0, (name, rows)
    n_recv = 0 if recv is None else recv.shape[0]

    def body(*refs):
        w_ref, m_ref, v_ref, g_ref = refs[:4]
        g_out, d_out, m_out, v_out = refs[-4:]
        g = g_ref[...]
        for k in range(n_recv):
            g = g + refs[4][k].astype(F32)
        m_new = ADAM_B1 * m_ref[...] + (1.0 - ADAM_B1) * g
        v_new = ADAM_B2 * v_ref[...] + (1.0 - ADAM_B2) * (g * g)
        m_hat = m_new / (1.0 - ADAM_B1 ** ADAM_STEP)
        v_hat = v_new / (1.0 - ADAM_B2 ** ADAM_STEP)
        g_out[...] = g
        d_out[...] = -ADAM_LR * (m_hat / (jnp.sqrt(v_hat) + ADAM_EPS) + ADAM_WD * w_ref[...])
        m_out[...] = m_new
        v_out[...] = v_new

    flat = pl.BlockSpec((tr, LANES), lambda i: (i, 0))
    in_specs = [flat] * 4 + ([pl.BlockSpec((n_recv, tr, LANES), lambda i: (0, i, 0))] if n_recv else [])
    args = (w, m, v, g_own) + ((recv,) if n_recv else ())
    return pl.pallas_call(
        body, name=name, grid=(rows // tr,), in_specs=in_specs, out_specs=[flat] * 4,
        out_shape=[jax.ShapeDtypeStruct((rows, LANES), F32)] * 4,
        compiler_params=_params("parallel"),
    )(*args)


def _round_up(n, mult):
    return (n + mult - 1) // mult * mult


def _flatten(parts, row_mult):
    flat = jnp.concatenate([p.reshape(-1) for p in parts])
    rows = _round_up(-(-flat.shape[0] // LANES), row_mult)
    return jnp.pad(flat, (0, rows * LANES - flat.shape[0])).reshape(rows, LANES)


def _unflatten(flat, shapes):
    flat = flat.reshape(-1)
    out, off = [], 0
    for shp in shapes:
        n = 1
        for dim in shp:
            n *= dim
        out.append(flat[off:off + n].reshape(shp))
        off += n
    return out


def kernel(x, norm1_g, w_in, b_f, q_norm_g, k_norm_g, conv_w, conv_b, conv_ln_g, conv_ln_b, w_o, norm2_g, w_mlp_in, w_mlp_out, loss_target, m_norm1_g, m_w_in, m_b_f, m_q_norm_g, m_k_norm_g, m_conv_w, m_conv_b, m_conv_ln_g, m_conv_ln_b, m_w_o, m_norm2_g, m_w_mlp_in, m_w_mlp_out, v_norm1_g, v_w_in, v_b_f, v_q_norm_g, v_k_norm_g, v_conv_w, v_conv_b, v_conv_ln_g, v_conv_ln_b, v_w_o, v_norm2_g, v_w_mlp_in, v_w_mlp_out):
    depth, d_model, n_in_loc = w_in.shape
    n_heads = b_f.shape[1]
    aw = n_heads * HEAD_DIM
    cc = conv_b.shape[1]
    n_in = n_in_loc * N_DEV
    o_f = 3 * aw
    n_all = 3 * aw + 2 * cc + LANES
    assert n_in == 3 * aw + n_heads + 2 * cc and aw + cc == d_model and n_heads % 2 == 0
    assert aw % LANES == 0 and cc % LANES == 0 and x.shape[0] == 1
    me = 4 * lax.axis_index("x") + 2 * lax.axis_index("y") + lax.axis_index("c")

    big = (w_in, w_o, w_mlp_in, w_mlp_out)
    big_m = (m_w_in, m_w_o, m_w_mlp_in, m_w_mlp_out)
    big_v = (v_w_in, v_w_o, v_w_mlp_in, v_w_mlp_out)
    big_shapes = [w.shape for w in big]
    rows_of = [w[0].size // LANES for w in big]
    assert all(w[0].size % LANES == 0 for w in big)

    conv_rows = conv_w.size // LANES
    assert conv_w.size % LANES == 0
    gathered = _all_gather(_flatten([w.astype(BF16) for w in big] + list(_split3(conv_w)), 16), "all_gather_weights")
    conv_off = sum(rows_of) * depth
    conv_full = sum(
        jnp.moveaxis(gathered[:, conv_off + t * conv_rows:conv_off + (t + 1) * conv_rows, :].astype(F32)
                     .reshape((N_DEV,) + conv_w.shape), 0, 2).reshape(depth, CONV_TAPS, cc)
        for t in range(3))

    def whole(idx, layer):
        off = sum(rows_of[i] * depth for i in range(idx)) + rows_of[idx] * layer
        part = gathered[:, off:off + rows_of[idx], :].reshape((N_DEV,) + big_shapes[idx][1:])
        if idx in (0, 2):
            return jnp.moveaxis(part, 0, 1).reshape(part.shape[1], -1)
        return part.reshape(-1, part.shape[2])

    def to_all(w):
        return jnp.concatenate([w[:, :o_f], w[:, o_f + n_heads:], w[:, o_f:o_f + n_heads],
                                jnp.zeros((w.shape[0], LANES - n_heads), w.dtype)], axis=1)

    def from_all(w):
        return jnp.concatenate([w[:, :o_f], w[:, n_all - LANES:n_all - LANES + n_heads], w[:, o_f:n_all - LANES]], axis=1)

    w_all = [to_all(whole(0, l)) for l in range(depth)]
    w_out = [whole(1, l) for l in range(depth)]
    w_ff1 = [whole(2, l) for l in range(depth)]
    w_ff2 = [whole(3, l) for l in range(depth)]

    def row(p, l, width=None):
        v = p[l].reshape(1, -1)
        return v if width is None else jnp.pad(v, ((0, 0), (0, width - v.shape[1])))

    a_col, g_col = 3 * aw // LANES, (3 * aw + cc) // LANES

    gq = [jnp.tile(row(q_norm_g, l), (1, n_heads)) for l in range(depth)]
    gk = [jnp.tile(row(k_norm_g, l), (1, n_heads)) for l in range(depth)]
    bfp = [row(b_f, l, LANES) for l in range(depth)]
    w32 = [jnp.pad(conv_full[l], ((0, CONV_PAD - CONV_TAPS), (0, 0))) for l in range(depth)]
    add_res = lambda acc, res: (acc + res,)

    h = x[0]
    saved = []
    for l in range(depth):
        u1 = _rms_fwd(h, row(norm1_g, l), f"rms1_fwd_{l}")
        proj = _matmul(u1, w_all[l], mode="nn", out_dtypes=(F32,), name=f"mm_in_{l}", tn=n_all)
        qa, ka, vb = _prep_fwd(proj, gq[l], gk[l], bfp[l], n_heads, f"prep_fwd_{l}")
        att, lse = _attn_fwd(qa, ka, vb, n_heads, f"attn_fwd_{l}")
        yc = _conv_fwd(proj, w32[l], row(conv_b, l), cc, a_col, g_col, f"conv_fwd_{l}")
        hc = _ln_silu_fwd(yc, row(conv_ln_g, l), row(conv_ln_b, l), f"ln_silu_fwd_{l}")
        mixed = jnp.concatenate([att.astype(BF16), hc], axis=1)
        x1 = _matmul(mixed, w_out[l], mode="nn", out_dtypes=(F32,), name=f"mm_o_{l}", epilogue=add_res, extras=(h,))
        u2 = _rms_fwd(x1, row(norm2_g, l), f"rms2_fwd_{l}")
        r, a = _matmul(u2, w_ff1[l], mode="nn", out_dtypes=(BF16, BF16), name=f"mm_ff1_{l}",
                       epilogue=lambda acc: (jnp.maximum(acc, 0.0), jnp.square(jnp.maximum(acc, 0.0))))
        x2 = _matmul(a, w_ff2[l], mode="nn", out_dtypes=(F32,), name=f"mm_ff2_{l}", epilogue=add_res, extras=(x1,))
        saved.append(dict(x_in=h, u1=u1, proj=proj, qa=qa, ka=ka, vb=vb, att=att, lse=lse, yc=yc, mixed=mixed,
                          x1=x1, u2=u2, r=r, a=a))
        h = x2

    dh, sq = _loss_grad(h, loss_target[0], "loss_grad")
    loss = lax.psum(0.5 * jnp.sum(sq) / d_model, ("x", "y", "c"))

    gw_big = [[None] * depth for _ in big]
    gs = {n: [None] * depth for n in ("norm1", "bf", "qn", "kn", "convw", "convb", "lng", "lnb", "norm2")}
    for l in reversed(range(depth)):
        sv = saved[l]
        dh1 = _matmul(dh, w_ff2[l], mode="nt", out_dtypes=(BF16,), name=f"mm_dff2_{l}",
                      epilogue=lambda acc, rr: (acc * (2.0 * rr.astype(F32)),), extras=(sv["r"],))
        gw_big[3][l] = _matmul(sv["a"], dh, mode="tn", out_dtypes=(F32,), name=f"mm_dw2_{l}", tm=1024)
        gw_big[2][l] = _matmul(sv["u2"], dh1, mode="tn", out_dtypes=(F32,), name=f"mm_dw1_{l}", tm=1024)
        du2 = _matmul(dh1, w_ff1[l], mode="nt", out_dtypes=(F32,), name=f"mm_du2_{l}")
        dx1, gs["norm2"][l] = _rms_bwd(sv["x1"], row(norm2_g, l), du2, dh, f"rms2_bwd_{l}")

        dmixed = _matmul(dx1, w_out[l], mode="nt", out_dtypes=(F32,), name=f"mm_dmixed_{l}")
        gw_big[1][l] = _matmul(sv["mixed"], dx1, mode="tn", out_dtypes=(F32,), name=f"mm_dwo_{l}", tm=1024)
        dyc, gs["lng"][l], gs["lnb"][l] = _ln_silu_bwd(sv["yc"], row(conv_ln_g, l), row(conv_ln_b, l), dmixed, f"ln_silu_bwd_{l}")
        dpa, dpg, dw32 = _conv_bwd(sv["proj"], w32[l], dyc, cc, a_col, g_col, f"conv_bwd_{l}")
        gs["convw"][l], gs["convb"][l] = dw32[:CONV_TAPS], dw32[CONV_TAPS:CONV_TAPS + 1]
        dq, dka, dv, drow, dcol = _attn_bwd(sv["qa"], sv["ka"], sv["vb"], sv["att"], sv["lse"], dmixed, n_heads, f"attn_bwd_{l}")
        dpq, dpk, dpf, dgq, dgk, dbf = _prep_bwd(sv["proj"], dq, dka, drow, dcol, gq[l], gk[l], bfp[l], n_heads, f"prep_bwd_{l}")
        gs["qn"][l] = dgq.reshape(n_heads, HEAD_DIM).sum(axis=0)
        gs["kn"][l] = dgk.reshape(n_heads, HEAD_DIM).sum(axis=0)
        gs["bf"][l] = dbf[0, :n_heads]
        dproj = jnp.concatenate([dpq, dpk, dv.astype(BF16), dpa, dpg, dpf], axis=1)
        du1 = _matmul(dproj, w_all[l], mode="nt", out_dtypes=(F32,), name=f"mm_du1_{l}", tk=n_all)
        gw_big[0][l] = from_all(_matmul(sv["u1"], dproj, mode="tn", out_dtypes=(F32,), name=f"mm_dwall_{l}", tm=1024, tn=n_all // 3))
        dh, gs["norm1"][l] = _rms_bwd(sv["x_in"], row(norm1_g, l), du1, dx1, f"rms1_bwd_{l}")
    grad_x = dh[None]

    def slabs(idx, g):
        if idx in (0, 2):
            g = jnp.moveaxis(g.reshape(g.shape[0], N_DEV, -1), 1, 0)
        return g.reshape(N_DEV, rows_of[idx], LANES)

    g_slabs = jnp.concatenate([slabs(idx, gw_big[idx][l]) for idx in range(len(big)) for l in range(depth)], axis=1)
    rows_big = _round_up(g_slabs.shape[1], FLAT_ROWS)
    g_slabs = jnp.pad(g_slabs, ((0, 0), (0, rows_big - g_slabs.shape[1]), (0, 0)))
    recv = _scatter_slabs(g_slabs.astype(BF16), "scatter_grads")
    g_own = lax.dynamic_index_in_dim(g_slabs, me, axis=0, keepdims=False)
    big_out = _adamw(_flatten(big, FLAT_ROWS), _flatten(big_m, FLAT_ROWS), _flatten(big_v, FLAT_ROWS), g_own, recv, "adamw_big")
    big_out = [_unflatten(o, big_shapes) for o in big_out]

    small_g = [jnp.stack(gs[n]).reshape(shp) for n, shp in (
        ("norm1", norm1_g.shape), ("bf", b_f.shape), ("qn", q_norm_g.shape), ("kn", k_norm_g.shape),
        ("convw", (depth, CONV_TAPS, cc)), ("convb", conv_b.shape), ("lng", conv_ln_g.shape), ("lnb", conv_ln_b.shape),
        ("norm2", norm2_g.shape))]
    small_shapes = [g.shape for g in small_g]
    small_g = _unflatten(_all_reduce_small(_flatten(small_g, 8), "all_reduce_small"), small_shapes)
    cw = conv_w.shape[2]
    small_g[4] = lax.dynamic_slice_in_dim(small_g[4], me * cw, cw, axis=2)
    small = (norm1_g, b_f, q_norm_g, k_norm_g, conv_w, conv_b, conv_ln_g, conv_ln_b, norm2_g)
    small_m = (m_norm1_g, m_b_f, m_q_norm_g, m_k_norm_g, m_conv_w, m_conv_b, m_conv_ln_g, m_conv_ln_b, m_norm2_g)
    small_v = (v_norm1_g, v_b_f, v_q_norm_g, v_k_norm_g, v_conv_w, v_conv_b, v_conv_ln_g, v_conv_ln_b, v_norm2_g)
    small_out = _adamw(_flatten(small, 8), _flatten(small_m, 8), _flatten(small_v, 8), _flatten(small_g, 8), None, "adamw_small")
    small_out = [_unflatten(o, [w.shape for w in small]) for o in small_out]

    def group(kind):
        s_, b_ = small_out[kind], big_out[kind]
        return [s_[0], b_[0], s_[1], s_[2], s_[3], s_[4], s_[5], s_[6], s_[7], b_[1], s_[8], b_[2], b_[3]]

    return (loss, grad_x, *group(0), *group(1), *group(2), *group(3))
```

```python
import functools

import jax
import jax.numpy as jnp
from jax import lax
from jax.experimental import pallas as pl
from jax.experimental.pallas import tpu as pltpu

F32 = jnp.float32
BF16 = jnp.bfloat16

EPS = 1e-6
HEAD_DIM = 64
LANES = 128
PAIR = 2 * LANES
N_DEV = 8
CONV_TAPS = 31
CONV_PAD = 32
NEG = -1e30

ADAM_LR = 0.001
ADAM_B1 = 0.9
ADAM_B2 = 0.999
ADAM_EPS = 1e-08
ADAM_WD = 0.01
ADAM_STEP = 10

TM = 512
TQ = 512
CONV_ROWS = 128
FLAT_ROWS = 1024
MESH = pl.DeviceIdType.MESH


def _params(*sem):
    return pltpu.CompilerParams(dimension_semantics=sem, vmem_limit_bytes=56 * 1024 * 1024)


def _split3(x):
    hi = x.astype(BF16)
    r1 = x - hi.astype(F32)
    mid = r1.astype(BF16)
    lo = (r1 - mid.astype(F32)).astype(BF16)
    return hi, mid, lo


def _dot(a, b):
    return jnp.dot(a, b, preferred_element_type=F32)


def _dot_nt(a, b):
    return lax.dot_general(a, b, (((1,), (1,)), ((), ())), preferred_element_type=F32)


def _dot_tn(a, b):
    return lax.dot_general(a, b, (((0,), (0,)), ((), ())), preferred_element_type=F32)


def _dot3(x, mat):
    hi, mid, lo = _split3(x)
    return _dot(hi, mat) + _dot(mid, mat) + _dot(lo, mat)


def _dot3_r(mat, x):
    hi, mid, lo = _split3(x)
    return _dot(mat, hi) + _dot(mat, mid) + _dot(mat, lo)


def _iota(shape, dim):
    return lax.broadcasted_iota(jnp.int32, shape, dim)


def _sigmoid(x):
    return 1.0 / (1.0 + jnp.exp(-x))


def _matmul(a, b, *, mode, out_dtypes, name, epilogue=None, extras=(), tm=TM, tn=1024, tk=1024):
    if mode == "nn":
        (m, k), (k2, n) = a.shape, b.shape
    elif mode == "nt":
        (m, k), (n, k2) = a.shape, b.shape
    else:
        (k, m), (k2, n) = a.shape, b.shape
    assert k == k2, (name, a.shape, b.shape)
    tm, tn, tk = min(tm, m), min(tn, n), min(tk, k)
    assert m % tm == 0 and n % tn == 0 and k % tk == 0, (name, m, n, k, tm, tn, tk)
    nk = k // tk
    if mode == "tn":
        a_spec = pl.BlockSpec((tk, tm), lambda i, j, kk: (kk, i))
    else:
        a_spec = pl.BlockSpec((tm, tk), lambda i, j, kk: (i, kk))
    b_mode = dict(pipeline_mode=pl.Buffered(1)) if (n == tn and nk == 1) else {}
    if mode == "nt":
        b_spec = pl.BlockSpec((tn, tk), lambda i, j, kk: (j, kk), **b_mode)
    else:
        b_spec = pl.BlockSpec((tk, tn), lambda i, j, kk: (kk, j), **b_mode)
    dot = {"nn": _dot, "nt": _dot_nt, "tn": _dot_tn}[mode]
    tile = pl.BlockSpec((tm, tn), lambda i, j, kk: (i, j))
    n_ex, n_out = len(extras), len(out_dtypes)
    acc_in_out = nk > 1 and epilogue is None and out_dtypes[0] == F32

    def body(a_ref, b_ref, *rest):
        ex_refs, out_refs = rest[:n_ex], rest[n_ex:n_ex + n_out]
        part = dot(a_ref[...].astype(BF16), b_ref[...].astype(BF16))

        def finish(acc):
            res = epilogue(acc, *[e[...] for e in ex_refs]) if epilogue is not None else (acc,) * n_out
            for o_ref, r in zip(out_refs, res):
                o_ref[...] = r.astype(o_ref.dtype)

        if nk == 1:
            finish(part)
        else:
            acc_ref = out_refs[0] if acc_in_out else rest[-1]
            kk = pl.program_id(2)

            @pl.when(kk == 0)
            def _():
                acc_ref[...] = part

            @pl.when(kk > 0)
            def _():
                acc_ref[...] += part

            @pl.when(kk == nk - 1)
            def _():
                if acc_in_out:
                    for o_ref in out_refs[1:]:
                        o_ref[...] = acc_ref[...].astype(o_ref.dtype)
                else:
                    finish(acc_ref[...])

    outs = pl.pallas_call(
        body,
        name=name,
        grid=(m // tm, n // tn, nk),
        in_specs=[a_spec, b_spec] + [tile] * n_ex,
        out_specs=[tile] * n_out,
        out_shape=[jax.ShapeDtypeStruct((m, n), dt) for dt in out_dtypes],
        scratch_shapes=[pltpu.VMEM((tm, tn), F32)] if nk > 1 and not acc_in_out else [],
        compiler_params=_params("parallel", "parallel", "arbitrary"),
    )(a, b, *extras)
    return outs if n_out > 1 else outs[0]


def _rms_fwd(x, g, name):
    s, d = x.shape
    ts = min(TM, s)

    def body(x_ref, g_ref, u_ref):
        xv = x_ref[...]
        y = xv * lax.rsqrt(jnp.mean(xv * xv, axis=-1, keepdims=True) + EPS)
        u_ref[...] = (y * g_ref[...]).astype(BF16)

    return pl.pallas_call(
        body, name=name, grid=(s // ts,),
        in_specs=[pl.BlockSpec((ts, d), lambda i: (i, 0)), pl.BlockSpec((1, d), lambda i: (0, 0))],
        out_specs=pl.BlockSpec((ts, d), lambda i: (i, 0)),
        out_shape=jax.ShapeDtypeStruct((s, d), BF16),
        compiler_params=_params("parallel"),
    )(x, g)


def _rms_bwd(x, g, du, dres, name):
    s, d = x.shape
    ts = min(TM, s)

    def body(x_ref, g_ref, du_ref, dres_ref, dx_ref, dx16_ref, dg_ref):
        @pl.when(pl.program_id(0) == 0)
        def _():
            dg_ref[...] = jnp.zeros_like(dg_ref)

        xv, duv = x_ref[...], du_ref[...]
        r = lax.rsqrt(jnp.mean(xv * xv, axis=-1, keepdims=True) + EPS)
        xh = xv * r
        dxh = duv * g_ref[...]
        dx = dres_ref[...] + r * (dxh - xh * jnp.mean(dxh * xh, axis=-1, keepdims=True))
        dx_ref[...] = dx
        dx16_ref[...] = dx.astype(BF16)
        dg_ref[...] += jnp.sum(duv * xh, axis=0, keepdims=True)

    row = pl.BlockSpec((ts, d), lambda i: (i, 0))
    vec = pl.BlockSpec((1, d), lambda i: (0, 0))
    return pl.pallas_call(
        body, name=name, grid=(s // ts,),
        in_specs=[row, vec, row, row], out_specs=[row, row, vec],
        out_shape=[jax.ShapeDtypeStruct((s, d), F32), jax.ShapeDtypeStruct((s, d), BF16), jax.ShapeDtypeStruct((1, d), F32)],
        compiler_params=_params("arbitrary"),
    )(x, g, du, dres)


def _ln_silu_fwd(y, g, b, name):
    s, c = y.shape
    ts = min(TM, s)

    def body(y_ref, g_ref, b_ref, h_ref):
        yv = y_ref[...]
        mu = jnp.mean(yv, axis=-1, keepdims=True)
        yc = yv - mu
        z = yc * lax.rsqrt(jnp.mean(yc * yc, axis=-1, keepdims=True) + EPS) * g_ref[...] + b_ref[...]
        h_ref[...] = (z * _sigmoid(z)).astype(BF16)

    row = pl.BlockSpec((ts, c), lambda i: (i, 0))
    vec = pl.BlockSpec((1, c), lambda i: (0, 0))
    return pl.pallas_call(
        body, name=name, grid=(s // ts,), in_specs=[row, vec, vec], out_specs=row,
        out_shape=jax.ShapeDtypeStruct((s, c), BF16), compiler_params=_params("parallel"),
    )(y, g, b)


def _ln_silu_bwd(y, g, b, dmixed, name):
    s, c = y.shape
    ts = min(TM, s)

    def body(y_ref, g_ref, b_ref, dh_ref, dy_ref, dg_ref, db_ref):
        @pl.when(pl.program_id(0) == 0)
        def _():
            dg_ref[...] = jnp.zeros_like(dg_ref)
            db_ref[...] = jnp.zeros_like(db_ref)

        yv = y_ref[...]
        mu = jnp.mean(yv, axis=-1, keepdims=True)
        yc = yv - mu
        r = lax.rsqrt(jnp.mean(yc * yc, axis=-1, keepdims=True) + EPS)
        yh = yc * r
        z = yh * g_ref[...] + b_ref[...]
        sg = _sigmoid(z)
        dz = dh_ref[...] * (sg * (1.0 + z * (1.0 - sg)))
        dg_ref[...] += jnp.sum(dz * yh, axis=0, keepdims=True)
        db_ref[...] += jnp.sum(dz, axis=0, keepdims=True)
        dyh = dz * g_ref[...]
        dy_ref[...] = r * (dyh - jnp.mean(dyh, axis=-1, keepdims=True) - yh * jnp.mean(dyh * yh, axis=-1, keepdims=True))

    row = pl.BlockSpec((ts, c), lambda i: (i, 0))
    vec = pl.BlockSpec((1, c), lambda i: (0, 0))
    return pl.pallas_call(
        body, name=name, grid=(s // ts,),
        in_specs=[row, vec, vec, pl.BlockSpec((ts, c), lambda i: (i, 1))], out_specs=[row, vec, vec],
        out_shape=[jax.ShapeDtypeStruct((s, c), F32), jax.ShapeDtypeStruct((1, c), F32), jax.ShapeDtypeStruct((1, c), F32)],
        compiler_params=_params("arbitrary"),
    )(y, g, b, dmixed)


def _loss_grad(y, target, name):
    s, d = y.shape
    ts = min(TM, s)

    def body(y_ref, t_ref, dy_ref, dy16_ref, sq_ref):
        @pl.when(pl.program_id(0) == 0)
        def _():
            sq_ref[...] = jnp.zeros_like(sq_ref)

        err = y_ref[...] - t_ref[...]
        dy = err * (1.0 / d)
        dy_ref[...] = dy
        dy16_ref[...] = dy.astype(BF16)
        sq_ref[...] += jnp.sum(err * err, axis=0, keepdims=True)

    row = pl.BlockSpec((ts, d), lambda i: (i, 0))
    vec = pl.BlockSpec((1, d), lambda i: (0, 0))
    return pl.pallas_call(
        body, name=name, grid=(s // ts,), in_specs=[row, row], out_specs=[row, row, vec],
        out_shape=[jax.ShapeDtypeStruct((s, d), F32), jax.ShapeDtypeStruct((s, d), BF16), jax.ShapeDtypeStruct((1, d), F32)],
        compiler_params=_params("arbitrary"),
    )(y, target)


def _head_masks():
    lane2 = _iota((1, PAIR), 1)
    lane1 = _iota((1, LANES), 1)
    qa = (lane2 < HEAD_DIM) | ((lane2 >= LANES) & (lane2 < LANES + 3))
    qb = ((lane2 >= HEAD_DIM) & (lane2 < LANES)) | ((lane2 >= LANES + 3) & (lane2 < LANES + 6))
    return (qa, qb), (lane1 < HEAD_DIM, lane1 >= HEAD_DIM)


def _group_matrix(width):
    shift = HEAD_DIM.bit_length() - 1
    return ((_iota((width, width), 0) >> shift) == (_iota((width, width), 1) >> shift)).astype(BF16)


def _prep_fwd(proj, gq, gk, bf, n_heads, name):
    s = proj.shape[0]
    aw = n_heads * HEAD_DIM
    n_pairs = n_heads // 2
    ts = min(TM, s)
    f_col = (proj.shape[1] - LANES) // LANES

    def body(q_ref, k_ref, v_ref, f_ref, gq_ref, gk_ref, bf_ref, qa_ref, ka_ref, vb_ref, carry_ref):
        @pl.when(pl.program_id(0) == 0)
        def _():
            carry_ref[...] = jnp.zeros_like(carry_ref)

        gmat = _group_matrix(aw)

        def head_norm(xv, g):
            ms = _dot3(xv * xv, gmat) * (1.0 / HEAD_DIM)
            return xv * lax.rsqrt(ms + EPS) * g

        qn = head_norm(q_ref[...], gq_ref[...]) * (HEAD_DIM ** -0.5)
        kn = head_norm(k_ref[...], gk_ref[...])
        z = f_ref[...] + bf_ref[...]
        logf = jnp.minimum(z, 0.0) - jnp.log(1.0 + jnp.exp(-jnp.abs(z)))
        tri = (_iota((ts, ts), 0) >= _iota((ts, ts), 1)).astype(BF16)
        c = _dot3_r(tri, logf) + carry_ref[...]
        carry_ref[...] = c[ts - 1:ts, :]
        terms = _split3(-c)
        row, col = _iota((LANES, LANES), 0), _iota((LANES, LANES), 1)
        ones = jnp.where(_iota((ts, LANES), 1) < 6, 1.0, 0.0).astype(BF16)
        for p in range(n_pairs):
            extra = jnp.zeros((ts, LANES), F32)
            for t, term in enumerate(terms):
                sel = ((row == 2 * p) & (col == t)) | ((row == 2 * p + 1) & (col == 3 + t))
                extra += _dot(term, sel.astype(BF16))
            lo, hi = p * PAIR, p * PAIR + LANES
            ka_ref[:, lo:hi] = kn[:, p * LANES:(p + 1) * LANES].astype(BF16)
            ka_ref[:, hi:hi + LANES] = extra.astype(BF16)
            qa_ref[:, lo:hi] = qn[:, p * LANES:(p + 1) * LANES].astype(BF16)
            qa_ref[:, hi:hi + LANES] = ones
        vb_ref[...] = v_ref[...].astype(BF16)

    blk = lambda j: pl.BlockSpec((ts, aw), lambda i: (i, j))
    vec = lambda w: pl.BlockSpec((1, w), lambda i: (0, 0))
    return pl.pallas_call(
        body, name=name, grid=(s // ts,),
        in_specs=[blk(0), blk(1), blk(2), pl.BlockSpec((ts, LANES), lambda i: (i, f_col)), vec(aw), vec(aw), vec(LANES)],
        out_specs=[pl.BlockSpec((ts, n_pairs * PAIR), lambda i: (i, 0)), pl.BlockSpec((ts, n_pairs * PAIR), lambda i: (i, 0)), blk(0)],
        out_shape=[jax.ShapeDtypeStruct((s, n_pairs * PAIR), BF16), jax.ShapeDtypeStruct((s, n_pairs * PAIR), BF16),
                   jax.ShapeDtypeStruct((s, aw), BF16)],
        scratch_shapes=[pltpu.VMEM((1, LANES), F32)],
        compiler_params=_params("arbitrary"),
    )(proj, proj, proj, proj, gq, gk, bf)


def _prep_bwd(proj, dq, dka, drow, dcol, gq, gk, bf, n_heads, name):
    s = proj.shape[0]
    aw = n_heads * HEAD_DIM
    n_pairs = n_heads // 2
    ts = min(TM, s)
    nt = s // ts
    f_col = (proj.shape[1] - LANES) // LANES
    shift = HEAD_DIM.bit_length() - 1

    def body(q_ref, k_ref, f_ref, dq_ref, dka_ref, drow_ref, dcol_ref, gq_ref, gk_ref, bf_ref,
             dpq_ref, dpk_ref, dpf_ref, dgq_ref, dgk_ref, dbf_ref, carry_ref):
        @pl.when(pl.program_id(0) == 0)
        def _():
            carry_ref[...] = jnp.zeros_like(carry_ref)
            dgq_ref[...] = jnp.zeros_like(dgq_ref)
            dgk_ref[...] = jnp.zeros_like(dgk_ref)
            dbf_ref[...] = jnp.zeros_like(dbf_ref)

        gmat = _group_matrix(aw)

        def head_norm_bwd(xv, g, dn):
            r = lax.rsqrt(_dot3(xv * xv, gmat) * (1.0 / HEAD_DIM) + EPS)
            xh = xv * r
            dxh = dn * g
            dx = r * (dxh - xh * (_dot3(dxh * xh, gmat) * (1.0 / HEAD_DIM)))
            return dx, jnp.sum(dn * xh, axis=0, keepdims=True)

        dkav = dka_ref[...]
        dx, dg = head_norm_bwd(q_ref[...], gq_ref[...], dq_ref[...] * (HEAD_DIM ** -0.5))
        dpq_ref[...] = dx.astype(BF16)
        dgq_ref[...] += dg
        dkn = jnp.concatenate([dkav[:, p * PAIR:p * PAIR + LANES] for p in range(n_pairs)], axis=1)
        dx, dg = head_norm_bwd(k_ref[...], gk_ref[...], dkn)
        dpk_ref[...] = dx.astype(BF16)
        dgk_ref[...] += dg

        pick = (_iota((aw, LANES), 0) == (_iota((aw, LANES), 1) << shift)).astype(BF16)
        dc = _dot3(drow_ref[...], pick)
        r16, c16 = _iota((16, LANES), 0), _iota((16, LANES), 1)
        for p in range(n_pairs):
            place = ((r16 < 2) & (c16 == 2 * p + r16)).astype(BF16)
            for term in _split3(dcol_ref[p]):
                dc -= _dot_tn(term, place)
        triu = (_iota((ts, ts), 0) <= _iota((ts, ts), 1)).astype(BF16)
        dlogf = _dot3_r(triu, dc) + carry_ref[...]
        carry_ref[...] = dlogf[0:1, :]
        z = f_ref[...] + bf_ref[...]
        dz = dlogf * (1.0 / (1.0 + jnp.exp(z)))
        dpf_ref[...] = dz.astype(BF16)
        dbf_ref[...] += jnp.sum(dz, axis=0, keepdims=True)

    rev = lambda w, j: pl.BlockSpec((ts, w), lambda i: (nt - 1 - i, j))
    vec = lambda w: pl.BlockSpec((1, w), lambda i: (0, 0))
    return pl.pallas_call(
        body, name=name, grid=(nt,),
        in_specs=[rev(aw, 0), rev(aw, 1), rev(LANES, f_col), rev(aw, 0), rev(n_pairs * PAIR, 0), rev(aw, 0),
                  pl.BlockSpec((n_pairs, 16, ts), lambda i: (0, 0, nt - 1 - i)), vec(aw), vec(aw), vec(LANES)],
        out_specs=[rev(aw, 0), rev(aw, 0), rev(LANES, 0), vec(aw), vec(aw), vec(LANES)],
        out_shape=[jax.ShapeDtypeStruct((s, aw), BF16), jax.ShapeDtypeStruct((s, aw), BF16), jax.ShapeDtypeStruct((s, LANES), BF16),
                   jax.ShapeDtypeStruct((1, aw), F32), jax.ShapeDtypeStruct((1, aw), F32), jax.ShapeDtypeStruct((1, LANES), F32)],
        scratch_shapes=[pltpu.VMEM((1, LANES), F32)],
        compiler_params=_params("arbitrary"),
    )(proj, proj, proj, dq, dka, drow, dcol, gq, gk, bf)


def _attn_fwd(qa, ka, vb, n_heads, name):
    s = qa.shape[0]
    aw = n_heads * HEAD_DIM
    n_pairs = n_heads // 2
    tq = min(TQ, s)

    def body(q_ref, k_ref, v_ref, o_ref, lse_ref):
        i = pl.program_id(1)
        qmasks, omasks = _head_masks()
        qv = q_ref[...]
        causal = _iota((tq, tq), 1) <= _iota((tq, tq), 0)
        res = []
        for h in range(2):
            qh = jnp.where(qmasks[h], qv, jnp.zeros_like(qv))

            def step(j, carry, masked, qh=qh):
                m, l, acc = carry
                off = pl.multiple_of(j * tq, tq)
                sc = _dot_nt(qh, k_ref[pl.ds(off, tq), :])
                if masked:
                    sc = jnp.where(causal, sc, NEG)
                m_new = jnp.maximum(m, jnp.max(sc, axis=1, keepdims=True))
                p = jnp.exp(sc - m_new)
                alpha = jnp.exp(m - m_new)
                l = alpha * l + jnp.sum(p, axis=1, keepdims=True)
                acc = alpha * acc + _dot(p.astype(BF16), v_ref[pl.ds(off, tq), :])
                return m_new, l, acc

            init = (jnp.full((tq, 1), NEG, F32), jnp.zeros((tq, 1), F32), jnp.zeros((tq, LANES), F32))
            carry = lax.fori_loop(0, i, functools.partial(step, masked=False), init)
            m, l, acc = step(i, carry, True)
            res.append((acc * (1.0 / l), m + jnp.log(l)))
        o_ref[...] = jnp.where(omasks[0], res[0][0], res[1][0])
        lse_ref[...] = jnp.where(omasks[0], res[0][1], res[1][1])

    return pl.pallas_call(
        body, name=name, grid=(n_pairs, s // tq),
        in_specs=[pl.BlockSpec((tq, PAIR), lambda p, i: (i, p)), pl.BlockSpec((s, PAIR), lambda p, i: (0, p)),
                  pl.BlockSpec((s, LANES), lambda p, i: (0, p))],
        out_specs=[pl.BlockSpec((tq, LANES), lambda p, i: (i, p))] * 2,
        out_shape=[jax.ShapeDtypeStruct((s, aw), F32)] * 2,
        compiler_params=_params("parallel", "parallel"),
    )(qa, ka, vb)


def _attn_bwd(qa, ka, vb, o, lse, dmixed, n_heads, name):
    s = qa.shape[0]
    aw = n_heads * HEAD_DIM
    n_pairs = n_heads // 2
    tq = min(TQ, s)
    nq = s // tq

    def body(q_ref, k_ref, v_ref, o_ref, lse_ref, do_ref, dq_ref, dka_ref, dv_ref, drow_ref, dcol_ref, delta_ref):
        j = pl.program_id(1)
        qmasks, omasks = _head_masks()

        @pl.when(j == 0)
        def _():
            dq_ref[...] = jnp.zeros_like(dq_ref)
            drow_ref[...] = jnp.zeros_like(drow_ref)
            for c in range(nq):
                rows = slice(c * tq, (c + 1) * tq)
                prod = do_ref[rows, :] * o_ref[rows, :]
                da = jnp.sum(jnp.where(omasks[0], prod, 0.0), axis=1, keepdims=True)
                db = jnp.sum(jnp.where(omasks[1], prod, 0.0), axis=1, keepdims=True)
                delta_ref[rows, :] = jnp.where(omasks[0], da, db)

        dka_ref[...] = jnp.zeros_like(dka_ref)
        dv_ref[...] = jnp.zeros_like(dv_ref)
        dcol_ref[...] = jnp.zeros_like(dcol_ref)
        kv = k_ref[...]
        kk = kv[:, :LANES]
        vv = v_ref[...]
        causal = _iota((tq, tq), 1) <= _iota((tq, tq), 0)

        def step(i, masked):
            off = pl.multiple_of(i * tq, tq)
            qv = q_ref[pl.ds(off, tq), :]
            dov = do_ref[pl.ds(off, tq), :]
            lsev = lse_ref[pl.ds(off, tq), :]
            dlv = delta_ref[pl.ds(off, tq), :]
            for h in range(2):
                qh = jnp.where(qmasks[h], qv, jnp.zeros_like(qv))
                doh = jnp.where(omasks[h], dov, 0.0).astype(BF16)
                lane = h * HEAD_DIM
                sc = _dot_nt(qh, kv)
                if masked:
                    sc = jnp.where(causal, sc, NEG)
                p = jnp.exp(sc - lsev[:, lane:lane + 1])
                dv_ref[...] += _dot_tn(p.astype(BF16), doh)
                dp = _dot_nt(doh, vv)
                dsf = p * (dp - dlv[:, lane:lane + 1])
                drow_ref[pl.ds(off, tq), :] += jnp.where(omasks[h], jnp.sum(dsf, axis=1, keepdims=True), 0.0)
                dcol_ref[0, h:h + 1, :] += jnp.sum(dsf, axis=0, keepdims=True)
                ds = dsf.astype(BF16)
                dka_ref[...] += _dot_tn(ds, qh)
                dq_ref[pl.ds(off, tq), :] += jnp.where(omasks[h], _dot(ds, kk), 0.0)

        step(j, True)

        def loop_body(i, carry):
            step(i, False)
            return carry

        lax.fori_loop(j + 1, nq, loop_body, 0)

    full = lambda w: pl.BlockSpec((s, w), lambda p, j: (0, p))
    blk = lambda w: pl.BlockSpec((tq, w), lambda p, j: (j, p))
    return pl.pallas_call(
        body, name=name, grid=(n_pairs, nq),
        in_specs=[full(PAIR), blk(PAIR), blk(LANES), full(LANES), full(LANES), full(LANES)],
        out_specs=[full(LANES), blk(PAIR), blk(LANES), full(LANES), pl.BlockSpec((1, 16, tq), lambda p, j: (p, 0, j))],
        out_shape=[jax.ShapeDtypeStruct((s, aw), F32), jax.ShapeDtypeStruct((s, n_pairs * PAIR), F32),
                   jax.ShapeDtypeStruct((s, aw), F32), jax.ShapeDtypeStruct((s, aw), F32),
                   jax.ShapeDtypeStruct((n_pairs, 16, s), F32)],
        scratch_shapes=[pltpu.VMEM((s, LANES), F32)],
        compiler_params=_params("parallel", "arbitrary"),
    )(qa, ka, vb, o, lse, dmixed)


def _conv_fwd(proj, w32, bias, n_ch, a_col, g_col, name):
    s = proj.shape[0]
    rows = min(CONV_ROWS, s)

    def body(a_ref, g_ref, w_ref, b_ref, y_ref, pad_ref):
        pad_ref[0:CONV_PAD, :] = jnp.zeros((CONV_PAD, LANES), F32)
        pad_ref[CONV_PAD:CONV_PAD + s, :] = a_ref[...] * _sigmoid(g_ref[...])
        wv = w_ref[...]
        for c in range(s // rows):
            acc = jnp.broadcast_to(b_ref[...], (rows, LANES))
            for t in range(CONV_TAPS):
                start = c * rows + CONV_PAD - (CONV_TAPS - 1) + t
                acc = acc + wv[t:t + 1, :] * pad_ref[start:start + rows, :]
            y_ref[c * rows:(c + 1) * rows, :] = acc

    col = lambda j0: pl.BlockSpec((s, LANES), lambda c: (0, j0 + c))
    return pl.pallas_call(
        body, name=name, grid=(n_ch // LANES,),
        in_specs=[col(a_col), col(g_col), pl.BlockSpec((CONV_PAD, LANES), lambda c: (0, c)), pl.BlockSpec((1, LANES), lambda c: (0, c))],
        out_specs=pl.BlockSpec((s, LANES), lambda c: (0, c)),
        out_shape=jax.ShapeDtypeStruct((s, n_ch), F32),
        scratch_shapes=[pltpu.VMEM((s + CONV_PAD, LANES), F32)],
        compiler_params=_params("parallel"),
    )(proj, proj, w32, bias)


def _conv_bwd(proj, w32, dy, n_ch, a_col, g_col, name):
    s = proj.shape[0]
    rows = min(CONV_ROWS, s)
    sub = 8

    def fold(x):
        acc = x[0:sub, :]
        for r in range(1, rows // sub):
            acc = acc + x[r * sub:(r + 1) * sub, :]
        return acc

    def body(a_ref, g_ref, w_ref, dy_ref, da_ref, dg_ref, dw_ref, padh_ref, padd_ref):
        sg = _sigmoid(g_ref[...])
        padh_ref[0:CONV_PAD, :] = jnp.zeros((CONV_PAD, LANES), F32)
        padh_ref[CONV_PAD:CONV_PAD + s, :] = a_ref[...] * sg
        padd_ref[0:s, :] = dy_ref[...]
        padd_ref[s:s + CONV_PAD, :] = jnp.zeros((CONV_PAD, LANES), F32)
        wv = w_ref[...]
        dw = [jnp.zeros((sub, LANES), F32) for _ in range(CONV_TAPS + 1)]
        for c in range(s // rows):
            r0 = c * rows
            acc = jnp.zeros((rows, LANES), F32)
            dyc = dy_ref[r0:r0 + rows, :]
            for t in range(CONV_TAPS):
                back = r0 + (CONV_TAPS - 1) - t
                acc = acc + wv[t:t + 1, :] * padd_ref[back:back + rows, :]
                start = r0 + CONV_PAD - (CONV_TAPS - 1) + t
                dw[t] = dw[t] + fold(dyc * padh_ref[start:start + rows, :])
            dw[CONV_TAPS] = dw[CONV_TAPS] + fold(dyc)
            av = a_ref[r0:r0 + rows, :]
            sgc = _sigmoid(g_ref[r0:r0 + rows, :])
            da_ref[r0:r0 + rows, :] = (acc * sgc).astype(BF16)
            dg_ref[r0:r0 + rows, :] = (acc * av * sgc * (1.0 - sgc)).astype(BF16)
        for t in range(CONV_TAPS + 1):
            dw_ref[t:t + 1, :] = jnp.sum(dw[t], axis=0, keepdims=True)

    col = lambda j0: pl.BlockSpec((s, LANES), lambda c: (0, j0 + c))
    wspec = pl.BlockSpec((CONV_PAD, LANES), lambda c: (0, c))
    return pl.pallas_call(
        body, name=name, grid=(n_ch // LANES,),
        in_specs=[col(a_col), col(g_col), wspec, col(0)],
        out_specs=[col(0), col(0), wspec],
        out_shape=[jax.ShapeDtypeStruct((s, n_ch), BF16), jax.ShapeDtypeStruct((s, n_ch), BF16),
                   jax.ShapeDtypeStruct((CONV_PAD, n_ch), F32)],
        scratch_shapes=[pltpu.VMEM((s + CONV_PAD, LANES), F32), pltpu.VMEM((s + CONV_PAD, LANES), F32)],
        compiler_params=_params("parallel"),
    )(proj, proj, w32, dy)


def _my_place():
    return lax.axis_index("x"), lax.axis_index("y"), lax.axis_index("c")


def _flip(place, k):
    x, y, c = place
    return (1 - x if k & 4 else x, 1 - y if k & 2 else y, 1 - c if k & 1 else c)


def _dev_id(place):
    return 4 * place[0] + 2 * place[1] + place[2]


def _wait_all(ref, send_sem, recv_sem, place):
    pltpu.make_async_remote_copy(src_ref=ref, dst_ref=ref, send_sem=send_sem, recv_sem=recv_sem,
                                 device_id=place, device_id_type=MESH).wait()


def _gather_weights(win, wo, w1, w2, cv, name):
    depth, d, _ = win.shape
    r, f, f2 = wo.shape[1], w1.shape[2], w2.shape[1]
    lands = [(N_DEV,) + win.shape, (depth, N_DEV * r, d), (depth, d, N_DEV * f), (depth, N_DEV * f2, d), (N_DEV,) + cv.shape]
    n_units = 4 * depth + 1

    def body(win_ref, wo_ref, w1_ref, w2_ref, cv_ref, lin, lo, l1, l2, lc, send_sems, recv_sems, local_sems):
        place = _my_place()
        me = _dev_id(place)

        def rows(n):
            return pl.ds(pl.multiple_of(me * n, n), n)

        units = []
        for l in range(depth):
            units.append((win_ref.at[l], lin.at[me, l], lin.at[pl.ds(0, N_DEV - 1), l]))
            if l == 0:
                units.append((cv_ref, lc.at[me], lc.at[pl.ds(0, N_DEV - 1)]))
            units.append((wo_ref.at[l], lo.at[l, rows(r), :], lo.at[l, pl.ds(0, (N_DEV - 1) * r), :]))
            units.append((w1_ref.at[l], l1.at[l, :, rows(f)], l1.at[l, :, pl.ds(0, (N_DEV - 1) * f)]))
            units.append((w2_ref.at[l], l2.at[l, rows(f2), :], l2.at[l, pl.ds(0, (N_DEV - 1) * f2), :]))
        local = []
        for u, (src, dst, _) in enumerate(units):
            local.append(pltpu.make_async_copy(src, dst, local_sems.at[u]))
            local[-1].start()
            for k in range(1, N_DEV):
                pltpu.make_async_remote_copy(src_ref=src, dst_ref=dst, send_sem=send_sems.at[u], recv_sem=recv_sems.at[u],
                                             device_id=_flip(place, k), device_id_type=MESH).start()
        for u, (_, _, seven) in enumerate(units):
            _wait_all(seven, send_sems.at[u], recv_sems.at[u], place)
            local[u].wait()

    hbm = pl.BlockSpec(memory_space=pl.ANY)
    return pl.pallas_call(
        body, name=name,
        out_shape=[jax.ShapeDtypeStruct(shp, BF16) for shp in lands],
        in_specs=[hbm] * 5, out_specs=[hbm] * 5,
        scratch_shapes=[pltpu.SemaphoreType.DMA((n_units,)), pltpu.SemaphoreType.DMA((n_units,)), pltpu.SemaphoreType.DMA((n_units,))],
    )(win, wo, w1, w2, cv)


def _scatter_grads(g_in, g_o, g_1, g_2, name):
    depth = len(g_in)
    d, nl = g_in[0].shape[1:]
    r, f, f2 = g_o[0][0].shape[0] // N_DEV, g_1[0][0].shape[1] // N_DEV, g_2[0][0].shape[0] // N_DEV
    slab = {"in": (d, nl), "o": (r, d), "1": (d, f), "2": (f2, d)}
    kinds = ("in", "o", "1", "2")
    n_units = len(kinds) * depth
    n_own = 3 * depth

    def body(*refs):
        ins, outs = refs[:7 * depth], refs[7 * depth:7 * depth + n_units + n_own]
        send_sems, recv_sems, local_sems = refs[-3:]
        place = _my_place()
        me = _dev_id(place)
        recv = {(kd, l): outs[i * depth + l] for i, kd in enumerate(kinds) for l in range(depth)}
        own = {(kd, l): outs[n_units + i * depth + l] for i, kd in enumerate(kinds[1:]) for l in range(depth)}
        src16 = {("in", l): ins[l] for l in range(depth)}
        src32 = {}
        for i, kd in enumerate(kinds[1:]):
            for l in range(depth):
                src32[(kd, l)] = ins[depth + (2 * i) * depth + l]
                src16[(kd, l)] = ins[depth + (2 * i + 1) * depth + l]

        def window(kd, ref, dev):
            if kd == "in":
                return ref.at[dev]
            if kd == "o":
                return ref.at[pl.ds(pl.multiple_of(dev * r, r), r), :]
            if kd == "1":
                return ref.at[:, pl.ds(pl.multiple_of(dev * f, f), f)]
            return ref.at[pl.ds(pl.multiple_of(dev * f2, f2), f2), :]

        local = []
        for l in reversed(range(depth)):
            for kd in kinds:
                u = kinds.index(kd) * depth + l
                if kd != "in":
                    local.append(pltpu.make_async_copy(window(kd, src32[(kd, l)], me), own[(kd, l)], local_sems.at[u]))
                    local[-1].start()
                for k in range(1, N_DEV):
                    peer = _flip(place, k)
                    pltpu.make_async_remote_copy(
                        src_ref=window(kd, src16[(kd, l)], _dev_id(peer)), dst_ref=recv[(kd, l)].at[k - 1],
                        send_sem=send_sems.at[u], recv_sem=recv_sems.at[u], device_id=peer, device_id_type=MESH).start()
        for u in range(n_units):
            kd, l = kinds[u // depth], u % depth
            _wait_all(recv[(kd, l)], send_sems.at[u], recv_sems.at[u], place)
        for cp in local:
            cp.wait()

    hbm = pl.BlockSpec(memory_space=pl.ANY)
    args = list(g_in)
    for g in (g_o, g_1, g_2):
        args += [g[l][0] for l in range(depth)] + [g[l][1] for l in range(depth)]
    out_shape = [jax.ShapeDtypeStruct((N_DEV - 1,) + slab[kd], BF16) for kd in kinds for _ in range(depth)]
    out_shape += [jax.ShapeDtypeStruct(slab[kd], F32) for kd in kinds[1:] for _ in range(depth)]
    outs = pl.pallas_call(
        body, name=name, out_shape=out_shape,
        in_specs=[hbm] * len(args), out_specs=[hbm] * len(out_shape),
        scratch_shapes=[pltpu.SemaphoreType.DMA((n_units,)), pltpu.SemaphoreType.DMA((n_units,)), pltpu.SemaphoreType.DMA((n_units,))],
    )(*args)
    recv = {(kd, l): outs[i * depth + l] for i, kd in enumerate(kinds) for l in range(depth)}
    own = {(kd, l): outs[n_units + i * depth + l] for i, kd in enumerate(kinds[1:]) for l in range(depth)}
    return recv, own


def _all_reduce_small(g, name):
    r, w = g.shape

    def body(g_ref, out_ref, buf_ref, send_sems, recv_sems):
        place = _my_place()
        me = 4 * place[0] + 2 * place[1] + place[2]
        buf_ref[me] = g_ref[...]
        copies = []
        for k in range(1, N_DEV):
            copies.append(pltpu.make_async_remote_copy(
                src_ref=g_ref, dst_ref=buf_ref.at[me],
                send_sem=send_sems.at[k - 1], recv_sem=recv_sems.at[k - 1], device_id=_flip(place, k), device_id_type=MESH))
        for cp in copies:
            cp.start()
        for cp in copies:
            cp.wait()
        acc = buf_ref[0]
        for d in range(1, N_DEV):
            acc = acc + buf_ref[d]
        out_ref[...] = acc

    return pl.pallas_call(
        body, name=name,
        out_shape=jax.ShapeDtypeStruct((r, w), F32),
        in_specs=[pl.BlockSpec(memory_space=pltpu.VMEM)], out_specs=pl.BlockSpec(memory_space=pltpu.VMEM),
        scratch_shapes=[pltpu.VMEM((N_DEV, r, w), F32), pltpu.SemaphoreType.DMA((7,)), pltpu.SemaphoreType.DMA((7,))],
    )(g)


def _adamw_math(w, m, v, g):
    m_new = ADAM_B1 * m + (1.0 - ADAM_B1) * g
    v_new = ADAM_B2 * v + (1.0 - ADAM_B2) * (g * g)
    m_hat = m_new / (1.0 - ADAM_B1 ** ADAM_STEP)
    v_hat = v_new / (1.0 - ADAM_B2 ** ADAM_STEP)
    return -ADAM_LR * (m_hat / (jnp.sqrt(v_hat) + ADAM_EPS) + ADAM_WD * w), m_new, v_new


def _adamw(w, m, v, g, name):
    rows = w.shape[0]
    tr = min(FLAT_ROWS, rows)
    assert rows % tr == 0, (name, rows)

    def body(w_ref, m_ref, v_ref, g_ref, d_out, m_out, v_out):
        d_out[...], m_out[...], v_out[...] = _adamw_math(w_ref[...], m_ref[...], v_ref[...], g_ref[...])

    flat = pl.BlockSpec((tr, LANES), lambda i: (i, 0))
    return pl.pallas_call(
        body, name=name, grid=(rows // tr,), in_specs=[flat] * 4, out_specs=[flat] * 3,
        out_shape=[jax.ShapeDtypeStruct((rows, LANES), F32)] * 3,
        compiler_params=_params("parallel"),
    )(w, m, v, g)


def _adamw_shard(w, m, v, g_own, recv, layer, prev, name):
    depth, a, b = w.shape
    ta = min(256, a)
    assert a % ta == 0

    def body(w_ref, m_ref, v_ref, g_ref, r_ref, *rest):
        g_out, d_out, m_out, v_out = rest[-4:]
        g = g_ref[...]
        for k in range(N_DEV - 1):
            g = g + r_ref[k].astype(F32)
        g_out[0] = g
        d_out[0], m_out[0], v_out[0] = _adamw_math(w_ref[0], m_ref[0], v_ref[0], g)

    lay = pl.BlockSpec((1, ta, b), lambda i: (layer, i, 0))
    in_specs = [lay] * 3 + [pl.BlockSpec((ta, b), lambda i: (i, 0)), pl.BlockSpec((N_DEV - 1, ta, b), lambda i: (0, i, 0))]
    args = [w, m, v, g_own, recv]
    aliases = {}
    if prev is not None:
        in_specs += [pl.BlockSpec(memory_space=pl.ANY)] * 4
        args += list(prev)
        aliases = {5 + i: i for i in range(4)}
    return pl.pallas_call(
        body, name=name, grid=(a // ta,), in_specs=in_specs, out_specs=[lay] * 4,
        out_shape=[jax.ShapeDtypeStruct(w.shape, F32)] * 4, input_output_aliases=aliases,
        compiler_params=_params("parallel"),
    )(*args)


def _round_up(n, mult):
    return (n + mult - 1) // mult * mult


def _flatten(parts, row_mult):
    flat = jnp.concatenate([p.reshape(-1) for p in parts])
    rows = _round_up(-(-flat.shape[0] // LANES), row_mult)
    return jnp.pad(flat, (0, rows * LANES - flat.shape[0])).reshape(rows, LANES)


def _unflatten(flat, shapes):
    flat = flat.reshape(-1)
    out, off = [], 0
    for shp in shapes:
        n = 1
        for dim in shp:
            n *= dim
        out.append(flat[off:off + n].reshape(shp))
        off += n
    return out


def kernel(x, norm1_g, w_in, b_f, q_norm_g, k_norm_g, conv_w, conv_b, conv_ln_g, conv_ln_b, w_o, norm2_g, w_mlp_in, w_mlp_out, loss_target, m_norm1_g, m_w_in, m_b_f, m_q_norm_g, m_k_norm_g, m_conv_w, m_conv_b, m_conv_ln_g, m_conv_ln_b, m_w_o, m_norm2_g, m_w_mlp_in, m_w_mlp_out, v_norm1_g, v_w_in, v_b_f, v_q_norm_g, v_k_norm_g, v_conv_w, v_conv_b, v_conv_ln_g, v_conv_ln_b, v_w_o, v_norm2_g, v_w_mlp_in, v_w_mlp_out):
    depth, d_model, n_in_loc = w_in.shape
    n_heads = b_f.shape[1]
    aw = n_heads * HEAD_DIM
    cc = conv_b.shape[1]
    n_in = n_in_loc * N_DEV
    o_f = 3 * aw
    n_all = 3 * aw + 2 * cc + LANES
    assert n_in == 3 * aw + n_heads + 2 * cc and aw + cc == d_model and n_heads % 2 == 0
    assert aw % LANES == 0 and cc % LANES == 0 and x.shape[0] == 1
    me = 4 * lax.axis_index("x") + 2 * lax.axis_index("y") + lax.axis_index("c")

    d_ff = w_mlp_in.shape[2] * N_DEV

    lin, lo, l1, l2, lc = _gather_weights(w_in.astype(BF16), w_o.astype(BF16), w_mlp_in.astype(BF16), w_mlp_out.astype(BF16),
                                          jnp.stack(_split3(conv_w)), "gather_weights")
    lc = lc.astype(F32)
    conv_full = jnp.moveaxis(lc[:, 0] + lc[:, 1] + lc[:, 2], 0, 2).reshape(depth, CONV_TAPS, cc)

    def to_all(w):
        return jnp.concatenate([w[:, :o_f], w[:, o_f + n_heads:], w[:, o_f:o_f + n_heads],
                                jnp.zeros((w.shape[0], LANES - n_heads), w.dtype)], axis=1)

    def from_all(w):
        return jnp.concatenate([w[:, :o_f], w[:, n_all - LANES:n_all - LANES + n_heads], w[:, o_f:n_all - LANES]], axis=1)

    w_all = [to_all(jnp.moveaxis(lin[:, l], 0, 1).reshape(d_model, n_in)) for l in range(depth)]
    w_out = [lo[l] for l in range(depth)]
    w_ff1 = [l1[l] for l in range(depth)]
    w_ff2 = [l2[l] for l in range(depth)]

    def row(p, l, width=None):
        v = p[l].reshape(1, -1)
        return v if width is None else jnp.pad(v, ((0, 0), (0, width - v.shape[1])))

    a_col, g_col = 3 * aw // LANES, (3 * aw + cc) // LANES

    gq = [jnp.tile(row(q_norm_g, l), (1, n_heads)) for l in range(depth)]
    gk = [jnp.tile(row(k_norm_g, l), (1, n_heads)) for l in range(depth)]
    bfp = [row(b_f, l, LANES) for l in range(depth)]
    w32 = [jnp.pad(conv_full[l], ((0, CONV_PAD - CONV_TAPS), (0, 0))) for l in range(depth)]
    add_res = lambda acc, res: (acc + res,)

    h = x[0]
    saved = []
    for l in range(depth):
        u1 = _rms_fwd(h, row(norm1_g, l), f"rms1_fwd_{l}")
        proj = _matmul(u1, w_all[l], mode="nn", out_dtypes=(F32,), name=f"mm_in_{l}", tn=n_all)
        qa, ka, vb = _prep_fwd(proj, gq[l], gk[l], bfp[l], n_heads, f"prep_fwd_{l}")
        att, lse = _attn_fwd(qa, ka, vb, n_heads, f"attn_fwd_{l}")
        yc = _conv_fwd(proj, w32[l], row(conv_b, l), cc, a_col, g_col, f"conv_fwd_{l}")
        hc = _ln_silu_fwd(yc, row(conv_ln_g, l), row(conv_ln_b, l), f"ln_silu_fwd_{l}")
        mixed = jnp.concatenate([att.astype(BF16), hc], axis=1)
        x1 = _matmul(mixed, w_out[l], mode="nn", out_dtypes=(F32,), name=f"mm_o_{l}", epilogue=add_res, extras=(h,))
        u2 = _rms_fwd(x1, row(norm2_g, l), f"rms2_fwd_{l}")
        r, a = _matmul(u2, w_ff1[l], mode="nn", out_dtypes=(BF16, BF16), name=f"mm_ff1_{l}", tm=256, tn=d_ff,
                       epilogue=lambda acc: (jnp.maximum(acc, 0.0), jnp.square(jnp.maximum(acc, 0.0))))
        x2 = _matmul(a, w_ff2[l], mode="nn", out_dtypes=(F32,), name=f"mm_ff2_{l}", epilogue=add_res, extras=(x1,), tk=d_ff)
        saved.append(dict(x_in=h, u1=u1, proj=proj, qa=qa, ka=ka, vb=vb, att=att, lse=lse, yc=yc, mixed=mixed,
                          x1=x1, u2=u2, r=r, a=a))
        h = x2

    dh, dh16, sq = _loss_grad(h, loss_target[0], "loss_grad")
    loss = lax.psum(0.5 * jnp.sum(sq) / d_model, ("x", "y", "c"))

    g_in, g_in_own = [None] * depth, [None] * depth
    g_o, g_1, g_2 = [None] * depth, [None] * depth, [None] * depth
    gs = {n: [None] * depth for n in ("norm1", "bf", "qn", "kn", "convw", "convb", "lng", "lnb", "norm2")}
    both = (F32, BF16)
    for l in reversed(range(depth)):
        sv = saved[l]
        dh1 = _matmul(dh16, w_ff2[l], mode="nt", out_dtypes=(BF16,), name=f"mm_dff2_{l}", tm=256, tn=d_ff,
                      epilogue=lambda acc, rr: (acc * (2.0 * rr.astype(F32)),), extras=(sv["r"],))
        g_2[l] = _matmul(sv["a"], dh16, mode="tn", out_dtypes=both, name=f"mm_dw2_{l}", tm=2048, tk=512)
        g_1[l] = _matmul(sv["u2"], dh1, mode="tn", out_dtypes=both, name=f"mm_dw1_{l}", tm=1024, tn=2048, tk=512)
        du2 = _matmul(dh1, w_ff1[l], mode="nt", out_dtypes=(F32,), name=f"mm_du2_{l}", tk=d_ff)
        dx1, dx16, gs["norm2"][l] = _rms_bwd(sv["x1"], row(norm2_g, l), du2, dh, f"rms2_bwd_{l}")

        dmixed = _matmul(dx16, w_out[l], mode="nt", out_dtypes=(F32,), name=f"mm_dmixed_{l}")
        g_o[l] = _matmul(sv["mixed"], dx16, mode="tn", out_dtypes=both, name=f"mm_dwo_{l}", tm=1024)
        dyc, gs["lng"][l], gs["lnb"][l] = _ln_silu_bwd(sv["yc"], row(conv_ln_g, l), row(conv_ln_b, l), dmixed, f"ln_silu_bwd_{l}")
        dpa, dpg, dw32 = _conv_bwd(sv["proj"], w32[l], dyc, cc, a_col, g_col, f"conv_bwd_{l}")
        gs["convw"][l], gs["convb"][l] = dw32[:CONV_TAPS], dw32[CONV_TAPS:CONV_TAPS + 1]
        dq, dka, dv, drow, dcol = _attn_bwd(sv["qa"], sv["ka"], sv["vb"], sv["att"], sv["lse"], dmixed, n_heads, f"attn_bwd_{l}")
        dpq, dpk, dpf, dgq, dgk, dbf = _prep_bwd(sv["proj"], dq, dka, drow, dcol, gq[l], gk[l], bfp[l], n_heads, f"prep_bwd_{l}")
        gs["qn"][l] = dgq.reshape(n_heads, HEAD_DIM).sum(axis=0)
        gs["kn"][l] = dgk.reshape(n_heads, HEAD_DIM).sum(axis=0)
        gs["bf"][l] = dbf[0, :n_heads]
        dproj = jnp.concatenate([dpq, dpk, dv.astype(BF16), dpa, dpg, dpf], axis=1)
        du1 = _matmul(dproj, w_all[l], mode="nt", out_dtypes=(F32,), name=f"mm_du1_{l}", tk=n_all)
        dwall, dwall16 = _matmul(sv["u1"], dproj, mode="tn", out_dtypes=both, name=f"mm_dwall_{l}", tm=1024, tn=n_all // 3)
        g_in[l] = jnp.moveaxis(from_all(dwall16).reshape(d_model, N_DEV, n_in_loc), 1, 0)
        g_in_own[l] = lax.dynamic_slice_in_dim(from_all(dwall), me * n_in_loc, n_in_loc, axis=1)
        dh, dh16, gs["norm1"][l] = _rms_bwd(sv["x_in"], row(norm1_g, l), du1, dx1, f"rms1_bwd_{l}")
    grad_x = dh[None]

    recv, own = _scatter_grads(g_in, g_o, g_1, g_2, "scatter_grads")
    own.update({("in", l): g_in_own[l] for l in range(depth)})
    big_out = []
    for kd, w, m, v in (("in", w_in, m_w_in, v_w_in), ("o", w_o, m_w_o, v_w_o),
                        ("1", w_mlp_in, m_w_mlp_in, v_w_mlp_in), ("2", w_mlp_out, m_w_mlp_out, v_w_mlp_out)):
        outs = None
        for l in reversed(range(depth)):
            outs = _adamw_shard(w, m, v, own[(kd, l)], recv[(kd, l)], l, outs, f"adamw_{kd}_{l}")
        big_out.append(outs)
    big_out = [[big_out[wi][kind] for wi in range(4)] for kind in range(4)]

    small_g = [jnp.stack(gs[n]).reshape(shp) for n, shp in (
        ("norm1", norm1_g.shape), ("bf", b_f.shape), ("qn", q_norm_g.shape), ("kn", k_norm_g.shape),
        ("convw", (depth, CONV_TAPS, cc)), ("convb", conv_b.shape), ("lng", conv_ln_g.shape), ("lnb", conv_ln_b.shape),
        ("norm2", norm2_g.shape))]
    small_shapes = [g.shape for g in small_g]
    small_g = _unflatten(_all_reduce_small(_flatten(small_g, 8), "all_reduce_small"), small_shapes)
    cw = conv_w.shape[2]
    small_g[4] = lax.dynamic_slice_in_dim(small_g[4], me * cw, cw, axis=2)
    small = (norm1_g, b_f, q_norm_g, k_norm_g, conv_w, conv_b, conv_ln_g, conv_ln_b, norm2_g)
    small_m = (m_norm1_g, m_b_f, m_q_norm_g, m_k_norm_g, m_conv_w, m_conv_b, m_conv_ln_g, m_conv_ln_b, m_norm2_g)
    small_v = (v_norm1_g, v_b_f, v_q_norm_g, v_k_norm_g, v_conv_w, v_conv_b, v_conv_ln_g, v_conv_ln_b, v_norm2_g)
    small_out = _adamw(_flatten(small, 8), _flatten(small_m, 8), _flatten(small_v, 8), _flatten(small_g, 8), "adamw_small")
    small_out = [small_g] + [_unflatten(o, [w.shape for w in small]) for o in small_out]

    def group(kind):
        s_, b_ = small_out[kind], big_out[kind]
        return [s_[0], b_[0], s_[1], s_[2], s_[3], s_[4], s_[5], s_[6], s_[7], b_[1], s_[8], b_[2], b_[3]]

    return (loss, grad_x, *group(0), *group(1), *group(2), *group(3))
```

```python
import functools

import jax
import jax.numpy as jnp
from jax import lax
from jax.experimental import pallas as pl
from jax.experimental.pallas import tpu as pltpu

F32 = jnp.float32
BF16 = jnp.bfloat16

EPS = 1e-6
HEAD_DIM = 64
LANES = 128
PAIR = 2 * LANES
N_DEV = 8
CONV_TAPS = 31
CONV_PAD = 32
NEG = -1e30

ADAM_LR = 0.001
ADAM_B1 = 0.9
ADAM_B2 = 0.999
ADAM_EPS = 1e-08
ADAM_WD = 0.01
ADAM_STEP = 10

TM = 512
TQ = 512
CONV_ROWS = 128
FLAT_ROWS = 1024
MESH = pl.DeviceIdType.MESH


def _params(*sem):
    return pltpu.CompilerParams(dimension_semantics=sem, vmem_limit_bytes=56 * 1024 * 1024)


def _split3(x):
    hi = x.astype(BF16)
    r1 = x - hi.astype(F32)
    mid = r1.astype(BF16)
    lo = (r1 - mid.astype(F32)).astype(BF16)
    return hi, mid, lo


def _dot(a, b):
    return jnp.dot(a, b, preferred_element_type=F32)


def _dot_nt(a, b):
    return lax.dot_general(a, b, (((1,), (1,)), ((), ())), preferred_element_type=F32)


def _dot_tn(a, b):
    return lax.dot_general(a, b, (((0,), (0,)), ((), ())), preferred_element_type=F32)


def _dot3(x, mat):
    hi, mid, lo = _split3(x)
    return _dot(hi, mat) + _dot(mid, mat) + _dot(lo, mat)


def _dot3_r(mat, x):
    hi, mid, lo = _split3(x)
    return _dot(mat, hi) + _dot(mat, mid) + _dot(mat, lo)


def _iota(shape, dim):
    return lax.broadcasted_iota(jnp.int32, shape, dim)


def _sigmoid(x):
    return 1.0 / (1.0 + jnp.exp(-x))


def _matmul(a, b, *, mode, out_dtypes, name, epilogue=None, extras=(), tm=TM, tn=1024, tk=1024):
    if mode == "nn":
        (m, k), (k2, n) = a.shape, b.shape
    elif mode == "nt":
        (m, k), (n, k2) = a.shape, b.shape
    else:
        (k, m), (k2, n) = a.shape, b.shape
    assert k == k2, (name, a.shape, b.shape)
    tm, tn, tk = min(tm, m), min(tn, n), min(tk, k)
    assert m % tm == 0 and n % tn == 0 and k % tk == 0, (name, m, n, k, tm, tn, tk)
    nk = k // tk
    if mode == "tn":
        a_spec = pl.BlockSpec((tk, tm), lambda i, j, kk: (kk, i))
    else:
        a_spec = pl.BlockSpec((tm, tk), lambda i, j, kk: (i, kk))
    b_mode = dict(pipeline_mode=pl.Buffered(1)) if (n == tn and nk == 1) else {}
    if mode == "nt":
        b_spec = pl.BlockSpec((tn, tk), lambda i, j, kk: (j, kk), **b_mode)
    else:
        b_spec = pl.BlockSpec((tk, tn), lambda i, j, kk: (kk, j), **b_mode)
    dot = {"nn": _dot, "nt": _dot_nt, "tn": _dot_tn}[mode]
    tile = pl.BlockSpec((tm, tn), lambda i, j, kk: (i, j))
    n_ex, n_out = len(extras), len(out_dtypes)
    acc_in_out = nk > 1 and epilogue is None and out_dtypes[0] == F32

    def body(a_ref, b_ref, *rest):
        ex_refs, out_refs = rest[:n_ex], rest[n_ex:n_ex + n_out]
        part = dot(a_ref[...].astype(BF16), b_ref[...].astype(BF16))

        def finish(acc):
            res = epilogue(acc, *[e[...] for e in ex_refs]) if epilogue is not None else (acc,) * n_out
            for o_ref, r in zip(out_refs, res):
                o_ref[...] = r.astype(o_ref.dtype)

        if nk == 1:
            finish(part)
        else:
            acc_ref = out_refs[0] if acc_in_out else rest[-1]
            kk = pl.program_id(2)

            @pl.when(kk == 0)
            def _():
                acc_ref[...] = part

            @pl.when(kk > 0)
            def _():
                acc_ref[...] += part

            @pl.when(kk == nk - 1)
            def _():
                if acc_in_out:
                    for o_ref in out_refs[1:]:
                        o_ref[...] = acc_ref[...].astype(o_ref.dtype)
                else:
                    finish(acc_ref[...])

    outs = pl.pallas_call(
        body,
        name=name,
        grid=(m // tm, n // tn, nk),
        in_specs=[a_spec, b_spec] + [tile] * n_ex,
        out_specs=[tile] * n_out,
        out_shape=[jax.ShapeDtypeStruct((m, n), dt) for dt in out_dtypes],
        scratch_shapes=[pltpu.VMEM((tm, tn), F32)] if nk > 1 and not acc_in_out else [],
        compiler_params=_params("parallel", "parallel", "arbitrary"),
    )(a, b, *extras)
    return outs if n_out > 1 else outs[0]


def _rms_fwd(x, g, name):
    s, d = x.shape
    ts = min(TM, s)

    def body(x_ref, g_ref, u_ref):
        xv = x_ref[...]
        y = xv * lax.rsqrt(jnp.mean(xv * xv, axis=-1, keepdims=True) + EPS)
        u_ref[...] = (y * g_ref[...]).astype(BF16)

    return pl.pallas_call(
        body, name=name, grid=(s // ts,),
        in_specs=[pl.BlockSpec((ts, d), lambda i: (i, 0)), pl.BlockSpec((1, d), lambda i: (0, 0))],
        out_specs=pl.BlockSpec((ts, d), lambda i: (i, 0)),
        out_shape=jax.ShapeDtypeStruct((s, d), BF16),
        compiler_params=_params("parallel"),
    )(x, g)


def _rms_bwd(x, g, du, dres, name):
    s, d = x.shape
    ts = min(TM, s)

    def body(x_ref, g_ref, du_ref, dres_ref, dx_ref, dx16_ref, dg_ref):
        @pl.when(pl.program_id(0) == 0)
        def _():
            dg_ref[...] = jnp.zeros_like(dg_ref)

        xv, duv = x_ref[...], du_ref[...]
        r = lax.rsqrt(jnp.mean(xv * xv, axis=-1, keepdims=True) + EPS)
        xh = xv * r
        dxh = duv * g_ref[...]
        dx = dres_ref[...] + r * (dxh - xh * jnp.mean(dxh * xh, axis=-1, keepdims=True))
        dx_ref[...] = dx
        dx16_ref[...] = dx.astype(BF16)
        dg_ref[...] += jnp.sum(duv * xh, axis=0, keepdims=True)

    row = pl.BlockSpec((ts, d), lambda i: (i, 0))
    vec = pl.BlockSpec((1, d), lambda i: (0, 0))
    return pl.pallas_call(
        body, name=name, grid=(s // ts,),
        in_specs=[row, vec, row, row], out_specs=[row, row, vec],
        out_shape=[jax.ShapeDtypeStruct((s, d), F32), jax.ShapeDtypeStruct((s, d), BF16), jax.ShapeDtypeStruct((1, d), F32)],
        compiler_params=_params("arbitrary"),
    )(x, g, du, dres)


def _ln_silu_fwd(y, g, b, name):
    s, c = y.shape
    ts = min(TM, s)

    def body(y_ref, g_ref, b_ref, h_ref):
        yv = y_ref[...]
        mu = jnp.mean(yv, axis=-1, keepdims=True)
        yc = yv - mu
        z = yc * lax.rsqrt(jnp.mean(yc * yc, axis=-1, keepdims=True) + EPS) * g_ref[...] + b_ref[...]
        h_ref[...] = (z * _sigmoid(z)).astype(BF16)

    row = pl.BlockSpec((ts, c), lambda i: (i, 0))
    vec = pl.BlockSpec((1, c), lambda i: (0, 0))
    return pl.pallas_call(
        body, name=name, grid=(s // ts,), in_specs=[row, vec, vec], out_specs=row,
        out_shape=jax.ShapeDtypeStruct((s, c), BF16), compiler_params=_params("parallel"),
    )(y, g, b)


def _ln_silu_bwd(y, g, b, dmixed, name):
    s, c = y.shape
    ts = min(TM, s)

    def body(y_ref, g_ref, b_ref, dh_ref, dy_ref, dg_ref, db_ref):
        @pl.when(pl.program_id(0) == 0)
        def _():
            dg_ref[...] = jnp.zeros_like(dg_ref)
            db_ref[...] = jnp.zeros_like(db_ref)

        yv = y_ref[...]
        mu = jnp.mean(yv, axis=-1, keepdims=True)
        yc = yv - mu
        r = lax.rsqrt(jnp.mean(yc * yc, axis=-1, keepdims=True) + EPS)
        yh = yc * r
        z = yh * g_ref[...] + b_ref[...]
        sg = _sigmoid(z)
        dz = dh_ref[...] * (sg * (1.0 + z * (1.0 - sg)))
        dg_ref[...] += jnp.sum(dz * yh, axis=0, keepdims=True)
        db_ref[...] += jnp.sum(dz, axis=0, keepdims=True)
        dyh = dz * g_ref[...]
        dy_ref[...] = r * (dyh - jnp.mean(dyh, axis=-1, keepdims=True) - yh * jnp.mean(dyh * yh, axis=-1, keepdims=True))

    row = pl.BlockSpec((ts, c), lambda i: (i, 0))
    vec = pl.BlockSpec((1, c), lambda i: (0, 0))
    return pl.pallas_call(
        body, name=name, grid=(s // ts,),
        in_specs=[row, vec, vec, pl.BlockSpec((ts, c), lambda i: (i, 1))], out_specs=[row, vec, vec],
        out_shape=[jax.ShapeDtypeStruct((s, c), F32), jax.ShapeDtypeStruct((1, c), F32), jax.ShapeDtypeStruct((1, c), F32)],
        compiler_params=_params("arbitrary"),
    )(y, g, b, dmixed)


def _loss_grad(y, target, name):
    s, d = y.shape
    ts = min(TM, s)

    def body(y_ref, t_ref, dy_ref, dy16_ref, sq_ref):
        @pl.when(pl.program_id(0) == 0)
        def _():
            sq_ref[...] = jnp.zeros_like(sq_ref)

        err = y_ref[...] - t_ref[...]
        dy = err * (1.0 / d)
        dy_ref[...] = dy
        dy16_ref[...] = dy.astype(BF16)
        sq_ref[...] += jnp.sum(err * err, axis=0, keepdims=True)

    row = pl.BlockSpec((ts, d), lambda i: (i, 0))
    vec = pl.BlockSpec((1, d), lambda i: (0, 0))
    return pl.pallas_call(
        body, name=name, grid=(s // ts,), in_specs=[row, row], out_specs=[row, row, vec],
        out_shape=[jax.ShapeDtypeStruct((s, d), F32), jax.ShapeDtypeStruct((s, d), BF16), jax.ShapeDtypeStruct((1, d), F32)],
        compiler_params=_params("arbitrary"),
    )(y, target)


def _head_masks():
    lane2 = _iota((1, PAIR), 1)
    lane1 = _iota((1, LANES), 1)
    qa = (lane2 < HEAD_DIM) | ((lane2 >= LANES) & (lane2 < LANES + 3))
    qb = ((lane2 >= HEAD_DIM) & (lane2 < LANES)) | ((lane2 >= LANES + 3) & (lane2 < LANES + 6))
    return (qa, qb), (lane1 < HEAD_DIM, lane1 >= HEAD_DIM)


def _group_matrix(width):
    shift = HEAD_DIM.bit_length() - 1
    return ((_iota((width, width), 0) >> shift) == (_iota((width, width), 1) >> shift)).astype(BF16)


def _prep_fwd(proj, gq, gk, bf, n_heads, name):
    s = proj.shape[0]
    aw = n_heads * HEAD_DIM
    n_pairs = n_heads // 2
    ts = min(TM, s)
    f_col = (proj.shape[1] - LANES) // LANES

    def body(q_ref, k_ref, v_ref, f_ref, gq_ref, gk_ref, bf_ref, qa_ref, ka_ref, vb_ref, carry_ref):
        @pl.when(pl.program_id(0) == 0)
        def _():
            carry_ref[...] = jnp.zeros_like(carry_ref)

        gmat = _group_matrix(aw)

        def head_norm(xv, g):
            ms = _dot3(xv * xv, gmat) * (1.0 / HEAD_DIM)
            return xv * lax.rsqrt(ms + EPS) * g

        qn = head_norm(q_ref[...], gq_ref[...]) * (HEAD_DIM ** -0.5)
        kn = head_norm(k_ref[...], gk_ref[...])
        z = f_ref[...] + bf_ref[...]
        logf = jnp.minimum(z, 0.0) - jnp.log(1.0 + jnp.exp(-jnp.abs(z)))
        tri = (_iota((ts, ts), 0) >= _iota((ts, ts), 1)).astype(BF16)
        c = _dot3_r(tri, logf) + carry_ref[...]
        carry_ref[...] = c[ts - 1:ts, :]
        terms = _split3(-c)
        row, col = _iota((LANES, LANES), 0), _iota((LANES, LANES), 1)
        ones = jnp.where(_iota((ts, LANES), 1) < 6, 1.0, 0.0).astype(BF16)
        for p in range(n_pairs):
            extra = jnp.zeros((ts, LANES), F32)
            for t, term in enumerate(terms):
                sel = ((row == 2 * p) & (col == t)) | ((row == 2 * p + 1) & (col == 3 + t))
                extra += _dot(term, sel.astype(BF16))
            lo, hi = p * PAIR, p * PAIR + LANES
            ka_ref[:, lo:hi] = kn[:, p * LANES:(p + 1) * LANES].astype(BF16)
            ka_ref[:, hi:hi + LANES] = extra.astype(BF16)
            qa_ref[:, lo:hi] = qn[:, p * LANES:(p + 1) * LANES].astype(BF16)
            qa_ref[:, hi:hi + LANES] = ones
        vb_ref[...] = v_ref[...].astype(BF16)

    blk = lambda j: pl.BlockSpec((ts, aw), lambda i: (i, j))
    vec = lambda w: pl.BlockSpec((1, w), lambda i: (0, 0))
    return pl.pallas_call(
        body, name=name, grid=(s // ts,),
        in_specs=[blk(0), blk(1), blk(2), pl.BlockSpec((ts, LANES), lambda i: (i, f_col)), vec(aw), vec(aw), vec(LANES)],
        out_specs=[pl.BlockSpec((ts, n_pairs * PAIR), lambda i: (i, 0)), pl.BlockSpec((ts, n_pairs * PAIR), lambda i: (i, 0)), blk(0)],
        out_shape=[jax.ShapeDtypeStruct((s, n_pairs * PAIR), BF16), jax.ShapeDtypeStruct((s, n_pairs * PAIR), BF16),
                   jax.ShapeDtypeStruct((s, aw), BF16)],
        scratch_shapes=[pltpu.VMEM((1, LANES), F32)],
        compiler_params=_params("arbitrary"),
    )(proj, proj, proj, proj, gq, gk, bf)


def _prep_bwd(proj, dq, dka, drow, dcol, gq, gk, bf, n_heads, name):
    s = proj.shape[0]
    aw = n_heads * HEAD_DIM
    n_pairs = n_heads // 2
    ts = min(TM, s)
    nt = s // ts
    f_col = (proj.shape[1] - LANES) // LANES
    shift = HEAD_DIM.bit_length() - 1

    def body(q_ref, k_ref, f_ref, dq_ref, dka_ref, drow_ref, dcol_ref, gq_ref, gk_ref, bf_ref,
             dpq_ref, dpk_ref, dpf_ref, dgq_ref, dgk_ref, dbf_ref, carry_ref):
        @pl.when(pl.program_id(0) == 0)
        def _():
            carry_ref[...] = jnp.zeros_like(carry_ref)
            dgq_ref[...] = jnp.zeros_like(dgq_ref)
            dgk_ref[...] = jnp.zeros_like(dgk_ref)
            dbf_ref[...] = jnp.zeros_like(dbf_ref)

        gmat = _group_matrix(aw)

        def head_norm_bwd(xv, g, dn):
            r = lax.rsqrt(_dot3(xv * xv, gmat) * (1.0 / HEAD_DIM) + EPS)
            xh = xv * r
            dxh = dn * g
            dx = r * (dxh - xh * (_dot3(dxh * xh, gmat) * (1.0 / HEAD_DIM)))
            return dx, jnp.sum(dn * xh, axis=0, keepdims=True)

        dkav = dka_ref[...]
        dx, dg = head_norm_bwd(q_ref[...], gq_ref[...], dq_ref[...] * (HEAD_DIM ** -0.5))
        dpq_ref[...] = dx.astype(BF16)
        dgq_ref[...] += dg
        dkn = jnp.concatenate([dkav[:, p * PAIR:p * PAIR + LANES] for p in range(n_pairs)], axis=1)
        dx, dg = head_norm_bwd(k_ref[...], gk_ref[...], dkn)
        dpk_ref[...] = dx.astype(BF16)
        dgk_ref[...] += dg

        pick = (_iota((aw, LANES), 0) == (_iota((aw, LANES), 1) << shift)).astype(BF16)
        dc = _dot3(drow_ref[...], pick)
        r16, c16 = _iota((16, LANES), 0), _iota((16, LANES), 1)
        for p in range(n_pairs):
            place = ((r16 < 2) & (c16 == 2 * p + r16)).astype(BF16)
            for term in _split3(dcol_ref[p]):
                dc -= _dot_tn(term, place)
        triu = (_iota((ts, ts), 0) <= _iota((ts, ts), 1)).astype(BF16)
        dlogf = _dot3_r(triu, dc) + carry_ref[...]
        carry_ref[...] = dlogf[0:1, :]
        z = f_ref[...] + bf_ref[...]
        dz = dlogf * (1.0 / (1.0 + jnp.exp(z)))
        dpf_ref[...] = dz.astype(BF16)
        dbf_ref[...] += jnp.sum(dz, axis=0, keepdims=True)

    rev = lambda w, j: pl.BlockSpec((ts, w), lambda i: (nt - 1 - i, j))
    vec = lambda w: pl.BlockSpec((1, w), lambda i: (0, 0))
    return pl.pallas_call(
        body, name=name, grid=(nt,),
        in_specs=[rev(aw, 0), rev(aw, 1), rev(LANES, f_col), rev(aw, 0), rev(n_pairs * PAIR, 0), rev(aw, 0),
                  pl.BlockSpec((n_pairs, 16, ts), lambda i: (0, 0, nt - 1 - i)), vec(aw), vec(aw), vec(LANES)],
        out_specs=[rev(aw, 0), rev(aw, 0), rev(LANES, 0), vec(aw), vec(aw), vec(LANES)],
        out_shape=[jax.ShapeDtypeStruct((s, aw), BF16), jax.ShapeDtypeStruct((s, aw), BF16), jax.ShapeDtypeStruct((s, LANES), BF16),
                   jax.ShapeDtypeStruct((1, aw), F32), jax.ShapeDtypeStruct((1, aw), F32), jax.ShapeDtypeStruct((1, LANES), F32)],
        scratch_shapes=[pltpu.VMEM((1, LANES), F32)],
        compiler_params=_params("arbitrary"),
    )(proj, proj, proj, dq, dka, drow, dcol, gq, gk, bf)


def _attn_fwd(qa, ka, vb, n_heads, name):
    s = qa.shape[0]
    aw = n_heads * HEAD_DIM
    n_pairs = n_heads // 2
    tq = min(TQ, s)

    def body(q_ref, k_ref, v_ref, o_ref, lse_ref):
        i = pl.program_id(1)
        qmasks, omasks = _head_masks()
        qv = q_ref[...]
        causal = _iota((tq, tq), 1) <= _iota((tq, tq), 0)
        res = []
        for h in range(2):
            qh = jnp.where(qmasks[h], qv, jnp.zeros_like(qv))

            def step(j, carry, masked, qh=qh):
                m, l, acc = carry
                off = pl.multiple_of(j * tq, tq)
                sc = _dot_nt(qh, k_ref[pl.ds(off, tq), :])
                if masked:
                    sc = jnp.where(causal, sc, NEG)
                m_new = jnp.maximum(m, jnp.max(sc, axis=1, keepdims=True))
                p = jnp.exp(sc - m_new)
                alpha = jnp.exp(m - m_new)
                l = alpha * l + jnp.sum(p, axis=1, keepdims=True)
                acc = alpha * acc + _dot(p.astype(BF16), v_ref[pl.ds(off, tq), :])
                return m_new, l, acc

            init = (jnp.full((tq, 1), NEG, F32), jnp.zeros((tq, 1), F32), jnp.zeros((tq, LANES), F32))
            carry = lax.fori_loop(0, i, functools.partial(step, masked=False), init)
            m, l, acc = step(i, carry, True)
            res.append((acc * (1.0 / l), m + jnp.log(l)))
        o_ref[...] = jnp.where(omasks[0], res[0][0], res[1][0])
        lse_ref[...] = jnp.where(omasks[0], res[0][1], res[1][1])

    return pl.pallas_call(
        body, name=name, grid=(n_pairs, s // tq),
        in_specs=[pl.BlockSpec((tq, PAIR), lambda p, i: (i, p)), pl.BlockSpec((s, PAIR), lambda p, i: (0, p)),
                  pl.BlockSpec((s, LANES), lambda p, i: (0, p))],
        out_specs=[pl.BlockSpec((tq, LANES), lambda p, i: (i, p))] * 2,
        out_shape=[jax.ShapeDtypeStruct((s, aw), F32)] * 2,
        compiler_params=_params("parallel", "parallel"),
    )(qa, ka, vb)


def _attn_bwd(qa, ka, vb, o, lse, dmixed, n_heads, name):
    s = qa.shape[0]
    aw = n_heads * HEAD_DIM
    n_pairs = n_heads // 2
    tq = min(TQ, s)
    nq = s // tq

    def body(q_ref, k_ref, v_ref, o_ref, lse_ref, do_ref, dq_ref, dka_ref, dv_ref, drow_ref, dcol_ref, delta_ref):
        j = pl.program_id(1)
        qmasks, omasks = _head_masks()

        @pl.when(j == 0)
        def _():
            dq_ref[...] = jnp.zeros_like(dq_ref)
            drow_ref[...] = jnp.zeros_like(drow_ref)
            for c in range(nq):
                rows = slice(c * tq, (c + 1) * tq)
                prod = do_ref[rows, :] * o_ref[rows, :]
                da = jnp.sum(jnp.where(omasks[0], prod, 0.0), axis=1, keepdims=True)
                db = jnp.sum(jnp.where(omasks[1], prod, 0.0), axis=1, keepdims=True)
                delta_ref[rows, :] = jnp.where(omasks[0], da, db)

        dka_ref[...] = jnp.zeros_like(dka_ref)
        dv_ref[...] = jnp.zeros_like(dv_ref)
        dcol_ref[...] = jnp.zeros_like(dcol_ref)
        kv = k_ref[...]
        kk = kv[:, :LANES]
        vv = v_ref[...]
        causal = _iota((tq, tq), 1) <= _iota((tq, tq), 0)

        def step(i, masked):
            off = pl.multiple_of(i * tq, tq)
            qv = q_ref[pl.ds(off, tq), :]
            dov = do_ref[pl.ds(off, tq), :]
            lsev = lse_ref[pl.ds(off, tq), :]
            dlv = delta_ref[pl.ds(off, tq), :]
            for h in range(2):
                qh = jnp.where(qmasks[h], qv, jnp.zeros_like(qv))
                doh = jnp.where(omasks[h], dov, 0.0).astype(BF16)
                lane = h * HEAD_DIM
                sc = _dot_nt(qh, kv)
                if masked:
                    sc = jnp.where(causal, sc, NEG)
                p = jnp.exp(sc - lsev[:, lane:lane + 1])
                dv_ref[...] += _dot_tn(p.astype(BF16), doh)
                dp = _dot_nt(doh, vv)
                dsf = p * (dp - dlv[:, lane:lane + 1])
                drow_ref[pl.ds(off, tq), :] += jnp.where(omasks[h], jnp.sum(dsf, axis=1, keepdims=True), 0.0)
                dcol_ref[0, h:h + 1, :] += jnp.sum(dsf, axis=0, keepdims=True)
                ds = dsf.astype(BF16)
                dka_ref[...] += _dot_tn(ds, qh)
                dq_ref[pl.ds(off, tq), :] += jnp.where(omasks[h], _dot(ds, kk), 0.0)

        step(j, True)

        def loop_body(i, carry):
            step(i, False)
            return carry

        lax.fori_loop(j + 1, nq, loop_body, 0)

    full = lambda w: pl.BlockSpec((s, w), lambda p, j: (0, p))
    blk = lambda w: pl.BlockSpec((tq, w), lambda p, j: (j, p))
    return pl.pallas_call(
        body, name=name, grid=(n_pairs, nq),
        in_specs=[full(PAIR), blk(PAIR), blk(LANES), full(LANES), full(LANES), full(LANES)],
        out_specs=[full(LANES), blk(PAIR), blk(LANES), full(LANES), pl.BlockSpec((1, 16, tq), lambda p, j: (p, 0, j))],
        out_shape=[jax.ShapeDtypeStruct((s, aw), F32), jax.ShapeDtypeStruct((s, n_pairs * PAIR), F32),
                   jax.ShapeDtypeStruct((s, aw), F32), jax.ShapeDtypeStruct((s, aw), F32),
                   jax.ShapeDtypeStruct((n_pairs, 16, s), F32)],
        scratch_shapes=[pltpu.VMEM((s, LANES), F32)],
        compiler_params=_params("parallel", "arbitrary"),
    )(qa, ka, vb, o, lse, dmixed)


def _conv_fwd(proj, w32, bias, n_ch, a_col, g_col, name):
    s = proj.shape[0]
    rows = min(CONV_ROWS, s)

    def body(a_ref, g_ref, w_ref, b_ref, y_ref, pad_ref):
        pad_ref[0:CONV_PAD, :] = jnp.zeros((CONV_PAD, LANES), F32)
        pad_ref[CONV_PAD:CONV_PAD + s, :] = a_ref[...] * _sigmoid(g_ref[...])
        wv = w_ref[...]
        for c in range(s // rows):
            acc = jnp.broadcast_to(b_ref[...], (rows, LANES))
            for t in range(CONV_TAPS):
                start = c * rows + CONV_PAD - (CONV_TAPS - 1) + t
                acc = acc + wv[t:t + 1, :] * pad_ref[start:start + rows, :]
            y_ref[c * rows:(c + 1) * rows, :] = acc

    col = lambda j0: pl.BlockSpec((s, LANES), lambda c: (0, j0 + c))
    return pl.pallas_call(
        body, name=name, grid=(n_ch // LANES,),
        in_specs=[col(a_col), col(g_col), pl.BlockSpec((CONV_PAD, LANES), lambda c: (0, c)), pl.BlockSpec((1, LANES), lambda c: (0, c))],
        out_specs=pl.BlockSpec((s, LANES), lambda c: (0, c)),
        out_shape=jax.ShapeDtypeStruct((s, n_ch), F32),
        scratch_shapes=[pltpu.VMEM((s + CONV_PAD, LANES), F32)],
        compiler_params=_params("parallel"),
    )(proj, proj, w32, bias)


def _conv_bwd(proj, w32, dy, n_ch, a_col, g_col, name):
    s = proj.shape[0]
    rows = min(CONV_ROWS, s)
    sub = 8

    def fold(x):
        acc = x[0:sub, :]
        for r in range(1, rows // sub):
            acc = acc + x[r * sub:(r + 1) * sub, :]
        return acc

    def body(a_ref, g_ref, w_ref, dy_ref, da_ref, dg_ref, dw_ref, padh_ref, padd_ref):
        sg = _sigmoid(g_ref[...])
        padh_ref[0:CONV_PAD, :] = jnp.zeros((CONV_PAD, LANES), F32)
        padh_ref[CONV_PAD:CONV_PAD + s, :] = a_ref[...] * sg
        padd_ref[0:s, :] = dy_ref[...]
        padd_ref[s:s + CONV_PAD, :] = jnp.zeros((CONV_PAD, LANES), F32)
        wv = w_ref[...]
        dw = [jnp.zeros((sub, LANES), F32) for _ in range(CONV_TAPS + 1)]
        for c in range(s // rows):
            r0 = c * rows
            acc = jnp.zeros((rows, LANES), F32)
            dyc = dy_ref[r0:r0 + rows, :]
            for t in range(CONV_TAPS):
                back = r0 + (CONV_TAPS - 1) - t
                acc = acc + wv[t:t + 1, :] * padd_ref[back:back + rows, :]
                start = r0 + CONV_PAD - (CONV_TAPS - 1) + t
                dw[t] = dw[t] + fold(dyc * padh_ref[start:start + rows, :])
            dw[CONV_TAPS] = dw[CONV_TAPS] + fold(dyc)
            av = a_ref[r0:r0 + rows, :]
            sgc = _sigmoid(g_ref[r0:r0 + rows, :])
            da_ref[r0:r0 + rows, :] = (acc * sgc).astype(BF16)
            dg_ref[r0:r0 + rows, :] = (acc * av * sgc * (1.0 - sgc)).astype(BF16)
        for t in range(CONV_TAPS + 1):
            dw_ref[t:t + 1, :] = jnp.sum(dw[t], axis=0, keepdims=True)

    col = lambda j0: pl.BlockSpec((s, LANES), lambda c: (0, j0 + c))
    wspec = pl.BlockSpec((CONV_PAD, LANES), lambda c: (0, c))
    return pl.pallas_call(
        body, name=name, grid=(n_ch // LANES,),
        in_specs=[col(a_col), col(g_col), wspec, col(0)],
        out_specs=[col(0), col(0), wspec],
        out_shape=[jax.ShapeDtypeStruct((s, n_ch), BF16), jax.ShapeDtypeStruct((s, n_ch), BF16),
                   jax.ShapeDtypeStruct((CONV_PAD, n_ch), F32)],
        scratch_shapes=[pltpu.VMEM((s + CONV_PAD, LANES), F32), pltpu.VMEM((s + CONV_PAD, LANES), F32)],
        compiler_params=_params("parallel"),
    )(proj, proj, w32, dy)


def _my_place():
    return lax.axis_index("x"), lax.axis_index("y"), lax.axis_index("c")


def _flip(place, k):
    x, y, c = place
    return (1 - x if k & 4 else x, 1 - y if k & 2 else y, 1 - c if k & 1 else c)


def _dev_id(place):
    return 4 * place[0] + 2 * place[1] + place[2]


def _wait_all(ref, send_sem, recv_sem, place):
    pltpu.make_async_remote_copy(src_ref=ref, dst_ref=ref, send_sem=send_sem, recv_sem=recv_sem,
                                 device_id=place, device_id_type=MESH).wait()


def _window(kind, ref, dev, n):
    if kind == "slot":
        return ref.at[dev]
    if kind == "rows":
        return ref.at[pl.ds(pl.multiple_of(dev * n, n), n), :]
    return ref.at[:, pl.ds(pl.multiple_of(dev * n, n), n)]


def _seven(spec, ref):
    mode, kind, n = spec
    if mode == "scatter":
        return ref
    if kind == "slot":
        return ref.at[pl.ds(0, N_DEV - 1)]
    if kind == "rows":
        return ref.at[pl.ds(0, (N_DEV - 1) * n), :]
    return ref.at[:, pl.ds(0, (N_DEV - 1) * n)]


def _hbm(x):
    return pltpu.with_memory_space_constraint(x, pltpu.HBM)


_EFFECT = pltpu.SideEffectType.DATAFLOW_SIDE_EFFECTING


def _place_own(srcs, lands, specs, name):
    n = len(srcs)

    def body(*refs):
        src_refs, out_refs, sems = refs[:n], refs[n:2 * n], refs[-1]
        me = _dev_id(_my_place())
        copies = [pltpu.make_async_copy(src_refs[u], _window(specs[u][1], out_refs[u], me, specs[u][2]), sems.at[u]) for u in range(n)]
        for cp in copies:
            cp.start()
        for cp in copies:
            cp.wait()

    hbm = pl.BlockSpec(memory_space=pl.ANY)
    return pl.pallas_call(
        body, name=name, out_shape=[jax.ShapeDtypeStruct(shp, dt) for shp, dt in lands],
        in_specs=[hbm] * n, out_specs=[hbm] * n, scratch_shapes=[pltpu.SemaphoreType.DMA((n,))],
    )(*srcs)


def _exchange_start(srcs, lands, specs, groups, name):
    n = len(srcs)
    n_g = len(groups)

    def body(*refs):
        src_refs, land_refs = refs[:n], refs[n:2 * n]
        sems = refs[2 * n:2 * n + 2 * n_g]
        token = refs[-1]
        place = _my_place()
        me = _dev_id(place)
        for g, units in enumerate(groups):
            for j, u in enumerate(units):
                mode, kind, cnt = specs[u]
                for k in range(1, N_DEV):
                    peer = _flip(place, k)
                    if mode == "gather":
                        src, dst = src_refs[u], _window(kind, land_refs[u], me, cnt)
                    else:
                        src, dst = _window(kind, src_refs[u], _dev_id(peer), cnt), land_refs[u].at[k - 1]
                    pltpu.make_async_remote_copy(src_ref=src, dst_ref=dst, send_sem=sems[2 * g].at[j], recv_sem=sems[2 * g + 1].at[j],
                                                 device_id=peer, device_id_type=MESH).start()
        token[...] = jnp.zeros_like(token)

    hbm = pl.BlockSpec(memory_space=pltpu.HBM)
    sem = pl.BlockSpec(memory_space=pltpu.SEMAPHORE)
    out_shape = [pltpu.SemaphoreType.DMA((len(units),)) for units in groups for _ in range(2)]
    out_shape += [pltpu.HBM(x.shape, x.dtype) for x in lands] + [jax.ShapeDtypeStruct((8, LANES), F32)]
    outs = pl.pallas_call(
        body, name=name, out_shape=out_shape,
        in_specs=[hbm] * (2 * n), out_specs=[sem] * (2 * n_g) + [hbm] * n + [pl.BlockSpec(memory_space=pltpu.VMEM)],
        input_output_aliases={n + u: 2 * n_g + u for u in range(n)},
        compiler_params=pltpu.CompilerParams(has_side_effects=_EFFECT),
    )(*[_hbm(x) for x in srcs], *[_hbm(x) for x in lands])
    sem_pairs = [(outs[2 * g], outs[2 * g + 1]) for g in range(n_g)]
    return sem_pairs, list(outs[2 * n_g:2 * n_g + n]), outs[-1]


def _exchange_wait(srcs, lands, specs, sem_pair, after, name):
    n = len(lands)

    def body(*refs):
        land_refs = refs[n:2 * n]
        send_sems, recv_sems = refs[2 * n], refs[2 * n + 1]
        place = _my_place()
        for u in range(n):
            _wait_all(_seven(specs[u], land_refs[u]), send_sems.at[u], recv_sems.at[u], place)

    hbm = pl.BlockSpec(memory_space=pltpu.HBM)
    sem = pl.BlockSpec(memory_space=pltpu.SEMAPHORE)
    outs = pl.pallas_call(
        body, name=name, out_shape=[pltpu.HBM(x.shape, x.dtype) for x in lands],
        in_specs=[hbm] * (2 * n) + [sem, sem, pl.BlockSpec(memory_space=pl.ANY)], out_specs=[hbm] * n,
        input_output_aliases={n + u: u for u in range(n)},
        compiler_params=pltpu.CompilerParams(has_side_effects=_EFFECT),
    )(*[_hbm(x) for x in srcs], *lands, sem_pair[0], sem_pair[1], after)
    return list(outs)


def _all_reduce_small(g, name):
    r, w = g.shape

    def body(g_ref, out_ref, buf_ref, send_sems, recv_sems):
        place = _my_place()
        me = 4 * place[0] + 2 * place[1] + place[2]
        buf_ref[me] = g_ref[...]
        copies = []
        for k in range(1, N_DEV):
            copies.append(pltpu.make_async_remote_copy(
                src_ref=g_ref, dst_ref=buf_ref.at[me],
                send_sem=send_sems.at[k - 1], recv_sem=recv_sems.at[k - 1], device_id=_flip(place, k), device_id_type=MESH))
        for cp in copies:
            cp.start()
        for cp in copies:
            cp.wait()
        acc = buf_ref[0]
        for d in range(1, N_DEV):
            acc = acc + buf_ref[d]
        out_ref[...] = acc

    return pl.pallas_call(
        body, name=name,
        out_shape=jax.ShapeDtypeStruct((r, w), F32),
        in_specs=[pl.BlockSpec(memory_space=pltpu.VMEM)], out_specs=pl.BlockSpec(memory_space=pltpu.VMEM),
        scratch_shapes=[pltpu.VMEM((N_DEV, r, w), F32), pltpu.SemaphoreType.DMA((7,)), pltpu.SemaphoreType.DMA((7,))],
    )(g)


def _adamw_math(w, m, v, g):
    m_new = ADAM_B1 * m + (1.0 - ADAM_B1) * g
    v_new = ADAM_B2 * v + (1.0 - ADAM_B2) * (g * g)
    m_hat = m_new / (1.0 - ADAM_B1 ** ADAM_STEP)
    v_hat = v_new / (1.0 - ADAM_B2 ** ADAM_STEP)
    return -ADAM_LR * (m_hat / (jnp.sqrt(v_hat) + ADAM_EPS) + ADAM_WD * w), m_new, v_new


def _adamw(w, m, v, g, name):
    rows = w.shape[0]
    tr = min(FLAT_ROWS, rows)
    assert rows % tr == 0, (name, rows)

    def body(w_ref, m_ref, v_ref, g_ref, d_out, m_out, v_out):
        d_out[...], m_out[...], v_out[...] = _adamw_math(w_ref[...], m_ref[...], v_ref[...], g_ref[...])

    flat = pl.BlockSpec((tr, LANES), lambda i: (i, 0))
    return pl.pallas_call(
        body, name=name, grid=(rows // tr,), in_specs=[flat] * 4, out_specs=[flat] * 3,
        out_shape=[jax.ShapeDtypeStruct((rows, LANES), F32)] * 3,
        compiler_params=_params("parallel"),
    )(w, m, v, g)


def _adamw_shard(w, m, v, g_own, recv, layer, prev, name):
    depth, a, b = w.shape
    ta = min(256, a)
    assert a % ta == 0

    def body(w_ref, m_ref, v_ref, g_ref, r_ref, *rest):
        g_out, d_out, m_out, v_out = rest[-4:]
        g = g_ref[...]
        for k in range(N_DEV - 1):
            g = g + r_ref[k].astype(F32)
        g_out[0] = g
        d_out[0], m_out[0], v_out[0] = _adamw_math(w_ref[0], m_ref[0], v_ref[0], g)

    lay = pl.BlockSpec((1, ta, b), lambda i: (layer, i, 0))
    in_specs = [lay] * 3 + [pl.BlockSpec((ta, b), lambda i: (i, 0)), pl.BlockSpec((N_DEV - 1, ta, b), lambda i: (0, i, 0))]
    args = [w, m, v, g_own, recv]
    aliases = {}
    if prev is not None:
        in_specs += [pl.BlockSpec(memory_space=pl.ANY)] * 4
        args += list(prev)
        aliases = {5 + i: i for i in range(4)}
    return pl.pallas_call(
        body, name=name, grid=(a // ta,), in_specs=in_specs, out_specs=[lay] * 4,
        out_shape=[jax.ShapeDtypeStruct(w.shape, F32)] * 4, input_output_aliases=aliases,
        compiler_params=_params("parallel"),
    )(*args)


def _round_up(n, mult):
    return (n + mult - 1) // mult * mult


def _flatten(parts, row_mult):
    flat = jnp.concatenate([p.reshape(-1) for p in parts])
    rows = _round_up(-(-flat.shape[0] // LANES), row_mult)
    return jnp.pad(flat, (0, rows * LANES - flat.shape[0])).reshape(rows, LANES)


def _unflatten(flat, shapes):
    flat = flat.reshape(-1)
    out, off = [], 0
    for shp in shapes:
        n = 1
        for dim in shp:
            n *= dim
        out.append(flat[off:off + n].reshape(shp))
        off += n
    return out


def kernel(x, norm1_g, w_in, b_f, q_norm_g, k_norm_g, conv_w, conv_b, conv_ln_g, conv_ln_b, w_o, norm2_g, w_mlp_in, w_mlp_out, loss_target, m_norm1_g, m_w_in, m_b_f, m_q_norm_g, m_k_norm_g, m_conv_w, m_conv_b, m_conv_ln_g, m_conv_ln_b, m_w_o, m_norm2_g, m_w_mlp_in, m_w_mlp_out, v_norm1_g, v_w_in, v_b_f, v_q_norm_g, v_k_norm_g, v_conv_w, v_conv_b, v_conv_ln_g, v_conv_ln_b, v_w_o, v_norm2_g, v_w_mlp_in, v_w_mlp_out):
    depth, d_model, n_in_loc = w_in.shape
    n_heads = b_f.shape[1]
    aw = n_heads * HEAD_DIM
    cc = conv_b.shape[1]
    n_in = n_in_loc * N_DEV
    o_f = 3 * aw
    n_all = 3 * aw + 2 * cc + LANES
    assert n_in == 3 * aw + n_heads + 2 * cc and aw + cc == d_model and n_heads % 2 == 0
    assert aw % LANES == 0 and cc % LANES == 0 and x.shape[0] == 1
    me = 4 * lax.axis_index("x") + 2 * lax.axis_index("y") + lax.axis_index("c")

    d_ff = w_mlp_in.shape[2] * N_DEV

    r_o, f_1, f_2 = w_o.shape[1], w_mlp_in.shape[2], w_mlp_out.shape[1]

    ag_src, ag_land, ag_spec = [], [], []
    for l in range(depth):
        ag_src += [w_in[l].astype(BF16), w_o[l].astype(BF16), w_mlp_in[l].astype(BF16), w_mlp_out[l].astype(BF16)]
        ag_land += [(N_DEV, d_model, n_in_loc), (N_DEV * r_o, d_model), (d_model, N_DEV * f_1), (N_DEV * f_2, d_model)]
        ag_spec += [("gather", "slot", 1), ("gather", "rows", r_o), ("gather", "cols", f_1), ("gather", "rows", f_2)]
    ag_src.append(jnp.stack(_split3(conv_w)))
    ag_land.append((N_DEV, 3) + conv_w.shape)
    ag_spec.append(("gather", "slot", 1))
    ag_groups = [[0, 4 * depth], [1], [2, 3]] + [[4 * l + i for i in range(4)] for l in range(1, depth)]
    ag_land = _place_own(ag_src, [(shp, BF16) for shp in ag_land], ag_spec, "place_own_shards")
    ag_sems, ag_land, ag_token = _exchange_start(ag_src, ag_land, ag_spec, ag_groups, "gather_start")

    def gathered(g, after):
        units = ag_groups[g]
        return _exchange_wait([ag_src[u] for u in units], [ag_land[u] for u in units], [ag_spec[u] for u in units],
                              ag_sems[g], after, f"gather_wait_{g}")

    def to_all(w):
        return jnp.concatenate([w[:, :o_f], w[:, o_f + n_heads:], w[:, o_f:o_f + n_heads],
                                jnp.zeros((w.shape[0], LANES - n_heads), w.dtype)], axis=1)

    def from_all(w):
        return jnp.concatenate([w[:, :o_f], w[:, n_all - LANES:n_all - LANES + n_heads], w[:, o_f:n_all - LANES]], axis=1)

    def whole_in(lin):
        return to_all(jnp.moveaxis(lin, 0, 1).reshape(d_model, n_in))

    def row(p, l, width=None):
        v = p[l].reshape(1, -1)
        return v if width is None else jnp.pad(v, ((0, 0), (0, width - v.shape[1])))

    a_col, g_col = 3 * aw // LANES, (3 * aw + cc) // LANES

    gq = [jnp.tile(row(q_norm_g, l), (1, n_heads)) for l in range(depth)]
    gk = [jnp.tile(row(k_norm_g, l), (1, n_heads)) for l in range(depth)]
    bfp = [row(b_f, l, LANES) for l in range(depth)]
    add_res = lambda acc, res: (acc + res,)
    w_all, w_out, w_ff1, w_ff2 = [None] * depth, [None] * depth, [None] * depth, [None] * depth

    h = x[0]
    saved = []
    for l in range(depth):
        u1 = _rms_fwd(h, row(norm1_g, l) + ag_token[0, 0] if l == 0 else row(norm1_g, l), f"rms1_fwd_{l}")
        if l == 0:
            lin, lc = gathered(0, u1)
            lc = lc.astype(F32)
            conv_full = jnp.moveaxis(lc[:, 0] + lc[:, 1] + lc[:, 2], 0, 2).reshape(depth, CONV_TAPS, cc)
            w32 = [jnp.pad(conv_full[i], ((0, CONV_PAD - CONV_TAPS), (0, 0))) for i in range(depth)]
        else:
            lin, w_out[l], w_ff1[l], w_ff2[l] = gathered(2 + l, u1)
        w_all[l] = whole_in(lin)
        proj = _matmul(u1, w_all[l], mode="nn", out_dtypes=(F32,), name=f"mm_in_{l}", tn=n_all)
        qa, ka, vb = _prep_fwd(proj, gq[l], gk[l], bfp[l], n_heads, f"prep_fwd_{l}")
        att, lse = _attn_fwd(qa, ka, vb, n_heads, f"attn_fwd_{l}")
        yc = _conv_fwd(proj, w32[l], row(conv_b, l), cc, a_col, g_col, f"conv_fwd_{l}")
        hc = _ln_silu_fwd(yc, row(conv_ln_g, l), row(conv_ln_b, l), f"ln_silu_fwd_{l}")
        mixed = jnp.concatenate([att.astype(BF16), hc], axis=1)
        if l == 0:
            w_out[l], = gathered(1, mixed)
        x1 = _matmul(mixed, w_out[l], mode="nn", out_dtypes=(F32,), name=f"mm_o_{l}", epilogue=add_res, extras=(h,))
        u2 = _rms_fwd(x1, row(norm2_g, l), f"rms2_fwd_{l}")
        if l == 0:
            w_ff1[l], w_ff2[l] = gathered(2, u2)
        r, a = _matmul(u2, w_ff1[l], mode="nn", out_dtypes=(BF16, BF16), name=f"mm_ff1_{l}", tm=256, tn=d_ff,
                       epilogue=lambda acc: (jnp.maximum(acc, 0.0), jnp.square(jnp.maximum(acc, 0.0))))
        x2 = _matmul(a, w_ff2[l], mode="nn", out_dtypes=(F32,), name=f"mm_ff2_{l}", epilogue=add_res, extras=(x1,), tk=d_ff)
        saved.append(dict(x_in=h, u1=u1, proj=proj, qa=qa, ka=ka, vb=vb, att=att, lse=lse, yc=yc, mixed=mixed,
                          x1=x1, u2=u2, r=r, a=a))
        h = x2

    dh, dh16, sq = _loss_grad(h, loss_target[0], "loss_grad")
    loss = lax.psum(0.5 * jnp.sum(sq) / d_model, ("x", "y", "c"))

    g_in, g_in_own = [None] * depth, [None] * depth
    g_o, g_1, g_2 = [None] * depth, [None] * depth, [None] * depth
    gs = {n: [None] * depth for n in ("norm1", "bf", "qn", "kn", "convw", "convb", "lng", "lnb", "norm2")}
    both = (F32, BF16)
    scattering = {}

    def scatter_start(stage, l, srcs, specs, slabs):
        lands = [lax.empty((N_DEV - 1,) + shp, BF16) for shp in slabs]
        sems, lands, token = _exchange_start(srcs, lands, specs, [list(range(len(srcs)))], f"scatter_start_{stage}_{l}")
        scattering[(stage, l)] = (srcs, lands, specs, sems[0])
        return token[0, 0]

    for l in reversed(range(depth)):
        sv = saved[l]
        dh1 = _matmul(dh16, w_ff2[l], mode="nt", out_dtypes=(BF16,), name=f"mm_dff2_{l}", tm=256, tn=d_ff,
                      epilogue=lambda acc, rr: (acc * (2.0 * rr.astype(F32)),), extras=(sv["r"],))
        g_2[l] = _matmul(sv["a"], dh16, mode="tn", out_dtypes=both, name=f"mm_dw2_{l}", tm=2048, tk=512)
        g_1[l] = _matmul(sv["u2"], dh1, mode="tn", out_dtypes=both, name=f"mm_dw1_{l}", tm=1024, tn=2048, tk=512)
        tok = scatter_start("ff", l, [g_1[l][1], g_2[l][1]], [("scatter", "cols", f_1), ("scatter", "rows", f_2)],
                            [(d_model, f_1), (f_2, d_model)])
        du2 = _matmul(dh1, w_ff1[l], mode="nt", out_dtypes=(F32,), name=f"mm_du2_{l}", tk=d_ff)
        dx1, dx16, gs["norm2"][l] = _rms_bwd(sv["x1"], row(norm2_g, l) + tok, du2, dh, f"rms2_bwd_{l}")

        dmixed = _matmul(dx16, w_out[l], mode="nt", out_dtypes=(F32,), name=f"mm_dmixed_{l}")
        g_o[l] = _matmul(sv["mixed"], dx16, mode="tn", out_dtypes=both, name=f"mm_dwo_{l}", tm=1024)
        dyc, gs["lng"][l], gs["lnb"][l] = _ln_silu_bwd(sv["yc"], row(conv_ln_g, l), row(conv_ln_b, l), dmixed, f"ln_silu_bwd_{l}")
        dpa, dpg, dw32 = _conv_bwd(sv["proj"], w32[l], dyc, cc, a_col, g_col, f"conv_bwd_{l}")
        gs["convw"][l], gs["convb"][l] = dw32[:CONV_TAPS], dw32[CONV_TAPS:CONV_TAPS + 1]
        dq, dka, dv, drow, dcol = _attn_bwd(sv["qa"], sv["ka"], sv["vb"], sv["att"], sv["lse"], dmixed, n_heads, f"attn_bwd_{l}")
        dpq, dpk, dpf, dgq, dgk, dbf = _prep_bwd(sv["proj"], dq, dka, drow, dcol, gq[l], gk[l], bfp[l], n_heads, f"prep_bwd_{l}")
        gs["qn"][l] = dgq.reshape(n_heads, HEAD_DIM).sum(axis=0)
        gs["kn"][l] = dgk.reshape(n_heads, HEAD_DIM).sum(axis=0)
        gs["bf"][l] = dbf[0, :n_heads]
        dproj = jnp.concatenate([dpq, dpk, dv.astype(BF16), dpa, dpg, dpf], axis=1)
        du1 = _matmul(dproj, w_all[l], mode="nt", out_dtypes=(F32,), name=f"mm_du1_{l}", tk=n_all)
        dwall, dwall16 = _matmul(sv["u1"], dproj, mode="tn", out_dtypes=both, name=f"mm_dwall_{l}", tm=1024, tn=n_all // 3)
        g_in[l] = jnp.moveaxis(from_all(dwall16).reshape(d_model, N_DEV, n_in_loc), 1, 0)
        g_in_own[l] = lax.dynamic_slice_in_dim(from_all(dwall), me * n_in_loc, n_in_loc, axis=1)
        tok = scatter_start("mix", l, [g_in[l], g_o[l][1]], [("scatter", "slot", 1), ("scatter", "rows", r_o)],
                            [(d_model, n_in_loc), (r_o, d_model)])
        dh, dh16, gs["norm1"][l] = _rms_bwd(sv["x_in"], row(norm1_g, l) + tok, du1, dx1, f"rms1_bwd_{l}")
    grad_x = dh[None]

    def landed(stage, l, after):
        srcs, lands, specs, sems = scattering[(stage, l)]
        return _exchange_wait(srcs, lands, specs, sems, after, f"scatter_wait_{stage}_{l}")

    def adamw_layers(kd, w, m, v, own, recv):
        outs = None
        for l in reversed(range(depth)):
            outs = _adamw_shard(w, m, v, own[l], recv[l], l, outs, f"adamw_{kd}_{l}")
        return outs

    recv_1, recv_2, recv_in, recv_o = [None] * depth, [None] * depth, [None] * depth, [None] * depth
    for l in reversed(range(depth)):
        recv_1[l], recv_2[l] = landed("ff", l, dh)
    out_1 = adamw_layers("1", w_mlp_in, m_w_mlp_in, v_w_mlp_in,
                         [lax.dynamic_slice_in_dim(g_1[l][0], me * f_1, f_1, axis=1) for l in range(depth)], recv_1)
    out_2 = adamw_layers("2", w_mlp_out, m_w_mlp_out, v_w_mlp_out,
                         [lax.dynamic_slice_in_dim(g_2[l][0], me * f_2, f_2, axis=0) for l in range(depth)], recv_2)
    for l in reversed(range(depth)):
        recv_in[l], recv_o[l] = landed("mix", l, out_2[1])
    out_in = adamw_layers("in", w_in, m_w_in, v_w_in, g_in_own, recv_in)
    out_o = adamw_layers("o", w_o, m_w_o, v_w_o,
                         [lax.dynamic_slice_in_dim(g_o[l][0], me * r_o, r_o, axis=0) for l in range(depth)], recv_o)
    big_out = [[outs[kind] for outs in (out_in, out_o, out_1, out_2)] for kind in range(4)]

    small_g = [jnp.stack(gs[n]).reshape(shp) for n, shp in (
        ("norm1", norm1_g.shape), ("bf", b_f.shape), ("qn", q_norm_g.shape), ("kn", k_norm_g.shape),
        ("convw", (depth, CONV_TAPS, cc)), ("convb", conv_b.shape), ("lng", conv_ln_g.shape), ("lnb", conv_ln_b.shape),
        ("norm2", norm2_g.shape))]
    small_shapes = [g.shape for g in small_g]
    small_g = _unflatten(_all_reduce_small(_flatten(small_g, 8), "all_reduce_small"), small_shapes)
    cw = conv_w.shape[2]
    small_g[4] = lax.dynamic_slice_in_dim(small_g[4], me * cw, cw, axis=2)
    small = (norm1_g, b_f, q_norm_g, k_norm_g, conv_w, conv_b, conv_ln_g, conv_ln_b, norm2_g)
    small_m = (m_norm1_g, m_b_f, m_q_norm_g, m_k_norm_g, m_conv_w, m_conv_b, m_conv_ln_g, m_conv_ln_b, m_norm2_g)
    small_v = (v_norm1_g, v_b_f, v_q_norm_g, v_k_norm_g, v_conv_w, v_conv_b, v_conv_ln_g, v_conv_ln_b, v_norm2_g)
    small_out = _adamw(_flatten(small, 8), _flatten(small_m, 8), _flatten(small_v, 8), _flatten(small_g, 8), "adamw_small")
    small_out = [small_g] + [_unflatten(o, [w.shape for w in small]) for o in small_out]

    def group(kind):
        s_, b_ = small_out[kind], big_out[kind]
        return [s_[0], b_[0], s_[1], s_[2], s_[3], s_[4], s_[5], s_[6], s_[7], b_[1], s_[8], b_[2], b_[3]]

    return (loss, grad_x, *group(0), *group(1), *group(2), *group(3))
```

```python
import functools

import jax
import jax.numpy as jnp
from jax import lax
from jax.experimental import pallas as pl
from jax.experimental.pallas import tpu as pltpu

F32 = jnp.float32
BF16 = jnp.bfloat16

EPS = 1e-6
HEAD_DIM = 64
LANES = 128
PAIR = 2 * LANES
N_DEV = 8
CONV_TAPS = 31
CONV_PAD = 32
NEG = -1e30

ADAM_LR = 0.001
ADAM_B1 = 0.9
ADAM_B2 = 0.999
ADAM_EPS = 1e-08
ADAM_WD = 0.01
ADAM_STEP = 10

TM = 512
TQ = 512
CONV_ROWS = 128
FLAT_ROWS = 1024
MESH = pl.DeviceIdType.MESH


def _params(*sem):
    return pltpu.CompilerParams(dimension_semantics=sem, vmem_limit_bytes=56 * 1024 * 1024)


def _split3(x):
    hi = x.astype(BF16)
    r1 = x - hi.astype(F32)
    mid = r1.astype(BF16)
    lo = (r1 - mid.astype(F32)).astype(BF16)
    return hi, mid, lo


def _dot(a, b):
    return jnp.dot(a, b, preferred_element_type=F32)


def _dot_nt(a, b):
    return lax.dot_general(a, b, (((1,), (1,)), ((), ())), preferred_element_type=F32)


def _dot_tn(a, b):
    return lax.dot_general(a, b, (((0,), (0,)), ((), ())), preferred_element_type=F32)


def _dot3(x, mat):
    hi, mid, lo = _split3(x)
    return _dot(hi, mat) + _dot(mid, mat) + _dot(lo, mat)


def _dot3_r(mat, x):
    hi, mid, lo = _split3(x)
    return _dot(mat, hi) + _dot(mat, mid) + _dot(mat, lo)


def _iota(shape, dim):
    return lax.broadcasted_iota(jnp.int32, shape, dim)


def _sigmoid(x):
    return 1.0 / (1.0 + jnp.exp(-x))


def _matmul(a, b, *, mode, out_dtypes, name, epilogue=None, extras=(), tm=TM, tn=1024, tk=1024):
    if mode == "nn":
        (m, k), (k2, n) = a.shape, b.shape
    elif mode == "nt":
        (m, k), (n, k2) = a.shape, b.shape
    else:
        (k, m), (k2, n) = a.shape, b.shape
    assert k == k2, (name, a.shape, b.shape)
    tm, tn, tk = min(tm, m), min(tn, n), min(tk, k)
    assert m % tm == 0 and n % tn == 0 and k % tk == 0, (name, m, n, k, tm, tn, tk)
    nk = k // tk
    if mode == "tn":
        a_spec = pl.BlockSpec((tk, tm), lambda i, j, kk: (kk, i))
    else:
        a_spec = pl.BlockSpec((tm, tk), lambda i, j, kk: (i, kk))
    b_mode = dict(pipeline_mode=pl.Buffered(1)) if (n == tn and nk == 1) else {}
    if mode == "nt":
        b_spec = pl.BlockSpec((tn, tk), lambda i, j, kk: (j, kk), **b_mode)
    else:
        b_spec = pl.BlockSpec((tk, tn), lambda i, j, kk: (kk, j), **b_mode)
    dot = {"nn": _dot, "nt": _dot_nt, "tn": _dot_tn}[mode]
    tile = pl.BlockSpec((tm, tn), lambda i, j, kk: (i, j))
    n_ex, n_out = len(extras), len(out_dtypes)
    acc_in_out = nk > 1 and epilogue is None and out_dtypes[0] == F32

    def body(a_ref, b_ref, *rest):
        ex_refs, out_refs = rest[:n_ex], rest[n_ex:n_ex + n_out]
        part = dot(a_ref[...].astype(BF16), b_ref[...].astype(BF16))

        def finish(acc):
            res = epilogue(acc, *[e[...] for e in ex_refs]) if epilogue is not None else (acc,) * n_out
            for o_ref, r in zip(out_refs, res):
                o_ref[...] = r.astype(o_ref.dtype)

        if nk == 1:
            finish(part)
        else:
            acc_ref = out_refs[0] if acc_in_out else rest[-1]
            kk = pl.program_id(2)

            @pl.when(kk == 0)
            def _():
                acc_ref[...] = part

            @pl.when(kk > 0)
            def _():
                acc_ref[...] += part

            @pl.when(kk == nk - 1)
            def _():
                if acc_in_out:
                    for o_ref in out_refs[1:]:
                        o_ref[...] = acc_ref[...].astype(o_ref.dtype)
                else:
                    finish(acc_ref[...])

    outs = pl.pallas_call(
        body,
        name=name,
        grid=(m // tm, n // tn, nk),
        in_specs=[a_spec, b_spec] + [tile] * n_ex,
        out_specs=[tile] * n_out,
        out_shape=[jax.ShapeDtypeStruct((m, n), dt) for dt in out_dtypes],
        scratch_shapes=[pltpu.VMEM((tm, tn), F32)] if nk > 1 and not acc_in_out else [],
        compiler_params=_params("parallel", "parallel", "arbitrary"),
    )(a, b, *extras)
    return outs if n_out > 1 else outs[0]


def _rms_fwd(x, g, name):
    s, d = x.shape
    ts = min(TM, s)

    def body(x_ref, g_ref, u_ref):
        xv = x_ref[...]
        y = xv * lax.rsqrt(jnp.mean(xv * xv, axis=-1, keepdims=True) + EPS)
        u_ref[...] = (y * g_ref[...]).astype(BF16)

    return pl.pallas_call(
        body, name=name, grid=(s // ts,),
        in_specs=[pl.BlockSpec((ts, d), lambda i: (i, 0)), pl.BlockSpec((1, d), lambda i: (0, 0))],
        out_specs=pl.BlockSpec((ts, d), lambda i: (i, 0)),
        out_shape=jax.ShapeDtypeStruct((s, d), BF16),
        compiler_params=_params("parallel"),
    )(x, g)


def _rms_bwd(x, g, du, dres, name):
    s, d = x.shape
    ts = min(TM, s)

    def body(x_ref, g_ref, du_ref, dres_ref, dx_ref, dx16_ref, dg_ref):
        @pl.when(pl.program_id(0) == 0)
        def _():
            dg_ref[...] = jnp.zeros_like(dg_ref)

        xv, duv = x_ref[...], du_ref[...]
        r = lax.rsqrt(jnp.mean(xv * xv, axis=-1, keepdims=True) + EPS)
        xh = xv * r
        dxh = duv * g_ref[...]
        dx = dres_ref[...] + r * (dxh - xh * jnp.mean(dxh * xh, axis=-1, keepdims=True))
        dx_ref[...] = dx
        dx16_ref[...] = dx.astype(BF16)
        dg_ref[...] += jnp.sum(duv * xh, axis=0, keepdims=True)

    row = pl.BlockSpec((ts, d), lambda i: (i, 0))
    vec = pl.BlockSpec((1, d), lambda i: (0, 0))
    return pl.pallas_call(
        body, name=name, grid=(s // ts,),
        in_specs=[row, vec, row, row], out_specs=[row, row, vec],
        out_shape=[jax.ShapeDtypeStruct((s, d), F32), jax.ShapeDtypeStruct((s, d), BF16), jax.ShapeDtypeStruct((1, d), F32)],
        compiler_params=_params("arbitrary"),
    )(x, g, du, dres)


def _ln_silu_fwd(y, g, b, name):
    s, c = y.shape
    ts = min(TM, s)

    def body(y_ref, g_ref, b_ref, h_ref):
        yv = y_ref[...]
        mu = jnp.mean(yv, axis=-1, keepdims=True)
        yc = yv - mu
        z = yc * lax.rsqrt(jnp.mean(yc * yc, axis=-1, keepdims=True) + EPS) * g_ref[...] + b_ref[...]
        h_ref[...] = (z * _sigmoid(z)).astype(BF16)

    row = pl.BlockSpec((ts, c), lambda i: (i, 0))
    vec = pl.BlockSpec((1, c), lambda i: (0, 0))
    return pl.pallas_call(
        body, name=name, grid=(s // ts,), in_specs=[row, vec, vec], out_specs=row,
        out_shape=jax.ShapeDtypeStruct((s, c), BF16), compiler_params=_params("parallel"),
    )(y, g, b)


def _ln_silu_bwd(y, g, b, dmixed, name):
    s, c = y.shape
    ts = min(TM, s)

    def body(y_ref, g_ref, b_ref, dh_ref, dy_ref, dg_ref, db_ref):
        @pl.when(pl.program_id(0) == 0)
        def _():
            dg_ref[...] = jnp.zeros_like(dg_ref)
            db_ref[...] = jnp.zeros_like(db_ref)

        yv = y_ref[...]
        mu = jnp.mean(yv, axis=-1, keepdims=True)
        yc = yv - mu
        r = lax.rsqrt(jnp.mean(yc * yc, axis=-1, keepdims=True) + EPS)
        yh = yc * r
        z = yh * g_ref[...] + b_ref[...]
        sg = _sigmoid(z)
        dz = dh_ref[...] * (sg * (1.0 + z * (1.0 - sg)))
        dg_ref[...] += jnp.sum(dz * yh, axis=0, keepdims=True)
        db_ref[...] += jnp.sum(dz, axis=0, keepdims=True)
        dyh = dz * g_ref[...]
        dy_ref[...] = r * (dyh - jnp.mean(dyh, axis=-1, keepdims=True) - yh * jnp.mean(dyh * yh, axis=-1, keepdims=True))

    row = pl.BlockSpec((ts, c), lambda i: (i, 0))
    vec = pl.BlockSpec((1, c), lambda i: (0, 0))
    return pl.pallas_call(
        body, name=name, grid=(s // ts,),
        in_specs=[row, vec, vec, pl.BlockSpec((ts, c), lambda i: (i, 1))], out_specs=[row, vec, vec],
        out_shape=[jax.ShapeDtypeStruct((s, c), F32), jax.ShapeDtypeStruct((1, c), F32), jax.ShapeDtypeStruct((1, c), F32)],
        compiler_params=_params("arbitrary"),
    )(y, g, b, dmixed)


def _loss_grad(y, target, name):
    s, d = y.shape
    ts = min(TM, s)

    def body(y_ref, t_ref, dy_ref, dy16_ref, sq_ref):
        @pl.when(pl.program_id(0) == 0)
        def _():
            sq_ref[...] = jnp.zeros_like(sq_ref)

        err = y_ref[...] - t_ref[...]
        dy = err * (1.0 / d)
        dy_ref[...] = dy
        dy16_ref[...] = dy.astype(BF16)
        sq_ref[...] += jnp.sum(err * err, axis=0, keepdims=True)

    row = pl.BlockSpec((ts, d), lambda i: (i, 0))
    vec = pl.BlockSpec((1, d), lambda i: (0, 0))
    return pl.pallas_call(
        body, name=name, grid=(s // ts,), in_specs=[row, row], out_specs=[row, row, vec],
        out_shape=[jax.ShapeDtypeStruct((s, d), F32), jax.ShapeDtypeStruct((s, d), BF16), jax.ShapeDtypeStruct((1, d), F32)],
        compiler_params=_params("arbitrary"),
    )(y, target)


def _head_masks():
    lane2 = _iota((1, PAIR), 1)
    lane1 = _iota((1, LANES), 1)
    qa = (lane2 < HEAD_DIM) | ((lane2 >= LANES) & (lane2 < LANES + 3))
    qb = ((lane2 >= HEAD_DIM) & (lane2 < LANES)) | ((lane2 >= LANES + 3) & (lane2 < LANES + 6))
    return (qa, qb), (lane1 < HEAD_DIM, lane1 >= HEAD_DIM)


def _group_matrix(width):
    shift = HEAD_DIM.bit_length() - 1
    return ((_iota((width, width), 0) >> shift) == (_iota((width, width), 1) >> shift)).astype(BF16)


def _prep_fwd(proj, gq, gk, bf, n_heads, name):
    s = proj.shape[0]
    aw = n_heads * HEAD_DIM
    n_pairs = n_heads // 2
    ts = min(TM, s)
    f_col = (proj.shape[1] - LANES) // LANES

    def body(q_ref, k_ref, v_ref, f_ref, gq_ref, gk_ref, bf_ref, qa_ref, ka_ref, vb_ref, carry_ref):
        @pl.when(pl.program_id(0) == 0)
        def _():
            carry_ref[...] = jnp.zeros_like(carry_ref)

        gmat = _group_matrix(aw)

        def head_norm(xv, g):
            ms = _dot3(xv * xv, gmat) * (1.0 / HEAD_DIM)
            return xv * lax.rsqrt(ms + EPS) * g

        qn = head_norm(q_ref[...], gq_ref[...]) * (HEAD_DIM ** -0.5)
        kn = head_norm(k_ref[...], gk_ref[...])
        z = f_ref[...] + bf_ref[...]
        logf = jnp.minimum(z, 0.0) - jnp.log(1.0 + jnp.exp(-jnp.abs(z)))
        tri = (_iota((ts, ts), 0) >= _iota((ts, ts), 1)).astype(BF16)
        c = _dot3_r(tri, logf) + carry_ref[...]
        carry_ref[...] = c[ts - 1:ts, :]
        terms = _split3(-c)
        row, col = _iota((LANES, LANES), 0), _iota((LANES, LANES), 1)
        ones = jnp.where(_iota((ts, LANES), 1) < 6, 1.0, 0.0).astype(BF16)
        for p in range(n_pairs):
            extra = jnp.zeros((ts, LANES), F32)
            for t, term in enumerate(terms):
                sel = ((row == 2 * p) & (col == t)) | ((row == 2 * p + 1) & (col == 3 + t))
                extra += _dot(term, sel.astype(BF16))
            lo, hi = p * PAIR, p * PAIR + LANES
            ka_ref[:, lo:hi] = kn[:, p * LANES:(p + 1) * LANES].astype(BF16)
            ka_ref[:, hi:hi + LANES] = extra.astype(BF16)
            qa_ref[:, lo:hi] = qn[:, p * LANES:(p + 1) * LANES].astype(BF16)
            qa_ref[:, hi:hi + LANES] = ones
            vb_ref[:, lo:hi] = v_ref[:, p * LANES:(p + 1) * LANES].astype(BF16)
            vb_ref[:, hi:hi + LANES] = jnp.ones((ts, LANES), BF16)

    blk = lambda j: pl.BlockSpec((ts, aw), lambda i: (i, j))
    vec = lambda w: pl.BlockSpec((1, w), lambda i: (0, 0))
    return pl.pallas_call(
        body, name=name, grid=(s // ts,),
        in_specs=[blk(0), blk(1), blk(2), pl.BlockSpec((ts, LANES), lambda i: (i, f_col)), vec(aw), vec(aw), vec(LANES)],
        out_specs=[pl.BlockSpec((ts, n_pairs * PAIR), lambda i: (i, 0))] * 3,
        out_shape=[jax.ShapeDtypeStruct((s, n_pairs * PAIR), BF16)] * 3,
        scratch_shapes=[pltpu.VMEM((1, LANES), F32)],
        compiler_params=_params("arbitrary"),
    )(proj, proj, proj, proj, gq, gk, bf)


def _prep_bwd(proj, dq, dka, drow, dcol, gq, gk, bf, n_heads, name):
    s = proj.shape[0]
    aw = n_heads * HEAD_DIM
    n_pairs = n_heads // 2
    ts = min(TM, s)
    nt = s // ts
    f_col = (proj.shape[1] - LANES) // LANES
    shift = HEAD_DIM.bit_length() - 1

    def body(q_ref, k_ref, f_ref, dq_ref, dka_ref, drow_ref, dcol_ref, gq_ref, gk_ref, bf_ref,
             dpq_ref, dpk_ref, dpf_ref, dgq_ref, dgk_ref, dbf_ref, carry_ref):
        @pl.when(pl.program_id(0) == 0)
        def _():
            carry_ref[...] = jnp.zeros_like(carry_ref)
            dgq_ref[...] = jnp.zeros_like(dgq_ref)
            dgk_ref[...] = jnp.zeros_like(dgk_ref)
            dbf_ref[...] = jnp.zeros_like(dbf_ref)

        gmat = _group_matrix(aw)

        def head_norm_bwd(xv, g, dn):
            r = lax.rsqrt(_dot3(xv * xv, gmat) * (1.0 / HEAD_DIM) + EPS)
            xh = xv * r
            dxh = dn * g
            dx = r * (dxh - xh * (_dot3(dxh * xh, gmat) * (1.0 / HEAD_DIM)))
            return dx, jnp.sum(dn * xh, axis=0, keepdims=True)

        dkav = dka_ref[...]
        dx, dg = head_norm_bwd(q_ref[...], gq_ref[...], dq_ref[...] * (HEAD_DIM ** -0.5))
        dpq_ref[...] = dx.astype(BF16)
        dgq_ref[...] += dg
        dkn = jnp.concatenate([dkav[:, p * PAIR:p * PAIR + LANES] for p in range(n_pairs)], axis=1)
        dx, dg = head_norm_bwd(k_ref[...], gk_ref[...], dkn)
        dpk_ref[...] = dx.astype(BF16)
        dgk_ref[...] += dg

        pick = (_iota((aw, LANES), 0) == (_iota((aw, LANES), 1) << shift)).astype(BF16)
        dc = _dot3(drow_ref[...], pick)
        r16, c16 = _iota((16, LANES), 0), _iota((16, LANES), 1)
        for p in range(n_pairs):
            place = ((r16 < 2) & (c16 == 2 * p + r16)).astype(BF16)
            for term in _split3(dcol_ref[p]):
                dc -= _dot_tn(term, place)
        triu = (_iota((ts, ts), 0) <= _iota((ts, ts), 1)).astype(BF16)
        dlogf = _dot3_r(triu, dc) + carry_ref[...]
        carry_ref[...] = dlogf[0:1, :]
        z = f_ref[...] + bf_ref[...]
        dz = dlogf * (1.0 / (1.0 + jnp.exp(z)))
        dpf_ref[...] = dz.astype(BF16)
        dbf_ref[...] += jnp.sum(dz, axis=0, keepdims=True)

    rev = lambda w, j: pl.BlockSpec((ts, w), lambda i: (nt - 1 - i, j))
    vec = lambda w: pl.BlockSpec((1, w), lambda i: (0, 0))
    return pl.pallas_call(
        body, name=name, grid=(nt,),
        in_specs=[rev(aw, 0), rev(aw, 1), rev(LANES, f_col), rev(aw, 0), rev(n_pairs * PAIR, 0), rev(aw, 0),
                  pl.BlockSpec((n_pairs, 16, ts), lambda i: (0, 0, nt - 1 - i)), vec(aw), vec(aw), vec(LANES)],
        out_specs=[rev(aw, 0), rev(aw, 0), rev(LANES, 0), vec(aw), vec(aw), vec(LANES)],
        out_shape=[jax.ShapeDtypeStruct((s, aw), BF16), jax.ShapeDtypeStruct((s, aw), BF16), jax.ShapeDtypeStruct((s, LANES), BF16),
                   jax.ShapeDtypeStruct((1, aw), F32), jax.ShapeDtypeStruct((1, aw), F32), jax.ShapeDtypeStruct((1, LANES), F32)],
        scratch_shapes=[pltpu.VMEM((1, LANES), F32)],
        compiler_params=_params("arbitrary"),
    )(proj, proj, proj, dq, dka, drow, dcol, gq, gk, bf)


def _attn_fwd(qa, ka, vb, n_heads, name):
    s = qa.shape[0]
    aw = n_heads * HEAD_DIM
    n_pairs = n_heads // 2
    tq = min(TQ, s)

    def body(q_ref, k_ref, v_ref, o_ref, lse_ref):
        i = pl.program_id(1)
        qmasks, omasks = _head_masks()
        qv = q_ref[...]
        causal = _iota((tq, tq), 1) <= _iota((tq, tq), 0)
        qhs = [jnp.where(qmasks[h], qv, jnp.zeros_like(qv)) for h in range(2)]

        def scores(j):
            kv = k_ref[pl.ds(pl.multiple_of(j * tq, tq), tq), :]
            return tuple(_dot_nt(qhs[h], kv) for h in range(2))

        def update(j, state, scs, masked):
            vv = v_ref[pl.ds(pl.multiple_of(j * tq, tq), tq), :]
            out = []
            for h in range(2):
                m, acc = state[h]
                sc = jnp.where(causal, scs[h], NEG) if masked else scs[h]
                m_new = jnp.maximum(m, jnp.max(sc, axis=1, keepdims=True))
                p = jnp.exp(sc - m_new).astype(BF16)
                out.append((m_new, jnp.exp(m - m_new) * acc + _dot(p, vv)))
            return tuple(out)

        def body(j, state):
            return update(j, state, scores(j), False)

        init = ((jnp.full((tq, 1), NEG, F32), jnp.zeros((tq, PAIR), F32)),) * 2
        state = lax.fori_loop(0, i, body, init)
        res = []
        for m, acc in update(i, state, scores(i), True):
            l = acc[:, LANES:LANES + 1]
            res.append((acc[:, :LANES] * (1.0 / l), m + jnp.log(l)))
        o_ref[...] = jnp.where(omasks[0], res[0][0], res[1][0])
        lse_ref[...] = jnp.where(omasks[0], res[0][1], res[1][1])

    return pl.pallas_call(
        body, name=name, grid=(n_pairs, s // tq),
        in_specs=[pl.BlockSpec((tq, PAIR), lambda p, i: (i, p)), pl.BlockSpec((s, PAIR), lambda p, i: (0, p)),
                  pl.BlockSpec((s, PAIR), lambda p, i: (0, p))],
        out_specs=[pl.BlockSpec((tq, LANES), lambda p, i: (i, p))] * 2,
        out_shape=[jax.ShapeDtypeStruct((s, aw), F32)] * 2,
        compiler_params=_params("parallel", "parallel"),
    )(qa, ka, vb)


def _attn_bwd(qa, ka, vb, o, lse, dmixed, n_heads, name):
    s = qa.shape[0]
    aw = n_heads * HEAD_DIM
    n_pairs = n_heads // 2
    tq = min(TQ, s)
    nq = s // tq

    def body(q_ref, k_ref, v_ref, o_ref, lse_ref, do_ref, dq_ref, dka_ref, dv_ref, drow_ref, dcol_ref, delta_ref):
        j = pl.program_id(1)
        qmasks, omasks = _head_masks()

        @pl.when(j == 0)
        def _():
            dq_ref[...] = jnp.zeros_like(dq_ref)
            drow_ref[...] = jnp.zeros_like(drow_ref)
            for c in range(nq):
                rows = slice(c * tq, (c + 1) * tq)
                prod = do_ref[rows, :] * o_ref[rows, :]
                da = jnp.sum(jnp.where(omasks[0], prod, 0.0), axis=1, keepdims=True)
                db = jnp.sum(jnp.where(omasks[1], prod, 0.0), axis=1, keepdims=True)
                delta_ref[rows, :] = jnp.where(omasks[0], da, db)

        dka_ref[...] = jnp.zeros_like(dka_ref)
        dv_ref[...] = jnp.zeros_like(dv_ref)
        dcol_ref[...] = jnp.zeros_like(dcol_ref)
        kv = k_ref[...]
        kk = kv[:, :LANES]
        vv = v_ref[...]
        causal = _iota((tq, tq), 1) <= _iota((tq, tq), 0)

        def step(i, masked):
            off = pl.multiple_of(i * tq, tq)
            qv = q_ref[pl.ds(off, tq), :]
            dov = do_ref[pl.ds(off, tq), :]
            lsev = lse_ref[pl.ds(off, tq), :]
            dlv = delta_ref[pl.ds(off, tq), :]
            for h in range(2):
                qh = jnp.where(qmasks[h], qv, jnp.zeros_like(qv))
                doh = jnp.where(omasks[h], dov, 0.0).astype(BF16)
                lane = h * HEAD_DIM
                sc = _dot_nt(qh, kv)
                if masked:
                    sc = jnp.where(causal, sc, NEG)
                p = jnp.exp(sc - lsev[:, lane:lane + 1])
                dv_ref[...] += _dot_tn(p.astype(BF16), doh)
                dp = _dot_nt(doh, vv)
                dsf = p * (dp - dlv[:, lane:lane + 1])
                drow_ref[pl.ds(off, tq), :] += jnp.where(omasks[h], jnp.sum(dsf, axis=1, keepdims=True), 0.0)
                dcol_ref[0, h:h + 1, :] += jnp.sum(dsf, axis=0, keepdims=True)
                ds = dsf.astype(BF16)
                dka_ref[...] += _dot_tn(ds, qh)
                dq_ref[pl.ds(off, tq), :] += jnp.where(omasks[h], _dot(ds, kk), 0.0)

        step(j, True)

        def loop_body(i, carry):
            step(i, False)
            return carry

        lax.fori_loop(j + 1, nq, loop_body, 0)

    full = lambda w: pl.BlockSpec((s, w), lambda p, j: (0, p))
    blk = lambda w: pl.BlockSpec((tq, w), lambda p, j: (j, p))
    return pl.pallas_call(
        body, name=name, grid=(n_pairs, nq),
        in_specs=[full(PAIR), blk(PAIR), pl.BlockSpec((tq, LANES), lambda p, j: (j, 2 * p)), full(LANES), full(LANES), full(LANES)],
        out_specs=[full(LANES), blk(PAIR), blk(LANES), full(LANES), pl.BlockSpec((1, 16, tq), lambda p, j: (p, 0, j))],
        out_shape=[jax.ShapeDtypeStruct((s, aw), F32), jax.ShapeDtypeStruct((s, n_pairs * PAIR), F32),
                   jax.ShapeDtypeStruct((s, aw), F32), jax.ShapeDtypeStruct((s, aw), F32),
                   jax.ShapeDtypeStruct((n_pairs, 16, s), F32)],
        scratch_shapes=[pltpu.VMEM((s, LANES), F32)],
        compiler_params=_params("parallel", "arbitrary"),
    )(qa, ka, vb, o, lse, dmixed)


def _conv_fwd(proj, w32, bias, n_ch, a_col, g_col, name):
    s = proj.shape[0]
    rows = min(CONV_ROWS, s)

    def body(a_ref, g_ref, w_ref, b_ref, y_ref, pad_ref):
        pad_ref[0:CONV_PAD, :] = jnp.zeros((CONV_PAD, LANES), F32)
        pad_ref[CONV_PAD:CONV_PAD + s, :] = a_ref[...] * _sigmoid(g_ref[...])
        wv = w_ref[...]
        for c in range(s // rows):
            acc = jnp.broadcast_to(b_ref[...], (rows, LANES))
            for t in range(CONV_TAPS):
                start = c * rows + CONV_PAD - (CONV_TAPS - 1) + t
                acc = acc + wv[t:t + 1, :] * pad_ref[start:start + rows, :]
            y_ref[c * rows:(c + 1) * rows, :] = acc

    col = lambda j0: pl.BlockSpec((s, LANES), lambda c: (0, j0 + c))
    return pl.pallas_call(
        body, name=name, grid=(n_ch // LANES,),
        in_specs=[col(a_col), col(g_col), pl.BlockSpec((CONV_PAD, LANES), lambda c: (0, c)), pl.BlockSpec((1, LANES), lambda c: (0, c))],
        out_specs=pl.BlockSpec((s, LANES), lambda c: (0, c)),
        out_shape=jax.ShapeDtypeStruct((s, n_ch), F32),
        scratch_shapes=[pltpu.VMEM((s + CONV_PAD, LANES), F32)],
        compiler_params=_params("parallel"),
    )(proj, proj, w32, bias)


def _conv_bwd(proj, w32, dy, n_ch, a_col, g_col, name):
    s = proj.shape[0]
    rows = min(CONV_ROWS, s)
    sub = 8

    def fold(x):
        acc = x[0:sub, :]
        for r in range(1, rows // sub):
            acc = acc + x[r * sub:(r + 1) * sub, :]
        return acc

    def body(a_ref, g_ref, w_ref, dy_ref, da_ref, dg_ref, dw_ref, padh_ref, padd_ref):
        sg = _sigmoid(g_ref[...])
        padh_ref[0:CONV_PAD, :] = jnp.zeros((CONV_PAD, LANES), F32)
        padh_ref[CONV_PAD:CONV_PAD + s, :] = a_ref[...] * sg
        padd_ref[0:s, :] = dy_ref[...]
        padd_ref[s:s + CONV_PAD, :] = jnp.zeros((CONV_PAD, LANES), F32)
        wv = w_ref[...]
        dw = [jnp.zeros((sub, LANES), F32) for _ in range(CONV_TAPS + 1)]
        for c in range(s // rows):
            r0 = c * rows
            acc = jnp.zeros((rows, LANES), F32)
            dyc = dy_ref[r0:r0 + rows, :]
            for t in range(CONV_TAPS):
                back = r0 + (CONV_TAPS - 1) - t
                acc = acc + wv[t:t + 1, :] * padd_ref[back:back + rows, :]
                start = r0 + CONV_PAD - (CONV_TAPS - 1) + t
                dw[t] = dw[t] + fold(dyc * padh_ref[start:start + rows, :])
            dw[CONV_TAPS] = dw[CONV_TAPS] + fold(dyc)
            av = a_ref[r0:r0 + rows, :]
            sgc = _sigmoid(g_ref[r0:r0 + rows, :])
            da_ref[r0:r0 + rows, :] = (acc * sgc).astype(BF16)
            dg_ref[r0:r0 + rows, :] = (acc * av * sgc * (1.0 - sgc)).astype(BF16)
        for t in range(CONV_TAPS + 1):
            dw_ref[t:t + 1, :] = jnp.sum(dw[t], axis=0, keepdims=True)

    col = lambda j0: pl.BlockSpec((s, LANES), lambda c: (0, j0 + c))
    wspec = pl.BlockSpec((CONV_PAD, LANES), lambda c: (0, c))
    return pl.pallas_call(
        body, name=name, grid=(n_ch // LANES,),
        in_specs=[col(a_col), col(g_col), wspec, col(0)],
        out_specs=[col(0), col(0), wspec],
        out_shape=[jax.ShapeDtypeStruct((s, n_ch), BF16), jax.ShapeDtypeStruct((s, n_ch), BF16),
                   jax.ShapeDtypeStruct((CONV_PAD, n_ch), F32)],
        scratch_shapes=[pltpu.VMEM((s + CONV_PAD, LANES), F32), pltpu.VMEM((s + CONV_PAD, LANES), F32)],
        compiler_params=_params("parallel"),
    )(proj, proj, w32, dy)


def _my_place():
    return lax.axis_index("x"), lax.axis_index("y"), lax.axis_index("c")


def _flip(place, k):
    x, y, c = place
    return (1 - x if k & 4 else x, 1 - y if k & 2 else y, 1 - c if k & 1 else c)


def _dev_id(place):
    return 4 * place[0] + 2 * place[1] + place[2]


def _wait_all(ref, send_sem, recv_sem, place):
    pltpu.make_async_remote_copy(src_ref=ref, dst_ref=ref, send_sem=send_sem, recv_sem=recv_sem,
                                 device_id=place, device_id_type=MESH).wait()


def _window(kind, ref, dev, n):
    if kind == "slot":
        return ref.at[dev]
    if kind == "rows":
        return ref.at[pl.ds(pl.multiple_of(dev * n, n), n), :]
    return ref.at[:, pl.ds(pl.multiple_of(dev * n, n), n)]


def _seven(spec, ref):
    mode, kind, n = spec
    if mode == "scatter":
        return ref
    if kind == "slot":
        return ref.at[pl.ds(0, N_DEV - 1)]
    if kind == "rows":
        return ref.at[pl.ds(0, (N_DEV - 1) * n), :]
    return ref.at[:, pl.ds(0, (N_DEV - 1) * n)]


def _hbm(x):
    return pltpu.with_memory_space_constraint(x, pltpu.HBM)


_EFFECT = pltpu.SideEffectType.DATAFLOW_SIDE_EFFECTING


def _exchange_start(srcs, lands, specs, groups, name):
    n = len(srcs)
    n_g = len(groups)

    def body(*refs):
        src_refs, land_refs = refs[:n], refs[n:2 * n]
        sems = refs[2 * n:2 * n + 2 * n_g]
        token = refs[-1]
        place = _my_place()
        me = _dev_id(place)
        for g, units in enumerate(groups):
            for j, u in enumerate(units):
                mode, kind, cnt = specs[u]
                for k in range(1, N_DEV):
                    peer = _flip(place, k)
                    if mode == "gather":
                        src, dst = src_refs[u], _window(kind, land_refs[u], me, cnt)
                    else:
                        src, dst = _window(kind, src_refs[u], _dev_id(peer), cnt), land_refs[u].at[k - 1]
                    pltpu.make_async_remote_copy(src_ref=src, dst_ref=dst, send_sem=sems[2 * g].at[j], recv_sem=sems[2 * g + 1].at[j],
                                                 device_id=peer, device_id_type=MESH).start()
        token[...] = jnp.zeros_like(token)

    hbm = pl.BlockSpec(memory_space=pltpu.HBM)
    sem = pl.BlockSpec(memory_space=pltpu.SEMAPHORE)
    out_shape = [pltpu.SemaphoreType.DMA((len(units),)) for units in groups for _ in range(2)]
    out_shape += [pltpu.HBM(x.shape, x.dtype) for x in lands] + [jax.ShapeDtypeStruct((8, LANES), F32)]
    outs = pl.pallas_call(
        body, name=name, out_shape=out_shape,
        in_specs=[hbm] * (2 * n), out_specs=[sem] * (2 * n_g) + [hbm] * n + [pl.BlockSpec(memory_space=pltpu.VMEM)],
        input_output_aliases={n + u: 2 * n_g + u for u in range(n)},
        compiler_params=pltpu.CompilerParams(has_side_effects=_EFFECT),
    )(*[_hbm(x) for x in srcs], *[_hbm(x) for x in lands])
    sem_pairs = [(outs[2 * g], outs[2 * g + 1]) for g in range(n_g)]
    return sem_pairs, list(outs[2 * n_g:2 * n_g + n]), outs[-1]


def _exchange_wait(srcs, lands, specs, sem_pair, after, name):
    n = len(lands)

    def body(*refs):
        land_refs = refs[n:2 * n]
        send_sems, recv_sems = refs[2 * n], refs[2 * n + 1]
        place = _my_place()
        for u in range(n):
            _wait_all(_seven(specs[u], land_refs[u]), send_sems.at[u], recv_sems.at[u], place)

    hbm = pl.BlockSpec(memory_space=pltpu.HBM)
    sem = pl.BlockSpec(memory_space=pltpu.SEMAPHORE)
    outs = pl.pallas_call(
        body, name=name, out_shape=[pltpu.HBM(x.shape, x.dtype) for x in lands],
        in_specs=[hbm] * (2 * n) + [sem, sem, pl.BlockSpec(memory_space=pl.ANY)], out_specs=[hbm] * n,
        input_output_aliases={n + u: u for u in range(n)},
        compiler_params=pltpu.CompilerParams(has_side_effects=_EFFECT),
    )(*[_hbm(x) for x in srcs], *lands, sem_pair[0], sem_pair[1], after)
    return list(outs)


def _all_reduce_small(g, name):
    r, w = g.shape

    def body(g_ref, out_ref, buf_ref, send_sems, recv_sems):
        place = _my_place()
        me = 4 * place[0] + 2 * place[1] + place[2]
        buf_ref[me] = g_ref[...]
        copies = []
        for k in range(1, N_DEV):
            copies.append(pltpu.make_async_remote_copy(
                src_ref=g_ref, dst_ref=buf_ref.at[me],
                send_sem=send_sems.at[k - 1], recv_sem=recv_sems.at[k - 1], device_id=_flip(place, k), device_id_type=MESH))
        for cp in copies:
            cp.start()
        for cp in copies:
            cp.wait()
        acc = buf_ref[0]
        for d in range(1, N_DEV):
            acc = acc + buf_ref[d]
        out_ref[...] = acc

    return pl.pallas_call(
        body, name=name,
        out_shape=jax.ShapeDtypeStruct((r, w), F32),
        in_specs=[pl.BlockSpec(memory_space=pltpu.VMEM)], out_specs=pl.BlockSpec(memory_space=pltpu.VMEM),
        scratch_shapes=[pltpu.VMEM((N_DEV, r, w), F32), pltpu.SemaphoreType.DMA((7,)), pltpu.SemaphoreType.DMA((7,))],
    )(g)


def _adamw_math(w, m, v, g):
    m_new = ADAM_B1 * m + (1.0 - ADAM_B1) * g
    v_new = ADAM_B2 * v + (1.0 - ADAM_B2) * (g * g)
    m_hat = m_new / (1.0 - ADAM_B1 ** ADAM_STEP)
    v_hat = v_new / (1.0 - ADAM_B2 ** ADAM_STEP)
    return -ADAM_LR * (m_hat / (jnp.sqrt(v_hat) + ADAM_EPS) + ADAM_WD * w), m_new, v_new


def _adamw(w, m, v, g, name):
    rows = w.shape[0]
    tr = min(FLAT_ROWS, rows)
    assert rows % tr == 0, (name, rows)

    def body(w_ref, m_ref, v_ref, g_ref, d_out, m_out, v_out):
        d_out[...], m_out[...], v_out[...] = _adamw_math(w_ref[...], m_ref[...], v_ref[...], g_ref[...])

    flat = pl.BlockSpec((tr, LANES), lambda i: (i, 0))
    return pl.pallas_call(
        body, name=name, grid=(rows // tr,), in_specs=[flat] * 4, out_specs=[flat] * 3,
        out_shape=[jax.ShapeDtypeStruct((rows, LANES), F32)] * 3,
        compiler_params=_params("parallel"),
    )(w, m, v, g)


def _adamw_shard(w, m, v, g_own, recv, layer, prev, name):
    depth, a, b = w.shape
    ta = min(256, a)
    assert a % ta == 0

    def body(w_ref, m_ref, v_ref, g_ref, r_ref, *rest):
        g_out, d_out, m_out, v_out = rest[-4:]
        g = g_ref[...]
        for k in range(N_DEV - 1):
            g = g + r_ref[k].astype(F32)
        g_out[0] = g
        d_out[0], m_out[0], v_out[0] = _adamw_math(w_ref[0], m_ref[0], v_ref[0], g)

    lay = pl.BlockSpec((1, ta, b), lambda i: (layer, i, 0))
    in_specs = [lay] * 3 + [pl.BlockSpec((ta, b), lambda i: (i, 0)), pl.BlockSpec((N_DEV - 1, ta, b), lambda i: (0, i, 0))]
    args = [w, m, v, g_own, recv]
    aliases = {}
    if prev is not None:
        in_specs += [pl.BlockSpec(memory_space=pl.ANY)] * 4
        args += list(prev)
        aliases = {5 + i: i for i in range(4)}
    return pl.pallas_call(
        body, name=name, grid=(a // ta,), in_specs=in_specs, out_specs=[lay] * 4,
        out_shape=[jax.ShapeDtypeStruct(w.shape, F32)] * 4, input_output_aliases=aliases,
        compiler_params=_params("parallel"),
    )(*args)


def _round_up(n, mult):
    return (n + mult - 1) // mult * mult


def _flatten(parts, row_mult):
    flat = jnp.concatenate([p.reshape(-1) for p in parts])
    rows = _round_up(-(-flat.shape[0] // LANES), row_mult)
    return jnp.pad(flat, (0, rows * LANES - flat.shape[0])).reshape(rows, LANES)


def _unflatten(flat, shapes):
    flat = flat.reshape(-1)
    out, off = [], 0
    for shp in shapes:
        n = 1
        for dim in shp:
            n *= dim
        out.append(flat[off:off + n].reshape(shp))
        off += n
    return out


def kernel(x, norm1_g, w_in, b_f, q_norm_g, k_norm_g, conv_w, conv_b, conv_ln_g, conv_ln_b, w_o, norm2_g, w_mlp_in, w_mlp_out, loss_target, m_norm1_g, m_w_in, m_b_f, m_q_norm_g, m_k_norm_g, m_conv_w, m_conv_b, m_conv_ln_g, m_conv_ln_b, m_w_o, m_norm2_g, m_w_mlp_in, m_w_mlp_out, v_norm1_g, v_w_in, v_b_f, v_q_norm_g, v_k_norm_g, v_conv_w, v_conv_b, v_conv_ln_g, v_conv_ln_b, v_w_o, v_norm2_g, v_w_mlp_in, v_w_mlp_out):
    depth, d_model, n_in_loc = w_in.shape
    n_heads = b_f.shape[1]
    aw = n_heads * HEAD_DIM
    cc = conv_b.shape[1]
    n_in = n_in_loc * N_DEV
    o_f = 3 * aw
    n_all = 3 * aw + 2 * cc + LANES
    assert n_in == 3 * aw + n_heads + 2 * cc and aw + cc == d_model and n_heads % 2 == 0
    assert aw % LANES == 0 and cc % LANES == 0 and x.shape[0] == 1
    me = 4 * lax.axis_index("x") + 2 * lax.axis_index("y") + lax.axis_index("c")

    d_ff = w_mlp_in.shape[2] * N_DEV

    r_o, f_1, f_2 = w_o.shape[1], w_mlp_in.shape[2], w_mlp_out.shape[1]

    ag_src, ag_land, ag_spec = [], [], []
    for l in range(depth):
        ag_src += [w_in[l].astype(BF16), w_o[l].astype(BF16), w_mlp_in[l].astype(BF16), w_mlp_out[l].astype(BF16)]
        ag_land += [(N_DEV, d_model, n_in_loc), (N_DEV * r_o, d_model), (d_model, N_DEV * f_1), (N_DEV * f_2, d_model)]
        ag_spec += [("gather", "slot", 1), ("gather", "rows", r_o), ("gather", "cols", f_1), ("gather", "rows", f_2)]
    ag_src.append(jnp.stack(_split3(conv_w)))
    ag_land.append((N_DEV, 3) + conv_w.shape)
    ag_spec.append(("gather", "slot", 1))
    ag_groups = [[0, 4 * depth], [1], [2, 3]] + [[4 * l + i for i in range(4)] for l in range(1, depth)]

    def own_placed(src, shape, spec):
        buf = lax.empty(shape, BF16)
        if spec[1] == "slot":
            return lax.dynamic_update_index_in_dim(buf, src, me, 0)
        return lax.dynamic_update_slice_in_dim(buf, src, me * spec[2], 0 if spec[1] == "rows" else 1)

    ag_land = [own_placed(src, shp, spec) for src, shp, spec in zip(ag_src, ag_land, ag_spec)]
    ag_sems, ag_land, ag_token = _exchange_start(ag_src, ag_land, ag_spec, ag_groups, "gather_start")

    def gathered(g, after):
        units = ag_groups[g]
        return _exchange_wait([ag_src[u] for u in units], [ag_land[u] for u in units], [ag_spec[u] for u in units],
                              ag_sems[g], after, f"gather_wait_{g}")

    def to_all(w):
        return jnp.concatenate([w[:, :o_f], w[:, o_f + n_heads:], w[:, o_f:o_f + n_heads],
                                jnp.zeros((w.shape[0], LANES - n_heads), w.dtype)], axis=1)

    def from_all(w):
        return jnp.concatenate([w[:, :o_f], w[:, n_all - LANES:n_all - LANES + n_heads], w[:, o_f:n_all - LANES]], axis=1)

    def whole_in(lin):
        return to_all(jnp.moveaxis(lin, 0, 1).reshape(d_model, n_in))

    def row(p, l, width=None):
        v = p[l].reshape(1, -1)
        return v if width is None else jnp.pad(v, ((0, 0), (0, width - v.shape[1])))

    a_col, g_col = 3 * aw // LANES, (3 * aw + cc) // LANES

    gq = [jnp.tile(row(q_norm_g, l), (1, n_heads)) for l in range(depth)]
    gk = [jnp.tile(row(k_norm_g, l), (1, n_heads)) for l in range(depth)]
    bfp = [row(b_f, l, LANES) for l in range(depth)]
    add_res = lambda acc, res: (acc + res,)
    w_all, w_out, w_ff1, w_ff2 = [None] * depth, [None] * depth, [None] * depth, [None] * depth

    h = x[0]
    saved = []
    for l in range(depth):
        u1 = _rms_fwd(h, row(norm1_g, l) + ag_token[0, 0] if l == 0 else row(norm1_g, l), f"rms1_fwd_{l}")
        if l == 0:
            lin, lc = gathered(0, u1)
            lc = lc.astype(F32)
            conv_full = jnp.moveaxis(lc[:, 0] + lc[:, 1] + lc[:, 2], 0, 2).reshape(depth, CONV_TAPS, cc)
            w32 = [jnp.pad(conv_full[i], ((0, CONV_PAD - CONV_TAPS), (0, 0))) for i in range(depth)]
        else:
            lin, w_out[l], w_ff1[l], w_ff2[l] = gathered(2 + l, u1)
        w_all[l] = whole_in(lin)
        proj = _matmul(u1, w_all[l], mode="nn", out_dtypes=(F32,), name=f"mm_in_{l}", tn=n_all)
        qa, ka, vb = _prep_fwd(proj, gq[l], gk[l], bfp[l], n_heads, f"prep_fwd_{l}")
        att, lse = _attn_fwd(qa, ka, vb, n_heads, f"attn_fwd_{l}")
        yc = _conv_fwd(proj, w32[l], row(conv_b, l), cc, a_col, g_col, f"conv_fwd_{l}")
        hc = _ln_silu_fwd(yc, row(conv_ln_g, l), row(conv_ln_b, l), f"ln_silu_fwd_{l}")
        mixed = jnp.concatenate([att.astype(BF16), hc], axis=1)
        if l == 0:
            w_out[l], = gathered(1, mixed)
        x1 = _matmul(mixed, w_out[l], mode="nn", out_dtypes=(F32,), name=f"mm_o_{l}", epilogue=add_res, extras=(h,))
        u2 = _rms_fwd(x1, row(norm2_g, l), f"rms2_fwd_{l}")
        if l == 0:
            w_ff1[l], w_ff2[l] = gathered(2, u2)
        r, a = _matmul(u2, w_ff1[l], mode="nn", out_dtypes=(BF16, BF16), name=f"mm_ff1_{l}", tm=256, tn=d_ff,
                       epilogue=lambda acc: (jnp.maximum(acc, 0.0), jnp.square(jnp.maximum(acc, 0.0))))
        x2 = _matmul(a, w_ff2[l], mode="nn", out_dtypes=(F32,), name=f"mm_ff2_{l}", epilogue=add_res, extras=(x1,), tk=d_ff)
        saved.append(dict(x_in=h, u1=u1, proj=proj, qa=qa, ka=ka, vb=vb, att=att, lse=lse, yc=yc, mixed=mixed,
                          x1=x1, u2=u2, r=r, a=a))
        h = x2

    dh, dh16, sq = _loss_grad(h, loss_target[0], "loss_grad")
    loss = lax.psum(0.5 * jnp.sum(sq) / d_model, ("x", "y", "c"))

    g_in, g_in_own = [None] * depth, [None] * depth
    g_o, g_1, g_2 = [None] * depth, [None] * depth, [None] * depth
    gs = {n: [None] * depth for n in ("norm1", "bf", "qn", "kn", "convw", "convb", "lng", "lnb", "norm2")}
    both = (F32, BF16)
    scattering = {}

    def scatter_start(stage, l, srcs, specs, slabs):
        lands = [lax.empty((N_DEV - 1,) + shp, BF16) for shp in slabs]
        sems, lands, token = _exchange_start(srcs, lands, specs, [list(range(len(srcs)))], f"scatter_start_{stage}_{l}")
        scattering[(stage, l)] = (srcs, lands, specs, sems[0])
        return token[0, 0]

    for l in reversed(range(depth)):
        sv = saved[l]
        dh1 = _matmul(dh16, w_ff2[l], mode="nt", out_dtypes=(BF16,), name=f"mm_dff2_{l}", tm=256, tn=d_ff,
                      epilogue=lambda acc, rr: (acc * (2.0 * rr.astype(F32)),), extras=(sv["r"],))
        g_2[l] = _matmul(sv["a"], dh16, mode="tn", out_dtypes=both, name=f"mm_dw2_{l}", tm=2048, tk=512)
        g_1[l] = _matmul(sv["u2"], dh1, mode="tn", out_dtypes=both, name=f"mm_dw1_{l}", tm=1024, tn=2048, tk=512)
        tok = scatter_start("ff", l, [g_1[l][1], g_2[l][1]], [("scatter", "cols", f_1), ("scatter", "rows", f_2)],
                            [(d_model, f_1), (f_2, d_model)])
        du2 = _matmul(dh1, w_ff1[l], mode="nt", out_dtypes=(F32,), name=f"mm_du2_{l}", tk=d_ff)
        dx1, dx16, gs["norm2"][l] = _rms_bwd(sv["x1"], row(norm2_g, l) + tok, du2, dh, f"rms2_bwd_{l}")

        dmixed = _matmul(dx16, w_out[l], mode="nt", out_dtypes=(F32,), name=f"mm_dmixed_{l}")
        g_o[l] = _matmul(sv["mixed"], dx16, mode="tn", out_dtypes=both, name=f"mm_dwo_{l}", tm=1024)
        dyc, gs["lng"][l], gs["lnb"][l] = _ln_silu_bwd(sv["yc"], row(conv_ln_g, l), row(conv_ln_b, l), dmixed, f"ln_silu_bwd_{l}")
        dpa, dpg, dw32 = _conv_bwd(sv["proj"], w32[l], dyc, cc, a_col, g_col, f"conv_bwd_{l}")
        gs["convw"][l], gs["convb"][l] = dw32[:CONV_TAPS], dw32[CONV_TAPS:CONV_TAPS + 1]
        dq, dka, dv, drow, dcol = _attn_bwd(sv["qa"], sv["ka"], sv["vb"], sv["att"], sv["lse"], dmixed, n_heads, f"attn_bwd_{l}")
        dpq, dpk, dpf, dgq, dgk, dbf = _prep_bwd(sv["proj"], dq, dka, drow, dcol, gq[l], gk[l], bfp[l], n_heads, f"prep_bwd_{l}")
        gs["qn"][l] = dgq.reshape(n_heads, HEAD_DIM).sum(axis=0)
        gs["kn"][l] = dgk.reshape(n_heads, HEAD_DIM).sum(axis=0)
        gs["bf"][l] = dbf[0, :n_heads]
        dproj = jnp.concatenate([dpq, dpk, dv.astype(BF16), dpa, dpg, dpf], axis=1)
        du1 = _matmul(dproj, w_all[l], mode="nt", out_dtypes=(F32,), name=f"mm_du1_{l}", tk=n_all)
        dwall, dwall16 = _matmul(sv["u1"], dproj, mode="tn", out_dtypes=both, name=f"mm_dwall_{l}", tm=1024, tn=n_all // 3)
        g_in[l] = jnp.moveaxis(from_all(dwall16).reshape(d_model, N_DEV, n_in_loc), 1, 0)
        g_in_own[l] = lax.dynamic_slice_in_dim(from_all(dwall), me * n_in_loc, n_in_loc, axis=1)
        tok = scatter_start("mix", l, [g_in[l], g_o[l][1]], [("scatter", "slot", 1), ("scatter", "rows", r_o)],
                            [(d_model, n_in_loc), (r_o, d_model)])
        dh, dh16, gs["norm1"][l] = _rms_bwd(sv["x_in"], row(norm1_g, l) + tok, du1, dx1, f"rms1_bwd_{l}")
    grad_x = dh[None]

    def landed(stage, l, after):
        srcs, lands, specs, sems = scattering[(stage, l)]
        return _exchange_wait(srcs, lands, specs, sems, after, f"scatter_wait_{stage}_{l}")

    def adamw_layers(kd, w, m, v, own, recv):
        outs = None
        for l in reversed(range(depth)):
            outs = _adamw_shard(w, m, v, own[l], recv[l], l, outs, f"adamw_{kd}_{l}")
        return outs

    recv_1, recv_2, recv_in, recv_o = [None] * depth, [None] * depth, [None] * depth, [None] * depth
    for l in reversed(range(depth)):
        recv_1[l], recv_2[l] = landed("ff", l, dh)
    out_1 = adamw_layers("1", w_mlp_in, m_w_mlp_in, v_w_mlp_in,
                         [lax.dynamic_slice_in_dim(g_1[l][0], me * f_1, f_1, axis=1) for l in range(depth)], recv_1)
    out_2 = adamw_layers("2", w_mlp_out, m_w_mlp_out, v_w_mlp_out,
                         [lax.dynamic_slice_in_dim(g_2[l][0], me * f_2, f_2, axis=0) for l in range(depth)], recv_2)
    for l in reversed(range(depth)):
        recv_in[l], recv_o[l] = landed("mix", l, out_2[1])
    out_in = adamw_layers("in", w_in, m_w_in, v_w_in, g_in_own, recv_in)
    out_o = adamw_layers("o", w_o, m_w_o, v_w_o,
                         [lax.dynamic_slice_in_dim(g_o[l][0], me * r_o, r_o, axis=0) for l in range(depth)], recv_o)
    big_out = [[outs[kind] for outs in (out_in, out_o, out_1, out_2)] for kind in range(4)]

    small_g = [jnp.stack(gs[n]).reshape(shp) for n, shp in (
        ("norm1", norm1_g.shape), ("bf", b_f.shape), ("qn", q_norm_g.shape), ("kn", k_norm_g.shape),
        ("convw", (depth, CONV_TAPS, cc)), ("convb", conv_b.shape), ("lng", conv_ln_g.shape), ("lnb", conv_ln_b.shape),
        ("norm2", norm2_g.shape))]
    small_shapes = [g.shape for g in small_g]
    small_g = _unflatten(_all_reduce_small(_flatten(small_g, 8), "all_reduce_small"), small_shapes)
    cw = conv_w.shape[2]
    small_g[4] = lax.dynamic_slice_in_dim(small_g[4], me * cw, cw, axis=2)
    small = (norm1_g, b_f, q_norm_g, k_norm_g, conv_w, conv_b, conv_ln_g, conv_ln_b, norm2_g)
    small_m = (m_norm1_g, m_b_f, m_q_norm_g, m_k_norm_g, m_conv_w, m_conv_b, m_conv_ln_g, m_conv_ln_b, m_norm2_g)
    small_v = (v_norm1_g, v_b_f, v_q_norm_g, v_k_norm_g, v_conv_w, v_conv_b, v_conv_ln_g, v_conv_ln_b, v_norm2_g)
    small_out = _adamw(_flatten(small, 8), _flatten(small_m, 8), _flatten(small_v, 8), _flatten(small_g, 8), "adamw_small")
    small_out = [small_g] + [_unflatten(o, [w.shape for w in small]) for o in small_out]

    def group(kind):
        s_, b_ = small_out[kind], big_out[kind]
        return [s_[0], b_[0], s_[1], s_[2], s_[3], s_[4], s_[5], s_[6], s_[7], b_[1], s_[8], b_[2], b_[3]]

    return (loss, grad_x, *group(0), *group(1), *group(2), *group(3))
```

```python
import functools

import jax
import jax.numpy as jnp
from jax import lax
from jax.experimental import pallas as pl
from jax.experimental.pallas import tpu as pltpu

F32 = jnp.float32
BF16 = jnp.bfloat16

EPS = 1e-6
HEAD_DIM = 64
LANES = 128
PAIR = 2 * LANES
N_DEV = 8
CONV_TAPS = 31
CONV_PAD = 32
NEG = -1e30

ADAM_LR = 0.001
ADAM_B1 = 0.9
ADAM_B2 = 0.999
ADAM_EPS = 1e-08
ADAM_WD = 0.01
ADAM_STEP = 10

TM = 512
TQ = 512
CONV_ROWS = 128
FLAT_ROWS = 1024
MESH = pl.DeviceIdType.MESH


def _params(*sem):
    return pltpu.CompilerParams(dimension_semantics=sem, vmem_limit_bytes=56 * 1024 * 1024)


def _split3(x):
    hi = x.astype(BF16)
    r1 = x - hi.astype(F32)
    mid = r1.astype(BF16)
    lo = (r1 - mid.astype(F32)).astype(BF16)
    return hi, mid, lo


def _dot(a, b):
    return jnp.dot(a, b, preferred_element_type=F32)


def _dot_nt(a, b):
    return lax.dot_general(a, b, (((1,), (1,)), ((), ())), preferred_element_type=F32)


def _dot_tn(a, b):
    return lax.dot_general(a, b, (((0,), (0,)), ((), ())), preferred_element_type=F32)


def _dot3(x, mat):
    hi, mid, lo = _split3(x)
    return _dot(hi, mat) + _dot(mid, mat) + _dot(lo, mat)


def _dot2(x, mat):
    hi = x.astype(BF16)
    lo = (x - hi.astype(F32)).astype(BF16)
    return _dot(hi, mat) + _dot(lo, mat)


def _dot3_r(mat, x):
    hi, mid, lo = _split3(x)
    return _dot(mat, hi) + _dot(mat, mid) + _dot(mat, lo)


def _iota(shape, dim):
    return lax.broadcasted_iota(jnp.int32, shape, dim)


def _sigmoid(x):
    return 1.0 / (1.0 + jnp.exp(-x))


def _matmul(a, b, *, mode, out_dtypes, name, epilogue=None, extras=(), tm=TM, tn=1024, tk=1024):
    if mode == "nn":
        (m, k), (k2, n) = a.shape, b.shape
    elif mode == "nt":
        (m, k), (n, k2) = a.shape, b.shape
    else:
        (k, m), (k2, n) = a.shape, b.shape
    assert k == k2, (name, a.shape, b.shape)
    tm, tn, tk = min(tm, m), min(tn, n), min(tk, k)
    assert m % tm == 0 and n % tn == 0 and k % tk == 0, (name, m, n, k, tm, tn, tk)
    nk = k // tk
    a_mode = dict(pipeline_mode=pl.Buffered(1)) if (m == tm and nk == 1) else {}
    b_mode = dict(pipeline_mode=pl.Buffered(1)) if (n == tn and nk == 1) else {}
    if mode == "tn":
        a_spec = pl.BlockSpec((tk, tm), lambda i, j, kk: (kk, i), **a_mode)
    else:
        a_spec = pl.BlockSpec((tm, tk), lambda i, j, kk: (i, kk), **a_mode)
    if mode == "nt":
        b_spec = pl.BlockSpec((tn, tk), lambda i, j, kk: (j, kk), **b_mode)
    else:
        b_spec = pl.BlockSpec((tk, tn), lambda i, j, kk: (kk, j), **b_mode)
    dot = {"nn": _dot, "nt": _dot_nt, "tn": _dot_tn}[mode]
    tile = pl.BlockSpec((tm, tn), lambda i, j, kk: (i, j))
    n_ex, n_out = len(extras), len(out_dtypes)
    acc_in_out = nk > 1 and epilogue is None and out_dtypes[0] == F32

    def body(a_ref, b_ref, *rest):
        ex_refs, out_refs = rest[:n_ex], rest[n_ex:n_ex + n_out]
        part = dot(a_ref[...].astype(BF16), b_ref[...].astype(BF16))

        def finish(acc):
            res = epilogue(acc, *[e[...] for e in ex_refs]) if epilogue is not None else (acc,) * n_out
            for o_ref, r in zip(out_refs, res):
                o_ref[...] = r.astype(o_ref.dtype)

        if nk == 1:
            finish(part)
        else:
            acc_ref = out_refs[0] if acc_in_out else rest[-1]
            kk = pl.program_id(2)

            @pl.when(kk == 0)
            def _():
                acc_ref[...] = part

            @pl.when(kk > 0)
            def _():
                acc_ref[...] += part

            @pl.when(kk == nk - 1)
            def _():
                if acc_in_out:
                    for o_ref in out_refs[1:]:
                        o_ref[...] = acc_ref[...].astype(o_ref.dtype)
                else:
                    finish(acc_ref[...])

    outs = pl.pallas_call(
        body,
        name=name,
        grid=(m // tm, n // tn, nk),
        in_specs=[a_spec, b_spec] + [tile] * n_ex,
        out_specs=[tile] * n_out,
        out_shape=[jax.ShapeDtypeStruct((m, n), dt) for dt in out_dtypes],
        scratch_shapes=[pltpu.VMEM((tm, tn), F32)] if nk > 1 and not acc_in_out else [],
        compiler_params=_params("parallel", "parallel", "arbitrary"),
    )(a, b, *extras)
    return outs if n_out > 1 else outs[0]


def _matmul_rms_bwd(a, b, x, g, dres, name):
    s, k = a.shape
    d = b.shape[0]
    ts = min(TM, s)

    def body(a_ref, b_ref, x_ref, g_ref, dres_ref, dx_ref, dx16_ref, dg_ref):
        @pl.when(pl.program_id(0) == 0)
        def _():
            dg_ref[...] = jnp.zeros_like(dg_ref)

        duv = _dot_nt(a_ref[...], b_ref[...])
        xv = x_ref[...]
        r = lax.rsqrt(jnp.mean(xv * xv, axis=-1, keepdims=True) + EPS)
        xh = xv * r
        dxh = duv * g_ref[...]
        dx = dres_ref[...] + r * (dxh - xh * jnp.mean(dxh * xh, axis=-1, keepdims=True))
        dx_ref[...] = dx
        dx16_ref[...] = dx.astype(BF16)
        dg_ref[...] += jnp.sum(duv * xh, axis=0, keepdims=True)

    row = pl.BlockSpec((ts, d), lambda i: (i, 0))
    vec = pl.BlockSpec((1, d), lambda i: (0, 0))
    return pl.pallas_call(
        body, name=name, grid=(s // ts,),
        in_specs=[pl.BlockSpec((ts, k), lambda i: (i, 0)), pl.BlockSpec((d, k), lambda i: (0, 0), pipeline_mode=pl.Buffered(1)),
                  row, vec, row],
        out_specs=[row, row, vec],
        out_shape=[jax.ShapeDtypeStruct((s, d), F32), jax.ShapeDtypeStruct((s, d), BF16), jax.ShapeDtypeStruct((1, d), F32)],
        compiler_params=_params("arbitrary"),
    )(a, b, x, g, dres)


def _matmul_loss(a, b, res, target, name):
    s, k = a.shape
    d = b.shape[1]
    ts = min(TM, s)

    def body(a_ref, b_ref, res_ref, t_ref, dy_ref, dy16_ref, sq_ref):
        @pl.when(pl.program_id(0) == 0)
        def _():
            sq_ref[...] = jnp.zeros_like(sq_ref)

        err = _dot(a_ref[...], b_ref[...]) + res_ref[...] - t_ref[...]
        dy = err * (1.0 / d)
        dy_ref[...] = dy
        dy16_ref[...] = dy.astype(BF16)
        sq_ref[...] += jnp.sum(err * err, axis=0, keepdims=True)

    row = pl.BlockSpec((ts, d), lambda i: (i, 0))
    vec = pl.BlockSpec((1, d), lambda i: (0, 0))
    return pl.pallas_call(
        body, name=name, grid=(s // ts,),
        in_specs=[pl.BlockSpec((ts, k), lambda i: (i, 0)), pl.BlockSpec((k, d), lambda i: (0, 0), pipeline_mode=pl.Buffered(1)),
                  row, row],
        out_specs=[row, row, vec],
        out_shape=[jax.ShapeDtypeStruct((s, d), F32), jax.ShapeDtypeStruct((s, d), BF16), jax.ShapeDtypeStruct((1, d), F32)],
        compiler_params=_params("arbitrary"),
    )(a, b, res, target)


def _rms_fwd(x, g, name):
    s, d = x.shape
    ts = min(TM, s)

    def body(x_ref, g_ref, u_ref):
        xv = x_ref[...]
        y = xv * lax.rsqrt(jnp.mean(xv * xv, axis=-1, keepdims=True) + EPS)
        u_ref[...] = (y * g_ref[...]).astype(BF16)

    return pl.pallas_call(
        body, name=name, grid=(s // ts,),
        in_specs=[pl.BlockSpec((ts, d), lambda i: (i, 0)), pl.BlockSpec((1, d), lambda i: (0, 0))],
        out_specs=pl.BlockSpec((ts, d), lambda i: (i, 0)),
        out_shape=jax.ShapeDtypeStruct((s, d), BF16),
        compiler_params=_params("parallel"),
    )(x, g)


def _ln_silu_fwd(y, g, b, name):
    s, c = y.shape
    ts = min(TM, s)

    def body(y_ref, g_ref, b_ref, h_ref):
        yv = y_ref[...]
        mu = jnp.mean(yv, axis=-1, keepdims=True)
        yc = yv - mu
        z = yc * lax.rsqrt(jnp.mean(yc * yc, axis=-1, keepdims=True) + EPS) * g_ref[...] + b_ref[...]
        h_ref[...] = (z * _sigmoid(z)).astype(BF16)

    row = pl.BlockSpec((ts, c), lambda i: (i, 0))
    vec = pl.BlockSpec((1, c), lambda i: (0, 0))
    return pl.pallas_call(
        body, name=name, grid=(s // ts,), in_specs=[row, vec, vec], out_specs=row,
        out_shape=jax.ShapeDtypeStruct((s, c), BF16), compiler_params=_params("parallel"),
    )(y, g, b)


def _ln_silu_bwd(y, g, b, dmixed, name):
    s, c = y.shape
    ts = min(TM, s)

    def body(y_ref, g_ref, b_ref, dh_ref, dy_ref, dg_ref, db_ref):
        @pl.when(pl.program_id(0) == 0)
        def _():
            dg_ref[...] = jnp.zeros_like(dg_ref)
            db_ref[...] = jnp.zeros_like(db_ref)

        yv = y_ref[...]
        mu = jnp.mean(yv, axis=-1, keepdims=True)
        yc = yv - mu
        r = lax.rsqrt(jnp.mean(yc * yc, axis=-1, keepdims=True) + EPS)
        yh = yc * r
        z = yh * g_ref[...] + b_ref[...]
        sg = _sigmoid(z)
        dz = dh_ref[...] * (sg * (1.0 + z * (1.0 - sg)))
        dg_ref[...] += jnp.sum(dz * yh, axis=0, keepdims=True)
        db_ref[...] += jnp.sum(dz, axis=0, keepdims=True)
        dyh = dz * g_ref[...]
        dy_ref[...] = r * (dyh - jnp.mean(dyh, axis=-1, keepdims=True) - yh * jnp.mean(dyh * yh, axis=-1, keepdims=True))

    row = pl.BlockSpec((ts, c), lambda i: (i, 0))
    vec = pl.BlockSpec((1, c), lambda i: (0, 0))
    return pl.pallas_call(
        body, name=name, grid=(s // ts,),
        in_specs=[row, vec, vec, pl.BlockSpec((ts, c), lambda i: (i, 1))], out_specs=[row, vec, vec],
        out_shape=[jax.ShapeDtypeStruct((s, c), F32), jax.ShapeDtypeStruct((1, c), F32), jax.ShapeDtypeStruct((1, c), F32)],
        compiler_params=_params("arbitrary"),
    )(y, g, b, dmixed)


def _head_masks():
    lane2 = _iota((1, PAIR), 1)
    lane1 = _iota((1, LANES), 1)
    qa = (lane2 < HEAD_DIM) | ((lane2 >= LANES) & (lane2 < LANES + 3))
    qb = ((lane2 >= HEAD_DIM) & (lane2 < LANES)) | ((lane2 >= LANES + 3) & (lane2 < LANES + 6))
    return (qa, qb), (lane1 < HEAD_DIM, lane1 >= HEAD_DIM)


def _group_matrix(width):
    shift = HEAD_DIM.bit_length() - 1
    return ((_iota((width, width), 0) >> shift) == (_iota((width, width), 1) >> shift)).astype(BF16)


def _prep_fwd(proj, gq, gk, bf, n_heads, name):
    s = proj.shape[0]
    aw = n_heads * HEAD_DIM
    n_pairs = n_heads // 2
    ts = min(TM, s)
    f_col = (proj.shape[1] - LANES) // LANES

    def body(q_ref, k_ref, v_ref, f_ref, gq_ref, gk_ref, bf_ref, qa_ref, ka_ref, vb_ref, carry_ref):
        @pl.when(pl.program_id(0) == 0)
        def _():
            carry_ref[...] = jnp.zeros_like(carry_ref)

        gmat = _group_matrix(aw)

        def head_norm(xv, g):
            ms = _dot2(xv * xv, gmat) * (1.0 / HEAD_DIM)
            return xv * lax.rsqrt(ms + EPS) * g

        qn = head_norm(q_ref[...], gq_ref[...]) * (HEAD_DIM ** -0.5)
        kn = head_norm(k_ref[...], gk_ref[...])
        z = f_ref[...] + bf_ref[...]
        logf = jnp.minimum(z, 0.0) - jnp.log(1.0 + jnp.exp(-jnp.abs(z)))
        tri = (_iota((ts, ts), 0) >= _iota((ts, ts), 1)).astype(BF16)
        c = _dot3_r(tri, logf) + carry_ref[...]
        carry_ref[...] = c[ts - 1:ts, :]
        terms = _split3(-c)
        row, col = _iota((LANES, LANES), 0), _iota((LANES, LANES), 1)
        ones = jnp.where(_iota((ts, LANES), 1) < 6, 1.0, 0.0).astype(BF16)
        for p in range(n_pairs):
            extra = jnp.zeros((ts, LANES), F32)
            for t, term in enumerate(terms):
                sel = ((row == 2 * p) & (col == t)) | ((row == 2 * p + 1) & (col == 3 + t))
                extra += _dot(term, sel.astype(BF16))
            lo, hi = p * PAIR, p * PAIR + LANES
            ka_ref[:, lo:hi] = kn[:, p * LANES:(p + 1) * LANES].astype(BF16)
            ka_ref[:, hi:hi + LANES] = extra.astype(BF16)
            qa_ref[:, lo:hi] = qn[:, p * LANES:(p + 1) * LANES].astype(BF16)
            qa_ref[:, hi:hi + LANES] = ones
            vb_ref[:, lo:hi] = v_ref[:, p * LANES:(p + 1) * LANES].astype(BF16)
            vb_ref[:, hi:hi + LANES] = jnp.ones((ts, LANES), BF16)

    blk = lambda j: pl.BlockSpec((ts, aw), lambda i: (i, j))
    vec = lambda w: pl.BlockSpec((1, w), lambda i: (0, 0))
    return pl.pallas_call(
        body, name=name, grid=(s // ts,),
        in_specs=[blk(0), blk(1), blk(2), pl.BlockSpec((ts, LANES), lambda i: (i, f_col)), vec(aw), vec(aw), vec(LANES)],
        out_specs=[pl.BlockSpec((ts, n_pairs * PAIR), lambda i: (i, 0))] * 3,
        out_shape=[jax.ShapeDtypeStruct((s, n_pairs * PAIR), BF16)] * 3,
        scratch_shapes=[pltpu.VMEM((1, LANES), F32)],
        compiler_params=_params("arbitrary"),
    )(proj, proj, proj, proj, gq, gk, bf)


def _prep_bwd(proj, dq, dka, drow, dcol, gq, gk, bf, n_heads, name):
    s = proj.shape[0]
    aw = n_heads * HEAD_DIM
    n_pairs = n_heads // 2
    ts = min(TM, s)
    nt = s // ts
    f_col = (proj.shape[1] - LANES) // LANES
    shift = HEAD_DIM.bit_length() - 1

    def body(q_ref, k_ref, f_ref, dq_ref, dka_ref, drow_ref, dcol_ref, gq_ref, gk_ref, bf_ref,
             dpq_ref, dpk_ref, dpf_ref, dgq_ref, dgk_ref, dbf_ref, carry_ref):
        @pl.when(pl.program_id(0) == 0)
        def _():
            carry_ref[...] = jnp.zeros_like(carry_ref)
            dgq_ref[...] = jnp.zeros_like(dgq_ref)
            dgk_ref[...] = jnp.zeros_like(dgk_ref)
            dbf_ref[...] = jnp.zeros_like(dbf_ref)

        gmat = _group_matrix(aw)

        def head_norm_bwd(xv, g, dn):
            r = lax.rsqrt(_dot2(xv * xv, gmat) * (1.0 / HEAD_DIM) + EPS)
            xh = xv * r
            dxh = dn * g
            dx = r * (dxh - xh * (_dot2(dxh * xh, gmat) * (1.0 / HEAD_DIM)))
            return dx, jnp.sum(dn * xh, axis=0, keepdims=True)

        dkav = dka_ref[...]
        dx, dg = head_norm_bwd(q_ref[...], gq_ref[...], dq_ref[...] * (HEAD_DIM ** -0.5))
        dpq_ref[...] = dx.astype(BF16)
        dgq_ref[...] += dg
        dkn = jnp.concatenate([dkav[:, p * PAIR:p * PAIR + LANES] for p in range(n_pairs)], axis=1)
        dx, dg = head_norm_bwd(k_ref[...], gk_ref[...], dkn)
        dpk_ref[...] = dx.astype(BF16)
        dgk_ref[...] += dg

        pick = (_iota((aw, LANES), 0) == (_iota((aw, LANES), 1) << shift)).astype(BF16)
        dc = _dot3(drow_ref[...], pick)
        r16, c16 = _iota((16, LANES), 0), _iota((16, LANES), 1)
        for p in range(n_pairs):
            place = ((r16 < 2) & (c16 == 2 * p + r16)).astype(BF16)
            for term in _split3(dcol_ref[p]):
                dc -= _dot_tn(term, place)
        triu = (_iota((ts, ts), 0) <= _iota((ts, ts), 1)).astype(BF16)
        dlogf = _dot3_r(triu, dc) + carry_ref[...]
        carry_ref[...] = dlogf[0:1, :]
        z = f_ref[...] + bf_ref[...]
        dz = dlogf * (1.0 / (1.0 + jnp.exp(z)))
        dpf_ref[...] = dz.astype(BF16)
        dbf_ref[...] += jnp.sum(dz, axis=0, keepdims=True)

    rev = lambda w, j: pl.BlockSpec((ts, w), lambda i: (nt - 1 - i, j))
    vec = lambda w: pl.BlockSpec((1, w), lambda i: (0, 0))
    return pl.pallas_call(
        body, name=name, grid=(nt,),
        in_specs=[rev(aw, 0), rev(aw, 1), rev(LANES, f_col), rev(aw, 0), rev(n_pairs * PAIR, 0), rev(aw, 0),
                  pl.BlockSpec((n_pairs, 16, ts), lambda i: (0, 0, nt - 1 - i)), vec(aw), vec(aw), vec(LANES)],
        out_specs=[rev(aw, 0), rev(aw, 0), rev(LANES, 0), vec(aw), vec(aw), vec(LANES)],
        out_shape=[jax.ShapeDtypeStruct((s, aw), BF16), jax.ShapeDtypeStruct((s, aw), BF16), jax.ShapeDtypeStruct((s, LANES), BF16),
                   jax.ShapeDtypeStruct((1, aw), F32), jax.ShapeDtypeStruct((1, aw), F32), jax.ShapeDtypeStruct((1, LANES), F32)],
        scratch_shapes=[pltpu.VMEM((1, LANES), F32)],
        compiler_params=_params("arbitrary"),
    )(proj, proj, proj, dq, dka, drow, dcol, gq, gk, bf)


def _attn_fwd(qa, ka, vb, n_heads, name):
    s = qa.shape[0]
    aw = n_heads * HEAD_DIM
    n_pairs = n_heads // 2
    tq = min(TQ, s)

    def body(q_ref, k_ref, v_ref, o_ref, lse_ref):
        i = pl.program_id(1)
        qmasks, omasks = _head_masks()
        qv = q_ref[...]
        causal = _iota((tq, tq), 1) <= _iota((tq, tq), 0)
        qhs = [jnp.where(qmasks[h], qv, jnp.zeros_like(qv)) for h in range(2)]

        def scores(j):
            kv = k_ref[pl.ds(pl.multiple_of(j * tq, tq), tq), :]
            return tuple(_dot_nt(qhs[h], kv) for h in range(2))

        def update(j, state, scs, masked):
            vv = v_ref[pl.ds(pl.multiple_of(j * tq, tq), tq), :]
            out = []
            for h in range(2):
                m, acc = state[h]
                sc = jnp.where(causal, scs[h], NEG) if masked else scs[h]
                m_new = jnp.maximum(m, jnp.max(sc, axis=1, keepdims=True))
                p = jnp.exp(sc - m_new).astype(BF16)
                out.append((m_new, jnp.exp(m - m_new) * acc + _dot(p, vv)))
            return tuple(out)

        def body(j, state):
            return update(j, state, scores(j), False)

        init = ((jnp.full((tq, 1), NEG, F32), jnp.zeros((tq, PAIR), F32)),) * 2
        state = lax.fori_loop(0, i, body, init)
        res = []
        for m, acc in update(i, state, scores(i), True):
            l = acc[:, LANES:LANES + 1]
            res.append((acc[:, :LANES] * (1.0 / l), m + jnp.log(l)))
        o_ref[...] = jnp.where(omasks[0], res[0][0], res[1][0])
        lse_ref[...] = jnp.where(omasks[0], res[0][1], res[1][1])

    return pl.pallas_call(
        body, name=name, grid=(n_pairs, s // tq),
        in_specs=[pl.BlockSpec((tq, PAIR), lambda p, i: (i, p)), pl.BlockSpec((s, PAIR), lambda p, i: (0, p)),
                  pl.BlockSpec((s, PAIR), lambda p, i: (0, p))],
        out_specs=[pl.BlockSpec((tq, LANES), lambda p, i: (i, p))] * 2,
        out_shape=[jax.ShapeDtypeStruct((s, aw), F32)] * 2,
        compiler_params=_params("parallel", "parallel"),
    )(qa, ka, vb)


def _attn_bwd(qa, ka, vb, o, lse, dmixed, n_heads, name):
    s = qa.shape[0]
    aw = n_heads * HEAD_DIM
    n_pairs = n_heads // 2
    tq = min(TQ, s)
    nq = s // tq

    def body(q_ref, k_ref, v_ref, o_ref, lse_ref, do_ref, dq_ref, dka_ref, dv_ref, drow_ref, dcol_ref, delta_ref):
        j = pl.program_id(1)
        qmasks, omasks = _head_masks()

        @pl.when(j == 0)
        def _():
            dq_ref[...] = jnp.zeros_like(dq_ref)
            drow_ref[...] = jnp.zeros_like(drow_ref)
            for c in range(nq):
                rows = slice(c * tq, (c + 1) * tq)
                prod = do_ref[rows, :] * o_ref[rows, :]
                da = jnp.sum(jnp.where(omasks[0], prod, 0.0), axis=1, keepdims=True)
                db = jnp.sum(jnp.where(omasks[1], prod, 0.0), axis=1, keepdims=True)
                delta_ref[rows, :] = jnp.where(omasks[0], da, db)

        dka_ref[...] = jnp.zeros_like(dka_ref)
        dv_ref[...] = jnp.zeros_like(dv_ref)
        dcol_ref[...] = jnp.zeros_like(dcol_ref)
        kv = k_ref[...]
        kk = kv[:, :LANES]
        vv = v_ref[...]
        causal = _iota((tq, tq), 1) <= _iota((tq, tq), 0)

        def step(i, masked):
            off = pl.multiple_of(i * tq, tq)
            qv = q_ref[pl.ds(off, tq), :]
            dov = do_ref[pl.ds(off, tq), :]
            lsev = lse_ref[pl.ds(off, tq), :]
            dlv = delta_ref[pl.ds(off, tq), :]
            for h in range(2):
                qh = jnp.where(qmasks[h], qv, jnp.zeros_like(qv))
                doh = jnp.where(omasks[h], dov, 0.0).astype(BF16)
                lane = h * HEAD_DIM
                sc = _dot_nt(qh, kv)
                if masked:
                    sc = jnp.where(causal, sc, NEG)
                p = jnp.exp(sc - lsev[:, lane:lane + 1])
                dv_ref[...] += _dot_tn(p.astype(BF16), doh)
                dp = _dot_nt(doh, vv)
                dsf = p * (dp - dlv[:, lane:lane + 1])
                drow_ref[pl.ds(off, tq), :] += jnp.where(omasks[h], jnp.sum(dsf, axis=1, keepdims=True), 0.0)
                dcol_ref[0, h:h + 1, :] += jnp.sum(dsf, axis=0, keepdims=True)
                ds = dsf.astype(BF16)
                dka_ref[...] += _dot_tn(ds, qh)
                dq_ref[pl.ds(off, tq), :] += jnp.where(omasks[h], _dot(ds, kk), 0.0)

        step(j, True)

        def loop_body(i, carry):
            step(i, False)
            return carry

        lax.fori_loop(j + 1, nq, loop_body, 0)

    full = lambda w: pl.BlockSpec((s, w), lambda p, j: (0, p))
    blk = lambda w: pl.BlockSpec((tq, w), lambda p, j: (j, p))
    return pl.pallas_call(
        body, name=name, grid=(n_pairs, nq),
        in_specs=[full(PAIR), blk(PAIR), pl.BlockSpec((tq, LANES), lambda p, j: (j, 2 * p)), full(LANES), full(LANES), full(LANES)],
        out_specs=[full(LANES), blk(PAIR), blk(LANES), full(LANES), pl.BlockSpec((1, 16, tq), lambda p, j: (p, 0, j))],
        out_shape=[jax.ShapeDtypeStruct((s, aw), F32), jax.ShapeDtypeStruct((s, n_pairs * PAIR), F32),
                   jax.ShapeDtypeStruct((s, aw), F32), jax.ShapeDtypeStruct((s, aw), F32),
                   jax.ShapeDtypeStruct((n_pairs, 16, s), F32)],
        scratch_shapes=[pltpu.VMEM((s, LANES), F32)],
        compiler_params=_params("parallel", "arbitrary"),
    )(qa, ka, vb, o, lse, dmixed)


def _conv_fwd(proj, w32, bias, n_ch, a_col, g_col, name):
    s = proj.shape[0]
    rows = min(CONV_ROWS, s)

    def body(a_ref, g_ref, w_ref, b_ref, y_ref, pad_ref):
        pad_ref[0:CONV_PAD, :] = jnp.zeros((CONV_PAD, LANES), F32)
        pad_ref[CONV_PAD:CONV_PAD + s, :] = a_ref[...] * _sigmoid(g_ref[...])
        wv = w_ref[...]
        for c in range(s // rows):
            acc = jnp.broadcast_to(b_ref[...], (rows, LANES))
            for t in range(CONV_TAPS):
                start = c * rows + CONV_PAD - (CONV_TAPS - 1) + t
                acc = acc + wv[t:t + 1, :] * pad_ref[start:start + rows, :]
            y_ref[c * rows:(c + 1) * rows, :] = acc

    col = lambda j0: pl.BlockSpec((s, LANES), lambda c: (0, j0 + c))
    return pl.pallas_call(
        body, name=name, grid=(n_ch // LANES,),
        in_specs=[col(a_col), col(g_col), pl.BlockSpec((CONV_PAD, LANES), lambda c: (0, c)), pl.BlockSpec((1, LANES), lambda c: (0, c))],
        out_specs=pl.BlockSpec((s, LANES), lambda c: (0, c)),
        out_shape=jax.ShapeDtypeStruct((s, n_ch), F32),
        scratch_shapes=[pltpu.VMEM((s + CONV_PAD, LANES), F32)],
        compiler_params=_params("parallel"),
    )(proj, proj, w32, bias)


def _conv_bwd(proj, w32, dy, n_ch, a_col, g_col, name):
    s = proj.shape[0]
    rows = min(CONV_ROWS, s)
    sub = 8

    def fold(x):
        acc = x[0:sub, :]
        for r in range(1, rows // sub):
            acc = acc + x[r * sub:(r + 1) * sub, :]
        return acc

    def body(a_ref, g_ref, w_ref, dy_ref, da_ref, dg_ref, dw_ref, padh_ref, padd_ref):
        sg = _sigmoid(g_ref[...])
        padh_ref[0:CONV_PAD, :] = jnp.zeros((CONV_PAD, LANES), F32)
        padh_ref[CONV_PAD:CONV_PAD + s, :] = a_ref[...] * sg
        padd_ref[0:s, :] = dy_ref[...]
        padd_ref[s:s + CONV_PAD, :] = jnp.zeros((CONV_PAD, LANES), F32)
        wv = w_ref[...]
        dw = [jnp.zeros((sub, LANES), F32) for _ in range(CONV_TAPS + 1)]
        for c in range(s // rows):
            r0 = c * rows
            acc = jnp.zeros((rows, LANES), F32)
            dyc = dy_ref[r0:r0 + rows, :]
            for t in range(CONV_TAPS):
                back = r0 + (CONV_TAPS - 1) - t
                acc = acc + wv[t:t + 1, :] * padd_ref[back:back + rows, :]
                start = r0 + CONV_PAD - (CONV_TAPS - 1) + t
                dw[t] = dw[t] + fold(dyc * padh_ref[start:start + rows, :])
            dw[CONV_TAPS] = dw[CONV_TAPS] + fold(dyc)
            av = a_ref[r0:r0 + rows, :]
            sgc = _sigmoid(g_ref[r0:r0 + rows, :])
            da_ref[r0:r0 + rows, :] = (acc * sgc).astype(BF16)
            dg_ref[r0:r0 + rows, :] = (acc * av * sgc * (1.0 - sgc)).astype(BF16)
        for t in range(CONV_TAPS + 1):
            dw_ref[t:t + 1, :] = jnp.sum(dw[t], axis=0, keepdims=True)

    col = lambda j0: pl.BlockSpec((s, LANES), lambda c: (0, j0 + c))
    wspec = pl.BlockSpec((CONV_PAD, LANES), lambda c: (0, c))
    return pl.pallas_call(
        body, name=name, grid=(n_ch // LANES,),
        in_specs=[col(a_col), col(g_col), wspec, col(0)],
        out_specs=[col(0), col(0), wspec],
        out_shape=[jax.ShapeDtypeStruct((s, n_ch), BF16), jax.ShapeDtypeStruct((s, n_ch), BF16),
                   jax.ShapeDtypeStruct((CONV_PAD, n_ch), F32)],
        scratch_shapes=[pltpu.VMEM((s + CONV_PAD, LANES), F32), pltpu.VMEM((s + CONV_PAD, LANES), F32)],
        compiler_params=_params("parallel"),
    )(proj, proj, w32, dy)


def _my_place():
    return lax.axis_index("x"), lax.axis_index("y"), lax.axis_index("c")


def _flip(place, k):
    x, y, c = place
    return (1 - x if k & 4 else x, 1 - y if k & 2 else y, 1 - c if k & 1 else c)


def _dev_id(place):
    return 4 * place[0] + 2 * place[1] + place[2]


def _wait_all(ref, send_sem, recv_sem, place):
    pltpu.make_async_remote_copy(src_ref=ref, dst_ref=ref, send_sem=send_sem, recv_sem=recv_sem,
                                 device_id=place, device_id_type=MESH).wait()


def _window(kind, ref, dev, n):
    if kind == "slot":
        return ref.at[dev]
    if kind == "rows":
        return ref.at[pl.ds(pl.multiple_of(dev * n, n), n), :]
    return ref.at[:, pl.ds(pl.multiple_of(dev * n, n), n)]


def _seven(spec, ref):
    mode, kind, n = spec
    if mode == "scatter":
        return ref
    if kind == "slot":
        return ref.at[pl.ds(0, N_DEV - 1)]
    if kind == "rows":
        return ref.at[pl.ds(0, (N_DEV - 1) * n), :]
    return ref.at[:, pl.ds(0, (N_DEV - 1) * n)]


def _hbm(x):
    return pltpu.with_memory_space_constraint(x, pltpu.HBM)


_EFFECT = pltpu.SideEffectType.DATAFLOW_SIDE_EFFECTING


def _exchange_start(srcs, lands, specs, groups, name):
    n = len(srcs)
    n_g = len(groups)

    def body(*refs):
        src_refs, land_refs = refs[:n], refs[n:2 * n]
        sems = refs[2 * n:2 * n + 2 * n_g]
        token = refs[-1]
        place = _my_place()
        me = _dev_id(place)
        for g, units in enumerate(groups):
            for j, u in enumerate(units):
                mode, kind, cnt = specs[u]
                for k in range(1, N_DEV):
                    peer = _flip(place, k)
                    if mode == "gather":
                        src, dst = src_refs[u], _window(kind, land_refs[u], me, cnt)
                    else:
                        src, dst = _window(kind, src_refs[u], _dev_id(peer), cnt), land_refs[u].at[k - 1]
                    pltpu.make_async_remote_copy(src_ref=src, dst_ref=dst, send_sem=sems[2 * g].at[j], recv_sem=sems[2 * g + 1].at[j],
                                                 device_id=peer, device_id_type=MESH).start()
        token[...] = jnp.zeros_like(token)

    hbm = pl.BlockSpec(memory_space=pltpu.HBM)
    sem = pl.BlockSpec(memory_space=pltpu.SEMAPHORE)
    out_shape = [pltpu.SemaphoreType.DMA((len(units),)) for units in groups for _ in range(2)]
    out_shape += [pltpu.HBM(x.shape, x.dtype) for x in lands] + [jax.ShapeDtypeStruct((8, LANES), F32)]
    outs = pl.pallas_call(
        body, name=name, out_shape=out_shape,
        in_specs=[hbm] * (2 * n), out_specs=[sem] * (2 * n_g) + [hbm] * n + [pl.BlockSpec(memory_space=pltpu.VMEM)],
        input_output_aliases={n + u: 2 * n_g + u for u in range(n)},
        compiler_params=pltpu.CompilerParams(has_side_effects=_EFFECT),
    )(*[_hbm(x) for x in srcs], *[_hbm(x) for x in lands])
    sem_pairs = [(outs[2 * g], outs[2 * g + 1]) for g in range(n_g)]
    return sem_pairs, list(outs[2 * n_g:2 * n_g + n]), outs[-1]


def _exchange_wait(srcs, lands, specs, sem_pair, after, name):
    n = len(lands)

    def body(*refs):
        land_refs = refs[n:2 * n]
        send_sems, recv_sems = refs[2 * n], refs[2 * n + 1]
        place = _my_place()
        for u in range(n):
            _wait_all(_seven(specs[u], land_refs[u]), send_sems.at[u], recv_sems.at[u], place)

    hbm = pl.BlockSpec(memory_space=pltpu.HBM)
    sem = pl.BlockSpec(memory_space=pltpu.SEMAPHORE)
    outs = pl.pallas_call(
        body, name=name, out_shape=[pltpu.HBM(x.shape, x.dtype) for x in lands],
        in_specs=[hbm] * (2 * n) + [sem, sem, pl.BlockSpec(memory_space=pl.ANY)], out_specs=[hbm] * n,
        input_output_aliases={n + u: u for u in range(n)},
        compiler_params=pltpu.CompilerParams(has_side_effects=_EFFECT),
    )(*[_hbm(x) for x in srcs], *lands, sem_pair[0], sem_pair[1], after)
    return list(outs)


def _all_reduce_small(g, name):
    r, w = g.shape

    def body(g_ref, out_ref, buf_ref, send_sems, recv_sems):
        place = _my_place()
        me = 4 * place[0] + 2 * place[1] + place[2]
        buf_ref[me] = g_ref[...]
        copies = []
        for k in range(1, N_DEV):
            copies.append(pltpu.make_async_remote_copy(
                src_ref=g_ref, dst_ref=buf_ref.at[me],
                send_sem=send_sems.at[k - 1], recv_sem=recv_sems.at[k - 1], device_id=_flip(place, k), device_id_type=MESH))
        for cp in copies:
            cp.start()
        for cp in copies:
            cp.wait()
        acc = buf_ref[0]
        for d in range(1, N_DEV):
            acc = acc + buf_ref[d]
        out_ref[...] = acc

    return pl.pallas_call(
        body, name=name,
        out_shape=jax.ShapeDtypeStruct((r, w), F32),
        in_specs=[pl.BlockSpec(memory_space=pltpu.VMEM)], out_specs=pl.BlockSpec(memory_space=pltpu.VMEM),
        scratch_shapes=[pltpu.VMEM((N_DEV, r, w), F32), pltpu.SemaphoreType.DMA((7,)), pltpu.SemaphoreType.DMA((7,))],
    )(g)


def _adamw_math(w, m, v, g):
    m_new = ADAM_B1 * m + (1.0 - ADAM_B1) * g
    v_new = ADAM_B2 * v + (1.0 - ADAM_B2) * (g * g)
    m_hat = m_new / (1.0 - ADAM_B1 ** ADAM_STEP)
    v_hat = v_new / (1.0 - ADAM_B2 ** ADAM_STEP)
    return -ADAM_LR * (m_hat / (jnp.sqrt(v_hat) + ADAM_EPS) + ADAM_WD * w), m_new, v_new


def _adamw(w, m, v, g, name):
    rows = w.shape[0]
    tr = min(FLAT_ROWS, rows)
    assert rows % tr == 0, (name, rows)

    def body(w_ref, m_ref, v_ref, g_ref, d_out, m_out, v_out):
        d_out[...], m_out[...], v_out[...] = _adamw_math(w_ref[...], m_ref[...], v_ref[...], g_ref[...])

    flat = pl.BlockSpec((tr, LANES), lambda i: (i, 0))
    return pl.pallas_call(
        body, name=name, grid=(rows // tr,), in_specs=[flat] * 4, out_specs=[flat] * 3,
        out_shape=[jax.ShapeDtypeStruct((rows, LANES), F32)] * 3,
        compiler_params=_params("parallel"),
    )(w, m, v, g)


def _adamw_shard(w, m, v, g_own, recv, layer, prev, name):
    depth, a, b = w.shape
    ta = min(256, a)
    assert a % ta == 0

    def body(w_ref, m_ref, v_ref, g_ref, r_ref, *rest):
        g_out, d_out, m_out, v_out = rest[-4:]
        g = g_ref[...]
        for k in range(N_DEV - 1):
            g = g + r_ref[k].astype(F32)
        g_out[0] = g
        d_out[0], m_out[0], v_out[0] = _adamw_math(w_ref[0], m_ref[0], v_ref[0], g)

    lay = pl.BlockSpec((1, ta, b), lambda i: (layer, i, 0))
    in_specs = [lay] * 3 + [pl.BlockSpec((ta, b), lambda i: (i, 0)), pl.BlockSpec((N_DEV - 1, ta, b), lambda i: (0, i, 0))]
    args = [w, m, v, g_own, recv]
    aliases = {}
    if prev is not None:
        in_specs += [pl.BlockSpec(memory_space=pl.ANY)] * 4
        args += list(prev)
        aliases = {5 + i: i for i in range(4)}
    return pl.pallas_call(
        body, name=name, grid=(a // ta,), in_specs=in_specs, out_specs=[lay] * 4,
        out_shape=[jax.ShapeDtypeStruct(w.shape, F32)] * 4, input_output_aliases=aliases,
        compiler_params=_params("parallel"),
    )(*args)


def _round_up(n, mult):
    return (n + mult - 1) // mult * mult


def _flatten(parts, row_mult):
    flat = jnp.concatenate([p.reshape(-1) for p in parts])
    rows = _round_up(-(-flat.shape[0] // LANES), row_mult)
    return jnp.pad(flat, (0, rows * LANES - flat.shape[0])).reshape(rows, LANES)


def _unflatten(flat, shapes):
    flat = flat.reshape(-1)
    out, off = [], 0
    for shp in shapes:
        n = 1
        for dim in shp:
            n *= dim
        out.append(flat[off:off + n].reshape(shp))
        off += n
    return out


def kernel(x, norm1_g, w_in, b_f, q_norm_g, k_norm_g, conv_w, conv_b, conv_ln_g, conv_ln_b, w_o, norm2_g, w_mlp_in, w_mlp_out, loss_target, m_norm1_g, m_w_in, m_b_f, m_q_norm_g, m_k_norm_g, m_conv_w, m_conv_b, m_conv_ln_g, m_conv_ln_b, m_w_o, m_norm2_g, m_w_mlp_in, m_w_mlp_out, v_norm1_g, v_w_in, v_b_f, v_q_norm_g, v_k_norm_g, v_conv_w, v_conv_b, v_conv_ln_g, v_conv_ln_b, v_w_o, v_norm2_g, v_w_mlp_in, v_w_mlp_out):
    depth, d_model, n_in_loc = w_in.shape
    seq = x.shape[1]
    n_heads = b_f.shape[1]
    aw = n_heads * HEAD_DIM
    cc = conv_b.shape[1]
    n_in = n_in_loc * N_DEV
    o_f = 3 * aw
    n_all = 3 * aw + 2 * cc + LANES
    assert n_in == 3 * aw + n_heads + 2 * cc and aw + cc == d_model and n_heads % 2 == 0
    assert aw % LANES == 0 and cc % LANES == 0 and x.shape[0] == 1
    me = 4 * lax.axis_index("x") + 2 * lax.axis_index("y") + lax.axis_index("c")

    d_ff = w_mlp_in.shape[2] * N_DEV

    r_o, f_1, f_2 = w_o.shape[1], w_mlp_in.shape[2], w_mlp_out.shape[1]

    ag_src, ag_land, ag_spec = [], [], []
    for l in range(depth):
        ag_src += [w_in[l].astype(BF16), w_o[l].astype(BF16), w_mlp_in[l].astype(BF16), w_mlp_out[l].astype(BF16)]
        ag_land += [(N_DEV, d_model, n_in_loc), (N_DEV * r_o, d_model), (d_model, N_DEV * f_1), (N_DEV * f_2, d_model)]
        ag_spec += [("gather", "slot", 1), ("gather", "rows", r_o), ("gather", "cols", f_1), ("gather", "rows", f_2)]
    ag_src.append(jnp.stack(_split3(conv_w)))
    ag_land.append((N_DEV, 3) + conv_w.shape)
    ag_spec.append(("gather", "slot", 1))
    ag_groups = [[0, 4 * depth], [1], [2, 3]] + [[4 * l + i for i in range(4)] for l in range(1, depth)]

    def own_placed(src, shape, spec):
        buf = lax.empty(shape, BF16)
        if spec[1] == "slot":
            return lax.dynamic_update_index_in_dim(buf, src, me, 0)
        return lax.dynamic_update_slice_in_dim(buf, src, me * spec[2], 0 if spec[1] == "rows" else 1)

    ag_land = [own_placed(src, shp, spec) for src, shp, spec in zip(ag_src, ag_land, ag_spec)]
    ag_sems, ag_land, ag_token = _exchange_start(ag_src, ag_land, ag_spec, ag_groups, "gather_start")

    def gathered(g, after):
        units = ag_groups[g]
        return _exchange_wait([ag_src[u] for u in units], [ag_land[u] for u in units], [ag_spec[u] for u in units],
                              ag_sems[g], after, f"gather_wait_{g}")

    def to_all(w):
        return jnp.concatenate([w[:, :o_f], w[:, o_f + n_heads:], w[:, o_f:o_f + n_heads],
                                jnp.zeros((w.shape[0], LANES - n_heads), w.dtype)], axis=1)

    def from_all(w):
        return jnp.concatenate([w[:, :o_f], w[:, n_all - LANES:n_all - LANES + n_heads], w[:, o_f:n_all - LANES]], axis=1)

    def whole_in(lin):
        return to_all(jnp.moveaxis(lin, 0, 1).reshape(d_model, n_in))

    def row(p, l, width=None):
        v = p[l].reshape(1, -1)
        return v if width is None else jnp.pad(v, ((0, 0), (0, width - v.shape[1])))

    a_col, g_col = 3 * aw // LANES, (3 * aw + cc) // LANES

    gq = [jnp.tile(row(q_norm_g, l), (1, n_heads)) for l in range(depth)]
    gk = [jnp.tile(row(k_norm_g, l), (1, n_heads)) for l in range(depth)]
    bfp = [row(b_f, l, LANES) for l in range(depth)]
    add_res = lambda acc, res: (acc + res,)
    w_all, w_out, w_ff1, w_ff2 = [None] * depth, [None] * depth, [None] * depth, [None] * depth

    h = x[0]
    saved = []
    for l in range(depth):
        u1 = _rms_fwd(h, row(norm1_g, l) + ag_token[0, 0] if l == 0 else row(norm1_g, l), f"rms1_fwd_{l}")
        if l == 0:
            lin, lc = gathered(0, u1)
            lc = lc.astype(F32)
            conv_full = jnp.moveaxis(lc[:, 0] + lc[:, 1] + lc[:, 2], 0, 2).reshape(depth, CONV_TAPS, cc)
            w32 = [jnp.pad(conv_full[i], ((0, CONV_PAD - CONV_TAPS), (0, 0))) for i in range(depth)]
        else:
            lin, w_out[l], w_ff1[l], w_ff2[l] = gathered(2 + l, u1)
        w_all[l] = whole_in(lin)
        proj = _matmul(u1, w_all[l], mode="nn", out_dtypes=(F32,), name=f"mm_in_{l}", tn=n_all)
        qa, ka, vb = _prep_fwd(proj, gq[l], gk[l], bfp[l], n_heads, f"prep_fwd_{l}")
        att, lse = _attn_fwd(qa, ka, vb, n_heads, f"attn_fwd_{l}")
        yc = _conv_fwd(proj, w32[l], row(conv_b, l), cc, a_col, g_col, f"conv_fwd_{l}")
        hc = _ln_silu_fwd(yc, row(conv_ln_g, l), row(conv_ln_b, l), f"ln_silu_fwd_{l}")
        mixed = jnp.concatenate([att.astype(BF16), hc], axis=1)
        if l == 0:
            w_out[l], = gathered(1, mixed)
        x1 = _matmul(mixed, w_out[l], mode="nn", out_dtypes=(F32,), name=f"mm_o_{l}", epilogue=add_res, extras=(h,))
        u2 = _rms_fwd(x1, row(norm2_g, l), f"rms2_fwd_{l}")
        if l == 0:
            w_ff1[l], w_ff2[l] = gathered(2, u2)
        r, a = _matmul(u2, w_ff1[l], mode="nn", out_dtypes=(BF16, BF16), name=f"mm_ff1_{l}", tm=256, tn=d_ff,
                       epilogue=lambda acc: (jnp.maximum(acc, 0.0), jnp.square(jnp.maximum(acc, 0.0))))
        if l < depth - 1:
            x2 = _matmul(a, w_ff2[l], mode="nn", out_dtypes=(F32,), name=f"mm_ff2_{l}", epilogue=add_res, extras=(x1,), tk=d_ff)
        else:
            x2 = None
            dh, dh16, sq = _matmul_loss(a, w_ff2[l], x1, loss_target[0], f"mm_ff2_loss_{l}")
        saved.append(dict(x_in=h, u1=u1, proj=proj, qa=qa, ka=ka, vb=vb, att=att, lse=lse, yc=yc, mixed=mixed,
                          x1=x1, u2=u2, r=r, a=a))
        h = x2

    loss = lax.psum(0.5 * jnp.sum(sq) / d_model, ("x", "y", "c"))

    g_in, g_in_own = [None] * depth, [None] * depth
    g_o, g_1, g_2 = [None] * depth, [None] * depth, [None] * depth
    gs = {n: [None] * depth for n in ("norm1", "bf", "qn", "kn", "convw", "convb", "lng", "lnb", "norm2")}
    both = (F32, BF16)
    scattering = {}

    def scatter_start(stage, l, srcs, specs, slabs):
        lands = [lax.empty((N_DEV - 1,) + shp, BF16) for shp in slabs]
        sems, lands, token = _exchange_start(srcs, lands, specs, [list(range(len(srcs)))], f"scatter_start_{stage}_{l}")
        scattering[(stage, l)] = (srcs, lands, specs, sems[0])
        return token[0, 0]

    for l in reversed(range(depth)):
        sv = saved[l]
        dh1 = _matmul(dh16, w_ff2[l], mode="nt", out_dtypes=(BF16,), name=f"mm_dff2_{l}", tm=256, tn=d_ff,
                      epilogue=lambda acc, rr: (acc * (2.0 * rr.astype(F32)),), extras=(sv["r"],))
        g_2[l] = _matmul(sv["a"], dh16, mode="tn", out_dtypes=both, name=f"mm_dw2_{l}", tm=1024, tk=seq)
        g_1[l] = _matmul(sv["u2"], dh1, mode="tn", out_dtypes=both, name=f"mm_dw1_{l}", tm=d_model, tk=seq)
        tok = scatter_start("ff", l, [g_1[l][1], g_2[l][1]], [("scatter", "cols", f_1), ("scatter", "rows", f_2)],
                            [(d_model, f_1), (f_2, d_model)])
        dx1, dx16, gs["norm2"][l] = _matmul_rms_bwd(dh1, w_ff1[l], sv["x1"], row(norm2_g, l) + tok, dh, f"mm_du2_rms_{l}")

        dmixed = _matmul(dx16, w_out[l], mode="nt", out_dtypes=(F32,), name=f"mm_dmixed_{l}")
        g_o[l] = _matmul(sv["mixed"], dx16, mode="tn", out_dtypes=both, name=f"mm_dwo_{l}", tm=1024, tk=seq // 2)
        dyc, gs["lng"][l], gs["lnb"][l] = _ln_silu_bwd(sv["yc"], row(conv_ln_g, l), row(conv_ln_b, l), dmixed, f"ln_silu_bwd_{l}")
        dpa, dpg, dw32 = _conv_bwd(sv["proj"], w32[l], dyc, cc, a_col, g_col, f"conv_bwd_{l}")
        gs["convw"][l], gs["convb"][l] = dw32[:CONV_TAPS], dw32[CONV_TAPS:CONV_TAPS + 1]
        dq, dka, dv, drow, dcol = _attn_bwd(sv["qa"], sv["ka"], sv["vb"], sv["att"], sv["lse"], dmixed, n_heads, f"attn_bwd_{l}")
        dpq, dpk, dpf, dgq, dgk, dbf = _prep_bwd(sv["proj"], dq, dka, drow, dcol, gq[l], gk[l], bfp[l], n_heads, f"prep_bwd_{l}")
        gs["qn"][l] = dgq.reshape(n_heads, HEAD_DIM).sum(axis=0)
        gs["kn"][l] = dgk.reshape(n_heads, HEAD_DIM).sum(axis=0)
        gs["bf"][l] = dbf[0, :n_heads]
        dproj = jnp.concatenate([dpq, dpk, dv.astype(BF16), dpa, dpg, dpf], axis=1)
        dwall, dwall16 = _matmul(sv["u1"], dproj, mode="tn", out_dtypes=both, name=f"mm_dwall_{l}", tm=d_model, tn=n_all // 3, tk=seq)
        g_in[l] = jnp.moveaxis(from_all(dwall16).reshape(d_model, N_DEV, n_in_loc), 1, 0)
        g_in_own[l] = lax.dynamic_slice_in_dim(from_all(dwall), me * n_in_loc, n_in_loc, axis=1)
        tok = scatter_start("mix", l, [g_in[l], g_o[l][1]], [("scatter", "slot", 1), ("scatter", "rows", r_o)],
                            [(d_model, n_in_loc), (r_o, d_model)])
        dh, dh16, gs["norm1"][l] = _matmul_rms_bwd(dproj, w_all[l], sv["x_in"], row(norm1_g, l) + tok, dx1, f"mm_du1_rms_{l}")
    grad_x = dh[None]

    def landed(stage, l, after):
        srcs, lands, specs, sems = scattering[(stage, l)]
        return _exchange_wait(srcs, lands, specs, sems, after, f"scatter_wait_{stage}_{l}")

    def adamw_layers(kd, w, m, v, own, recv):
        outs = None
        for l in reversed(range(depth)):
            outs = _adamw_shard(w, m, v, own[l], recv[l], l, outs, f"adamw_{kd}_{l}")
        return outs

    recv_1, recv_2, recv_in, recv_o = [None] * depth, [None] * depth, [None] * depth, [None] * depth
    for l in reversed(range(depth)):
        recv_1[l], recv_2[l] = landed("ff", l, dh)
    out_1 = adamw_layers("1", w_mlp_in, m_w_mlp_in, v_w_mlp_in,
                         [lax.dynamic_slice_in_dim(g_1[l][0], me * f_1, f_1, axis=1) for l in range(depth)], recv_1)
    out_2 = adamw_layers("2", w_mlp_out, m_w_mlp_out, v_w_mlp_out,
                         [lax.dynamic_slice_in_dim(g_2[l][0], me * f_2, f_2, axis=0) for l in range(depth)], recv_2)
    for l in reversed(range(depth)):
        recv_in[l], recv_o[l] = landed("mix", l, out_2[1])
    out_in = adamw_layers("in", w_in, m_w_in, v_w_in, g_in_own, recv_in)
    out_o = adamw_layers("o", w_o, m_w_o, v_w_o,
                         [lax.dynamic_slice_in_dim(g_o[l][0], me * r_o, r_o, axis=0) for l in range(depth)], recv_o)
    big_out = [[outs[kind] for outs in (out_in, out_o, out_1, out_2)] for kind in range(4)]

    small_g = [jnp.stack(gs[n]).reshape(shp) for n, shp in (
        ("norm1", norm1_g.shape), ("bf", b_f.shape), ("qn", q_norm_g.shape), ("kn", k_norm_g.shape),
        ("convw", (depth, CONV_TAPS, cc)), ("convb", conv_b.shape), ("lng", conv_ln_g.shape), ("lnb", conv_ln_b.shape),
        ("norm2", norm2_g.shape))]
    small_shapes = [g.shape for g in small_g]
    small_g = _unflatten(_all_reduce_small(_flatten(small_g, 8), "all_reduce_small"), small_shapes)
    cw = conv_w.shape[2]
    small_g[4] = lax.dynamic_slice_in_dim(small_g[4], me * cw, cw, axis=2)
    small = (norm1_g, b_f, q_norm_g, k_norm_g, conv_w, conv_b, conv_ln_g, conv_ln_b, norm2_g)
    small_m = (m_norm1_g, m_b_f, m_q_norm_g, m_k_norm_g, m_conv_w, m_conv_b, m_conv_ln_g, m_conv_ln_b, m_norm2_g)
    small_v = (v_norm1_g, v_b_f, v_q_norm_g, v_k_norm_g, v_conv_w, v_conv_b, v_conv_ln_g, v_conv_ln_b, v_norm2_g)
    small_out = _adamw(_flatten(small, 8), _flatten(small_m, 8), _flatten(small_v, 8), _flatten(small_g, 8), "adamw_small")
    small_out = [small_g] + [_unflatten(o, [w.shape for w in small]) for o in small_out]

    def group(kind):
        s_, b_ = small_out[kind], big_out[kind]
        return [s_[0], b_[0], s_[1], s_[2], s_[3], s_[4], s_[5], s_[6], s_[7], b_[1], s_[8], b_[2], b_[3]]

    return (loss, grad_x, *group(0), *group(1), *group(2), *group(3))
```

```python
import functools

import jax
import jax.numpy as jnp
from jax import lax
from jax.experimental import pallas as pl
from jax.experimental.pallas import tpu as pltpu

F32 = jnp.float32
BF16 = jnp.bfloat16

EPS = 1e-6
HEAD_DIM = 64
LANES = 128
PAIR = 2 * LANES
N_DEV = 8
CONV_TAPS = 31
CONV_PAD = 32
NEG = -1e30

ADAM_LR = 0.001
ADAM_B1 = 0.9
ADAM_B2 = 0.999
ADAM_EPS = 1e-08
ADAM_WD = 0.01
ADAM_STEP = 10

TM = 512
TQ = 512
CONV_ROWS = 128
FLAT_ROWS = 1024
MESH = pl.DeviceIdType.MESH


def _params(*sem):
    return pltpu.CompilerParams(dimension_semantics=sem, vmem_limit_bytes=56 * 1024 * 1024)


def _split3(x):
    hi = x.astype(BF16)
    r1 = x - hi.astype(F32)
    mid = r1.astype(BF16)
    lo = (r1 - mid.astype(F32)).astype(BF16)
    return hi, mid, lo


def _dot(a, b):
    return jnp.dot(a, b, preferred_element_type=F32)


def _dot_nt(a, b):
    return lax.dot_general(a, b, (((1,), (1,)), ((), ())), preferred_element_type=F32)


def _dot_tn(a, b):
    return lax.dot_general(a, b, (((0,), (0,)), ((), ())), preferred_element_type=F32)


def _dot3(x, mat):
    hi, mid, lo = _split3(x)
    return _dot(hi, mat) + _dot(mid, mat) + _dot(lo, mat)


def _dot2(x, mat):
    hi = x.astype(BF16)
    lo = (x - hi.astype(F32)).astype(BF16)
    return _dot(hi, mat) + _dot(lo, mat)


def _dot3_r(mat, x):
    hi, mid, lo = _split3(x)
    return _dot(mat, hi) + _dot(mat, mid) + _dot(mat, lo)


def _iota(shape, dim):
    return lax.broadcasted_iota(jnp.int32, shape, dim)


def _sigmoid(x):
    return 1.0 / (1.0 + jnp.exp(-x))


def _matmul(a, b, *, mode, out_dtypes, name, epilogue=None, extras=(), tm=TM, tn=1024, tk=1024):
    if mode == "nn":
        (m, k), (k2, n) = a.shape, b.shape
    elif mode == "nt":
        (m, k), (n, k2) = a.shape, b.shape
    else:
        (k, m), (k2, n) = a.shape, b.shape
    assert k == k2, (name, a.shape, b.shape)
    tm, tn, tk = min(tm, m), min(tn, n), min(tk, k)
    assert m % tm == 0 and n % tn == 0 and k % tk == 0, (name, m, n, k, tm, tn, tk)
    nk = k // tk
    a_mode = dict(pipeline_mode=pl.Buffered(1)) if (m == tm and nk == 1) else {}
    b_mode = dict(pipeline_mode=pl.Buffered(1)) if (n == tn and nk == 1) else {}
    if mode == "tn":
        a_spec = pl.BlockSpec((tk, tm), lambda i, j, kk: (kk, i), **a_mode)
    else:
        a_spec = pl.BlockSpec((tm, tk), lambda i, j, kk: (i, kk), **a_mode)
    if mode == "nt":
        b_spec = pl.BlockSpec((tn, tk), lambda i, j, kk: (j, kk), **b_mode)
    else:
        b_spec = pl.BlockSpec((tk, tn), lambda i, j, kk: (kk, j), **b_mode)
    dot = {"nn": _dot, "nt": _dot_nt, "tn": _dot_tn}[mode]
    tile = pl.BlockSpec((tm, tn), lambda i, j, kk: (i, j))
    n_ex, n_out = len(extras), len(out_dtypes)
    acc_in_out = nk > 1 and epilogue is None and out_dtypes[0] == F32

    def body(a_ref, b_ref, *rest):
        ex_refs, out_refs = rest[:n_ex], rest[n_ex:n_ex + n_out]
        part = dot(a_ref[...].astype(BF16), b_ref[...].astype(BF16))

        def finish(acc):
            res = epilogue(acc, *[e[...] for e in ex_refs]) if epilogue is not None else (acc,) * n_out
            for o_ref, r in zip(out_refs, res):
                o_ref[...] = r.astype(o_ref.dtype)

        if nk == 1:
            finish(part)
        else:
            acc_ref = out_refs[0] if acc_in_out else rest[-1]
            kk = pl.program_id(2)

            @pl.when(kk == 0)
            def _():
                acc_ref[...] = part

            @pl.when(kk > 0)
            def _():
                acc_ref[...] += part

            @pl.when(kk == nk - 1)
            def _():
                if acc_in_out:
                    for o_ref in out_refs[1:]:
                        o_ref[...] = acc_ref[...].astype(o_ref.dtype)
                else:
                    finish(acc_ref[...])

    outs = pl.pallas_call(
        body,
        name=name,
        grid=(m // tm, n // tn, nk),
        in_specs=[a_spec, b_spec] + [tile] * n_ex,
        out_specs=[tile] * n_out,
        out_shape=[jax.ShapeDtypeStruct((m, n), dt) for dt in out_dtypes],
        scratch_shapes=[pltpu.VMEM((tm, tn), F32)] if nk > 1 and not acc_in_out else [],
        compiler_params=_params("parallel", "parallel", "arbitrary"),
    )(a, b, *extras)
    return outs if n_out > 1 else outs[0]


def _rms_matmul(x, g, b, *, out_dtypes, name, epilogue=None, tm=TM):
    s, d = x.shape
    n = b.shape[1]
    ts = min(tm, s)
    n_out = len(out_dtypes)

    def body(x_ref, g_ref, b_ref, u_ref, *out_refs):
        xv = x_ref[...]
        u = (xv * lax.rsqrt(jnp.mean(xv * xv, axis=-1, keepdims=True) + EPS) * g_ref[...]).astype(BF16)
        u_ref[...] = u
        acc = _dot(u, b_ref[...])
        res = epilogue(acc) if epilogue is not None else (acc,)
        for o_ref, r in zip(out_refs, res):
            o_ref[...] = r.astype(o_ref.dtype)

    row = lambda w: pl.BlockSpec((ts, w), lambda i: (i, 0))
    return pl.pallas_call(
        body, name=name, grid=(s // ts,),
        in_specs=[row(d), pl.BlockSpec((1, d), lambda i: (0, 0)), pl.BlockSpec((d, n), lambda i: (0, 0), pipeline_mode=pl.Buffered(1))],
        out_specs=[row(d)] + [row(n)] * n_out,
        out_shape=[jax.ShapeDtypeStruct((s, d), BF16)] + [jax.ShapeDtypeStruct((s, n), dt) for dt in out_dtypes],
        compiler_params=_params("parallel"),
    )(x, g, b)


def _matmul_rms_bwd(a, b, x, g, dres, name):
    s, k = a.shape
    d = b.shape[0]
    ts = min(TM, s)

    def body(a_ref, b_ref, x_ref, g_ref, dres_ref, dx_ref, dx16_ref, dg_ref):
        @pl.when(pl.program_id(0) == 0)
        def _():
            dg_ref[...] = jnp.zeros_like(dg_ref)

        duv = _dot_nt(a_ref[...], b_ref[...])
        xv = x_ref[...]
        r = lax.rsqrt(jnp.mean(xv * xv, axis=-1, keepdims=True) + EPS)
        xh = xv * r
        dxh = duv * g_ref[...]
        dx = dres_ref[...] + r * (dxh - xh * jnp.mean(dxh * xh, axis=-1, keepdims=True))
        dx_ref[...] = dx
        dx16_ref[...] = dx.astype(BF16)
        dg_ref[...] += jnp.sum(duv * xh, axis=0, keepdims=True)

    row = pl.BlockSpec((ts, d), lambda i: (i, 0))
    vec = pl.BlockSpec((1, d), lambda i: (0, 0))
    return pl.pallas_call(
        body, name=name, grid=(s // ts,),
        in_specs=[pl.BlockSpec((ts, k), lambda i: (i, 0)), pl.BlockSpec((d, k), lambda i: (0, 0), pipeline_mode=pl.Buffered(1)),
                  row, vec, row],
        out_specs=[row, row, vec],
        out_shape=[jax.ShapeDtypeStruct((s, d), F32), jax.ShapeDtypeStruct((s, d), BF16), jax.ShapeDtypeStruct((1, d), F32)],
        compiler_params=_params("arbitrary"),
    )(a, b, x, g, dres)


def _matmul_loss(a, b, res, target, name):
    s, k = a.shape
    d = b.shape[1]
    ts = min(TM, s)

    def body(a_ref, b_ref, res_ref, t_ref, dy_ref, dy16_ref, sq_ref):
        @pl.when(pl.program_id(0) == 0)
        def _():
            sq_ref[...] = jnp.zeros_like(sq_ref)

        err = _dot(a_ref[...], b_ref[...]) + res_ref[...] - t_ref[...]
        dy = err * (1.0 / d)
        dy_ref[...] = dy
        dy16_ref[...] = dy.astype(BF16)
        sq_ref[...] += jnp.sum(err * err, axis=0, keepdims=True)

    row = pl.BlockSpec((ts, d), lambda i: (i, 0))
    vec = pl.BlockSpec((1, d), lambda i: (0, 0))
    return pl.pallas_call(
        body, name=name, grid=(s // ts,),
        in_specs=[pl.BlockSpec((ts, k), lambda i: (i, 0)), pl.BlockSpec((k, d), lambda i: (0, 0), pipeline_mode=pl.Buffered(1)),
                  row, row],
        out_specs=[row, row, vec],
        out_shape=[jax.ShapeDtypeStruct((s, d), F32), jax.ShapeDtypeStruct((s, d), BF16), jax.ShapeDtypeStruct((1, d), F32)],
        compiler_params=_params("arbitrary"),
    )(a, b, res, target)


def _ln_silu_fwd(y, g, b, mixed, name):
    s, c = y.shape
    ts = min(TM, s)
    assert mixed.shape == (s, 2 * c)

    def body(y_ref, g_ref, b_ref, mixed_ref, h_ref):
        yv = y_ref[...]
        mu = jnp.mean(yv, axis=-1, keepdims=True)
        yc = yv - mu
        z = yc * lax.rsqrt(jnp.mean(yc * yc, axis=-1, keepdims=True) + EPS) * g_ref[...] + b_ref[...]
        h_ref[...] = (z * _sigmoid(z)).astype(BF16)

    row = pl.BlockSpec((ts, c), lambda i: (i, 0))
    vec = pl.BlockSpec((1, c), lambda i: (0, 0))
    return pl.pallas_call(
        body, name=name, grid=(s // ts,), in_specs=[row, vec, vec, pl.BlockSpec(memory_space=pl.ANY)],
        out_specs=pl.BlockSpec((ts, c), lambda i: (i, 1)),
        out_shape=jax.ShapeDtypeStruct((s, 2 * c), BF16), input_output_aliases={3: 0},
        compiler_params=_params("parallel"),
    )(y, g, b, mixed)


def _ln_silu_bwd(y, g, b, dmixed, name):
    s, c = y.shape
    ts = min(TM, s)

    def body(y_ref, g_ref, b_ref, dh_ref, dy_ref, dg_ref, db_ref):
        @pl.when(pl.program_id(0) == 0)
        def _():
            dg_ref[...] = jnp.zeros_like(dg_ref)
            db_ref[...] = jnp.zeros_like(db_ref)

        yv = y_ref[...]
        mu = jnp.mean(yv, axis=-1, keepdims=True)
        yc = yv - mu
        r = lax.rsqrt(jnp.mean(yc * yc, axis=-1, keepdims=True) + EPS)
        yh = yc * r
        z = yh * g_ref[...] + b_ref[...]
        sg = _sigmoid(z)
        dz = dh_ref[...] * (sg * (1.0 + z * (1.0 - sg)))
        dg_ref[...] += jnp.sum(dz * yh, axis=0, keepdims=True)
        db_ref[...] += jnp.sum(dz, axis=0, keepdims=True)
        dyh = dz * g_ref[...]
        dy_ref[...] = r * (dyh - jnp.mean(dyh, axis=-1, keepdims=True) - yh * jnp.mean(dyh * yh, axis=-1, keepdims=True))

    row = pl.BlockSpec((ts, c), lambda i: (i, 0))
    vec = pl.BlockSpec((1, c), lambda i: (0, 0))
    return pl.pallas_call(
        body, name=name, grid=(s // ts,),
        in_specs=[row, vec, vec, pl.BlockSpec((ts, c), lambda i: (i, 1))], out_specs=[row, vec, vec],
        out_shape=[jax.ShapeDtypeStruct((s, c), F32), jax.ShapeDtypeStruct((1, c), F32), jax.ShapeDtypeStruct((1, c), F32)],
        compiler_params=_params("arbitrary"),
    )(y, g, b, dmixed)


def _head_masks():
    lane2 = _iota((1, PAIR), 1)
    lane1 = _iota((1, LANES), 1)
    qa = (lane2 < HEAD_DIM) | ((lane2 >= LANES) & (lane2 < LANES + 3))
    qb = ((lane2 >= HEAD_DIM) & (lane2 < LANES)) | ((lane2 >= LANES + 3) & (lane2 < LANES + 6))
    return (qa, qb), (lane1 < HEAD_DIM, lane1 >= HEAD_DIM)


def _group_matrix(width):
    shift = HEAD_DIM.bit_length() - 1
    return ((_iota((width, width), 0) >> shift) == (_iota((width, width), 1) >> shift)).astype(BF16)


def _prep_fwd(proj, gq, gk, bf, n_heads, name):
    s = proj.shape[0]
    aw = n_heads * HEAD_DIM
    n_pairs = n_heads // 2
    ts = min(TM, s)
    f_col = (proj.shape[1] - LANES) // LANES

    def body(q_ref, k_ref, v_ref, f_ref, gq_ref, gk_ref, bf_ref, qa_ref, ka_ref, vb_ref, carry_ref):
        @pl.when(pl.program_id(0) == 0)
        def _():
            carry_ref[...] = jnp.zeros_like(carry_ref)

        gmat = _group_matrix(aw)

        def head_norm(xv, g):
            ms = _dot2(xv * xv, gmat) * (1.0 / HEAD_DIM)
            return xv * lax.rsqrt(ms + EPS) * g

        qn = head_norm(q_ref[...], gq_ref[...]) * (HEAD_DIM ** -0.5)
        kn = head_norm(k_ref[...], gk_ref[...])
        z = f_ref[...] + bf_ref[...]
        logf = jnp.minimum(z, 0.0) - jnp.log(1.0 + jnp.exp(-jnp.abs(z)))
        tri = (_iota((ts, ts), 0) >= _iota((ts, ts), 1)).astype(BF16)
        c = _dot3_r(tri, logf) + carry_ref[...]
        carry_ref[...] = c[ts - 1:ts, :]
        terms = _split3(-c)
        row, col = _iota((LANES, LANES), 0), _iota((LANES, LANES), 1)
        ones = jnp.where(_iota((ts, LANES), 1) < 6, 1.0, 0.0).astype(BF16)
        for p in range(n_pairs):
            extra = jnp.zeros((ts, LANES), F32)
            for t, term in enumerate(terms):
                sel = ((row == 2 * p) & (col == t)) | ((row == 2 * p + 1) & (col == 3 + t))
                extra += _dot(term, sel.astype(BF16))
            lo, hi = p * PAIR, p * PAIR + LANES
            ka_ref[:, lo:hi] = kn[:, p * LANES:(p + 1) * LANES].astype(BF16)
            ka_ref[:, hi:hi + LANES] = extra.astype(BF16)
            qa_ref[:, lo:hi] = qn[:, p * LANES:(p + 1) * LANES].astype(BF16)
            qa_ref[:, hi:hi + LANES] = ones
            vb_ref[:, lo:hi] = v_ref[:, p * LANES:(p + 1) * LANES].astype(BF16)
            vb_ref[:, hi:hi + LANES] = jnp.ones((ts, LANES), BF16)

    blk = lambda j: pl.BlockSpec((ts, aw), lambda i: (i, j))
    vec = lambda w: pl.BlockSpec((1, w), lambda i: (0, 0))
    return pl.pallas_call(
        body, name=name, grid=(s // ts,),
        in_specs=[blk(0), blk(1), blk(2), pl.BlockSpec((ts, LANES), lambda i: (i, f_col)), vec(aw), vec(aw), vec(LANES)],
        out_specs=[pl.BlockSpec((ts, n_pairs * PAIR), lambda i: (i, 0))] * 3,
        out_shape=[jax.ShapeDtypeStruct((s, n_pairs * PAIR), BF16)] * 3,
        scratch_shapes=[pltpu.VMEM((1, LANES), F32)],
        compiler_params=_params("arbitrary"),
    )(proj, proj, proj, proj, gq, gk, bf)


def _prep_bwd(proj, dq, dka, drow, dcol, gq, gk, bf, n_heads, name):
    s = proj.shape[0]
    aw = n_heads * HEAD_DIM
    n_pairs = n_heads // 2
    ts = min(TM, s)
    nt = s // ts
    f_col = (proj.shape[1] - LANES) // LANES
    shift = HEAD_DIM.bit_length() - 1

    def body(q_ref, k_ref, f_ref, dq_ref, dka_ref, drow_ref, dcol_ref, gq_ref, gk_ref, bf_ref,
             dpq_ref, dpk_ref, dpf_ref, dgq_ref, dgk_ref, dbf_ref, carry_ref):
        @pl.when(pl.program_id(0) == 0)
        def _():
            carry_ref[...] = jnp.zeros_like(carry_ref)
            dgq_ref[...] = jnp.zeros_like(dgq_ref)
            dgk_ref[...] = jnp.zeros_like(dgk_ref)
            dbf_ref[...] = jnp.zeros_like(dbf_ref)

        gmat = _group_matrix(aw)

        def head_norm_bwd(xv, g, dn):
            r = lax.rsqrt(_dot2(xv * xv, gmat) * (1.0 / HEAD_DIM) + EPS)
            xh = xv * r
            dxh = dn * g
            dx = r * (dxh - xh * (_dot2(dxh * xh, gmat) * (1.0 / HEAD_DIM)))
            return dx, jnp.sum(dn * xh, axis=0, keepdims=True)

        dkav = dka_ref[...]
        dx, dg = head_norm_bwd(q_ref[...], gq_ref[...], dq_ref[...] * (HEAD_DIM ** -0.5))
        dpq_ref[...] = dx.astype(BF16)
        dgq_ref[...] += dg
        dkn = jnp.concatenate([dkav[:, p * PAIR:p * PAIR + LANES] for p in range(n_pairs)], axis=1)
        dx, dg = head_norm_bwd(k_ref[...], gk_ref[...], dkn)
        dpk_ref[...] = dx.astype(BF16)
        dgk_ref[...] += dg

        pick = (_iota((aw, LANES), 0) == (_iota((aw, LANES), 1) << shift)).astype(BF16)
        dc = _dot3(drow_ref[...], pick)
        r16, c16 = _iota((16, LANES), 0), _iota((16, LANES), 1)
        for p in range(n_pairs):
            place = ((r16 < 2) & (c16 == 2 * p + r16)).astype(BF16)
            for term in _split3(dcol_ref[p]):
                dc -= _dot_tn(term, place)
        triu = (_iota((ts, ts), 0) <= _iota((ts, ts), 1)).astype(BF16)
        dlogf = _dot3_r(triu, dc) + carry_ref[...]
        carry_ref[...] = dlogf[0:1, :]
        z = f_ref[...] + bf_ref[...]
        dz = dlogf * (1.0 / (1.0 + jnp.exp(z)))
        dpf_ref[...] = dz.astype(BF16)
        dbf_ref[...] += jnp.sum(dz, axis=0, keepdims=True)

    rev = lambda w, j: pl.BlockSpec((ts, w), lambda i: (nt - 1 - i, j))
    vec = lambda w: pl.BlockSpec((1, w), lambda i: (0, 0))
    return pl.pallas_call(
        body, name=name, grid=(nt,),
        in_specs=[rev(aw, 0), rev(aw, 1), rev(LANES, f_col), rev(aw, 0), rev(n_pairs * PAIR, 0), rev(aw, 0),
                  pl.BlockSpec((n_pairs, 16, ts), lambda i: (0, 0, nt - 1 - i)), vec(aw), vec(aw), vec(LANES)],
        out_specs=[rev(aw, 0), rev(aw, 0), rev(LANES, 0), vec(aw), vec(aw), vec(LANES)],
        out_shape=[jax.ShapeDtypeStruct((s, aw), BF16), jax.ShapeDtypeStruct((s, aw), BF16), jax.ShapeDtypeStruct((s, LANES), BF16),
                   jax.ShapeDtypeStruct((1, aw), F32), jax.ShapeDtypeStruct((1, aw), F32), jax.ShapeDtypeStruct((1, LANES), F32)],
        scratch_shapes=[pltpu.VMEM((1, LANES), F32)],
        compiler_params=_params("arbitrary"),
    )(proj, proj, proj, dq, dka, drow, dcol, gq, gk, bf)


def _attn_fwd(qa, ka, vb, n_heads, mix_width, name):
    s = qa.shape[0]
    aw = n_heads * HEAD_DIM
    n_pairs = n_heads // 2
    tq = min(TQ, s)

    def body(q_ref, k_ref, v_ref, o_ref, lse_ref, o16_ref):
        i = pl.program_id(1)
        qmasks, omasks = _head_masks()
        qv = q_ref[...]
        causal = _iota((tq, tq), 1) <= _iota((tq, tq), 0)
        qhs = [jnp.where(qmasks[h], qv, jnp.zeros_like(qv)) for h in range(2)]

        def scores(j):
            kv = k_ref[pl.ds(pl.multiple_of(j * tq, tq), tq), :]
            return tuple(_dot_nt(qhs[h], kv) for h in range(2))

        def update(j, state, scs, masked):
            vv = v_ref[pl.ds(pl.multiple_of(j * tq, tq), tq), :]
            out = []
            for h in range(2):
                m, acc = state[h]
                sc = jnp.where(causal, scs[h], NEG) if masked else scs[h]
                m_new = jnp.maximum(m, jnp.max(sc, axis=1, keepdims=True))
                p = jnp.exp(sc - m_new).astype(BF16)
                out.append((m_new, jnp.exp(m - m_new) * acc + _dot(p, vv)))
            return tuple(out)

        def body(j, state):
            return update(j, state, scores(j), False)

        init = ((jnp.full((tq, 1), NEG, F32), jnp.zeros((tq, PAIR), F32)),) * 2
        state = lax.fori_loop(0, i, body, init)
        res = []
        for m, acc in update(i, state, scores(i), True):
            l = acc[:, LANES:LANES + 1]
            res.append((acc[:, :LANES] * (1.0 / l), m + jnp.log(l)))
        out = jnp.where(omasks[0], res[0][0], res[1][0])
        o_ref[...] = out
        o16_ref[...] = out.astype(BF16)
        lse_ref[...] = jnp.where(omasks[0], res[0][1], res[1][1])

    return pl.pallas_call(
        body, name=name, grid=(n_pairs, s // tq),
        in_specs=[pl.BlockSpec((tq, PAIR), lambda p, i: (i, p)), pl.BlockSpec((s, PAIR), lambda p, i: (0, p)),
                  pl.BlockSpec((s, PAIR), lambda p, i: (0, p))],
        out_specs=[pl.BlockSpec((tq, LANES), lambda p, i: (i, p))] * 3,
        out_shape=[jax.ShapeDtypeStruct((s, aw), F32)] * 2 + [jax.ShapeDtypeStruct((s, mix_width), BF16)],
        compiler_params=_params("parallel", "parallel"),
    )(qa, ka, vb)


def _attn_bwd(qa, ka, vb, o, lse, dmixed, n_heads, name):
    s = qa.shape[0]
    aw = n_heads * HEAD_DIM
    n_pairs = n_heads // 2
    tq = min(TQ, s)
    nq = s // tq

    def body(q_ref, k_ref, v_ref, o_ref, lse_ref, do_ref, dq_ref, dka_ref, dv_ref, drow_ref, dcol_ref, delta_ref):
        j = pl.program_id(1)
        qmasks, omasks = _head_masks()

        @pl.when(j == 0)
        def _():
            dq_ref[...] = jnp.zeros_like(dq_ref)
            drow_ref[...] = jnp.zeros_like(drow_ref)
            for c in range(nq):
                rows = slice(c * tq, (c + 1) * tq)
                prod = do_ref[rows, :] * o_ref[rows, :]
                da = jnp.sum(jnp.where(omasks[0], prod, 0.0), axis=1, keepdims=True)
                db = jnp.sum(jnp.where(omasks[1], prod, 0.0), axis=1, keepdims=True)
                delta_ref[rows, :] = jnp.where(omasks[0], da, db)

        dka_ref[...] = jnp.zeros_like(dka_ref)
        dv_ref[...] = jnp.zeros_like(dv_ref)
        dcol_ref[...] = jnp.zeros_like(dcol_ref)
        kv = k_ref[...]
        kk = kv[:, :LANES]
        vv = v_ref[...]
        causal = _iota((tq, tq), 1) <= _iota((tq, tq), 0)

        def step(i, masked):
            off = pl.multiple_of(i * tq, tq)
            qv = q_ref[pl.ds(off, tq), :]
            dov = do_ref[pl.ds(off, tq), :]
            lsev = lse_ref[pl.ds(off, tq), :]
            dlv = delta_ref[pl.ds(off, tq), :]
            for h in range(2):
                qh = jnp.where(qmasks[h], qv, jnp.zeros_like(qv))
                doh = jnp.where(omasks[h], dov, 0.0).astype(BF16)
                lane = h * HEAD_DIM
                sc = _dot_nt(qh, kv)
                if masked:
                    sc = jnp.where(causal, sc, NEG)
                p = jnp.exp(sc - lsev[:, lane:lane + 1])
                dv_ref[...] += _dot_tn(p.astype(BF16), doh)
                dp = _dot_nt(doh, vv)
                dsf = p * (dp - dlv[:, lane:lane + 1])
                drow_ref[pl.ds(off, tq), :] += jnp.where(omasks[h], jnp.sum(dsf, axis=1, keepdims=True), 0.0)
                dcol_ref[0, h:h + 1, :] += jnp.sum(dsf, axis=0, keepdims=True)
                ds = dsf.astype(BF16)
                dka_ref[...] += _dot_tn(ds, qh)
                dq_ref[pl.ds(off, tq), :] += jnp.where(omasks[h], _dot(ds, kk), 0.0)

        step(j, True)

        def loop_body(i, carry):
            step(i, False)
            return carry

        lax.fori_loop(j + 1, nq, loop_body, 0)

    full = lambda w: pl.BlockSpec((s, w), lambda p, j: (0, p))
    blk = lambda w: pl.BlockSpec((tq, w), lambda p, j: (j, p))
    return pl.pallas_call(
        body, name=name, grid=(n_pairs, nq),
        in_specs=[full(PAIR), blk(PAIR), pl.BlockSpec((tq, LANES), lambda p, j: (j, 2 * p)), full(LANES), full(LANES), full(LANES)],
        out_specs=[full(LANES), blk(PAIR), blk(LANES), full(LANES), pl.BlockSpec((1, 16, tq), lambda p, j: (p, 0, j))],
        out_shape=[jax.ShapeDtypeStruct((s, aw), F32), jax.ShapeDtypeStruct((s, n_pairs * PAIR), F32),
                   jax.ShapeDtypeStruct((s, aw), F32), jax.ShapeDtypeStruct((s, aw), F32),
                   jax.ShapeDtypeStruct((n_pairs, 16, s), F32)],
        scratch_shapes=[pltpu.VMEM((s, LANES), F32)],
        compiler_params=_params("parallel", "arbitrary"),
    )(qa, ka, vb, o, lse, dmixed)


def _conv_fwd(proj, w32, bias, n_ch, a_col, g_col, name):
    s = proj.shape[0]
    rows = min(CONV_ROWS, s)

    def body(a_ref, g_ref, w_ref, b_ref, y_ref, pad_ref):
        pad_ref[0:CONV_PAD, :] = jnp.zeros((CONV_PAD, LANES), F32)
        pad_ref[CONV_PAD:CONV_PAD + s, :] = a_ref[...] * _sigmoid(g_ref[...])
        wv = w_ref[...]
        for c in range(s // rows):
            acc = jnp.broadcast_to(b_ref[...], (rows, LANES))
            for t in range(CONV_TAPS):
                start = c * rows + CONV_PAD - (CONV_TAPS - 1) + t
                acc = acc + wv[t:t + 1, :] * pad_ref[start:start + rows, :]
            y_ref[c * rows:(c + 1) * rows, :] = acc

    col = lambda j0: pl.BlockSpec((s, LANES), lambda c: (0, j0 + c))
    return pl.pallas_call(
        body, name=name, grid=(n_ch // LANES,),
        in_specs=[col(a_col), col(g_col), pl.BlockSpec((CONV_PAD, LANES), lambda c: (0, c)), pl.BlockSpec((1, LANES), lambda c: (0, c))],
        out_specs=pl.BlockSpec((s, LANES), lambda c: (0, c)),
        out_shape=jax.ShapeDtypeStruct((s, n_ch), F32),
        scratch_shapes=[pltpu.VMEM((s + CONV_PAD, LANES), F32)],
        compiler_params=_params("parallel"),
    )(proj, proj, w32, bias)


def _conv_bwd(proj, w32, dy, n_ch, a_col, g_col, name):
    s = proj.shape[0]
    rows = min(CONV_ROWS, s)
    sub = 8

    def fold(x):
        acc = x[0:sub, :]
        for r in range(1, rows // sub):
            acc = acc + x[r * sub:(r + 1) * sub, :]
        return acc

    def body(a_ref, g_ref, w_ref, dy_ref, da_ref, dg_ref, dw_ref, padh_ref, padd_ref):
        sg = _sigmoid(g_ref[...])
        padh_ref[0:CONV_PAD, :] = jnp.zeros((CONV_PAD, LANES), F32)
        padh_ref[CONV_PAD:CONV_PAD + s, :] = a_ref[...] * sg
        padd_ref[0:s, :] = dy_ref[...]
        padd_ref[s:s + CONV_PAD, :] = jnp.zeros((CONV_PAD, LANES), F32)
        wv = w_ref[...]
        dw = [jnp.zeros((sub, LANES), F32) for _ in range(CONV_TAPS + 1)]
        for c in range(s // rows):
            r0 = c * rows
            acc = jnp.zeros((rows, LANES), F32)
            dyc = dy_ref[r0:r0 + rows, :]
            for t in range(CONV_TAPS):
                back = r0 + (CONV_TAPS - 1) - t
                acc = acc + wv[t:t + 1, :] * padd_ref[back:back + rows, :]
                start = r0 + CONV_PAD - (CONV_TAPS - 1) + t
                dw[t] = dw[t] + fold(dyc * padh_ref[start:start + rows, :])
            dw[CONV_TAPS] = dw[CONV_TAPS] + fold(dyc)
            av = a_ref[r0:r0 + rows, :]
            sgc = _sigmoid(g_ref[r0:r0 + rows, :])
            da_ref[r0:r0 + rows, :] = (acc * sgc).astype(BF16)
            dg_ref[r0:r0 + rows, :] = (acc * av * sgc * (1.0 - sgc)).astype(BF16)
        for t in range(CONV_TAPS + 1):
            dw_ref[t:t + 1, :] = jnp.sum(dw[t], axis=0, keepdims=True)

    col = lambda j0: pl.BlockSpec((s, LANES), lambda c: (0, j0 + c))
    wspec = pl.BlockSpec((CONV_PAD, LANES), lambda c: (0, c))
    return pl.pallas_call(
        body, name=name, grid=(n_ch // LANES,),
        in_specs=[col(a_col), col(g_col), wspec, col(0)],
        out_specs=[col(0), col(0), wspec],
        out_shape=[jax.ShapeDtypeStruct((s, n_ch), BF16), jax.ShapeDtypeStruct((s, n_ch), BF16),
                   jax.ShapeDtypeStruct((CONV_PAD, n_ch), F32)],
        scratch_shapes=[pltpu.VMEM((s + CONV_PAD, LANES), F32), pltpu.VMEM((s + CONV_PAD, LANES), F32)],
        compiler_params=_params("parallel"),
    )(proj, proj, w32, dy)


def _my_place():
    return lax.axis_index("x"), lax.axis_index("y"), lax.axis_index("c")


def _flip(place, k):
    x, y, c = place
    return (1 - x if k & 4 else x, 1 - y if k & 2 else y, 1 - c if k & 1 else c)


def _dev_id(place):
    return 4 * place[0] + 2 * place[1] + place[2]


def _wait_all(ref, send_sem, recv_sem, place):
    pltpu.make_async_remote_copy(src_ref=ref, dst_ref=ref, send_sem=send_sem, recv_sem=recv_sem,
                                 device_id=place, device_id_type=MESH).wait()


def _window(kind, ref, dev, n):
    if kind == "slot":
        return ref.at[dev]
    if kind == "rows":
        return ref.at[pl.ds(pl.multiple_of(dev * n, n), n), :]
    return ref.at[:, pl.ds(pl.multiple_of(dev * n, n), n)]


def _first_peer(spec):
    return 0 if spec[0] == "gather" else 1


def _hbm(x):
    return pltpu.with_memory_space_constraint(x, pltpu.HBM)


_EFFECT = pltpu.SideEffectType.DATAFLOW_SIDE_EFFECTING


def _exchange_start(srcs, lands, specs, groups, name):
    n = len(srcs)
    n_g = len(groups)

    def body(*refs):
        src_refs, land_refs = refs[:n], refs[n:2 * n]
        sems = refs[2 * n:2 * n + 2 * n_g]
        token = refs[-1]
        place = _my_place()
        me = _dev_id(place)
        for g, units in enumerate(groups):
            for j, u in enumerate(units):
                mode, kind, cnt = specs[u]
                for k in range(_first_peer(specs[u]), N_DEV):
                    peer = _flip(place, k)
                    if mode == "gather":
                        src, dst = src_refs[u], _window(kind, land_refs[u], me, cnt)
                    else:
                        src, dst = _window(kind, src_refs[u], _dev_id(peer), cnt), land_refs[u].at[k - 1]
                    pltpu.make_async_remote_copy(src_ref=src, dst_ref=dst, send_sem=sems[2 * g].at[j], recv_sem=sems[2 * g + 1].at[j],
                                                 device_id=peer, device_id_type=MESH).start()
        token[...] = jnp.zeros_like(token)

    hbm = pl.BlockSpec(memory_space=pltpu.HBM)
    sem = pl.BlockSpec(memory_space=pltpu.SEMAPHORE)
    out_shape = [pltpu.SemaphoreType.DMA((len(units),)) for units in groups for _ in range(2)]
    out_shape += [pltpu.HBM(x.shape, x.dtype) for x in lands] + [jax.ShapeDtypeStruct((8, LANES), F32)]
    outs = pl.pallas_call(
        body, name=name, out_shape=out_shape,
        in_specs=[hbm] * (2 * n), out_specs=[sem] * (2 * n_g) + [hbm] * n + [pl.BlockSpec(memory_space=pltpu.VMEM)],
        input_output_aliases={n + u: 2 * n_g + u for u in range(n)},
        compiler_params=pltpu.CompilerParams(has_side_effects=_EFFECT),
    )(*[_hbm(x) for x in srcs], *[_hbm(x) for x in lands])
    sem_pairs = [(outs[2 * g], outs[2 * g + 1]) for g in range(n_g)]
    return sem_pairs, list(outs[2 * n_g:2 * n_g + n]), outs[-1]


def _exchange_wait(srcs, lands, specs, sem_pair, after, name):
    n = len(lands)

    def body(*refs):
        land_refs = refs[n:2 * n]
        send_sems, recv_sems = refs[2 * n], refs[2 * n + 1]
        place = _my_place()
        for u in range(n):
            _wait_all(land_refs[u], send_sems.at[u], recv_sems.at[u], place)

    hbm = pl.BlockSpec(memory_space=pltpu.HBM)
    sem = pl.BlockSpec(memory_space=pltpu.SEMAPHORE)
    outs = pl.pallas_call(
        body, name=name, out_shape=[pltpu.HBM(x.shape, x.dtype) for x in lands],
        in_specs=[hbm] * (2 * n) + [sem, sem, pl.BlockSpec(memory_space=pl.ANY)], out_specs=[hbm] * n,
        input_output_aliases={n + u: u for u in range(n)},
        compiler_params=pltpu.CompilerParams(has_side_effects=_EFFECT),
    )(*[_hbm(x) for x in srcs], *lands, sem_pair[0], sem_pair[1], after)
    return list(outs)


def _all_reduce_small(g, name):
    r, w = g.shape

    def body(g_ref, out_ref, buf_ref, send_sems, recv_sems):
        place = _my_place()
        me = 4 * place[0] + 2 * place[1] + place[2]
        buf_ref[me] = g_ref[...]
        copies = []
        for k in range(1, N_DEV):
            copies.append(pltpu.make_async_remote_copy(
                src_ref=g_ref, dst_ref=buf_ref.at[me],
                send_sem=send_sems.at[k - 1], recv_sem=recv_sems.at[k - 1], device_id=_flip(place, k), device_id_type=MESH))
        for cp in copies:
            cp.start()
        for cp in copies:
            cp.wait()
        acc = buf_ref[0]
        for d in range(1, N_DEV):
            acc = acc + buf_ref[d]
        out_ref[...] = acc

    return pl.pallas_call(
        body, name=name,
        out_shape=jax.ShapeDtypeStruct((r, w), F32),
        in_specs=[pl.BlockSpec(memory_space=pltpu.VMEM)], out_specs=pl.BlockSpec(memory_space=pltpu.VMEM),
        scratch_shapes=[pltpu.VMEM((N_DEV, r, w), F32), pltpu.SemaphoreType.DMA((7,)), pltpu.SemaphoreType.DMA((7,))],
    )(g)


def _adamw_math(w, m, v, g):
    m_new = ADAM_B1 * m + (1.0 - ADAM_B1) * g
    v_new = ADAM_B2 * v + (1.0 - ADAM_B2) * (g * g)
    m_hat = m_new / (1.0 - ADAM_B1 ** ADAM_STEP)
    v_hat = v_new / (1.0 - ADAM_B2 ** ADAM_STEP)
    return -ADAM_LR * (m_hat / (jnp.sqrt(v_hat) + ADAM_EPS) + ADAM_WD * w), m_new, v_new


def _adamw(w, m, v, g, name):
    rows = w.shape[0]
    tr = min(FLAT_ROWS, rows)
    assert rows % tr == 0, (name, rows)

    def body(w_ref, m_ref, v_ref, g_ref, d_out, m_out, v_out):
        d_out[...], m_out[...], v_out[...] = _adamw_math(w_ref[...], m_ref[...], v_ref[...], g_ref[...])

    flat = pl.BlockSpec((tr, LANES), lambda i: (i, 0))
    return pl.pallas_call(
        body, name=name, grid=(rows // tr,), in_specs=[flat] * 4, out_specs=[flat] * 3,
        out_shape=[jax.ShapeDtypeStruct((rows, LANES), F32)] * 3,
        compiler_params=_params("parallel"),
    )(w, m, v, g)


def _adamw_shard(w, m, v, g_own, recv, layer, prev, name):
    depth, a, b = w.shape
    ta = min(256, a)
    assert a % ta == 0

    def body(w_ref, m_ref, v_ref, g_ref, r_ref, *rest):
        g_out, d_out, m_out, v_out = rest[-4:]
        g = g_ref[...]
        for k in range(N_DEV - 1):
            g = g + r_ref[k].astype(F32)
        g_out[0] = g
        d_out[0], m_out[0], v_out[0] = _adamw_math(w_ref[0], m_ref[0], v_ref[0], g)

    lay = pl.BlockSpec((1, ta, b), lambda i: (layer, i, 0))
    in_specs = [lay] * 3 + [pl.BlockSpec((ta, b), lambda i: (i, 0)), pl.BlockSpec((N_DEV - 1, ta, b), lambda i: (0, i, 0))]
    args = [w, m, v, g_own, recv]
    aliases = {}
    if prev is not None:
        in_specs += [pl.BlockSpec(memory_space=pl.ANY)] * 4
        args += list(prev)
        aliases = {5 + i: i for i in range(4)}
    return pl.pallas_call(
        body, name=name, grid=(a // ta,), in_specs=in_specs, out_specs=[lay] * 4,
        out_shape=[jax.ShapeDtypeStruct(w.shape, F32)] * 4, input_output_aliases=aliases,
        compiler_params=_params("parallel"),
    )(*args)


def _round_up(n, mult):
    return (n + mult - 1) // mult * mult


def _flatten(parts, row_mult):
    flat = jnp.concatenate([p.reshape(-1) for p in parts])
    rows = _round_up(-(-flat.shape[0] // LANES), row_mult)
    return jnp.pad(flat, (0, rows * LANES - flat.shape[0])).reshape(rows, LANES)


def _unflatten(flat, shapes):
    flat = flat.reshape(-1)
    out, off = [], 0
    for shp in shapes:
        n = 1
        for dim in shp:
            n *= dim
        out.append(flat[off:off + n].reshape(shp))
        off += n
    return out


def kernel(x, norm1_g, w_in, b_f, q_norm_g, k_norm_g, conv_w, conv_b, conv_ln_g, conv_ln_b, w_o, norm2_g, w_mlp_in, w_mlp_out, loss_target, m_norm1_g, m_w_in, m_b_f, m_q_norm_g, m_k_norm_g, m_conv_w, m_conv_b, m_conv_ln_g, m_conv_ln_b, m_w_o, m_norm2_g, m_w_mlp_in, m_w_mlp_out, v_norm1_g, v_w_in, v_b_f, v_q_norm_g, v_k_norm_g, v_conv_w, v_conv_b, v_conv_ln_g, v_conv_ln_b, v_w_o, v_norm2_g, v_w_mlp_in, v_w_mlp_out):
    depth, d_model, n_in_loc = w_in.shape
    seq = x.shape[1]
    n_heads = b_f.shape[1]
    aw = n_heads * HEAD_DIM
    cc = conv_b.shape[1]
    n_in = n_in_loc * N_DEV
    o_f = 3 * aw
    n_all = 3 * aw + 2 * cc + LANES
    assert n_in == 3 * aw + n_heads + 2 * cc and aw + cc == d_model and n_heads % 2 == 0
    assert aw % LANES == 0 and cc % LANES == 0 and x.shape[0] == 1
    me = 4 * lax.axis_index("x") + 2 * lax.axis_index("y") + lax.axis_index("c")

    d_ff = w_mlp_in.shape[2] * N_DEV

    r_o, f_1, f_2 = w_o.shape[1], w_mlp_in.shape[2], w_mlp_out.shape[1]

    ag_src, ag_land, ag_spec = [], [], []
    for l in range(depth):
        ag_src += [w_in[l].astype(BF16), w_o[l].astype(BF16), w_mlp_in[l].astype(BF16), w_mlp_out[l].astype(BF16)]
        ag_land += [(N_DEV, d_model, n_in_loc), (N_DEV * r_o, d_model), (d_model, N_DEV * f_1), (N_DEV * f_2, d_model)]
        ag_spec += [("gather", "slot", 1), ("gather", "rows", r_o), ("gather", "cols", f_1), ("gather", "rows", f_2)]
    ag_src.append(jnp.stack(_split3(conv_w)))
    ag_land.append((N_DEV, 3) + conv_w.shape)
    ag_spec.append(("gather", "slot", 1))
    ag_groups = [[0, 4 * depth], [1], [2, 3]] + [[4 * l + i for i in range(4)] for l in range(1, depth)]
    ag_land = [lax.empty(shp, BF16) for shp in ag_land]
    ag_sems, ag_land, ag_token = _exchange_start(ag_src, ag_land, ag_spec, ag_groups, "gather_start")

    def gathered(g, after):
        units = ag_groups[g]
        return _exchange_wait([ag_src[u] for u in units], [ag_land[u] for u in units], [ag_spec[u] for u in units],
                              ag_sems[g], after, f"gather_wait_{g}")

    def to_all(w):
        return jnp.concatenate([w[:, :o_f], w[:, o_f + n_heads:], w[:, o_f:o_f + n_heads],
                                jnp.zeros((w.shape[0], LANES - n_heads), w.dtype)], axis=1)

    def from_all(w):
        return jnp.concatenate([w[:, :o_f], w[:, n_all - LANES:n_all - LANES + n_heads], w[:, o_f:n_all - LANES]], axis=1)

    def whole_in(lin):
        return to_all(jnp.moveaxis(lin, 0, 1).reshape(d_model, n_in))

    def row(p, l, width=None):
        v = p[l].reshape(1, -1)
        return v if width is None else jnp.pad(v, ((0, 0), (0, width - v.shape[1])))

    a_col, g_col = 3 * aw // LANES, (3 * aw + cc) // LANES

    gq = [jnp.tile(row(q_norm_g, l), (1, n_heads)) for l in range(depth)]
    gk = [jnp.tile(row(k_norm_g, l), (1, n_heads)) for l in range(depth)]
    bfp = [row(b_f, l, LANES) for l in range(depth)]
    add_res = lambda acc, res: (acc + res,)
    w_all, w_out, w_ff1, w_ff2 = [None] * depth, [None] * depth, [None] * depth, [None] * depth

    h = x[0]
    saved = []
    for l in range(depth):
        if l == 0:
            lin, lc = gathered(0, h)
            lc = lc.astype(F32)
            conv_full = jnp.moveaxis(lc[:, 0] + lc[:, 1] + lc[:, 2], 0, 2).reshape(depth, CONV_TAPS, cc)
            w32 = [jnp.pad(conv_full[i], ((0, CONV_PAD - CONV_TAPS), (0, 0))) for i in range(depth)]
        else:
            lin, w_out[l], w_ff1[l], w_ff2[l] = gathered(2 + l, h)
        w_all[l] = whole_in(lin)
        u1, proj = _rms_matmul(h, row(norm1_g, l), w_all[l], out_dtypes=(F32,), name=f"mm_in_{l}")
        qa, ka, vb = _prep_fwd(proj, gq[l], gk[l], bfp[l], n_heads, f"prep_fwd_{l}")
        att, lse, mixed = _attn_fwd(qa, ka, vb, n_heads, aw + cc, f"attn_fwd_{l}")
        yc = _conv_fwd(proj, w32[l], row(conv_b, l), cc, a_col, g_col, f"conv_fwd_{l}")
        mixed = _ln_silu_fwd(yc, row(conv_ln_g, l), row(conv_ln_b, l), mixed, f"ln_silu_fwd_{l}")
        if l == 0:
            w_out[l], = gathered(1, mixed)
        x1 = _matmul(mixed, w_out[l], mode="nn", out_dtypes=(F32,), name=f"mm_o_{l}", epilogue=add_res, extras=(h,))
        if l == 0:
            w_ff1[l], w_ff2[l] = gathered(2, x1)
        u2, r, a = _rms_matmul(x1, row(norm2_g, l), w_ff1[l], out_dtypes=(BF16, BF16), name=f"mm_ff1_{l}", tm=256,
                               epilogue=lambda acc: (jnp.maximum(acc, 0.0), jnp.square(jnp.maximum(acc, 0.0))))
        if l < depth - 1:
            x2 = _matmul(a, w_ff2[l], mode="nn", out_dtypes=(F32,), name=f"mm_ff2_{l}", epilogue=add_res, extras=(x1,), tk=d_ff)
        else:
            x2 = None
            dh, dh16, sq = _matmul_loss(a, w_ff2[l], x1, loss_target[0], f"mm_ff2_loss_{l}")
        saved.append(dict(x_in=h, u1=u1, proj=proj, qa=qa, ka=ka, vb=vb, att=att, lse=lse, yc=yc, mixed=mixed,
                          x1=x1, u2=u2, r=r, a=a))
        h = x2

    loss = lax.psum(0.5 * jnp.sum(sq) / d_model, ("x", "y", "c"))

    g_in, g_in_own = [None] * depth, [None] * depth
    g_o, g_1, g_2 = [None] * depth, [None] * depth, [None] * depth
    gs = {n: [None] * depth for n in ("norm1", "bf", "qn", "kn", "convw", "convb", "lng", "lnb", "norm2")}
    both = (F32, BF16)
    scattering = {}

    def scatter_start(stage, l, srcs, specs, slabs):
        lands = [lax.empty((N_DEV - 1,) + shp, BF16) for shp in slabs]
        sems, lands, token = _exchange_start(srcs, lands, specs, [list(range(len(srcs)))], f"scatter_start_{stage}_{l}")
        scattering[(stage, l)] = (srcs, lands, specs, sems[0])
        return token[0, 0]

    for l in reversed(range(depth)):
        sv = saved[l]
        dh1 = _matmul(dh16, w_ff2[l], mode="nt", out_dtypes=(BF16,), name=f"mm_dff2_{l}", tm=256, tn=d_ff,
                      epilogue=lambda acc, rr: (acc * (2.0 * rr.astype(F32)),), extras=(sv["r"],))
        g_2[l] = _matmul(sv["a"], dh16, mode="tn", out_dtypes=both, name=f"mm_dw2_{l}", tm=1024, tk=seq)
        g_1[l] = _matmul(sv["u2"], dh1, mode="tn", out_dtypes=both, name=f"mm_dw1_{l}", tm=d_model, tk=seq)
        tok = scatter_start("ff", l, [g_1[l][1], g_2[l][1]], [("scatter", "cols", f_1), ("scatter", "rows", f_2)],
                            [(d_model, f_1), (f_2, d_model)])
        dx1, dx16, gs["norm2"][l] = _matmul_rms_bwd(dh1, w_ff1[l], sv["x1"], row(norm2_g, l) + tok, dh, f"mm_du2_rms_{l}")

        dmixed = _matmul(dx16, w_out[l], mode="nt", out_dtypes=(F32,), name=f"mm_dmixed_{l}")
        g_o[l] = _matmul(sv["mixed"], dx16, mode="tn", out_dtypes=both, name=f"mm_dwo_{l}", tm=1024, tk=seq // 2)
        dyc, gs["lng"][l], gs["lnb"][l] = _ln_silu_bwd(sv["yc"], row(conv_ln_g, l), row(conv_ln_b, l), dmixed, f"ln_silu_bwd_{l}")
        dpa, dpg, dw32 = _conv_bwd(sv["proj"], w32[l], dyc, cc, a_col, g_col, f"conv_bwd_{l}")
        gs["convw"][l], gs["convb"][l] = dw32[:CONV_TAPS], dw32[CONV_TAPS:CONV_TAPS + 1]
        dq, dka, dv, drow, dcol = _attn_bwd(sv["qa"], sv["ka"], sv["vb"], sv["att"], sv["lse"], dmixed, n_heads, f"attn_bwd_{l}")
        dpq, dpk, dpf, dgq, dgk, dbf = _prep_bwd(sv["proj"], dq, dka, drow, dcol, gq[l], gk[l], bfp[l], n_heads, f"prep_bwd_{l}")
        gs["qn"][l] = dgq.reshape(n_heads, HEAD_DIM).sum(axis=0)
        gs["kn"][l] = dgk.reshape(n_heads, HEAD_DIM).sum(axis=0)
        gs["bf"][l] = dbf[0, :n_heads]
        dproj = jnp.concatenate([dpq, dpk, dv.astype(BF16), dpa, dpg, dpf], axis=1)
        dwall, dwall16 = _matmul(sv["u1"], dproj, mode="tn", out_dtypes=both, name=f"mm_dwall_{l}", tm=d_model, tn=n_all // 3, tk=seq)
        g_in[l] = jnp.moveaxis(from_all(dwall16).reshape(d_model, N_DEV, n_in_loc), 1, 0)
        g_in_own[l] = lax.dynamic_slice_in_dim(from_all(dwall), me * n_in_loc, n_in_loc, axis=1)
        tok = scatter_start("mix", l, [g_in[l], g_o[l][1]], [("scatter", "slot", 1), ("scatter", "rows", r_o)],
                            [(d_model, n_in_loc), (r_o, d_model)])
        dh, dh16, gs["norm1"][l] = _matmul_rms_bwd(dproj, w_all[l], sv["x_in"], row(norm1_g, l) + tok, dx1, f"mm_du1_rms_{l}")
    grad_x = dh[None]

    def landed(stage, l, after):
        srcs, lands, specs, sems = scattering[(stage, l)]
        return _exchange_wait(srcs, lands, specs, sems, after, f"scatter_wait_{stage}_{l}")

    def adamw_layers(kd, w, m, v, own, recv):
        outs = None
        for l in reversed(range(depth)):
            outs = _adamw_shard(w, m, v, own[l], recv[l], l, outs, f"adamw_{kd}_{l}")
        return outs

    recv_1, recv_2, recv_in, recv_o = [None] * depth, [None] * depth, [None] * depth, [None] * depth
    for l in reversed(range(depth)):
        recv_1[l], recv_2[l] = landed("ff", l, dh)
    out_1 = adamw_layers("1", w_mlp_in, m_w_mlp_in, v_w_mlp_in,
                         [lax.dynamic_slice_in_dim(g_1[l][0], me * f_1, f_1, axis=1) for l in range(depth)], recv_1)
    out_2 = adamw_layers("2", w_mlp_out, m_w_mlp_out, v_w_mlp_out,
                         [lax.dynamic_slice_in_dim(g_2[l][0], me * f_2, f_2, axis=0) for l in range(depth)], recv_2)
    for l in reversed(range(depth)):
        recv_in[l], recv_o[l] = landed("mix", l, out_2[1])
    out_in = adamw_layers("in", w_in, m_w_in, v_w_in, g_in_own, recv_in)
    out_o = adamw_layers("o", w_o, m_w_o, v_w_o,
                         [lax.dynamic_slice_in_dim(g_o[l][0], me * r_o, r_o, axis=0) for l in range(depth)], recv_o)
    big_out = [[outs[kind] for outs in (out_in, out_o, out_1, out_2)] for kind in range(4)]

    small_g = [jnp.stack(gs[n]).reshape(shp) for n, shp in (
        ("norm1", norm1_g.shape), ("bf", b_f.shape), ("qn", q_norm_g.shape), ("kn", k_norm_g.shape),
        ("convw", (depth, CONV_TAPS, cc)), ("convb", conv_b.shape), ("lng", conv_ln_g.shape), ("lnb", conv_ln_b.shape),
        ("norm2", norm2_g.shape))]
    small_shapes = [g.shape for g in small_g]
    small_g = _unflatten(_all_reduce_small(_flatten(small_g, 8), "all_reduce_small"), small_shapes)
    cw = conv_w.shape[2]
    small_g[4] = lax.dynamic_slice_in_dim(small_g[4], me * cw, cw, axis=2)
    small = (norm1_g, b_f, q_norm_g, k_norm_g, conv_w, conv_b, conv_ln_g, conv_ln_b, norm2_g)
    small_m = (m_norm1_g, m_b_f, m_q_norm_g, m_k_norm_g, m_conv_w, m_conv_b, m_conv_ln_g, m_conv_ln_b, m_norm2_g)
    small_v = (v_norm1_g, v_b_f, v_q_norm_g, v_k_norm_g, v_conv_w, v_conv_b, v_conv_ln_g, v_conv_ln_b, v_norm2_g)
    small_out = _adamw(_flatten(small, 8), _flatten(small_m, 8), _flatten(small_v, 8), _flatten(small_g, 8), "adamw_small")
    small_out = [small_g] + [_unflatten(o, [w.shape for w in small]) for o in small_out]

    def group(kind):
        s_, b_ = small_out[kind], big_out[kind]
        return [s_[0], b_[0], s_[1], s_[2], s_[3], s_[4], s_[5], s_[6], s_[7], b_[1], s_[8], b_[2], b_[3]]

    return (loss, grad_x, *group(0), *group(1), *group(2), *group(3))
```

```python
import functools

import jax
import jax.numpy as jnp
from jax import lax
from jax.experimental import pallas as pl
from jax.experimental.pallas import tpu as pltpu

F32 = jnp.float32
BF16 = jnp.bfloat16

EPS = 1e-6
HEAD_DIM = 64
LANES = 128
PAIR = 2 * LANES
N_DEV = 8
CONV_TAPS = 31
CONV_PAD = 32
NEG = -1e30

ADAM_LR = 0.001
ADAM_B1 = 0.9
ADAM_B2 = 0.999
ADAM_EPS = 1e-08
ADAM_WD = 0.01
ADAM_STEP = 10

TM = 512
TQ = 512
CONV_ROWS = 128
FLAT_ROWS = 1024
MESH = pl.DeviceIdType.MESH


def _params(*sem):
    return pltpu.CompilerParams(dimension_semantics=sem, vmem_limit_bytes=56 * 1024 * 1024)


def _split3(x):
    hi = x.astype(BF16)
    r1 = x - hi.astype(F32)
    mid = r1.astype(BF16)
    lo = (r1 - mid.astype(F32)).astype(BF16)
    return hi, mid, lo


def _dot(a, b):
    return jnp.dot(a, b, preferred_element_type=F32)


def _dot_nt(a, b):
    return lax.dot_general(a, b, (((1,), (1,)), ((), ())), preferred_element_type=F32)


def _dot_tn(a, b):
    return lax.dot_general(a, b, (((0,), (0,)), ((), ())), preferred_element_type=F32)


def _dot3(x, mat):
    hi, mid, lo = _split3(x)
    return _dot(hi, mat) + _dot(mid, mat) + _dot(lo, mat)


def _dot2(x, mat):
    hi = x.astype(BF16)
    lo = (x - hi.astype(F32)).astype(BF16)
    return _dot(hi, mat) + _dot(lo, mat)


def _dot3_r(mat, x):
    hi, mid, lo = _split3(x)
    return _dot(mat, hi) + _dot(mat, mid) + _dot(mat, lo)


def _iota(shape, dim):
    return lax.broadcasted_iota(jnp.int32, shape, dim)


def _sigmoid(x):
    return 1.0 / (1.0 + jnp.exp(-x))


def _matmul(a, b, *, mode, out_dtypes, name, epilogue=None, extras=(), tm=TM, tn=1024, tk=1024):
    if mode == "nn":
        (m, k), (k2, n) = a.shape, b.shape
    elif mode == "nt":
        (m, k), (n, k2) = a.shape, b.shape
    else:
        (k, m), (k2, n) = a.shape, b.shape
    assert k == k2, (name, a.shape, b.shape)
    tm, tn, tk = min(tm, m), min(tn, n), min(tk, k)
    assert m % tm == 0 and n % tn == 0 and k % tk == 0, (name, m, n, k, tm, tn, tk)
    nk = k // tk
    a_mode = dict(pipeline_mode=pl.Buffered(1)) if (m == tm and nk == 1) else {}
    b_mode = dict(pipeline_mode=pl.Buffered(1)) if (n == tn and nk == 1) else {}
    if mode == "tn":
        a_spec = pl.BlockSpec((tk, tm), lambda i, j, kk: (kk, i), **a_mode)
    else:
        a_spec = pl.BlockSpec((tm, tk), lambda i, j, kk: (i, kk), **a_mode)
    if mode == "nt":
        b_spec = pl.BlockSpec((tn, tk), lambda i, j, kk: (j, kk), **b_mode)
    else:
        b_spec = pl.BlockSpec((tk, tn), lambda i, j, kk: (kk, j), **b_mode)
    dot = {"nn": _dot, "nt": _dot_nt, "tn": _dot_tn}[mode]
    tile = pl.BlockSpec((tm, tn), lambda i, j, kk: (i, j))
    n_ex, n_out = len(extras), len(out_dtypes)
    acc_in_out = nk > 1 and epilogue is None and out_dtypes[0] == F32

    def body(a_ref, b_ref, *rest):
        ex_refs, out_refs = rest[:n_ex], rest[n_ex:n_ex + n_out]
        part = dot(a_ref[...].astype(BF16), b_ref[...].astype(BF16))

        def finish(acc):
            res = epilogue(acc, *[e[...] for e in ex_refs]) if epilogue is not None else (acc,) * n_out
            for o_ref, r in zip(out_refs, res):
                o_ref[...] = r.astype(o_ref.dtype)

        if nk == 1:
            finish(part)
        else:
            acc_ref = out_refs[0] if acc_in_out else rest[-1]
            kk = pl.program_id(2)

            @pl.when(kk == 0)
            def _():
                acc_ref[...] = part

            @pl.when(kk > 0)
            def _():
                acc_ref[...] += part

            @pl.when(kk == nk - 1)
            def _():
                if acc_in_out:
                    for o_ref in out_refs[1:]:
                        o_ref[...] = acc_ref[...].astype(o_ref.dtype)
                else:
                    finish(acc_ref[...])

    outs = pl.pallas_call(
        body,
        name=name,
        grid=(m // tm, n // tn, nk),
        in_specs=[a_spec, b_spec] + [tile] * n_ex,
        out_specs=[tile] * n_out,
        out_shape=[jax.ShapeDtypeStruct((m, n), dt) for dt in out_dtypes],
        scratch_shapes=[pltpu.VMEM((tm, tn), F32)] if nk > 1 and not acc_in_out else [],
        compiler_params=_params("parallel", "parallel", "arbitrary"),
    )(a, b, *extras)
    return outs if n_out > 1 else outs[0]


def _rms_matmul(x, g, b, *, out_dtypes, name, epilogue=None, tm=TM):
    s, d = x.shape
    n = b.shape[1]
    ts = min(tm, s)
    n_out = len(out_dtypes)

    def body(x_ref, g_ref, b_ref, u_ref, *out_refs):
        xv = x_ref[...]
        u = (xv * lax.rsqrt(jnp.mean(xv * xv, axis=-1, keepdims=True) + EPS) * g_ref[...]).astype(BF16)
        u_ref[...] = u
        acc = _dot(u, b_ref[...])
        res = epilogue(acc) if epilogue is not None else (acc,)
        for o_ref, r in zip(out_refs, res):
            o_ref[...] = r.astype(o_ref.dtype)

    row = lambda w: pl.BlockSpec((ts, w), lambda i: (i, 0))
    return pl.pallas_call(
        body, name=name, grid=(s // ts,),
        in_specs=[row(d), pl.BlockSpec((1, d), lambda i: (0, 0)), pl.BlockSpec((d, n), lambda i: (0, 0), pipeline_mode=pl.Buffered(1))],
        out_specs=[row(d)] + [row(n)] * n_out,
        out_shape=[jax.ShapeDtypeStruct((s, d), BF16)] + [jax.ShapeDtypeStruct((s, n), dt) for dt in out_dtypes],
        compiler_params=_params("parallel"),
    )(x, g, b)


def _matmul_rms_bwd(a, b, x, g, dres, name):
    s, k = a.shape
    d = b.shape[0]
    ts = min(TM, s)

    def body(a_ref, b_ref, x_ref, g_ref, dres_ref, dx_ref, dx16_ref, dg_ref):
        @pl.when(pl.program_id(0) == 0)
        def _():
            dg_ref[...] = jnp.zeros_like(dg_ref)

        duv = _dot_nt(a_ref[...], b_ref[...])
        xv = x_ref[...]
        r = lax.rsqrt(jnp.mean(xv * xv, axis=-1, keepdims=True) + EPS)
        xh = xv * r
        dxh = duv * g_ref[...]
        dx = dres_ref[...] + r * (dxh - xh * jnp.mean(dxh * xh, axis=-1, keepdims=True))
        dx_ref[...] = dx
        dx16_ref[...] = dx.astype(BF16)
        dg_ref[...] += jnp.sum(duv * xh, axis=0, keepdims=True)

    row = pl.BlockSpec((ts, d), lambda i: (i, 0))
    vec = pl.BlockSpec((1, d), lambda i: (0, 0))
    return pl.pallas_call(
        body, name=name, grid=(s // ts,),
        in_specs=[pl.BlockSpec((ts, k), lambda i: (i, 0)), pl.BlockSpec((d, k), lambda i: (0, 0), pipeline_mode=pl.Buffered(1)),
                  row, vec, row],
        out_specs=[row, row, vec],
        out_shape=[jax.ShapeDtypeStruct((s, d), F32), jax.ShapeDtypeStruct((s, d), BF16), jax.ShapeDtypeStruct((1, d), F32)],
        compiler_params=_params("arbitrary"),
    )(a, b, x, g, dres)


def _matmul_loss(a, b, res, target, name):
    s, k = a.shape
    d = b.shape[1]
    ts = min(TM, s)

    def body(a_ref, b_ref, res_ref, t_ref, dy_ref, dy16_ref, sq_ref):
        @pl.when(pl.program_id(0) == 0)
        def _():
            sq_ref[...] = jnp.zeros_like(sq_ref)

        err = _dot(a_ref[...], b_ref[...]) + res_ref[...] - t_ref[...]
        dy = err * (1.0 / d)
        dy_ref[...] = dy
        dy16_ref[...] = dy.astype(BF16)
        sq_ref[...] += jnp.sum(err * err, axis=0, keepdims=True)

    row = pl.BlockSpec((ts, d), lambda i: (i, 0))
    vec = pl.BlockSpec((1, d), lambda i: (0, 0))
    return pl.pallas_call(
        body, name=name, grid=(s // ts,),
        in_specs=[pl.BlockSpec((ts, k), lambda i: (i, 0)), pl.BlockSpec((k, d), lambda i: (0, 0), pipeline_mode=pl.Buffered(1)),
                  row, row],
        out_specs=[row, row, vec],
        out_shape=[jax.ShapeDtypeStruct((s, d), F32), jax.ShapeDtypeStruct((s, d), BF16), jax.ShapeDtypeStruct((1, d), F32)],
        compiler_params=_params("arbitrary"),
    )(a, b, res, target)


def _ln_silu_fwd(y, g, b, mixed, name):
    s, c = y.shape
    ts = min(TM, s)
    assert mixed.shape == (s, 2 * c)

    def body(y_ref, g_ref, b_ref, mixed_ref, h_ref):
        yv = y_ref[...]
        mu = jnp.mean(yv, axis=-1, keepdims=True)
        yc = yv - mu
        z = yc * lax.rsqrt(jnp.mean(yc * yc, axis=-1, keepdims=True) + EPS) * g_ref[...] + b_ref[...]
        h_ref[...] = (z * _sigmoid(z)).astype(BF16)

    row = pl.BlockSpec((ts, c), lambda i: (i, 0))
    vec = pl.BlockSpec((1, c), lambda i: (0, 0))
    return pl.pallas_call(
        body, name=name, grid=(s // ts,), in_specs=[row, vec, vec, pl.BlockSpec(memory_space=pl.ANY)],
        out_specs=pl.BlockSpec((ts, c), lambda i: (i, 1)),
        out_shape=jax.ShapeDtypeStruct((s, 2 * c), BF16), input_output_aliases={3: 0},
        compiler_params=_params("parallel"),
    )(y, g, b, mixed)


def _ln_silu_bwd(y, g, b, dmixed, name):
    s, c = y.shape
    ts = min(TM, s)

    def body(y_ref, g_ref, b_ref, dh_ref, dy_ref, dg_ref, db_ref):
        @pl.when(pl.program_id(0) == 0)
        def _():
            dg_ref[...] = jnp.zeros_like(dg_ref)
            db_ref[...] = jnp.zeros_like(db_ref)

        yv = y_ref[...]
        mu = jnp.mean(yv, axis=-1, keepdims=True)
        yc = yv - mu
        r = lax.rsqrt(jnp.mean(yc * yc, axis=-1, keepdims=True) + EPS)
        yh = yc * r
        z = yh * g_ref[...] + b_ref[...]
        sg = _sigmoid(z)
        dz = dh_ref[...] * (sg * (1.0 + z * (1.0 - sg)))
        dg_ref[...] += jnp.sum(dz * yh, axis=0, keepdims=True)
        db_ref[...] += jnp.sum(dz, axis=0, keepdims=True)
        dyh = dz * g_ref[...]
        dy_ref[...] = r * (dyh - jnp.mean(dyh, axis=-1, keepdims=True) - yh * jnp.mean(dyh * yh, axis=-1, keepdims=True))

    row = pl.BlockSpec((ts, c), lambda i: (i, 0))
    vec = pl.BlockSpec((1, c), lambda i: (0, 0))
    return pl.pallas_call(
        body, name=name, grid=(s // ts,),
        in_specs=[row, vec, vec, pl.BlockSpec((ts, c), lambda i: (i, 1))], out_specs=[row, vec, vec],
        out_shape=[jax.ShapeDtypeStruct((s, c), F32), jax.ShapeDtypeStruct((1, c), F32), jax.ShapeDtypeStruct((1, c), F32)],
        compiler_params=_params("arbitrary"),
    )(y, g, b, dmixed)


def _head_masks():
    lane2 = _iota((1, PAIR), 1)
    lane1 = _iota((1, LANES), 1)
    qa = (lane2 < HEAD_DIM) | ((lane2 >= LANES) & (lane2 < LANES + 3))
    qb = ((lane2 >= HEAD_DIM) & (lane2 < LANES)) | ((lane2 >= LANES + 3) & (lane2 < LANES + 6))
    return (qa, qb), (lane1 < HEAD_DIM, lane1 >= HEAD_DIM)


def _group_matrix(width):
    shift = HEAD_DIM.bit_length() - 1
    return ((_iota((width, width), 0) >> shift) == (_iota((width, width), 1) >> shift)).astype(BF16)


def _prep_fwd(proj, gq, gk, bf, n_heads, name):
    s = proj.shape[0]
    aw = n_heads * HEAD_DIM
    n_pairs = n_heads // 2
    ts = min(TM, s)
    f_col = (proj.shape[1] - LANES) // LANES

    def body(q_ref, k_ref, v_ref, f_ref, gq_ref, gk_ref, bf_ref, qa_ref, ka_ref, vb_ref, carry_ref):
        @pl.when(pl.program_id(0) == 0)
        def _():
            carry_ref[...] = jnp.zeros_like(carry_ref)

        gmat = _group_matrix(aw)

        def head_norm(xv, g):
            ms = _dot2(xv * xv, gmat) * (1.0 / HEAD_DIM)
            return xv * lax.rsqrt(ms + EPS) * g

        qn = head_norm(q_ref[...], gq_ref[...]) * (HEAD_DIM ** -0.5)
        kn = head_norm(k_ref[...], gk_ref[...])
        z = f_ref[...] + bf_ref[...]
        logf = jnp.minimum(z, 0.0) - jnp.log(1.0 + jnp.exp(-jnp.abs(z)))
        tri = (_iota((ts, ts), 0) >= _iota((ts, ts), 1)).astype(BF16)
        c = _dot3_r(tri, logf) + carry_ref[...]
        carry_ref[...] = c[ts - 1:ts, :]
        terms = _split3(-c)
        row, col = _iota((LANES, LANES), 0), _iota((LANES, LANES), 1)
        ones = jnp.where(_iota((ts, LANES), 1) < 6, 1.0, 0.0).astype(BF16)
        for p in range(n_pairs):
            extra = jnp.zeros((ts, LANES), F32)
            for t, term in enumerate(terms):
                sel = ((row == 2 * p) & (col == t)) | ((row == 2 * p + 1) & (col == 3 + t))
                extra += _dot(term, sel.astype(BF16))
            lo, hi = p * PAIR, p * PAIR + LANES
            ka_ref[:, lo:hi] = kn[:, p * LANES:(p + 1) * LANES].astype(BF16)
            ka_ref[:, hi:hi + LANES] = extra.astype(BF16)
            qa_ref[:, lo:hi] = qn[:, p * LANES:(p + 1) * LANES].astype(BF16)
            qa_ref[:, hi:hi + LANES] = ones
            vb_ref[:, lo:hi] = v_ref[:, p * LANES:(p + 1) * LANES].astype(BF16)
            vb_ref[:, hi:hi + LANES] = jnp.ones((ts, LANES), BF16)

    blk = lambda j: pl.BlockSpec((ts, aw), lambda i: (i, j))
    vec = lambda w: pl.BlockSpec((1, w), lambda i: (0, 0))
    return pl.pallas_call(
        body, name=name, grid=(s // ts,),
        in_specs=[blk(0), blk(1), blk(2), pl.BlockSpec((ts, LANES), lambda i: (i, f_col)), vec(aw), vec(aw), vec(LANES)],
        out_specs=[pl.BlockSpec((ts, n_pairs * PAIR), lambda i: (i, 0))] * 3,
        out_shape=[jax.ShapeDtypeStruct((s, n_pairs * PAIR), BF16)] * 3,
        scratch_shapes=[pltpu.VMEM((1, LANES), F32)],
        compiler_params=_params("arbitrary"),
    )(proj, proj, proj, proj, gq, gk, bf)


def _prep_bwd(proj, dq, dka, drow, dcol, gq, gk, bf, n_heads, name):
    s = proj.shape[0]
    aw = n_heads * HEAD_DIM
    n_pairs = n_heads // 2
    ts = min(TM, s)
    nt = s // ts
    f_col = (proj.shape[1] - LANES) // LANES
    shift = HEAD_DIM.bit_length() - 1

    def body(q_ref, k_ref, f_ref, dq_ref, dka_ref, drow_ref, dcol_ref, gq_ref, gk_ref, bf_ref,
             dpq_ref, dpk_ref, dpf_ref, dgq_ref, dgk_ref, dbf_ref, carry_ref):
        @pl.when(pl.program_id(0) == 0)
        def _():
            carry_ref[...] = jnp.zeros_like(carry_ref)
            dgq_ref[...] = jnp.zeros_like(dgq_ref)
            dgk_ref[...] = jnp.zeros_like(dgk_ref)
            dbf_ref[...] = jnp.zeros_like(dbf_ref)

        gmat = _group_matrix(aw)

        def head_norm_bwd(xv, g, dn):
            r = lax.rsqrt(_dot2(xv * xv, gmat) * (1.0 / HEAD_DIM) + EPS)
            xh = xv * r
            dxh = dn * g
            dx = r * (dxh - xh * (_dot2(dxh * xh, gmat) * (1.0 / HEAD_DIM)))
            return dx, jnp.sum(dn * xh, axis=0, keepdims=True)

        dkav = dka_ref[...]
        dx, dg = head_norm_bwd(q_ref[...], gq_ref[...], dq_ref[...] * (HEAD_DIM ** -0.5))
        dpq_ref[...] = dx.astype(BF16)
        dgq_ref[...] += dg
        dkn = jnp.concatenate([dkav[:, p * PAIR:p * PAIR + LANES] for p in range(n_pairs)], axis=1)
        dx, dg = head_norm_bwd(k_ref[...], gk_ref[...], dkn)
        dpk_ref[...] = dx.astype(BF16)
        dgk_ref[...] += dg

        pick = (_iota((aw, LANES), 0) == (_iota((aw, LANES), 1) << shift)).astype(BF16)
        dc = _dot3(drow_ref[...], pick)
        r16, c16 = _iota((16, LANES), 0), _iota((16, LANES), 1)
        for p in range(n_pairs):
            place = ((r16 < 2) & (c16 == 2 * p + r16)).astype(BF16)
            for term in _split3(dcol_ref[p]):
                dc -= _dot_tn(term, place)
        triu = (_iota((ts, ts), 0) <= _iota((ts, ts), 1)).astype(BF16)
        dlogf = _dot3_r(triu, dc) + carry_ref[...]
        carry_ref[...] = dlogf[0:1, :]
        z = f_ref[...] + bf_ref[...]
        dz = dlogf * (1.0 / (1.0 + jnp.exp(z)))
        dpf_ref[...] = dz.astype(BF16)
        dbf_ref[...] += jnp.sum(dz, axis=0, keepdims=True)

    rev = lambda w, j: pl.BlockSpec((ts, w), lambda i: (nt - 1 - i, j))
    vec = lambda w: pl.BlockSpec((1, w), lambda i: (0, 0))
    return pl.pallas_call(
        body, name=name, grid=(nt,),
        in_specs=[rev(aw, 0), rev(aw, 1), rev(LANES, f_col), rev(aw, 0), rev(n_pairs * PAIR, 0), rev(aw, 0),
                  pl.BlockSpec((n_pairs, 16, ts), lambda i: (0, 0, nt - 1 - i)), vec(aw), vec(aw), vec(LANES)],
        out_specs=[rev(aw, 0), rev(aw, 0), rev(LANES, 0), vec(aw), vec(aw), vec(LANES)],
        out_shape=[jax.ShapeDtypeStruct((s, aw), BF16), jax.ShapeDtypeStruct((s, aw), BF16), jax.ShapeDtypeStruct((s, LANES), BF16),
                   jax.ShapeDtypeStruct((1, aw), F32), jax.ShapeDtypeStruct((1, aw), F32), jax.ShapeDtypeStruct((1, LANES), F32)],
        scratch_shapes=[pltpu.VMEM((1, LANES), F32)],
        compiler_params=_params("arbitrary"),
    )(proj, proj, proj, dq, dka, drow, dcol, gq, gk, bf)


def _attn_fwd(qa, ka, vb, n_heads, mix_width, name):
    s = qa.shape[0]
    aw = n_heads * HEAD_DIM
    n_pairs = n_heads // 2
    tq = min(TQ, s)

    def body(q_ref, k_ref, v_ref, o_ref, lse_ref, o16_ref):
        i = pl.program_id(1)
        qmasks, omasks = _head_masks()
        qv = q_ref[...]
        causal = _iota((tq, tq), 1) <= _iota((tq, tq), 0)
        qhs = [jnp.where(qmasks[h], qv, jnp.zeros_like(qv)) for h in range(2)]

        def scores(j):
            kv = k_ref[pl.ds(pl.multiple_of(j * tq, tq), tq), :]
            return tuple(_dot_nt(qhs[h], kv) for h in range(2))

        def update(j, state, scs, masked):
            vv = v_ref[pl.ds(pl.multiple_of(j * tq, tq), tq), :]
            out = []
            for h in range(2):
                m, acc = state[h]
                sc = jnp.where(causal, scs[h], NEG) if masked else scs[h]
                m_new = jnp.maximum(m, jnp.max(sc, axis=1, keepdims=True))
                p = jnp.exp(sc - m_new).astype(BF16)
                out.append((m_new, jnp.exp(m - m_new) * acc + _dot(p, vv)))
            return tuple(out)

        def body(j, state):
            return update(j, state, scores(j), False)

        init = ((jnp.full((tq, 1), NEG, F32), jnp.zeros((tq, PAIR), F32)),) * 2
        state = lax.fori_loop(0, i, body, init)
        res = []
        for m, acc in update(i, state, scores(i), True):
            l = acc[:, LANES:LANES + 1]
            res.append((acc[:, :LANES] * (1.0 / l), m + jnp.log(l)))
        out = jnp.where(omasks[0], res[0][0], res[1][0])
        o_ref[...] = out
        o16_ref[...] = out.astype(BF16)
        lse_ref[...] = jnp.where(omasks[0], res[0][1], res[1][1])

    return pl.pallas_call(
        body, name=name, grid=(n_pairs, s // tq),
        in_specs=[pl.BlockSpec((tq, PAIR), lambda p, i: (i, p)), pl.BlockSpec((s, PAIR), lambda p, i: (0, p)),
                  pl.BlockSpec((s, PAIR), lambda p, i: (0, p))],
        out_specs=[pl.BlockSpec((tq, LANES), lambda p, i: (i, p))] * 3,
        out_shape=[jax.ShapeDtypeStruct((s, aw), F32)] * 2 + [jax.ShapeDtypeStruct((s, mix_width), BF16)],
        compiler_params=_params("parallel", "parallel"),
    )(qa, ka, vb)


def _attn_bwd(qa, ka, vb, o, lse, dmixed, n_heads, name):
    s = qa.shape[0]
    aw = n_heads * HEAD_DIM
    n_pairs = n_heads // 2
    tq = min(TQ, s)
    nq = s // tq

    def body(q_ref, k_ref, v_ref, o_ref, lse_ref, do_ref, dq_ref, dka_ref, dv_ref, drow_ref, dcol_ref, delta_ref):
        j = pl.program_id(1)
        qmasks, omasks = _head_masks()

        @pl.when(j == 0)
        def _():
            dq_ref[...] = jnp.zeros_like(dq_ref)
            drow_ref[...] = jnp.zeros_like(drow_ref)
            for c in range(nq):
                rows = slice(c * tq, (c + 1) * tq)
                prod = do_ref[rows, :] * o_ref[rows, :]
                da = jnp.sum(jnp.where(omasks[0], prod, 0.0), axis=1, keepdims=True)
                db = jnp.sum(jnp.where(omasks[1], prod, 0.0), axis=1, keepdims=True)
                delta_ref[rows, :] = jnp.where(omasks[0], da, db)

        dka_ref[...] = jnp.zeros_like(dka_ref)
        dv_ref[...] = jnp.zeros_like(dv_ref)
        dcol_ref[...] = jnp.zeros_like(dcol_ref)
        kv = k_ref[...]
        kk = kv[:, :LANES]
        vv = v_ref[...]
        causal = _iota((tq, tq), 1) <= _iota((tq, tq), 0)

        def step(i, masked):
            off = pl.multiple_of(i * tq, tq)
            qv = q_ref[pl.ds(off, tq), :]
            dov = do_ref[pl.ds(off, tq), :]
            lsev = lse_ref[pl.ds(off, tq), :]
            dlv = delta_ref[pl.ds(off, tq), :]
            for h in range(2):
                qh = jnp.where(qmasks[h], qv, jnp.zeros_like(qv))
                doh = jnp.where(omasks[h], dov, 0.0).astype(BF16)
                lane = h * HEAD_DIM
                sc = _dot_nt(qh, kv)
                if masked:
                    sc = jnp.where(causal, sc, NEG)
                p = jnp.exp(sc - lsev[:, lane:lane + 1])
                dv_ref[...] += _dot_tn(p.astype(BF16), doh)
                dp = _dot_nt(doh, vv)
                dsf = p * (dp - dlv[:, lane:lane + 1])
                drow_ref[pl.ds(off, tq), :] += jnp.where(omasks[h], jnp.sum(dsf, axis=1, keepdims=True), 0.0)
                dcol_ref[0, h:h + 1, :] += jnp.sum(dsf, axis=0, keepdims=True)
                ds = dsf.astype(BF16)
                dka_ref[...] += _dot_tn(ds, qh)
                dq_ref[pl.ds(off, tq), :] += jnp.where(omasks[h], _dot(ds, kk), 0.0)

        step(j, True)

        def loop_body(i, carry):
            step(i, False)
            return carry

        lax.fori_loop(j + 1, nq, loop_body, 0)

    full = lambda w: pl.BlockSpec((s, w), lambda p, j: (0, p))
    blk = lambda w: pl.BlockSpec((tq, w), lambda p, j: (j, p))
    return pl.pallas_call(
        body, name=name, grid=(n_pairs, nq),
        in_specs=[full(PAIR), blk(PAIR), pl.BlockSpec((tq, LANES), lambda p, j: (j, 2 * p)), full(LANES), full(LANES), full(LANES)],
        out_specs=[full(LANES), blk(PAIR), blk(LANES), full(LANES), pl.BlockSpec((1, 16, tq), lambda p, j: (p, 0, j))],
        out_shape=[jax.ShapeDtypeStruct((s, aw), F32), jax.ShapeDtypeStruct((s, n_pairs * PAIR), F32),
                   jax.ShapeDtypeStruct((s, aw), F32), jax.ShapeDtypeStruct((s, aw), F32),
                   jax.ShapeDtypeStruct((n_pairs, 16, s), F32)],
        scratch_shapes=[pltpu.VMEM((s, LANES), F32)],
        compiler_params=_params("parallel", "arbitrary"),
    )(qa, ka, vb, o, lse, dmixed)


def _conv_fwd(proj, w32, bias, n_ch, a_col, g_col, name):
    s = proj.shape[0]
    rows = min(CONV_ROWS, s)

    def body(a_ref, g_ref, w_ref, b_ref, y_ref, pad_ref):
        pad_ref[0:CONV_PAD, :] = jnp.zeros((CONV_PAD, LANES), F32)
        pad_ref[CONV_PAD:CONV_PAD + s, :] = a_ref[...] * _sigmoid(g_ref[...])
        wv = w_ref[...]
        for c in range(s // rows):
            acc = jnp.broadcast_to(b_ref[...], (rows, LANES))
            for t in range(CONV_TAPS):
                start = c * rows + CONV_PAD - (CONV_TAPS - 1) + t
                acc = acc + wv[t:t + 1, :] * pad_ref[start:start + rows, :]
            y_ref[c * rows:(c + 1) * rows, :] = acc

    col = lambda j0: pl.BlockSpec((s, LANES), lambda c: (0, j0 + c))
    return pl.pallas_call(
        body, name=name, grid=(n_ch // LANES,),
        in_specs=[col(a_col), col(g_col), pl.BlockSpec((CONV_PAD, LANES), lambda c: (0, c)), pl.BlockSpec((1, LANES), lambda c: (0, c))],
        out_specs=pl.BlockSpec((s, LANES), lambda c: (0, c)),
        out_shape=jax.ShapeDtypeStruct((s, n_ch), F32),
        scratch_shapes=[pltpu.VMEM((s + CONV_PAD, LANES), F32)],
        compiler_params=_params("parallel"),
    )(proj, proj, w32, bias)


def _conv_bwd(proj, w32, dy, n_ch, a_col, g_col, name):
    s = proj.shape[0]
    rows = min(CONV_ROWS, s)
    sub = 8

    def fold(x):
        acc = x[0:sub, :]
        for r in range(1, rows // sub):
            acc = acc + x[r * sub:(r + 1) * sub, :]
        return acc

    def body(a_ref, g_ref, w_ref, dy_ref, da_ref, dg_ref, dw_ref, padh_ref, padd_ref):
        sg = _sigmoid(g_ref[...])
        padh_ref[0:CONV_PAD, :] = jnp.zeros((CONV_PAD, LANES), F32)
        padh_ref[CONV_PAD:CONV_PAD + s, :] = a_ref[...] * sg
        padd_ref[0:s, :] = dy_ref[...]
        padd_ref[s:s + CONV_PAD, :] = jnp.zeros((CONV_PAD, LANES), F32)
        wv = w_ref[...]
        dw = [jnp.zeros((sub, LANES), F32) for _ in range(CONV_TAPS + 1)]
        for c in range(s // rows):
            r0 = c * rows
            acc = jnp.zeros((rows, LANES), F32)
            dyc = dy_ref[r0:r0 + rows, :]
            for t in range(CONV_TAPS):
                back = r0 + (CONV_TAPS - 1) - t
                acc = acc + wv[t:t + 1, :] * padd_ref[back:back + rows, :]
                start = r0 + CONV_PAD - (CONV_TAPS - 1) + t
                dw[t] = dw[t] + fold(dyc * padh_ref[start:start + rows, :])
            dw[CONV_TAPS] = dw[CONV_TAPS] + fold(dyc)
            av = a_ref[r0:r0 + rows, :]
            sgc = _sigmoid(g_ref[r0:r0 + rows, :])
            da_ref[r0:r0 + rows, :] = (acc * sgc).astype(BF16)
            dg_ref[r0:r0 + rows, :] = (acc * av * sgc * (1.0 - sgc)).astype(BF16)
        for t in range(CONV_TAPS + 1):
            dw_ref[t:t + 1, :] = jnp.sum(dw[t], axis=0, keepdims=True)

    col = lambda j0: pl.BlockSpec((s, LANES), lambda c: (0, j0 + c))
    wspec = pl.BlockSpec((CONV_PAD, LANES), lambda c: (0, c))
    return pl.pallas_call(
        body, name=name, grid=(n_ch // LANES,),
        in_specs=[col(a_col), col(g_col), wspec, col(0)],
        out_specs=[col(0), col(0), wspec],
        out_shape=[jax.ShapeDtypeStruct((s, n_ch), BF16), jax.ShapeDtypeStruct((s, n_ch), BF16),
                   jax.ShapeDtypeStruct((CONV_PAD, n_ch), F32)],
        scratch_shapes=[pltpu.VMEM((s + CONV_PAD, LANES), F32), pltpu.VMEM((s + CONV_PAD, LANES), F32)],
        compiler_params=_params("parallel"),
    )(proj, proj, w32, dy)


def _my_place():
    return lax.axis_index("x"), lax.axis_index("y"), lax.axis_index("c")


def _flip(place, k):
    x, y, c = place
    return (1 - x if k & 4 else x, 1 - y if k & 2 else y, 1 - c if k & 1 else c)


def _dev_id(place):
    return 4 * place[0] + 2 * place[1] + place[2]


def _wait_all(ref, send_sem, recv_sem, place):
    pltpu.make_async_remote_copy(src_ref=ref, dst_ref=ref, send_sem=send_sem, recv_sem=recv_sem,
                                 device_id=place, device_id_type=MESH).wait()


def _window(kind, ref, dev, n):
    if kind == "slot":
        return ref.at[dev]
    if kind == "rows":
        return ref.at[pl.ds(pl.multiple_of(dev * n, n), n), :]
    return ref.at[:, pl.ds(pl.multiple_of(dev * n, n), n)]


def _hbm(x):
    return pltpu.with_memory_space_constraint(x, pltpu.HBM)


_EFFECT = pltpu.SideEffectType.DATAFLOW_SIDE_EFFECTING


def _exchange_start(srcs, lands, specs, groups, name):
    n = len(srcs)
    n_g = len(groups)

    def body(*refs):
        src_refs, land_refs = refs[:n], refs[n:2 * n]
        sems = refs[2 * n:2 * n + 2 * n_g]
        token = refs[-1]
        place = _my_place()
        me = _dev_id(place)
        for g, units in enumerate(groups):
            for j, u in enumerate(units):
                mode, kind, cnt = specs[u]
                for k in range(N_DEV):
                    peer = _flip(place, k)
                    if mode == "gather":
                        src, dst = src_refs[u], _window(kind, land_refs[u], me, cnt)
                    else:
                        src, dst = _window(kind, src_refs[u], _dev_id(peer), cnt), land_refs[u].at[k]
                    pltpu.make_async_remote_copy(src_ref=src, dst_ref=dst, send_sem=sems[2 * g].at[j], recv_sem=sems[2 * g + 1].at[j],
                                                 device_id=peer, device_id_type=MESH).start()
        token[...] = jnp.zeros_like(token)

    hbm = pl.BlockSpec(memory_space=pltpu.HBM)
    sem = pl.BlockSpec(memory_space=pltpu.SEMAPHORE)
    out_shape = [pltpu.SemaphoreType.DMA((len(units),)) for units in groups for _ in range(2)]
    out_shape += [pltpu.HBM(x.shape, x.dtype) for x in lands] + [jax.ShapeDtypeStruct((8, LANES), F32)]
    outs = pl.pallas_call(
        body, name=name, out_shape=out_shape,
        in_specs=[hbm] * (2 * n), out_specs=[sem] * (2 * n_g) + [hbm] * n + [pl.BlockSpec(memory_space=pltpu.VMEM)],
        input_output_aliases={n + u: 2 * n_g + u for u in range(n)},
        compiler_params=pltpu.CompilerParams(has_side_effects=_EFFECT),
    )(*[_hbm(x) for x in srcs], *[_hbm(x) for x in lands])
    sem_pairs = [(outs[2 * g], outs[2 * g + 1]) for g in range(n_g)]
    return sem_pairs, list(outs[2 * n_g:2 * n_g + n]), outs[-1]


def _exchange_wait(srcs, lands, specs, sem_pair, after, name):
    n = len(lands)

    def body(*refs):
        land_refs = refs[n:2 * n]
        send_sems, recv_sems = refs[2 * n], refs[2 * n + 1]
        place = _my_place()
        for u in range(n):
            _wait_all(land_refs[u], send_sems.at[u], recv_sems.at[u], place)

    hbm = pl.BlockSpec(memory_space=pltpu.HBM)
    sem = pl.BlockSpec(memory_space=pltpu.SEMAPHORE)
    outs = pl.pallas_call(
        body, name=name, out_shape=[pltpu.HBM(x.shape, x.dtype) for x in lands],
        in_specs=[hbm] * (2 * n) + [sem, sem, pl.BlockSpec(memory_space=pl.ANY)], out_specs=[hbm] * n,
        input_output_aliases={n + u: u for u in range(n)},
        compiler_params=pltpu.CompilerParams(has_side_effects=_EFFECT),
    )(*[_hbm(x) for x in srcs], *lands, sem_pair[0], sem_pair[1], after)
    return list(outs)


def _all_reduce_small(g, name):
    r, w = g.shape

    def body(g_ref, out_ref, buf_ref, send_sems, recv_sems):
        place = _my_place()
        me = 4 * place[0] + 2 * place[1] + place[2]
        buf_ref[me] = g_ref[...]
        copies = []
        for k in range(1, N_DEV):
            copies.append(pltpu.make_async_remote_copy(
                src_ref=g_ref, dst_ref=buf_ref.at[me],
                send_sem=send_sems.at[k - 1], recv_sem=recv_sems.at[k - 1], device_id=_flip(place, k), device_id_type=MESH))
        for cp in copies:
            cp.start()
        for cp in copies:
            cp.wait()
        acc = buf_ref[0]
        for d in range(1, N_DEV):
            acc = acc + buf_ref[d]
        out_ref[...] = acc

    return pl.pallas_call(
        body, name=name,
        out_shape=jax.ShapeDtypeStruct((r, w), F32),
        in_specs=[pl.BlockSpec(memory_space=pltpu.VMEM)], out_specs=pl.BlockSpec(memory_space=pltpu.VMEM),
        scratch_shapes=[pltpu.VMEM((N_DEV, r, w), F32), pltpu.SemaphoreType.DMA((7,)), pltpu.SemaphoreType.DMA((7,))],
    )(g)


def _adamw_math(w, m, v, g):
    m_new = ADAM_B1 * m + (1.0 - ADAM_B1) * g
    v_new = ADAM_B2 * v + (1.0 - ADAM_B2) * (g * g)
    m_hat = m_new / (1.0 - ADAM_B1 ** ADAM_STEP)
    v_hat = v_new / (1.0 - ADAM_B2 ** ADAM_STEP)
    return -ADAM_LR * (m_hat / (jnp.sqrt(v_hat) + ADAM_EPS) + ADAM_WD * w), m_new, v_new


def _adamw(w, m, v, g, name):
    rows = w.shape[0]
    tr = min(FLAT_ROWS, rows)
    assert rows % tr == 0, (name, rows)

    def body(w_ref, m_ref, v_ref, g_ref, d_out, m_out, v_out):
        d_out[...], m_out[...], v_out[...] = _adamw_math(w_ref[...], m_ref[...], v_ref[...], g_ref[...])

    flat = pl.BlockSpec((tr, LANES), lambda i: (i, 0))
    return pl.pallas_call(
        body, name=name, grid=(rows // tr,), in_specs=[flat] * 4, out_specs=[flat] * 3,
        out_shape=[jax.ShapeDtypeStruct((rows, LANES), F32)] * 3,
        compiler_params=_params("parallel"),
    )(w, m, v, g)


def _adamw_shard(w, m, v, recv, layer, prev, name):
    depth, a, b = w.shape
    ta = min(256, a)
    assert a % ta == 0

    def body(w_ref, m_ref, v_ref, r_ref, *rest):
        g_out, d_out, m_out, v_out = rest[-4:]
        g = r_ref[0].astype(F32)
        for k in range(1, N_DEV):
            g = g + r_ref[k].astype(F32)
        g_out[0] = g
        d_out[0], m_out[0], v_out[0] = _adamw_math(w_ref[0], m_ref[0], v_ref[0], g)

    lay = pl.BlockSpec((1, ta, b), lambda i: (layer, i, 0))
    in_specs = [lay] * 3 + [pl.BlockSpec((N_DEV, ta, b), lambda i: (0, i, 0))]
    args = [w, m, v, recv]
    aliases = {}
    if prev is not None:
        in_specs += [pl.BlockSpec(memory_space=pl.ANY)] * 4
        args += list(prev)
        aliases = {4 + i: i for i in range(4)}
    return pl.pallas_call(
        body, name=name, grid=(a // ta,), in_specs=in_specs, out_specs=[lay] * 4,
        out_shape=[jax.ShapeDtypeStruct(w.shape, F32)] * 4, input_output_aliases=aliases,
        compiler_params=_params("parallel"),
    )(*args)


def _round_up(n, mult):
    return (n + mult - 1) // mult * mult


def _flatten(parts, row_mult):
    flat = jnp.concatenate([p.reshape(-1) for p in parts])
    rows = _round_up(-(-flat.shape[0] // LANES), row_mult)
    return jnp.pad(flat, (0, rows * LANES - flat.shape[0])).reshape(rows, LANES)


def _unflatten(flat, shapes):
    flat = flat.reshape(-1)
    out, off = [], 0
    for shp in shapes:
        n = 1
        for dim in shp:
            n *= dim
        out.append(flat[off:off + n].reshape(shp))
        off += n
    return out


def kernel(x, norm1_g, w_in, b_f, q_norm_g, k_norm_g, conv_w, conv_b, conv_ln_g, conv_ln_b, w_o, norm2_g, w_mlp_in, w_mlp_out, loss_target, m_norm1_g, m_w_in, m_b_f, m_q_norm_g, m_k_norm_g, m_conv_w, m_conv_b, m_conv_ln_g, m_conv_ln_b, m_w_o, m_norm2_g, m_w_mlp_in, m_w_mlp_out, v_norm1_g, v_w_in, v_b_f, v_q_norm_g, v_k_norm_g, v_conv_w, v_conv_b, v_conv_ln_g, v_conv_ln_b, v_w_o, v_norm2_g, v_w_mlp_in, v_w_mlp_out):
    depth, d_model, n_in_loc = w_in.shape
    seq = x.shape[1]
    n_heads = b_f.shape[1]
    aw = n_heads * HEAD_DIM
    cc = conv_b.shape[1]
    n_in = n_in_loc * N_DEV
    o_f = 3 * aw
    n_all = 3 * aw + 2 * cc + LANES
    assert n_in == 3 * aw + n_heads + 2 * cc and aw + cc == d_model and n_heads % 2 == 0
    assert aw % LANES == 0 and cc % LANES == 0 and x.shape[0] == 1
    me = 4 * lax.axis_index("x") + 2 * lax.axis_index("y") + lax.axis_index("c")

    d_ff = w_mlp_in.shape[2] * N_DEV

    r_o, f_1, f_2 = w_o.shape[1], w_mlp_in.shape[2], w_mlp_out.shape[1]

    ag_src, ag_land, ag_spec = [], [], []
    for l in range(depth):
        ag_src += [w_in[l].astype(BF16), w_o[l].astype(BF16), w_mlp_in[l].astype(BF16), w_mlp_out[l].astype(BF16)]
        ag_land += [(N_DEV, d_model, n_in_loc), (N_DEV * r_o, d_model), (d_model, N_DEV * f_1), (N_DEV * f_2, d_model)]
        ag_spec += [("gather", "slot", 1), ("gather", "rows", r_o), ("gather", "cols", f_1), ("gather", "rows", f_2)]
    ag_src.append(jnp.stack(_split3(conv_w)))
    ag_land.append((N_DEV, 3) + conv_w.shape)
    ag_spec.append(("gather", "slot", 1))
    ag_groups = [[0, 4 * depth], [1], [2, 3]] + [[4 * l + i for i in range(4)] for l in range(1, depth)]
    ag_land = [lax.empty(shp, BF16) for shp in ag_land]
    ag_sems, ag_land, ag_token = _exchange_start(ag_src, ag_land, ag_spec, ag_groups, "gather_start")

    def gathered(g, after):
        units = ag_groups[g]
        return _exchange_wait([ag_src[u] for u in units], [ag_land[u] for u in units], [ag_spec[u] for u in units],
                              ag_sems[g], after, f"gather_wait_{g}")

    def to_all(w):
        return jnp.concatenate([w[:, :o_f], w[:, o_f + n_heads:], w[:, o_f:o_f + n_heads],
                                jnp.zeros((w.shape[0], LANES - n_heads), w.dtype)], axis=1)

    def from_all(w):
        return jnp.concatenate([w[:, :o_f], w[:, n_all - LANES:n_all - LANES + n_heads], w[:, o_f:n_all - LANES]], axis=1)

    def whole_in(lin):
        return to_all(jnp.moveaxis(lin, 0, 1).reshape(d_model, n_in))

    def row(p, l, width=None):
        v = p[l].reshape(1, -1)
        return v if width is None else jnp.pad(v, ((0, 0), (0, width - v.shape[1])))

    a_col, g_col = 3 * aw // LANES, (3 * aw + cc) // LANES

    gq = [jnp.tile(row(q_norm_g, l), (1, n_heads)) for l in range(depth)]
    gk = [jnp.tile(row(k_norm_g, l), (1, n_heads)) for l in range(depth)]
    bfp = [row(b_f, l, LANES) for l in range(depth)]
    add_res = lambda acc, res: (acc + res,)
    w_all, w_out, w_ff1, w_ff2 = [None] * depth, [None] * depth, [None] * depth, [None] * depth

    h = x[0]
    saved = []
    for l in range(depth):
        if l == 0:
            lin, lc = gathered(0, h)
            lc = lc.astype(F32)
            conv_full = jnp.moveaxis(lc[:, 0] + lc[:, 1] + lc[:, 2], 0, 2).reshape(depth, CONV_TAPS, cc)
            w32 = [jnp.pad(conv_full[i], ((0, CONV_PAD - CONV_TAPS), (0, 0))) for i in range(depth)]
        else:
            lin, w_out[l], w_ff1[l], w_ff2[l] = gathered(2 + l, h)
        w_all[l] = whole_in(lin)
        u1, proj = _rms_matmul(h, row(norm1_g, l), w_all[l], out_dtypes=(F32,), name=f"mm_in_{l}")
        qa, ka, vb = _prep_fwd(proj, gq[l], gk[l], bfp[l], n_heads, f"prep_fwd_{l}")
        att, lse, mixed = _attn_fwd(qa, ka, vb, n_heads, aw + cc, f"attn_fwd_{l}")
        yc = _conv_fwd(proj, w32[l], row(conv_b, l), cc, a_col, g_col, f"conv_fwd_{l}")
        mixed = _ln_silu_fwd(yc, row(conv_ln_g, l), row(conv_ln_b, l), mixed, f"ln_silu_fwd_{l}")
        if l == 0:
            w_out[l], = gathered(1, mixed)
        x1 = _matmul(mixed, w_out[l], mode="nn", out_dtypes=(F32,), name=f"mm_o_{l}", epilogue=add_res, extras=(h,))
        if l == 0:
            w_ff1[l], w_ff2[l] = gathered(2, x1)
        u2, r, a = _rms_matmul(x1, row(norm2_g, l), w_ff1[l], out_dtypes=(BF16, BF16), name=f"mm_ff1_{l}", tm=256,
                               epilogue=lambda acc: (jnp.maximum(acc, 0.0), jnp.square(jnp.maximum(acc, 0.0))))
        if l < depth - 1:
            x2 = _matmul(a, w_ff2[l], mode="nn", out_dtypes=(F32,), name=f"mm_ff2_{l}", epilogue=add_res, extras=(x1,), tk=d_ff)
        else:
            x2 = None
            dh, dh16, sq = _matmul_loss(a, w_ff2[l], x1, loss_target[0], f"mm_ff2_loss_{l}")
        saved.append(dict(x_in=h, u1=u1, proj=proj, qa=qa, ka=ka, vb=vb, att=att, lse=lse, yc=yc, mixed=mixed,
                          x1=x1, u2=u2, r=r, a=a))
        h = x2

    loss = lax.psum(0.5 * jnp.sum(sq) / d_model, ("x", "y", "c"))

    g_in, g_o, g_1, g_2 = [None] * depth, [None] * depth, [None] * depth, [None] * depth
    gs = {n: [None] * depth for n in ("norm1", "bf", "qn", "kn", "convw", "convb", "lng", "lnb", "norm2")}
    scattering = {}

    def scatter_start(stage, l, srcs, specs, slabs):
        lands = [lax.empty((N_DEV,) + shp, BF16) for shp in slabs]
        sems, lands, token = _exchange_start(srcs, lands, specs, [list(range(len(srcs)))], f"scatter_start_{stage}_{l}")
        scattering[(stage, l)] = (srcs, lands, specs, sems[0])
        return token[0, 0]

    for l in reversed(range(depth)):
        sv = saved[l]
        dh1 = _matmul(dh16, w_ff2[l], mode="nt", out_dtypes=(BF16,), name=f"mm_dff2_{l}", tm=256, tn=d_ff,
                      epilogue=lambda acc, rr: (acc * (2.0 * rr.astype(F32)),), extras=(sv["r"],))
        g_2[l] = _matmul(sv["a"], dh16, mode="tn", out_dtypes=(BF16,), name=f"mm_dw2_{l}", tm=1024, tk=seq)
        g_1[l] = _matmul(sv["u2"], dh1, mode="tn", out_dtypes=(BF16,), name=f"mm_dw1_{l}", tm=d_model, tk=seq)
        tok = scatter_start("ff", l, [g_1[l], g_2[l]], [("scatter", "cols", f_1), ("scatter", "rows", f_2)],
                            [(d_model, f_1), (f_2, d_model)])
        dx1, dx16, gs["norm2"][l] = _matmul_rms_bwd(dh1, w_ff1[l], sv["x1"], row(norm2_g, l) + tok, dh, f"mm_du2_rms_{l}")

        dmixed = _matmul(dx16, w_out[l], mode="nt", out_dtypes=(F32,), name=f"mm_dmixed_{l}")
        g_o[l] = _matmul(sv["mixed"], dx16, mode="tn", out_dtypes=(BF16,), name=f"mm_dwo_{l}", tm=1024, tk=seq // 2)
        dyc, gs["lng"][l], gs["lnb"][l] = _ln_silu_bwd(sv["yc"], row(conv_ln_g, l), row(conv_ln_b, l), dmixed, f"ln_silu_bwd_{l}")
        dpa, dpg, dw32 = _conv_bwd(sv["proj"], w32[l], dyc, cc, a_col, g_col, f"conv_bwd_{l}")
        gs["convw"][l], gs["convb"][l] = dw32[:CONV_TAPS], dw32[CONV_TAPS:CONV_TAPS + 1]
        dq, dka, dv, drow, dcol = _attn_bwd(sv["qa"], sv["ka"], sv["vb"], sv["att"], sv["lse"], dmixed, n_heads, f"attn_bwd_{l}")
        dpq, dpk, dpf, dgq, dgk, dbf = _prep_bwd(sv["proj"], dq, dka, drow, dcol, gq[l], gk[l], bfp[l], n_heads, f"prep_bwd_{l}")
        gs["qn"][l] = dgq.reshape(n_heads, HEAD_DIM).sum(axis=0)
        gs["kn"][l] = dgk.reshape(n_heads, HEAD_DIM).sum(axis=0)
        gs["bf"][l] = dbf[0, :n_heads]
        dproj = jnp.concatenate([dpq, dpk, dv.astype(BF16), dpa, dpg, dpf], axis=1)
        dwall = _matmul(sv["u1"], dproj, mode="tn", out_dtypes=(BF16,), name=f"mm_dwall_{l}", tm=d_model, tn=n_all // 3, tk=seq)
        g_in[l] = jnp.moveaxis(from_all(dwall).reshape(d_model, N_DEV, n_in_loc), 1, 0)
        tok = scatter_start("mix", l, [g_in[l], g_o[l]], [("scatter", "slot", 1), ("scatter", "rows", r_o)],
                            [(d_model, n_in_loc), (r_o, d_model)])
        dh, dh16, gs["norm1"][l] = _matmul_rms_bwd(dproj, w_all[l], sv["x_in"], row(norm1_g, l) + tok, dx1, f"mm_du1_rms_{l}")
    grad_x = dh[None]

    def landed(stage, l, after):
        srcs, lands, specs, sems = scattering[(stage, l)]
        return _exchange_wait(srcs, lands, specs, sems, after, f"scatter_wait_{stage}_{l}")

    def adamw_layers(kd, w, m, v, recv):
        outs = None
        for l in reversed(range(depth)):
            outs = _adamw_shard(w, m, v, recv[l], l, outs, f"adamw_{kd}_{l}")
        return outs

    recv_1, recv_2, recv_in, recv_o = [None] * depth, [None] * depth, [None] * depth, [None] * depth
    for l in reversed(range(depth)):
        recv_1[l], recv_2[l] = landed("ff", l, dh)
    out_1 = adamw_layers("1", w_mlp_in, m_w_mlp_in, v_w_mlp_in, recv_1)
    out_2 = adamw_layers("2", w_mlp_out, m_w_mlp_out, v_w_mlp_out, recv_2)
    for l in reversed(range(depth)):
        recv_in[l], recv_o[l] = landed("mix", l, out_2[1])
    out_in = adamw_layers("in", w_in, m_w_in, v_w_in, recv_in)
    out_o = adamw_layers("o", w_o, m_w_o, v_w_o, recv_o)
    big_out = [[outs[kind] for outs in (out_in, out_o, out_1, out_2)] for kind in range(4)]

    small_g = [jnp.stack(gs[n]).reshape(shp) for n, shp in (
        ("norm1", norm1_g.shape), ("bf", b_f.shape), ("qn", q_norm_g.shape), ("kn", k_norm_g.shape),
        ("convw", (depth, CONV_TAPS, cc)), ("convb", conv_b.shape), ("lng", conv_ln_g.shape), ("lnb", conv_ln_b.shape),
        ("norm2", norm2_g.shape))]
    small_shapes = [g.shape for g in small_g]
    small_g = _unflatten(_all_reduce_small(_flatten(small_g, 8), "all_reduce_small"), small_shapes)
    cw = conv_w.shape[2]
    small_g[4] = lax.dynamic_slice_in_dim(small_g[4], me * cw, cw, axis=2)
    small = (norm1_g, b_f, q_norm_g, k_norm_g, conv_w, conv_b, conv_ln_g, conv_ln_b, norm2_g)
    small_m = (m_norm1_g, m_b_f, m_q_norm_g, m_k_norm_g, m_conv_w, m_conv_b, m_conv_ln_g, m_conv_ln_b, m_norm2_g)
    small_v = (v_norm1_g, v_b_f, v_q_norm_g, v_k_norm_g, v_conv_w, v_conv_b, v_conv_ln_g, v_conv_ln_b, v_norm2_g)
    small_out = _adamw(_flatten(small, 8), _flatten(small_m, 8), _flatten(small_v, 8), _flatten(small_g, 8), "adamw_small")
    small_out = [small_g] + [_unflatten(o, [w.shape for w in small]) for o in small_out]

    def group(kind):
        s_, b_ = small_out[kind], big_out[kind]
        return [s_[0], b_[0], s_[1], s_[2], s_[3], s_[4], s_[5], s_[6], s_[7], b_[1], s_[8], b_[2], b_[3]]

    return (loss, grad_x, *group(0), *group(1), *group(2), *group(3))
```

```python
import functools

import jax
import jax.numpy as jnp
from jax import lax
from jax.experimental import pallas as pl
from jax.experimental.pallas import tpu as pltpu

F32 = jnp.float32
BF16 = jnp.bfloat16

EPS = 1e-6
HEAD_DIM = 64
LANES = 128
PAIR = 2 * LANES
N_DEV = 8
CONV_TAPS = 31
CONV_PAD = 32
NEG = -1e30

ADAM_LR = 0.001
ADAM_B1 = 0.9
ADAM_B2 = 0.999
ADAM_EPS = 1e-08
ADAM_WD = 0.01
ADAM_STEP = 10

TM = 512
TQ = 512
CONV_ROWS = 128
FLAT_ROWS = 1024
MESH = pl.DeviceIdType.MESH


def _params(*sem):
    return pltpu.CompilerParams(dimension_semantics=sem, vmem_limit_bytes=56 * 1024 * 1024)


def _split3(x):
    hi = x.astype(BF16)
    r1 = x - hi.astype(F32)
    mid = r1.astype(BF16)
    lo = (r1 - mid.astype(F32)).astype(BF16)
    return hi, mid, lo


def _dot(a, b):
    return jnp.dot(a, b, preferred_element_type=F32)


def _dot_nt(a, b):
    return lax.dot_general(a, b, (((1,), (1,)), ((), ())), preferred_element_type=F32)


def _dot_tn(a, b):
    return lax.dot_general(a, b, (((0,), (0,)), ((), ())), preferred_element_type=F32)


def _dot3(x, mat):
    hi, mid, lo = _split3(x)
    return _dot(hi, mat) + _dot(mid, mat) + _dot(lo, mat)


def _dot2(x, mat):
    hi = x.astype(BF16)
    lo = (x - hi.astype(F32)).astype(BF16)
    return _dot(hi, mat) + _dot(lo, mat)


def _dot3_r(mat, x):
    hi, mid, lo = _split3(x)
    return _dot(mat, hi) + _dot(mat, mid) + _dot(mat, lo)


def _iota(shape, dim):
    return lax.broadcasted_iota(jnp.int32, shape, dim)


def _sigmoid(x):
    return 1.0 / (1.0 + jnp.exp(-x))


def _matmul(a, b, *, mode, out_dtypes, name, epilogue=None, extras=(), tm=TM, tn=1024, tk=1024):
    if mode == "nn":
        (m, k), (k2, n) = a.shape, b.shape
    elif mode == "nt":
        (m, k), (n, k2) = a.shape, b.shape
    else:
        (k, m), (k2, n) = a.shape, b.shape
    assert k == k2, (name, a.shape, b.shape)
    tm, tn, tk = min(tm, m), min(tn, n), min(tk, k)
    assert m % tm == 0 and n % tn == 0 and k % tk == 0, (name, m, n, k, tm, tn, tk)
    nk = k // tk
    a_mode = dict(pipeline_mode=pl.Buffered(1)) if (m == tm and nk == 1) else {}
    b_mode = dict(pipeline_mode=pl.Buffered(1)) if (n == tn and nk == 1) else {}
    if mode == "tn":
        a_spec = pl.BlockSpec((tk, tm), lambda i, j, kk: (kk, i), **a_mode)
    else:
        a_spec = pl.BlockSpec((tm, tk), lambda i, j, kk: (i, kk), **a_mode)
    if mode == "nt":
        b_spec = pl.BlockSpec((tn, tk), lambda i, j, kk: (j, kk), **b_mode)
    else:
        b_spec = pl.BlockSpec((tk, tn), lambda i, j, kk: (kk, j), **b_mode)
    dot = {"nn": _dot, "nt": _dot_nt, "tn": _dot_tn}[mode]
    tile = pl.BlockSpec((tm, tn), lambda i, j, kk: (i, j))
    n_ex, n_out = len(extras), len(out_dtypes)
    acc_in_out = nk > 1 and epilogue is None and out_dtypes[0] == F32

    def body(a_ref, b_ref, *rest):
        ex_refs, out_refs = rest[:n_ex], rest[n_ex:n_ex + n_out]
        part = dot(a_ref[...].astype(BF16), b_ref[...].astype(BF16))

        def finish(acc):
            res = epilogue(acc, *[e[...] for e in ex_refs]) if epilogue is not None else (acc,) * n_out
            for o_ref, r in zip(out_refs, res):
                o_ref[...] = r.astype(o_ref.dtype)

        if nk == 1:
            finish(part)
        else:
            acc_ref = out_refs[0] if acc_in_out else rest[-1]
            kk = pl.program_id(2)

            @pl.when(kk == 0)
            def _():
                acc_ref[...] = part

            @pl.when(kk > 0)
            def _():
                acc_ref[...] += part

            @pl.when(kk == nk - 1)
            def _():
                if acc_in_out:
                    for o_ref in out_refs[1:]:
                        o_ref[...] = acc_ref[...].astype(o_ref.dtype)
                else:
                    finish(acc_ref[...])

    outs = pl.pallas_call(
        body,
        name=name,
        grid=(m // tm, n // tn, nk),
        in_specs=[a_spec, b_spec] + [tile] * n_ex,
        out_specs=[tile] * n_out,
        out_shape=[jax.ShapeDtypeStruct((m, n), dt) for dt in out_dtypes],
        scratch_shapes=[pltpu.VMEM((tm, tn), F32)] if nk > 1 and not acc_in_out else [],
        compiler_params=_params("parallel", "parallel", "arbitrary"),
    )(a, b, *extras)
    return outs if n_out > 1 else outs[0]


def _rms_matmul(x, g, b, *, out_dtypes, name, epilogue=None, tm=TM):
    s, d = x.shape
    n = b.shape[1]
    ts = min(tm, s)
    n_out = len(out_dtypes)

    def body(x_ref, g_ref, b_ref, u_ref, *out_refs):
        xv = x_ref[...]
        u = (xv * lax.rsqrt(jnp.mean(xv * xv, axis=-1, keepdims=True) + EPS) * g_ref[...]).astype(BF16)
        u_ref[...] = u
        acc = _dot(u, b_ref[...])
        res = epilogue(acc) if epilogue is not None else (acc,)
        for o_ref, r in zip(out_refs, res):
            o_ref[...] = r.astype(o_ref.dtype)

    row = lambda w: pl.BlockSpec((ts, w), lambda i: (i, 0))
    return pl.pallas_call(
        body, name=name, grid=(s // ts,),
        in_specs=[row(d), pl.BlockSpec((1, d), lambda i: (0, 0)), pl.BlockSpec((d, n), lambda i: (0, 0), pipeline_mode=pl.Buffered(1))],
        out_specs=[row(d)] + [row(n)] * n_out,
        out_shape=[jax.ShapeDtypeStruct((s, d), BF16)] + [jax.ShapeDtypeStruct((s, n), dt) for dt in out_dtypes],
        compiler_params=_params("parallel"),
    )(x, g, b)


def _matmul_rms_bwd(a, b, x, g, dres, name):
    s, k = a.shape
    d = b.shape[0]
    ts = min(TM, s)

    def body(a_ref, b_ref, x_ref, g_ref, dres_ref, dx_ref, dx16_ref, dg_ref):
        @pl.when(pl.program_id(0) == 0)
        def _():
            dg_ref[...] = jnp.zeros_like(dg_ref)

        duv = _dot_nt(a_ref[...], b_ref[...])
        xv = x_ref[...]
        r = lax.rsqrt(jnp.mean(xv * xv, axis=-1, keepdims=True) + EPS)
        xh = xv * r
        dxh = duv * g_ref[...]
        dx = dres_ref[...] + r * (dxh - xh * jnp.mean(dxh * xh, axis=-1, keepdims=True))
        dx_ref[...] = dx
        dx16_ref[...] = dx.astype(BF16)
        dg_ref[...] += jnp.sum(duv * xh, axis=0, keepdims=True)

    row = pl.BlockSpec((ts, d), lambda i: (i, 0))
    vec = pl.BlockSpec((1, d), lambda i: (0, 0))
    return pl.pallas_call(
        body, name=name, grid=(s // ts,),
        in_specs=[pl.BlockSpec((ts, k), lambda i: (i, 0)), pl.BlockSpec((d, k), lambda i: (0, 0), pipeline_mode=pl.Buffered(1)),
                  row, vec, row],
        out_specs=[row, row, vec],
        out_shape=[jax.ShapeDtypeStruct((s, d), F32), jax.ShapeDtypeStruct((s, d), BF16), jax.ShapeDtypeStruct((1, d), F32)],
        compiler_params=_params("arbitrary"),
    )(a, b, x, g, dres)


def _matmul_loss(a, b, res, target, name):
    s, k = a.shape
    d = b.shape[1]
    ts = min(TM, s)

    def body(a_ref, b_ref, res_ref, t_ref, dy_ref, dy16_ref, sq_ref):
        @pl.when(pl.program_id(0) == 0)
        def _():
            sq_ref[...] = jnp.zeros_like(sq_ref)

        err = _dot(a_ref[...], b_ref[...]) + res_ref[...] - t_ref[...]
        dy = err * (1.0 / d)
        dy_ref[...] = dy
        dy16_ref[...] = dy.astype(BF16)
        sq_ref[...] += jnp.sum(err * err, axis=0, keepdims=True)

    row = pl.BlockSpec((ts, d), lambda i: (i, 0))
    vec = pl.BlockSpec((1, d), lambda i: (0, 0))
    return pl.pallas_call(
        body, name=name, grid=(s // ts,),
        in_specs=[pl.BlockSpec((ts, k), lambda i: (i, 0)), pl.BlockSpec((k, d), lambda i: (0, 0), pipeline_mode=pl.Buffered(1)),
                  row, row],
        out_specs=[row, row, vec],
        out_shape=[jax.ShapeDtypeStruct((s, d), F32), jax.ShapeDtypeStruct((s, d), BF16), jax.ShapeDtypeStruct((1, d), F32)],
        compiler_params=_params("arbitrary"),
    )(a, b, res, target)


def _ln_silu_fwd(y, g, b, mixed, name):
    s, c = y.shape
    ts = min(TM, s)
    assert mixed.shape == (s, 2 * c)

    def body(y_ref, g_ref, b_ref, mixed_ref, h_ref):
        yv = y_ref[...]
        mu = jnp.mean(yv, axis=-1, keepdims=True)
        yc = yv - mu
        z = yc * lax.rsqrt(jnp.mean(yc * yc, axis=-1, keepdims=True) + EPS) * g_ref[...] + b_ref[...]
        h_ref[...] = (z * _sigmoid(z)).astype(BF16)

    row = pl.BlockSpec((ts, c), lambda i: (i, 0))
    vec = pl.BlockSpec((1, c), lambda i: (0, 0))
    return pl.pallas_call(
        body, name=name, grid=(s // ts,), in_specs=[row, vec, vec, pl.BlockSpec(memory_space=pl.ANY)],
        out_specs=pl.BlockSpec((ts, c), lambda i: (i, 1)),
        out_shape=jax.ShapeDtypeStruct((s, 2 * c), BF16), input_output_aliases={3: 0},
        compiler_params=_params("parallel"),
    )(y, g, b, mixed)


def _ln_silu_bwd(y, g, b, dmixed, name):
    s, c = y.shape
    ts = min(TM, s)

    def body(y_ref, g_ref, b_ref, dh_ref, dy_ref, dg_ref, db_ref):
        @pl.when(pl.program_id(0) == 0)
        def _():
            dg_ref[...] = jnp.zeros_like(dg_ref)
            db_ref[...] = jnp.zeros_like(db_ref)

        yv = y_ref[...]
        mu = jnp.mean(yv, axis=-1, keepdims=True)
        yc = yv - mu
        r = lax.rsqrt(jnp.mean(yc * yc, axis=-1, keepdims=True) + EPS)
        yh = yc * r
        z = yh * g_ref[...] + b_ref[...]
        sg = _sigmoid(z)
        dz = dh_ref[...] * (sg * (1.0 + z * (1.0 - sg)))
        dg_ref[...] += jnp.sum(dz * yh, axis=0, keepdims=True)
        db_ref[...] += jnp.sum(dz, axis=0, keepdims=True)
        dyh = dz * g_ref[...]
        dy_ref[...] = r * (dyh - jnp.mean(dyh, axis=-1, keepdims=True) - yh * jnp.mean(dyh * yh, axis=-1, keepdims=True))

    row = pl.BlockSpec((ts, c), lambda i: (i, 0))
    vec = pl.BlockSpec((1, c), lambda i: (0, 0))
    return pl.pallas_call(
        body, name=name, grid=(s // ts,),
        in_specs=[row, vec, vec, pl.BlockSpec((ts, c), lambda i: (i, 1))], out_specs=[row, vec, vec],
        out_shape=[jax.ShapeDtypeStruct((s, c), F32), jax.ShapeDtypeStruct((1, c), F32), jax.ShapeDtypeStruct((1, c), F32)],
        compiler_params=_params("arbitrary"),
    )(y, g, b, dmixed)


def _head_masks():
    lane2 = _iota((1, PAIR), 1)
    lane1 = _iota((1, LANES), 1)
    qa = (lane2 < HEAD_DIM) | ((lane2 >= LANES) & (lane2 < LANES + 3))
    qb = ((lane2 >= HEAD_DIM) & (lane2 < LANES)) | ((lane2 >= LANES + 3) & (lane2 < LANES + 6))
    return (qa, qb), (lane1 < HEAD_DIM, lane1 >= HEAD_DIM)


def _group_matrix(width):
    shift = HEAD_DIM.bit_length() - 1
    return ((_iota((width, width), 0) >> shift) == (_iota((width, width), 1) >> shift)).astype(BF16)


def _prep_fwd(proj, gq, gk, bf, n_heads, name):
    s = proj.shape[0]
    aw = n_heads * HEAD_DIM
    n_pairs = n_heads // 2
    ts = min(TM, s)
    f_col = (proj.shape[1] - LANES) // LANES

    def body(q_ref, k_ref, v_ref, f_ref, gq_ref, gk_ref, bf_ref, qa_ref, ka_ref, vb_ref, carry_ref):
        @pl.when(pl.program_id(0) == 0)
        def _():
            carry_ref[...] = jnp.zeros_like(carry_ref)

        gmat = _group_matrix(aw)

        def head_norm(xv, g):
            ms = _dot2(xv * xv, gmat) * (1.0 / HEAD_DIM)
            return xv * lax.rsqrt(ms + EPS) * g

        qn = head_norm(q_ref[...], gq_ref[...]) * (HEAD_DIM ** -0.5)
        kn = head_norm(k_ref[...], gk_ref[...])
        z = f_ref[...] + bf_ref[...]
        logf = jnp.minimum(z, 0.0) - jnp.log(1.0 + jnp.exp(-jnp.abs(z)))
        tri = (_iota((ts, ts), 0) >= _iota((ts, ts), 1)).astype(BF16)
        c = _dot3_r(tri, logf) + carry_ref[...]
        carry_ref[...] = c[ts - 1:ts, :]
        terms = _split3(-c)
        row, col = _iota((LANES, LANES), 0), _iota((LANES, LANES), 1)
        ones = jnp.where(_iota((ts, LANES), 1) < 6, 1.0, 0.0).astype(BF16)
        for p in range(n_pairs):
            extra = jnp.zeros((ts, LANES), F32)
            for t, term in enumerate(terms):
                sel = ((row == 2 * p) & (col == t)) | ((row == 2 * p + 1) & (col == 3 + t))
                extra += _dot(term, sel.astype(BF16))
            lo, hi = p * PAIR, p * PAIR + LANES
            ka_ref[:, lo:hi] = kn[:, p * LANES:(p + 1) * LANES].astype(BF16)
            ka_ref[:, hi:hi + LANES] = extra.astype(BF16)
            qa_ref[:, lo:hi] = qn[:, p * LANES:(p + 1) * LANES].astype(BF16)
            qa_ref[:, hi:hi + LANES] = ones
            vb_ref[:, lo:hi] = v_ref[:, p * LANES:(p + 1) * LANES].astype(BF16)
            vb_ref[:, hi:hi + LANES] = jnp.ones((ts, LANES), BF16)

    blk = lambda j: pl.BlockSpec((ts, aw), lambda i: (i, j))
    vec = lambda w: pl.BlockSpec((1, w), lambda i: (0, 0))
    return pl.pallas_call(
        body, name=name, grid=(s // ts,),
        in_specs=[blk(0), blk(1), blk(2), pl.BlockSpec((ts, LANES), lambda i: (i, f_col)), vec(aw), vec(aw), vec(LANES)],
        out_specs=[pl.BlockSpec((ts, n_pairs * PAIR), lambda i: (i, 0))] * 3,
        out_shape=[jax.ShapeDtypeStruct((s, n_pairs * PAIR), BF16)] * 3,
        scratch_shapes=[pltpu.VMEM((1, LANES), F32)],
        compiler_params=_params("arbitrary"),
    )(proj, proj, proj, proj, gq, gk, bf)


def _prep_bwd(proj, dq, dka, drow, dcol, gq, gk, bf, n_heads, name):
    s = proj.shape[0]
    aw = n_heads * HEAD_DIM
    n_pairs = n_heads // 2
    ts = min(TM, s)
    nt = s // ts
    f_col = (proj.shape[1] - LANES) // LANES
    shift = HEAD_DIM.bit_length() - 1

    def body(q_ref, k_ref, f_ref, dq_ref, dka_ref, drow_ref, dcol_ref, gq_ref, gk_ref, bf_ref,
             dpq_ref, dpk_ref, dpf_ref, dgq_ref, dgk_ref, dbf_ref, carry_ref):
        @pl.when(pl.program_id(0) == 0)
        def _():
            carry_ref[...] = jnp.zeros_like(carry_ref)
            dgq_ref[...] = jnp.zeros_like(dgq_ref)
            dgk_ref[...] = jnp.zeros_like(dgk_ref)
            dbf_ref[...] = jnp.zeros_like(dbf_ref)

        gmat = _group_matrix(aw)

        def head_norm_bwd(xv, g, dn):
            r = lax.rsqrt(_dot2(xv * xv, gmat) * (1.0 / HEAD_DIM) + EPS)
            xh = xv * r
            dxh = dn * g
            dx = r * (dxh - xh * (_dot2(dxh * xh, gmat) * (1.0 / HEAD_DIM)))
            return dx, jnp.sum(dn * xh, axis=0, keepdims=True)

        dkav = dka_ref[...]
        dx, dg = head_norm_bwd(q_ref[...], gq_ref[...], dq_ref[...] * (HEAD_DIM ** -0.5))
        dpq_ref[...] = dx.astype(BF16)
        dgq_ref[...] += dg
        dkn = jnp.concatenate([dkav[:, p * PAIR:p * PAIR + LANES] for p in range(n_pairs)], axis=1)
        dx, dg = head_norm_bwd(k_ref[...], gk_ref[...], dkn)
        dpk_ref[...] = dx.astype(BF16)
        dgk_ref[...] += dg

        pick = (_iota((aw, LANES), 0) == (_iota((aw, LANES), 1) << shift)).astype(BF16)
        dc = _dot3(drow_ref[...], pick)
        r16, c16 = _iota((16, LANES), 0), _iota((16, LANES), 1)
        for p in range(n_pairs):
            place = ((r16 < 2) & (c16 == 2 * p + r16)).astype(BF16)
            for term in _split3(dcol_ref[p]):
                dc -= _dot_tn(term, place)
        triu = (_iota((ts, ts), 0) <= _iota((ts, ts), 1)).astype(BF16)
        dlogf = _dot3_r(triu, dc) + carry_ref[...]
        carry_ref[...] = dlogf[0:1, :]
        z = f_ref[...] + bf_ref[...]
        dz = dlogf * (1.0 / (1.0 + jnp.exp(z)))
        dpf_ref[...] = dz.astype(BF16)
        dbf_ref[...] += jnp.sum(dz, axis=0, keepdims=True)

    rev = lambda w, j: pl.BlockSpec((ts, w), lambda i: (nt - 1 - i, j))
    vec = lambda w: pl.BlockSpec((1, w), lambda i: (0, 0))
    return pl.pallas_call(
        body, name=name, grid=(nt,),
        in_specs=[rev(aw, 0), rev(aw, 1), rev(LANES, f_col), rev(aw, 0), rev(n_pairs * PAIR, 0), rev(aw, 0),
                  pl.BlockSpec((n_pairs, 16, ts), lambda i: (0, 0, nt - 1 - i)), vec(aw), vec(aw), vec(LANES)],
        out_specs=[rev(aw, 0), rev(aw, 0), rev(LANES, 0), vec(aw), vec(aw), vec(LANES)],
        out_shape=[jax.ShapeDtypeStruct((s, aw), BF16), jax.ShapeDtypeStruct((s, aw), BF16), jax.ShapeDtypeStruct((s, LANES), BF16),
                   jax.ShapeDtypeStruct((1, aw), F32), jax.ShapeDtypeStruct((1, aw), F32), jax.ShapeDtypeStruct((1, LANES), F32)],
        scratch_shapes=[pltpu.VMEM((1, LANES), F32)],
        compiler_params=_params("arbitrary"),
    )(proj, proj, proj, dq, dka, drow, dcol, gq, gk, bf)


def _attn_fwd(qa, ka, vb, n_heads, mix_width, name):
    s = qa.shape[0]
    aw = n_heads * HEAD_DIM
    n_pairs = n_heads // 2
    tq = min(TQ, s)

    def body(q_ref, k_ref, v_ref, o_ref, lse_ref, o16_ref):
        i = pl.program_id(1)
        qmasks, omasks = _head_masks()
        qv = q_ref[...]
        causal = _iota((tq, tq), 1) <= _iota((tq, tq), 0)
        qhs = [jnp.where(qmasks[h], qv, jnp.zeros_like(qv)) for h in range(2)]

        def scores(j):
            kv = k_ref[pl.ds(pl.multiple_of(j * tq, tq), tq), :]
            return tuple(_dot_nt(qhs[h], kv) for h in range(2))

        def update(j, state, scs, masked):
            vv = v_ref[pl.ds(pl.multiple_of(j * tq, tq), tq), :]
            out = []
            for h in range(2):
                m, acc = state[h]
                sc = jnp.where(causal, scs[h], NEG) if masked else scs[h]
                m_new = jnp.maximum(m, jnp.max(sc, axis=1, keepdims=True))
                p = jnp.exp(sc - m_new).astype(BF16)
                out.append((m_new, jnp.exp(m - m_new) * acc + _dot(p, vv)))
            return tuple(out)

        def body(j, state):
            return update(j, state, scores(j), False)

        init = ((jnp.full((tq, 1), NEG, F32), jnp.zeros((tq, PAIR), F32)),) * 2
        state = lax.fori_loop(0, i, body, init)
        res = []
        for m, acc in update(i, state, scores(i), True):
            l = acc[:, LANES:LANES + 1]
            res.append((acc[:, :LANES] * (1.0 / l), m + jnp.log(l)))
        out = jnp.where(omasks[0], res[0][0], res[1][0])
        o_ref[...] = out
        o16_ref[...] = out.astype(BF16)
        lse_ref[...] = jnp.where(omasks[0], res[0][1], res[1][1])

    return pl.pallas_call(
        body, name=name, grid=(n_pairs, s // tq),
        in_specs=[pl.BlockSpec((tq, PAIR), lambda p, i: (i, p)), pl.BlockSpec((s, PAIR), lambda p, i: (0, p)),
                  pl.BlockSpec((s, PAIR), lambda p, i: (0, p))],
        out_specs=[pl.BlockSpec((tq, LANES), lambda p, i: (i, p))] * 3,
        out_shape=[jax.ShapeDtypeStruct((s, aw), F32)] * 2 + [jax.ShapeDtypeStruct((s, mix_width), BF16)],
        compiler_params=_params("parallel", "parallel"),
    )(qa, ka, vb)


def _attn_bwd(qa, ka, vb, o, lse, dmixed, n_heads, name):
    s = qa.shape[0]
    aw = n_heads * HEAD_DIM
    n_pairs = n_heads // 2
    tq = min(TQ, s)
    nq = s // tq

    def body(q_ref, k_ref, v_ref, o_ref, lse_ref, do_ref, dq_ref, dka_ref, dv_ref, drow_ref, dcol_ref, delta_ref):
        j = pl.program_id(1)
        qmasks, omasks = _head_masks()

        @pl.when(j == 0)
        def _():
            dq_ref[...] = jnp.zeros_like(dq_ref)
            drow_ref[...] = jnp.zeros_like(drow_ref)
            for c in range(nq):
                rows = slice(c * tq, (c + 1) * tq)
                prod = do_ref[rows, :] * o_ref[rows, :]
                da = jnp.sum(jnp.where(omasks[0], prod, 0.0), axis=1, keepdims=True)
                db = jnp.sum(jnp.where(omasks[1], prod, 0.0), axis=1, keepdims=True)
                delta_ref[rows, :] = jnp.where(omasks[0], da, db)

        dka_ref[...] = jnp.zeros_like(dka_ref)
        dv_ref[...] = jnp.zeros_like(dv_ref)
        dcol_ref[...] = jnp.zeros_like(dcol_ref)
        kv = k_ref[...]
        kk = kv[:, :LANES]
        vv = v_ref[...]
        causal = _iota((tq, tq), 1) <= _iota((tq, tq), 0)

        def step(i, masked):
            off = pl.multiple_of(i * tq, tq)
            qv = q_ref[pl.ds(off, tq), :]
            dov = do_ref[pl.ds(off, tq), :]
            lsev = lse_ref[pl.ds(off, tq), :]
            dlv = delta_ref[pl.ds(off, tq), :]
            for h in range(2):
                qh = jnp.where(qmasks[h], qv, jnp.zeros_like(qv))
                doh = jnp.where(omasks[h], dov, 0.0).astype(BF16)
                lane = h * HEAD_DIM
                sc = _dot_nt(qh, kv)
                if masked:
                    sc = jnp.where(causal, sc, NEG)
                p = jnp.exp(sc - lsev[:, lane:lane + 1])
                dv_ref[...] += _dot_tn(p.astype(BF16), doh)
                dp = _dot_nt(doh, vv)
                dsf = p * (dp - dlv[:, lane:lane + 1])
                drow_ref[pl.ds(off, tq), :] += jnp.where(omasks[h], jnp.sum(dsf, axis=1, keepdims=True), 0.0)
                dcol_ref[0, h:h + 1, :] += jnp.sum(dsf, axis=0, keepdims=True)
                ds = dsf.astype(BF16)
                dka_ref[...] += _dot_tn(ds, qh)
                dq_ref[pl.ds(off, tq), :] += jnp.where(omasks[h], _dot(ds, kk), 0.0)

        step(j, True)

        def loop_body(i, carry):
            step(i, False)
            return carry

        lax.fori_loop(j + 1, nq, loop_body, 0)

    full = lambda w: pl.BlockSpec((s, w), lambda p, j: (0, p))
    blk = lambda w: pl.BlockSpec((tq, w), lambda p, j: (j, p))
    return pl.pallas_call(
        body, name=name, grid=(n_pairs, nq),
        in_specs=[full(PAIR), blk(PAIR), pl.BlockSpec((tq, LANES), lambda p, j: (j, 2 * p)), full(LANES), full(LANES), full(LANES)],
        out_specs=[full(LANES), blk(PAIR), blk(LANES), full(LANES), pl.BlockSpec((1, 16, tq), lambda p, j: (p, 0, j))],
        out_shape=[jax.ShapeDtypeStruct((s, aw), F32), jax.ShapeDtypeStruct((s, n_pairs * PAIR), F32),
                   jax.ShapeDtypeStruct((s, aw), F32), jax.ShapeDtypeStruct((s, aw), F32),
                   jax.ShapeDtypeStruct((n_pairs, 16, s), F32)],
        scratch_shapes=[pltpu.VMEM((s, LANES), F32)],
        compiler_params=_params("parallel", "arbitrary"),
    )(qa, ka, vb, o, lse, dmixed)


def _conv_fwd(proj, w32, bias, n_ch, a_col, g_col, name):
    s = proj.shape[0]
    rows = min(CONV_ROWS, s)

    def body(a_ref, g_ref, w_ref, b_ref, y_ref, pad_ref):
        pad_ref[0:CONV_PAD, :] = jnp.zeros((CONV_PAD, LANES), F32)
        pad_ref[CONV_PAD:CONV_PAD + s, :] = a_ref[...] * _sigmoid(g_ref[...])
        wv = w_ref[...]
        for c in range(s // rows):
            acc = jnp.broadcast_to(b_ref[...], (rows, LANES))
            for t in range(CONV_TAPS):
                start = c * rows + CONV_PAD - (CONV_TAPS - 1) + t
                acc = acc + wv[t:t + 1, :] * pad_ref[start:start + rows, :]
            y_ref[c * rows:(c + 1) * rows, :] = acc

    col = lambda j0: pl.BlockSpec((s, LANES), lambda c: (0, j0 + c))
    return pl.pallas_call(
        body, name=name, grid=(n_ch // LANES,),
        in_specs=[col(a_col), col(g_col), pl.BlockSpec((CONV_PAD, LANES), lambda c: (0, c)), pl.BlockSpec((1, LANES), lambda c: (0, c))],
        out_specs=pl.BlockSpec((s, LANES), lambda c: (0, c)),
        out_shape=jax.ShapeDtypeStruct((s, n_ch), F32),
        scratch_shapes=[pltpu.VMEM((s + CONV_PAD, LANES), F32)],
        compiler_params=_params("parallel"),
    )(proj, proj, w32, bias)


def _conv_bwd(proj, w32, dy, n_ch, a_col, g_col, name):
    s = proj.shape[0]
    rows = min(CONV_ROWS, s)
    sub = 8

    def fold(x):
        acc = x[0:sub, :]
        for r in range(1, rows // sub):
            acc = acc + x[r * sub:(r + 1) * sub, :]
        return acc

    def body(a_ref, g_ref, w_ref, dy_ref, da_ref, dg_ref, dw_ref, padh_ref, padd_ref):
        sg = _sigmoid(g_ref[...])
        padh_ref[0:CONV_PAD, :] = jnp.zeros((CONV_PAD, LANES), F32)
        padh_ref[CONV_PAD:CONV_PAD + s, :] = a_ref[...] * sg
        padd_ref[0:s, :] = dy_ref[...]
        padd_ref[s:s + CONV_PAD, :] = jnp.zeros((CONV_PAD, LANES), F32)
        wv = w_ref[...]
        dw = [jnp.zeros((sub, LANES), F32) for _ in range(CONV_TAPS + 1)]
        for c in range(s // rows):
            r0 = c * rows
            acc = jnp.zeros((rows, LANES), F32)
            dyc = dy_ref[r0:r0 + rows, :]
            for t in range(CONV_TAPS):
                back = r0 + (CONV_TAPS - 1) - t
                acc = acc + wv[t:t + 1, :] * padd_ref[back:back + rows, :]
                start = r0 + CONV_PAD - (CONV_TAPS - 1) + t
                dw[t] = dw[t] + fold(dyc * padh_ref[start:start + rows, :])
            dw[CONV_TAPS] = dw[CONV_TAPS] + fold(dyc)
            av = a_ref[r0:r0 + rows, :]
            sgc = _sigmoid(g_ref[r0:r0 + rows, :])
            da_ref[r0:r0 + rows, :] = (acc * sgc).astype(BF16)
            dg_ref[r0:r0 + rows, :] = (acc * av * sgc * (1.0 - sgc)).astype(BF16)
        for t in range(CONV_TAPS + 1):
            dw_ref[t:t + 1, :] = jnp.sum(dw[t], axis=0, keepdims=True)

    col = lambda j0: pl.BlockSpec((s, LANES), lambda c: (0, j0 + c))
    wspec = pl.BlockSpec((CONV_PAD, LANES), lambda c: (0, c))
    return pl.pallas_call(
        body, name=name, grid=(n_ch // LANES,),
        in_specs=[col(a_col), col(g_col), wspec, col(0)],
        out_specs=[col(0), col(0), wspec],
        out_shape=[jax.ShapeDtypeStruct((s, n_ch), BF16), jax.ShapeDtypeStruct((s, n_ch), BF16),
                   jax.ShapeDtypeStruct((CONV_PAD, n_ch), F32)],
        scratch_shapes=[pltpu.VMEM((s + CONV_PAD, LANES), F32), pltpu.VMEM((s + CONV_PAD, LANES), F32)],
        compiler_params=_params("parallel"),
    )(proj, proj, w32, dy)


def _my_place():
    return lax.axis_index("x"), lax.axis_index("y"), lax.axis_index("c")


def _flip(place, k):
    x, y, c = place
    return (1 - x if k & 4 else x, 1 - y if k & 2 else y, 1 - c if k & 1 else c)


def _dev_id(place):
    return 4 * place[0] + 2 * place[1] + place[2]


def _wait_all(ref, send_sem, recv_sem, place):
    pltpu.make_async_remote_copy(src_ref=ref, dst_ref=ref, send_sem=send_sem, recv_sem=recv_sem,
                                 device_id=place, device_id_type=MESH).wait()


def _window(kind, ref, dev, n):
    if kind == "slot":
        return ref.at[dev]
    if kind == "rows":
        return ref.at[pl.ds(pl.multiple_of(dev * n, n), n), :]
    return ref.at[:, pl.ds(pl.multiple_of(dev * n, n), n)]


def _hbm(x):
    return pltpu.with_memory_space_constraint(x, pltpu.HBM)


_EFFECT = pltpu.SideEffectType.DATAFLOW_SIDE_EFFECTING


def _exchange_start(srcs, lands, specs, groups, name):
    n = len(srcs)
    n_g = len(groups)

    def body(*refs):
        src_refs, land_refs = refs[:n], refs[n:2 * n]
        sems = refs[2 * n:2 * n + 2 * n_g]
        token = refs[-1]
        place = _my_place()
        me = _dev_id(place)
        for g, units in enumerate(groups):
            for j, u in enumerate(units):
                mode, kind, cnt = specs[u]
                for k in range(N_DEV):
                    peer = _flip(place, k)
                    if mode == "gather":
                        src, dst = src_refs[u], _window(kind, land_refs[u], me, cnt)
                    else:
                        src, dst = _window(kind, src_refs[u], _dev_id(peer), cnt), land_refs[u].at[k]
                    pltpu.make_async_remote_copy(src_ref=src, dst_ref=dst, send_sem=sems[2 * g].at[j], recv_sem=sems[2 * g + 1].at[j],
                                                 device_id=peer, device_id_type=MESH).start()
        token[...] = jnp.zeros_like(token)

    hbm = pl.BlockSpec(memory_space=pltpu.HBM)
    sem = pl.BlockSpec(memory_space=pltpu.SEMAPHORE)
    out_shape = [pltpu.SemaphoreType.DMA((len(units),)) for units in groups for _ in range(2)]
    out_shape += [pltpu.HBM(x.shape, x.dtype) for x in lands] + [jax.ShapeDtypeStruct((8, LANES), F32)]
    outs = pl.pallas_call(
        body, name=name, out_shape=out_shape,
        in_specs=[hbm] * (2 * n), out_specs=[sem] * (2 * n_g) + [hbm] * n + [pl.BlockSpec(memory_space=pltpu.VMEM)],
        input_output_aliases={n + u: 2 * n_g + u for u in range(n)},
        compiler_params=pltpu.CompilerParams(has_side_effects=_EFFECT),
    )(*[_hbm(x) for x in srcs], *[_hbm(x) for x in lands])
    sem_pairs = [(outs[2 * g], outs[2 * g + 1]) for g in range(n_g)]
    return sem_pairs, list(outs[2 * n_g:2 * n_g + n]), outs[-1]


def _exchange_wait(srcs, lands, specs, sem_pair, after, name):
    n = len(lands)

    def body(*refs):
        land_refs = refs[n:2 * n]
        send_sems, recv_sems = refs[2 * n], refs[2 * n + 1]
        place = _my_place()
        for u in range(n):
            _wait_all(land_refs[u], send_sems.at[u], recv_sems.at[u], place)

    hbm = pl.BlockSpec(memory_space=pltpu.HBM)
    sem = pl.BlockSpec(memory_space=pltpu.SEMAPHORE)
    outs = pl.pallas_call(
        body, name=name, out_shape=[pltpu.HBM(x.shape, x.dtype) for x in lands],
        in_specs=[hbm] * (2 * n) + [sem, sem, pl.BlockSpec(memory_space=pl.ANY)], out_specs=[hbm] * n,
        input_output_aliases={n + u: u for u in range(n)},
        compiler_params=pltpu.CompilerParams(has_side_effects=_EFFECT),
    )(*[_hbm(x) for x in srcs], *lands, sem_pair[0], sem_pair[1], after)
    return list(outs)


def _sum_devices(parts, name):
    _, r, w = parts.shape

    def body(p_ref, out_ref):
        acc = p_ref[0]
        for d in range(1, N_DEV):
            acc = acc + p_ref[d]
        out_ref[...] = acc

    return pl.pallas_call(
        body, name=name, out_shape=jax.ShapeDtypeStruct((r, w), F32),
        in_specs=[pl.BlockSpec(memory_space=pltpu.VMEM)], out_specs=pl.BlockSpec(memory_space=pltpu.VMEM),
    )(parts)


def _adamw_math(w, m, v, g):
    m_new = ADAM_B1 * m + (1.0 - ADAM_B1) * g
    v_new = ADAM_B2 * v + (1.0 - ADAM_B2) * (g * g)
    m_hat = m_new / (1.0 - ADAM_B1 ** ADAM_STEP)
    v_hat = v_new / (1.0 - ADAM_B2 ** ADAM_STEP)
    return -ADAM_LR * (m_hat / (jnp.sqrt(v_hat) + ADAM_EPS) + ADAM_WD * w), m_new, v_new


def _adamw(w, m, v, g, name):
    rows = w.shape[0]
    tr = min(FLAT_ROWS, rows)
    assert rows % tr == 0, (name, rows)

    def body(w_ref, m_ref, v_ref, g_ref, d_out, m_out, v_out):
        d_out[...], m_out[...], v_out[...] = _adamw_math(w_ref[...], m_ref[...], v_ref[...], g_ref[...])

    flat = pl.BlockSpec((tr, LANES), lambda i: (i, 0))
    return pl.pallas_call(
        body, name=name, grid=(rows // tr,), in_specs=[flat] * 4, out_specs=[flat] * 3,
        out_shape=[jax.ShapeDtypeStruct((rows, LANES), F32)] * 3,
        compiler_params=_params("parallel"),
    )(w, m, v, g)


def _adamw_shard(w, m, v, recv, layer, prev, name):
    depth, a, b = w.shape
    ta = min(256, a)
    assert a % ta == 0

    def body(w_ref, m_ref, v_ref, r_ref, *rest):
        g_out, d_out, m_out, v_out = rest[-4:]
        g = r_ref[0].astype(F32)
        for k in range(1, N_DEV):
            g = g + r_ref[k].astype(F32)
        g_out[0] = g
        d_out[0], m_out[0], v_out[0] = _adamw_math(w_ref[0], m_ref[0], v_ref[0], g)

    lay = pl.BlockSpec((1, ta, b), lambda i: (layer, i, 0))
    in_specs = [lay] * 3 + [pl.BlockSpec((N_DEV, ta, b), lambda i: (0, i, 0))]
    args = [w, m, v, recv]
    aliases = {}
    if prev is not None:
        in_specs += [pl.BlockSpec(memory_space=pl.ANY)] * 4
        args += list(prev)
        aliases = {4 + i: i for i in range(4)}
    return pl.pallas_call(
        body, name=name, grid=(a // ta,), in_specs=in_specs, out_specs=[lay] * 4,
        out_shape=[jax.ShapeDtypeStruct(w.shape, F32)] * 4, input_output_aliases=aliases,
        compiler_params=_params("parallel"),
    )(*args)


def _round_up(n, mult):
    return (n + mult - 1) // mult * mult


def _flatten(parts, row_mult):
    flat = jnp.concatenate([p.reshape(-1) for p in parts])
    rows = _round_up(-(-flat.shape[0] // LANES), row_mult)
    return jnp.pad(flat, (0, rows * LANES - flat.shape[0])).reshape(rows, LANES)


def _unflatten(flat, shapes):
    flat = flat.reshape(-1)
    out, off = [], 0
    for shp in shapes:
        n = 1
        for dim in shp:
            n *= dim
        out.append(flat[off:off + n].reshape(shp))
        off += n
    return out


def kernel(x, norm1_g, w_in, b_f, q_norm_g, k_norm_g, conv_w, conv_b, conv_ln_g, conv_ln_b, w_o, norm2_g, w_mlp_in, w_mlp_out, loss_target, m_norm1_g, m_w_in, m_b_f, m_q_norm_g, m_k_norm_g, m_conv_w, m_conv_b, m_conv_ln_g, m_conv_ln_b, m_w_o, m_norm2_g, m_w_mlp_in, m_w_mlp_out, v_norm1_g, v_w_in, v_b_f, v_q_norm_g, v_k_norm_g, v_conv_w, v_conv_b, v_conv_ln_g, v_conv_ln_b, v_w_o, v_norm2_g, v_w_mlp_in, v_w_mlp_out):
    depth, d_model, n_in_loc = w_in.shape
    seq = x.shape[1]
    n_heads = b_f.shape[1]
    aw = n_heads * HEAD_DIM
    cc = conv_b.shape[1]
    n_in = n_in_loc * N_DEV
    o_f = 3 * aw
    n_all = 3 * aw + 2 * cc + LANES
    assert n_in == 3 * aw + n_heads + 2 * cc and aw + cc == d_model and n_heads % 2 == 0
    assert aw % LANES == 0 and cc % LANES == 0 and x.shape[0] == 1
    me = 4 * lax.axis_index("x") + 2 * lax.axis_index("y") + lax.axis_index("c")

    d_ff = w_mlp_in.shape[2] * N_DEV

    r_o, f_1, f_2 = w_o.shape[1], w_mlp_in.shape[2], w_mlp_out.shape[1]

    ag_src, ag_land, ag_spec = [], [], []
    for l in range(depth):
        ag_src += [w_in[l].astype(BF16), w_o[l].astype(BF16), w_mlp_in[l].astype(BF16), w_mlp_out[l].astype(BF16)]
        ag_land += [(N_DEV, d_model, n_in_loc), (N_DEV * r_o, d_model), (d_model, N_DEV * f_1), (N_DEV * f_2, d_model)]
        ag_spec += [("gather", "slot", 1), ("gather", "rows", r_o), ("gather", "cols", f_1), ("gather", "rows", f_2)]
    ag_src.append(jnp.stack(_split3(conv_w)))
    ag_land.append((N_DEV, 3) + conv_w.shape)
    ag_spec.append(("gather", "slot", 1))
    ag_groups = [[0, 4 * depth], [1], [2, 3]] + [[4 * l + i for i in range(4)] for l in range(1, depth)]
    ag_land = [lax.empty(shp, BF16) for shp in ag_land]
    ag_sems, ag_land, ag_token = _exchange_start(ag_src, ag_land, ag_spec, ag_groups, "gather_start")

    def gathered(g, after):
        units = ag_groups[g]
        return _exchange_wait([ag_src[u] for u in units], [ag_land[u] for u in units], [ag_spec[u] for u in units],
                              ag_sems[g], after, f"gather_wait_{g}")

    def to_all(w):
        return jnp.concatenate([w[:, :o_f], w[:, o_f + n_heads:], w[:, o_f:o_f + n_heads],
                                jnp.zeros((w.shape[0], LANES - n_heads), w.dtype)], axis=1)

    def from_all(w):
        return jnp.concatenate([w[:, :o_f], w[:, n_all - LANES:n_all - LANES + n_heads], w[:, o_f:n_all - LANES]], axis=1)

    def whole_in(lin):
        return to_all(jnp.moveaxis(lin, 0, 1).reshape(d_model, n_in))

    def row(p, l, width=None):
        v = p[l].reshape(1, -1)
        return v if width is None else jnp.pad(v, ((0, 0), (0, width - v.shape[1])))

    a_col, g_col = 3 * aw // LANES, (3 * aw + cc) // LANES

    gq = [jnp.tile(row(q_norm_g, l), (1, n_heads)) for l in range(depth)]
    gk = [jnp.tile(row(k_norm_g, l), (1, n_heads)) for l in range(depth)]
    bfp = [row(b_f, l, LANES) for l in range(depth)]
    add_res = lambda acc, res: (acc + res,)
    w_all, w_out, w_ff1, w_ff2 = [None] * depth, [None] * depth, [None] * depth, [None] * depth

    h = x[0]
    saved = []
    for l in range(depth):
        if l == 0:
            lin, lc = gathered(0, h)
            lc = lc.astype(F32)
            conv_full = jnp.moveaxis(lc[:, 0] + lc[:, 1] + lc[:, 2], 0, 2).reshape(depth, CONV_TAPS, cc)
            w32 = [jnp.pad(conv_full[i], ((0, CONV_PAD - CONV_TAPS), (0, 0))) for i in range(depth)]
        else:
            lin, w_out[l], w_ff1[l], w_ff2[l] = gathered(2 + l, h)
        w_all[l] = whole_in(lin)
        u1, proj = _rms_matmul(h, row(norm1_g, l), w_all[l], out_dtypes=(F32,), name=f"mm_in_{l}")
        qa, ka, vb = _prep_fwd(proj, gq[l], gk[l], bfp[l], n_heads, f"prep_fwd_{l}")
        att, lse, mixed = _attn_fwd(qa, ka, vb, n_heads, aw + cc, f"attn_fwd_{l}")
        yc = _conv_fwd(proj, w32[l], row(conv_b, l), cc, a_col, g_col, f"conv_fwd_{l}")
        mixed = _ln_silu_fwd(yc, row(conv_ln_g, l), row(conv_ln_b, l), mixed, f"ln_silu_fwd_{l}")
        if l == 0:
            w_out[l], = gathered(1, mixed)
        x1 = _matmul(mixed, w_out[l], mode="nn", out_dtypes=(F32,), name=f"mm_o_{l}", epilogue=add_res, extras=(h,))
        if l == 0:
            w_ff1[l], w_ff2[l] = gathered(2, x1)
        u2, a = _rms_matmul(x1, row(norm2_g, l), w_ff1[l], out_dtypes=(BF16,), name=f"mm_ff1_{l}",
                            epilogue=lambda acc: (jnp.square(jnp.maximum(acc, 0.0)),))
        if l < depth - 1:
            x2 = _matmul(a, w_ff2[l], mode="nn", out_dtypes=(F32,), name=f"mm_ff2_{l}", epilogue=add_res, extras=(x1,), tk=d_ff)
        else:
            x2 = None
            dh, dh16, sq = _matmul_loss(a, w_ff2[l], x1, loss_target[0], f"mm_ff2_loss_{l}")
        saved.append(dict(x_in=h, u1=u1, proj=proj, qa=qa, ka=ka, vb=vb, att=att, lse=lse, yc=yc, mixed=mixed,
                          x1=x1, u2=u2, a=a))
        h = x2

    loss = lax.psum(0.5 * jnp.sum(sq) / d_model, ("x", "y", "c"))

    g_in, g_o, g_1, g_2 = [None] * depth, [None] * depth, [None] * depth, [None] * depth
    gs = {n: [None] * depth for n in ("norm1", "bf", "qn", "kn", "convw", "convb", "lng", "lnb", "norm2")}
    scattering = {}

    def scatter_start(stage, l, srcs, specs, slabs):
        lands = [lax.empty((N_DEV,) + shp, BF16) for shp in slabs]
        sems, lands, token = _exchange_start(srcs, lands, specs, [list(range(len(srcs)))], f"scatter_start_{stage}_{l}")
        scattering[(stage, l)] = (srcs, lands, specs, sems[0])
        return token[0, 0]

    for l in reversed(range(depth)):
        sv = saved[l]
        dh1 = _matmul(dh16, w_ff2[l], mode="nt", out_dtypes=(BF16,), name=f"mm_dff2_{l}", tm=256, tn=d_ff,
                      epilogue=lambda acc, aa: (acc * (2.0 * jnp.sqrt(aa.astype(F32))),), extras=(sv["a"],))
        g_2[l] = _matmul(sv["a"], dh16, mode="tn", out_dtypes=(BF16,), name=f"mm_dw2_{l}", tm=1024, tk=seq)
        g_1[l] = _matmul(sv["u2"], dh1, mode="tn", out_dtypes=(BF16,), name=f"mm_dw1_{l}", tm=d_model, tk=seq)
        tok = scatter_start("ff", l, [g_1[l], g_2[l]], [("scatter", "cols", f_1), ("scatter", "rows", f_2)],
                            [(d_model, f_1), (f_2, d_model)])
        dx1, dx16, gs["norm2"][l] = _matmul_rms_bwd(dh1, w_ff1[l], sv["x1"], row(norm2_g, l) + tok, dh, f"mm_du2_rms_{l}")

        dmixed = _matmul(dx16, w_out[l], mode="nt", out_dtypes=(F32,), name=f"mm_dmixed_{l}")
        g_o[l] = _matmul(sv["mixed"], dx16, mode="tn", out_dtypes=(BF16,), name=f"mm_dwo_{l}", tm=1024, tk=seq // 2)
        dyc, gs["lng"][l], gs["lnb"][l] = _ln_silu_bwd(sv["yc"], row(conv_ln_g, l), row(conv_ln_b, l), dmixed, f"ln_silu_bwd_{l}")
        dpa, dpg, dw32 = _conv_bwd(sv["proj"], w32[l], dyc, cc, a_col, g_col, f"conv_bwd_{l}")
        gs["convw"][l], gs["convb"][l] = dw32[:CONV_TAPS], dw32[CONV_TAPS:CONV_TAPS + 1]
        dq, dka, dv, drow, dcol = _attn_bwd(sv["qa"], sv["ka"], sv["vb"], sv["att"], sv["lse"], dmixed, n_heads, f"attn_bwd_{l}")
        dpq, dpk, dpf, dgq, dgk, dbf = _prep_bwd(sv["proj"], dq, dka, drow, dcol, gq[l], gk[l], bfp[l], n_heads, f"prep_bwd_{l}")
        gs["qn"][l] = dgq.reshape(n_heads, HEAD_DIM).sum(axis=0)
        gs["kn"][l] = dgk.reshape(n_heads, HEAD_DIM).sum(axis=0)
        gs["bf"][l] = dbf[0, :n_heads]
        dproj = jnp.concatenate([dpq, dpk, dv.astype(BF16), dpa, dpg, dpf], axis=1)
        dwall = _matmul(sv["u1"], dproj, mode="tn", out_dtypes=(BF16,), name=f"mm_dwall_{l}", tm=d_model, tn=n_all // 3, tk=seq)
        g_in[l] = jnp.moveaxis(from_all(dwall).reshape(d_model, N_DEV, n_in_loc), 1, 0)
        tok = scatter_start("mix", l, [g_in[l], g_o[l]], [("scatter", "slot", 1), ("scatter", "rows", r_o)],
                            [(d_model, n_in_loc), (r_o, d_model)])
        dh, dh16, gs["norm1"][l] = _matmul_rms_bwd(dproj, w_all[l], sv["x_in"], row(norm1_g, l) + tok, dx1, f"mm_du1_rms_{l}")
    grad_x = dh[None]

    small_g = [jnp.stack(gs[n]).reshape(shp) for n, shp in (
        ("norm1", norm1_g.shape), ("bf", b_f.shape), ("qn", q_norm_g.shape), ("kn", k_norm_g.shape),
        ("convw", (depth, CONV_TAPS, cc)), ("convb", conv_b.shape), ("lng", conv_ln_g.shape), ("lnb", conv_ln_b.shape),
        ("norm2", norm2_g.shape))]
    small_shapes = [g.shape for g in small_g]
    small_flat = _flatten(small_g, 8)
    small_spec = [("gather", "slot", 1)]
    small_sems, small_land, _ = _exchange_start([small_flat], [lax.empty((N_DEV,) + small_flat.shape, F32)], small_spec, [[0]],
                                                "small_grads_start")

    def landed(stage, l, after):
        srcs, lands, specs, sems = scattering[(stage, l)]
        return _exchange_wait(srcs, lands, specs, sems, after, f"scatter_wait_{stage}_{l}")

    def adamw_layers(kd, w, m, v, recv):
        outs = None
        for l in reversed(range(depth)):
            outs = _adamw_shard(w, m, v, recv[l], l, outs, f"adamw_{kd}_{l}")
        return outs

    recv_1, recv_2, recv_in, recv_o = [None] * depth, [None] * depth, [None] * depth, [None] * depth
    for l in reversed(range(depth)):
        recv_1[l], recv_2[l] = landed("ff", l, dh)
    out_1 = adamw_layers("1", w_mlp_in, m_w_mlp_in, v_w_mlp_in, recv_1)
    out_2 = adamw_layers("2", w_mlp_out, m_w_mlp_out, v_w_mlp_out, recv_2)
    for l in reversed(range(depth)):
        recv_in[l], recv_o[l] = landed("mix", l, out_2[1])
    out_in = adamw_layers("in", w_in, m_w_in, v_w_in, recv_in)
    out_o = adamw_layers("o", w_o, m_w_o, v_w_o, recv_o)
    big_out = [[outs[kind] for outs in (out_in, out_o, out_1, out_2)] for kind in range(4)]

    small_parts, = _exchange_wait([small_flat], small_land, small_spec, small_sems[0], out_o[1], "small_grads_wait")
    small_g = _unflatten(_sum_devices(small_parts, "small_grads_sum"), small_shapes)
    cw = conv_w.shape[2]
    small_g[4] = lax.dynamic_slice_in_dim(small_g[4], me * cw, cw, axis=2)
    small = (norm1_g, b_f, q_norm_g, k_norm_g, conv_w, conv_b, conv_ln_g, conv_ln_b, norm2_g)
    small_m = (m_norm1_g, m_b_f, m_q_norm_g, m_k_norm_g, m_conv_w, m_conv_b, m_conv_ln_g, m_conv_ln_b, m_norm2_g)
    small_v = (v_norm1_g, v_b_f, v_q_norm_g, v_k_norm_g, v_conv_w, v_conv_b, v_conv_ln_g, v_conv_ln_b, v_norm2_g)
    small_out = _adamw(_flatten(small, 8), _flatten(small_m, 8), _flatten(small_v, 8), _flatten(small_g, 8), "adamw_small")
    small_out = [small_g] + [_unflatten(o, [w.shape for w in small]) for o in small_out]

    def group(kind):
        s_, b_ = small_out[kind], big_out[kind]
        return [s_[0], b_[0], s_[1], s_[2], s_[3], s_[4], s_[5], s_[6], s_[7], b_[1], s_[8], b_[2], b_[3]]

    return (loss, grad_x, *group(0), *group(1), *group(2), *group(3))
```

```python
import functools

import jax
import jax.numpy as jnp
from jax import lax
from jax.experimental import pallas as pl
from jax.experimental.pallas import tpu as pltpu

F32 = jnp.float32
BF16 = jnp.bfloat16

EPS = 1e-6
HEAD_DIM = 64
LANES = 128
PAIR = 2 * LANES
N_DEV = 8
CONV_TAPS = 31
CONV_PAD = 32
NEG = -1e30

ADAM_LR = 0.001
ADAM_B1 = 0.9
ADAM_B2 = 0.999
ADAM_EPS = 1e-08
ADAM_WD = 0.01
ADAM_STEP = 10

TM = 512
TQ = 512
CONV_ROWS = 128
FLAT_ROWS = 1024
MESH = pl.DeviceIdType.MESH


def _params(*sem):
    return pltpu.CompilerParams(dimension_semantics=sem, vmem_limit_bytes=56 * 1024 * 1024)


def _split3(x):
    hi = x.astype(BF16)
    r1 = x - hi.astype(F32)
    mid = r1.astype(BF16)
    lo = (r1 - mid.astype(F32)).astype(BF16)
    return hi, mid, lo


def _dot(a, b):
    return jnp.dot(a, b, preferred_element_type=F32)


def _dot_nt(a, b):
    return lax.dot_general(a, b, (((1,), (1,)), ((), ())), preferred_element_type=F32)


def _dot_tn(a, b):
    return lax.dot_general(a, b, (((0,), (0,)), ((), ())), preferred_element_type=F32)


def _dot3(x, mat):
    hi, mid, lo = _split3(x)
    return _dot(hi, mat) + _dot(mid, mat) + _dot(lo, mat)


def _dot2(x, mat):
    hi = x.astype(BF16)
    lo = (x - hi.astype(F32)).astype(BF16)
    return _dot(hi, mat) + _dot(lo, mat)


def _dot3_r(mat, x):
    hi, mid, lo = _split3(x)
    return _dot(mat, hi) + _dot(mat, mid) + _dot(mat, lo)


def _iota(shape, dim):
    return lax.broadcasted_iota(jnp.int32, shape, dim)


def _sigmoid(x):
    return 1.0 / (1.0 + jnp.exp(-x))


def _matmul(a, b, *, mode, out_dtypes, name, epilogue=None, extras=(), tm=TM, tn=1024, tk=1024):
    if mode == "nn":
        (m, k), (k2, n) = a.shape, b.shape
    elif mode == "nt":
        (m, k), (n, k2) = a.shape, b.shape
    else:
        (k, m), (k2, n) = a.shape, b.shape
    assert k == k2, (name, a.shape, b.shape)
    tm, tn, tk = min(tm, m), min(tn, n), min(tk, k)
    assert m % tm == 0 and n % tn == 0 and k % tk == 0, (name, m, n, k, tm, tn, tk)
    nk = k // tk
    a_mode = dict(pipeline_mode=pl.Buffered(1)) if (m == tm and nk == 1) else {}
    b_mode = dict(pipeline_mode=pl.Buffered(1)) if (n == tn and nk == 1) else {}
    if mode == "tn":
        a_spec = pl.BlockSpec((tk, tm), lambda i, j, kk: (kk, i), **a_mode)
    else:
        a_spec = pl.BlockSpec((tm, tk), lambda i, j, kk: (i, kk), **a_mode)
    if mode == "nt":
        b_spec = pl.BlockSpec((tn, tk), lambda i, j, kk: (j, kk), **b_mode)
    else:
        b_spec = pl.BlockSpec((tk, tn), lambda i, j, kk: (kk, j), **b_mode)
    dot = {"nn": _dot, "nt": _dot_nt, "tn": _dot_tn}[mode]
    tile = pl.BlockSpec((tm, tn), lambda i, j, kk: (i, j))
    n_ex, n_out = len(extras), len(out_dtypes)
    acc_in_out = nk > 1 and epilogue is None and out_dtypes[0] == F32

    def body(a_ref, b_ref, *rest):
        ex_refs, out_refs = rest[:n_ex], rest[n_ex:n_ex + n_out]
        part = dot(a_ref[...].astype(BF16), b_ref[...].astype(BF16))

        def finish(acc):
            res = epilogue(acc, *[e[...] for e in ex_refs]) if epilogue is not None else (acc,) * n_out
            for o_ref, r in zip(out_refs, res):
                o_ref[...] = r.astype(o_ref.dtype)

        if nk == 1:
            finish(part)
        else:
            acc_ref = out_refs[0] if acc_in_out else rest[-1]
            kk = pl.program_id(2)

            @pl.when(kk == 0)
            def _():
                acc_ref[...] = part

            @pl.when(kk > 0)
            def _():
                acc_ref[...] += part

            @pl.when(kk == nk - 1)
            def _():
                if acc_in_out:
                    for o_ref in out_refs[1:]:
                        o_ref[...] = acc_ref[...].astype(o_ref.dtype)
                else:
                    finish(acc_ref[...])

    outs = pl.pallas_call(
        body,
        name=name,
        grid=(m // tm, n // tn, nk),
        in_specs=[a_spec, b_spec] + [tile] * n_ex,
        out_specs=[tile] * n_out,
        out_shape=[jax.ShapeDtypeStruct((m, n), dt) for dt in out_dtypes],
        scratch_shapes=[pltpu.VMEM((tm, tn), F32)] if nk > 1 and not acc_in_out else [],
        compiler_params=_params("parallel", "parallel", "arbitrary"),
    )(a, b, *extras)
    return outs if n_out > 1 else outs[0]


def _rms_matmul(x, g, b, *, out_dtypes, name, epilogue=None, tm=TM):
    s, d = x.shape
    n = b.shape[1]
    ts = min(tm, s)
    n_out = len(out_dtypes)

    def body(x_ref, g_ref, b_ref, u_ref, *out_refs):
        xv = x_ref[...]
        u = (xv * lax.rsqrt(jnp.mean(xv * xv, axis=-1, keepdims=True) + EPS) * g_ref[...]).astype(BF16)
        u_ref[...] = u
        acc = _dot(u, b_ref[...])
        res = epilogue(acc) if epilogue is not None else (acc,)
        for o_ref, r in zip(out_refs, res):
            o_ref[...] = r.astype(o_ref.dtype)

    row = lambda w: pl.BlockSpec((ts, w), lambda i: (i, 0))
    return pl.pallas_call(
        body, name=name, grid=(s // ts,),
        in_specs=[row(d), pl.BlockSpec((1, d), lambda i: (0, 0)), pl.BlockSpec((d, n), lambda i: (0, 0), pipeline_mode=pl.Buffered(1))],
        out_specs=[row(d)] + [row(n)] * n_out,
        out_shape=[jax.ShapeDtypeStruct((s, d), BF16)] + [jax.ShapeDtypeStruct((s, n), dt) for dt in out_dtypes],
        compiler_params=_params("parallel"),
    )(x, g, b)


def _matmul_rms_bwd(a, b, x, g, dres, name):
    s, k = a.shape
    d = b.shape[0]
    ts = min(TM, s)

    def body(a_ref, b_ref, x_ref, g_ref, dres_ref, dx_ref, dx16_ref, dg_ref):
        @pl.when(pl.program_id(0) == 0)
        def _():
            dg_ref[...] = jnp.zeros_like(dg_ref)

        duv = _dot_nt(a_ref[...], b_ref[...])
        xv = x_ref[...]
        r = lax.rsqrt(jnp.mean(xv * xv, axis=-1, keepdims=True) + EPS)
        xh = xv * r
        dxh = duv * g_ref[...]
        dx = dres_ref[...] + r * (dxh - xh * jnp.mean(dxh * xh, axis=-1, keepdims=True))
        dx_ref[...] = dx
        dx16_ref[...] = dx.astype(BF16)
        dg_ref[...] += jnp.sum(duv * xh, axis=0, keepdims=True)

    row = pl.BlockSpec((ts, d), lambda i: (i, 0))
    vec = pl.BlockSpec((1, d), lambda i: (0, 0))
    return pl.pallas_call(
        body, name=name, grid=(s // ts,),
        in_specs=[pl.BlockSpec((ts, k), lambda i: (i, 0)), pl.BlockSpec((d, k), lambda i: (0, 0), pipeline_mode=pl.Buffered(1)),
                  row, vec, row],
        out_specs=[row, row, vec],
        out_shape=[jax.ShapeDtypeStruct((s, d), F32), jax.ShapeDtypeStruct((s, d), BF16), jax.ShapeDtypeStruct((1, d), F32)],
        compiler_params=_params("arbitrary"),
    )(a, b, x, g, dres)


def _matmul_loss(a, b, res, target, name):
    s, k = a.shape
    d = b.shape[1]
    ts = min(TM, s)

    def body(a_ref, b_ref, res_ref, t_ref, dy_ref, dy16_ref, sq_ref):
        @pl.when(pl.program_id(0) == 0)
        def _():
            sq_ref[...] = jnp.zeros_like(sq_ref)

        err = _dot(a_ref[...], b_ref[...]) + res_ref[...] - t_ref[...]
        dy = err * (1.0 / d)
        dy_ref[...] = dy
        dy16_ref[...] = dy.astype(BF16)
        sq_ref[...] += jnp.sum(err * err, axis=0, keepdims=True)

    row = pl.BlockSpec((ts, d), lambda i: (i, 0))
    vec = pl.BlockSpec((1, d), lambda i: (0, 0))
    return pl.pallas_call(
        body, name=name, grid=(s // ts,),
        in_specs=[pl.BlockSpec((ts, k), lambda i: (i, 0)), pl.BlockSpec((k, d), lambda i: (0, 0), pipeline_mode=pl.Buffered(1)),
                  row, row],
        out_specs=[row, row, vec],
        out_shape=[jax.ShapeDtypeStruct((s, d), F32), jax.ShapeDtypeStruct((s, d), BF16), jax.ShapeDtypeStruct((1, d), F32)],
        compiler_params=_params("arbitrary"),
    )(a, b, res, target)


def _ln_silu_fwd(y, g, b, mixed, name):
    s, c = y.shape
    ts = min(TM, s)
    assert mixed.shape == (s, 2 * c)

    def body(y_ref, g_ref, b_ref, mixed_ref, h_ref):
        yv = y_ref[...]
        mu = jnp.mean(yv, axis=-1, keepdims=True)
        yc = yv - mu
        z = yc * lax.rsqrt(jnp.mean(yc * yc, axis=-1, keepdims=True) + EPS) * g_ref[...] + b_ref[...]
        h_ref[...] = (z * _sigmoid(z)).astype(BF16)

    row = pl.BlockSpec((ts, c), lambda i: (i, 0))
    vec = pl.BlockSpec((1, c), lambda i: (0, 0))
    return pl.pallas_call(
        body, name=name, grid=(s // ts,), in_specs=[row, vec, vec, pl.BlockSpec(memory_space=pl.ANY)],
        out_specs=pl.BlockSpec((ts, c), lambda i: (i, 1)),
        out_shape=jax.ShapeDtypeStruct((s, 2 * c), BF16), input_output_aliases={3: 0},
        compiler_params=_params("parallel"),
    )(y, g, b, mixed)


def _ln_silu_bwd(y, g, b, dmixed, name):
    s, c = y.shape
    ts = min(TM, s)

    def body(y_ref, g_ref, b_ref, dh_ref, dy_ref, dg_ref, db_ref):
        @pl.when(pl.program_id(0) == 0)
        def _():
            dg_ref[...] = jnp.zeros_like(dg_ref)
            db_ref[...] = jnp.zeros_like(db_ref)

        yv = y_ref[...]
        mu = jnp.mean(yv, axis=-1, keepdims=True)
        yc = yv - mu
        r = lax.rsqrt(jnp.mean(yc * yc, axis=-1, keepdims=True) + EPS)
        yh = yc * r
        z = yh * g_ref[...] + b_ref[...]
        sg = _sigmoid(z)
        dz = dh_ref[...] * (sg * (1.0 + z * (1.0 - sg)))
        dg_ref[...] += jnp.sum(dz * yh, axis=0, keepdims=True)
        db_ref[...] += jnp.sum(dz, axis=0, keepdims=True)
        dyh = dz * g_ref[...]
        dy_ref[...] = r * (dyh - jnp.mean(dyh, axis=-1, keepdims=True) - yh * jnp.mean(dyh * yh, axis=-1, keepdims=True))

    row = pl.BlockSpec((ts, c), lambda i: (i, 0))
    vec = pl.BlockSpec((1, c), lambda i: (0, 0))
    return pl.pallas_call(
        body, name=name, grid=(s // ts,),
        in_specs=[row, vec, vec, pl.BlockSpec((ts, c), lambda i: (i, 1))], out_specs=[row, vec, vec],
        out_shape=[jax.ShapeDtypeStruct((s, c), F32), jax.ShapeDtypeStruct((1, c), F32), jax.ShapeDtypeStruct((1, c), F32)],
        compiler_params=_params("arbitrary"),
    )(y, g, b, dmixed)


def _head_masks():
    lane2 = _iota((1, PAIR), 1)
    lane1 = _iota((1, LANES), 1)
    qa = (lane2 < HEAD_DIM) | ((lane2 >= LANES) & (lane2 < LANES + 3))
    qb = ((lane2 >= HEAD_DIM) & (lane2 < LANES)) | ((lane2 >= LANES + 3) & (lane2 < LANES + 6))
    return (qa, qb), (lane1 < HEAD_DIM, lane1 >= HEAD_DIM)


def _group_matrix(width):
    shift = HEAD_DIM.bit_length() - 1
    return ((_iota((width, width), 0) >> shift) == (_iota((width, width), 1) >> shift)).astype(BF16)


def _prep_fwd(proj, gq, gk, bf, n_heads, name):
    s = proj.shape[0]
    aw = n_heads * HEAD_DIM
    n_pairs = n_heads // 2
    ts = min(TM, s)
    f_col = (proj.shape[1] - LANES) // LANES

    def body(q_ref, k_ref, v_ref, f_ref, gq_ref, gk_ref, bf_ref, qa_ref, ka_ref, vb_ref, carry_ref):
        @pl.when(pl.program_id(0) == 0)
        def _():
            carry_ref[...] = jnp.zeros_like(carry_ref)

        gmat = _group_matrix(aw)

        def head_norm(xv, g):
            ms = _dot2(xv * xv, gmat) * (1.0 / HEAD_DIM)
            return xv * lax.rsqrt(ms + EPS) * g

        qn = head_norm(q_ref[...], gq_ref[...]) * (HEAD_DIM ** -0.5)
        kn = head_norm(k_ref[...], gk_ref[...])
        z = f_ref[...] + bf_ref[...]
        logf = jnp.minimum(z, 0.0) - jnp.log(1.0 + jnp.exp(-jnp.abs(z)))
        tri = (_iota((ts, ts), 0) >= _iota((ts, ts), 1)).astype(BF16)
        c = _dot3_r(tri, logf) + carry_ref[...]
        carry_ref[...] = c[ts - 1:ts, :]
        terms = _split3(-c)
        row, col = _iota((LANES, LANES), 0), _iota((LANES, LANES), 1)
        ones = jnp.where(_iota((ts, LANES), 1) < 6, 1.0, 0.0).astype(BF16)
        for p in range(n_pairs):
            extra = jnp.zeros((ts, LANES), F32)
            for t, term in enumerate(terms):
                sel = ((row == 2 * p) & (col == t)) | ((row == 2 * p + 1) & (col == 3 + t))
                extra += _dot(term, sel.astype(BF16))
            lo, hi = p * PAIR, p * PAIR + LANES
            ka_ref[:, lo:hi] = kn[:, p * LANES:(p + 1) * LANES].astype(BF16)
            ka_ref[:, hi:hi + LANES] = extra.astype(BF16)
            qa_ref[:, lo:hi] = qn[:, p * LANES:(p + 1) * LANES].astype(BF16)
            qa_ref[:, hi:hi + LANES] = ones
            vb_ref[:, lo:hi] = v_ref[:, p * LANES:(p + 1) * LANES].astype(BF16)
            vb_ref[:, hi:hi + LANES] = jnp.ones((ts, LANES), BF16)

    blk = lambda j: pl.BlockSpec((ts, aw), lambda i: (i, j))
    vec = lambda w: pl.BlockSpec((1, w), lambda i: (0, 0))
    return pl.pallas_call(
        body, name=name, grid=(s // ts,),
        in_specs=[blk(0), blk(1), blk(2), pl.BlockSpec((ts, LANES), lambda i: (i, f_col)), vec(aw), vec(aw), vec(LANES)],
        out_specs=[pl.BlockSpec((ts, n_pairs * PAIR), lambda i: (i, 0))] * 3,
        out_shape=[jax.ShapeDtypeStruct((s, n_pairs * PAIR), BF16)] * 3,
        scratch_shapes=[pltpu.VMEM((1, LANES), F32)],
        compiler_params=_params("arbitrary"),
    )(proj, proj, proj, proj, gq, gk, bf)


def _prep_bwd(proj, dq, dka, drow, dcol, gq, gk, bf, n_heads, name):
    s = proj.shape[0]
    aw = n_heads * HEAD_DIM
    n_pairs = n_heads // 2
    ts = min(TM, s)
    nt = s // ts
    f_col = (proj.shape[1] - LANES) // LANES
    shift = HEAD_DIM.bit_length() - 1

    def body(q_ref, k_ref, f_ref, dq_ref, dka_ref, drow_ref, dcol_ref, gq_ref, gk_ref, bf_ref,
             dpq_ref, dpk_ref, dpf_ref, dgq_ref, dgk_ref, dbf_ref, carry_ref):
        @pl.when(pl.program_id(0) == 0)
        def _():
            carry_ref[...] = jnp.zeros_like(carry_ref)
            dgq_ref[...] = jnp.zeros_like(dgq_ref)
            dgk_ref[...] = jnp.zeros_like(dgk_ref)
            dbf_ref[...] = jnp.zeros_like(dbf_ref)

        gmat = _group_matrix(aw)

        def head_norm_bwd(xv, g, dn):
            r = lax.rsqrt(_dot2(xv * xv, gmat) * (1.0 / HEAD_DIM) + EPS)
            xh = xv * r
            dxh = dn * g
            dx = r * (dxh - xh * (_dot2(dxh * xh, gmat) * (1.0 / HEAD_DIM)))
            return dx, jnp.sum(dn * xh, axis=0, keepdims=True)

        dkav = dka_ref[...]
        dx, dg = head_norm_bwd(q_ref[...], gq_ref[...], dq_ref[...] * (HEAD_DIM ** -0.5))
        dpq_ref[...] = dx.astype(BF16)
        dgq_ref[...] += dg
        dkn = jnp.concatenate([dkav[:, p * PAIR:p * PAIR + LANES] for p in range(n_pairs)], axis=1)
        dx, dg = head_norm_bwd(k_ref[...], gk_ref[...], dkn)
        dpk_ref[...] = dx.astype(BF16)
        dgk_ref[...] += dg

        pick = (_iota((aw, LANES), 0) == (_iota((aw, LANES), 1) << shift)).astype(BF16)
        dc = _dot3(drow_ref[...], pick)
        r16, c16 = _iota((16, LANES), 0), _iota((16, LANES), 1)
        for p in range(n_pairs):
            place = ((r16 < 2) & (c16 == 2 * p + r16)).astype(BF16)
            for term in _split3(dcol_ref[p]):
                dc -= _dot_tn(term, place)
        triu = (_iota((ts, ts), 0) <= _iota((ts, ts), 1)).astype(BF16)
        dlogf = _dot3_r(triu, dc) + carry_ref[...]
        carry_ref[...] = dlogf[0:1, :]
        z = f_ref[...] + bf_ref[...]
        dz = dlogf * (1.0 / (1.0 + jnp.exp(z)))
        dpf_ref[...] = dz.astype(BF16)
        dbf_ref[...] += jnp.sum(dz, axis=0, keepdims=True)

    rev = lambda w, j: pl.BlockSpec((ts, w), lambda i: (nt - 1 - i, j))
    vec = lambda w: pl.BlockSpec((1, w), lambda i: (0, 0))
    return pl.pallas_call(
        body, name=name, grid=(nt,),
        in_specs=[rev(aw, 0), rev(aw, 1), rev(LANES, f_col), rev(aw, 0), rev(n_pairs * PAIR, 0), rev(aw, 0),
                  pl.BlockSpec((n_pairs, 16, ts), lambda i: (0, 0, nt - 1 - i)), vec(aw), vec(aw), vec(LANES)],
        out_specs=[rev(aw, 0), rev(aw, 0), rev(LANES, 0), vec(aw), vec(aw), vec(LANES)],
        out_shape=[jax.ShapeDtypeStruct((s, aw), BF16), jax.ShapeDtypeStruct((s, aw), BF16), jax.ShapeDtypeStruct((s, LANES), BF16),
                   jax.ShapeDtypeStruct((1, aw), F32), jax.ShapeDtypeStruct((1, aw), F32), jax.ShapeDtypeStruct((1, LANES), F32)],
        scratch_shapes=[pltpu.VMEM((1, LANES), F32)],
        compiler_params=_params("arbitrary"),
    )(proj, proj, proj, dq, dka, drow, dcol, gq, gk, bf)


def _attn_fwd(qa, ka, vb, n_heads, mix_width, name):
    s = qa.shape[0]
    aw = n_heads * HEAD_DIM
    n_pairs = n_heads // 2
    tq = min(TQ, s)

    def body(q_ref, k_ref, v_ref, o_ref, lse_ref, o16_ref):
        i = pl.program_id(1)
        qmasks, omasks = _head_masks()
        qv = q_ref[...]
        causal = _iota((tq, tq), 1) <= _iota((tq, tq), 0)
        qhs = [jnp.where(qmasks[h], qv, jnp.zeros_like(qv)) for h in range(2)]

        def scores(j):
            kv = k_ref[pl.ds(pl.multiple_of(j * tq, tq), tq), :]
            return tuple(_dot_nt(qhs[h], kv) for h in range(2))

        def update(j, state, scs, masked):
            vv = v_ref[pl.ds(pl.multiple_of(j * tq, tq), tq), :]
            out = []
            for h in range(2):
                m, acc = state[h]
                sc = jnp.where(causal, scs[h], NEG) if masked else scs[h]
                m_new = jnp.maximum(m, jnp.max(sc, axis=1, keepdims=True))
                p = jnp.exp(sc - m_new).astype(BF16)
                out.append((m_new, jnp.exp(m - m_new) * acc + _dot(p, vv)))
            return tuple(out)

        def body(j, state):
            return update(j, state, scores(j), False)

        init = ((jnp.full((tq, 1), NEG, F32), jnp.zeros((tq, PAIR), F32)),) * 2
        state = lax.fori_loop(0, i, body, init)
        res = []
        for m, acc in update(i, state, scores(i), True):
            l = acc[:, LANES:LANES + 1]
            res.append((acc[:, :LANES] * (1.0 / l), m + jnp.log(l)))
        out = jnp.where(omasks[0], res[0][0], res[1][0])
        o_ref[...] = out
        o16_ref[...] = out.astype(BF16)
        lse_ref[...] = jnp.where(omasks[0], res[0][1], res[1][1])

    return pl.pallas_call(
        body, name=name, grid=(n_pairs, s // tq),
        in_specs=[pl.BlockSpec((tq, PAIR), lambda p, i: (i, p)), pl.BlockSpec((s, PAIR), lambda p, i: (0, p)),
                  pl.BlockSpec((s, PAIR), lambda p, i: (0, p))],
        out_specs=[pl.BlockSpec((tq, LANES), lambda p, i: (i, p))] * 3,
        out_shape=[jax.ShapeDtypeStruct((s, aw), F32)] * 2 + [jax.ShapeDtypeStruct((s, mix_width), BF16)],
        compiler_params=_params("parallel", "parallel"),
    )(qa, ka, vb)


def _attn_bwd(qa, ka, vb, o, lse, dmixed, n_heads, name):
    s = qa.shape[0]
    aw = n_heads * HEAD_DIM
    n_pairs = n_heads // 2
    tq = min(TQ, s)
    nq = s // tq

    def body(q_ref, k_ref, v_ref, o_ref, lse_ref, do_ref, dq_ref, dka_ref, dv_ref, drow_ref, dcol_ref, delta_ref):
        j = pl.program_id(1)
        qmasks, omasks = _head_masks()

        @pl.when(j == 0)
        def _():
            dq_ref[...] = jnp.zeros_like(dq_ref)
            drow_ref[...] = jnp.zeros_like(drow_ref)
            for c in range(nq):
                rows = slice(c * tq, (c + 1) * tq)
                prod = do_ref[rows, :] * o_ref[rows, :]
                da = jnp.sum(jnp.where(omasks[0], prod, 0.0), axis=1, keepdims=True)
                db = jnp.sum(jnp.where(omasks[1], prod, 0.0), axis=1, keepdims=True)
                delta_ref[rows, :] = jnp.where(omasks[0], da, db)

        dka_ref[...] = jnp.zeros_like(dka_ref)
        dv_ref[...] = jnp.zeros_like(dv_ref)
        dcol_ref[...] = jnp.zeros_like(dcol_ref)
        kv = k_ref[...]
        kk = kv[:, :LANES]
        vv = v_ref[...]
        causal = _iota((tq, tq), 1) <= _iota((tq, tq), 0)

        def step(i, masked):
            off = pl.multiple_of(i * tq, tq)
            qv = q_ref[pl.ds(off, tq), :]
            dov = do_ref[pl.ds(off, tq), :]
            lsev = lse_ref[pl.ds(off, tq), :]
            dlv = delta_ref[pl.ds(off, tq), :]
            for h in range(2):
                qh = jnp.where(qmasks[h], qv, jnp.zeros_like(qv))
                doh = jnp.where(omasks[h], dov, 0.0).astype(BF16)
                lane = h * HEAD_DIM
                sc = _dot_nt(qh, kv)
                if masked:
                    sc = jnp.where(causal, sc, NEG)
                p = jnp.exp(sc - lsev[:, lane:lane + 1])
                dv_ref[...] += _dot_tn(p.astype(BF16), doh)
                dp = _dot_nt(doh, vv)
                dsf = p * (dp - dlv[:, lane:lane + 1])
                drow_ref[pl.ds(off, tq), :] += jnp.where(omasks[h], jnp.sum(dsf, axis=1, keepdims=True), 0.0)
                dcol_ref[0, h:h + 1, :] += jnp.sum(dsf, axis=0, keepdims=True)
                ds = dsf.astype(BF16)
                dka_ref[...] += _dot_tn(ds, qh)
                dq_ref[pl.ds(off, tq), :] += jnp.where(omasks[h], _dot(ds, kk), 0.0)

        step(j, True)

        def loop_body(i, carry):
            step(i, False)
            return carry

        lax.fori_loop(j + 1, nq, loop_body, 0)

    full = lambda w: pl.BlockSpec((s, w), lambda p, j: (0, p))
    blk = lambda w: pl.BlockSpec((tq, w), lambda p, j: (j, p))
    return pl.pallas_call(
        body, name=name, grid=(n_pairs, nq),
        in_specs=[full(PAIR), blk(PAIR), pl.BlockSpec((tq, LANES), lambda p, j: (j, 2 * p)), full(LANES), full(LANES), full(LANES)],
        out_specs=[full(LANES), blk(PAIR), blk(LANES), full(LANES), pl.BlockSpec((1, 16, tq), lambda p, j: (p, 0, j))],
        out_shape=[jax.ShapeDtypeStruct((s, aw), F32), jax.ShapeDtypeStruct((s, n_pairs * PAIR), F32),
                   jax.ShapeDtypeStruct((s, aw), F32), jax.ShapeDtypeStruct((s, aw), F32),
                   jax.ShapeDtypeStruct((n_pairs, 16, s), F32)],
        scratch_shapes=[pltpu.VMEM((s, LANES), F32)],
        compiler_params=_params("parallel", "arbitrary"),
    )(qa, ka, vb, o, lse, dmixed)


def _conv_fwd(proj, w32, bias, n_ch, a_col, g_col, name):
    s = proj.shape[0]
    rows = min(CONV_ROWS, s)

    def body(a_ref, g_ref, w_ref, b_ref, y_ref, pad_ref):
        pad_ref[0:CONV_PAD, :] = jnp.zeros((CONV_PAD, LANES), F32)
        pad_ref[CONV_PAD:CONV_PAD + s, :] = a_ref[...] * _sigmoid(g_ref[...])
        wv = w_ref[...]
        for c in range(s // rows):
            acc = jnp.broadcast_to(b_ref[...], (rows, LANES))
            for t in range(CONV_TAPS):
                start = c * rows + CONV_PAD - (CONV_TAPS - 1) + t
                acc = acc + wv[t:t + 1, :] * pad_ref[start:start + rows, :]
            y_ref[c * rows:(c + 1) * rows, :] = acc

    col = lambda j0: pl.BlockSpec((s, LANES), lambda c: (0, j0 + c))
    return pl.pallas_call(
        body, name=name, grid=(n_ch // LANES,),
        in_specs=[col(a_col), col(g_col), pl.BlockSpec((CONV_PAD, LANES), lambda c: (0, c)), pl.BlockSpec((1, LANES), lambda c: (0, c))],
        out_specs=pl.BlockSpec((s, LANES), lambda c: (0, c)),
        out_shape=jax.ShapeDtypeStruct((s, n_ch), F32),
        scratch_shapes=[pltpu.VMEM((s + CONV_PAD, LANES), F32)],
        compiler_params=_params("parallel"),
    )(proj, proj, w32, bias)


def _conv_bwd(proj, w32, dy, n_ch, a_col, g_col, name):
    s = proj.shape[0]
    rows = min(CONV_ROWS, s)
    sub = 8

    def fold(x):
        acc = x[0:sub, :]
        for r in range(1, rows // sub):
            acc = acc + x[r * sub:(r + 1) * sub, :]
        return acc

    def body(a_ref, g_ref, w_ref, dy_ref, da_ref, dg_ref, dw_ref, padh_ref, padd_ref):
        sg = _sigmoid(g_ref[...])
        padh_ref[0:CONV_PAD, :] = jnp.zeros((CONV_PAD, LANES), F32)
        padh_ref[CONV_PAD:CONV_PAD + s, :] = a_ref[...] * sg
        padd_ref[0:s, :] = dy_ref[...]
        padd_ref[s:s + CONV_PAD, :] = jnp.zeros((CONV_PAD, LANES), F32)
        wv = w_ref[...]
        dw = [jnp.zeros((sub, LANES), F32) for _ in range(CONV_TAPS + 1)]
        for c in range(s // rows):
            r0 = c * rows
            acc = jnp.zeros((rows, LANES), F32)
            dyc = dy_ref[r0:r0 + rows, :]
            for t in range(CONV_TAPS):
                back = r0 + (CONV_TAPS - 1) - t
                acc = acc + wv[t:t + 1, :] * padd_ref[back:back + rows, :]
                start = r0 + CONV_PAD - (CONV_TAPS - 1) + t
                dw[t] = dw[t] + fold(dyc * padh_ref[start:start + rows, :])
            dw[CONV_TAPS] = dw[CONV_TAPS] + fold(dyc)
            av = a_ref[r0:r0 + rows, :]
            sgc = _sigmoid(g_ref[r0:r0 + rows, :])
            da_ref[r0:r0 + rows, :] = (acc * sgc).astype(BF16)
            dg_ref[r0:r0 + rows, :] = (acc * av * sgc * (1.0 - sgc)).astype(BF16)
        for t in range(CONV_TAPS + 1):
            dw_ref[t:t + 1, :] = jnp.sum(dw[t], axis=0, keepdims=True)

    col = lambda j0: pl.BlockSpec((s, LANES), lambda c: (0, j0 + c))
    wspec = pl.BlockSpec((CONV_PAD, LANES), lambda c: (0, c))
    return pl.pallas_call(
        body, name=name, grid=(n_ch // LANES,),
        in_specs=[col(a_col), col(g_col), wspec, col(0)],
        out_specs=[col(0), col(0), wspec],
        out_shape=[jax.ShapeDtypeStruct((s, n_ch), BF16), jax.ShapeDtypeStruct((s, n_ch), BF16),
                   jax.ShapeDtypeStruct((CONV_PAD, n_ch), F32)],
        scratch_shapes=[pltpu.VMEM((s + CONV_PAD, LANES), F32), pltpu.VMEM((s + CONV_PAD, LANES), F32)],
        compiler_params=_params("parallel"),
    )(proj, proj, w32, dy)


def _my_place():
    return lax.axis_index("x"), lax.axis_index("y"), lax.axis_index("c")


def _flip(place, k):
    x, y, c = place
    return (1 - x if k & 4 else x, 1 - y if k & 2 else y, 1 - c if k & 1 else c)


def _dev_id(place):
    return 4 * place[0] + 2 * place[1] + place[2]


def _wait_all(ref, send_sem, recv_sem, place):
    pltpu.make_async_remote_copy(src_ref=ref, dst_ref=ref, send_sem=send_sem, recv_sem=recv_sem,
                                 device_id=place, device_id_type=MESH).wait()


def _window(kind, ref, dev, n):
    if kind == "slot":
        return ref.at[dev]
    if kind == "rows":
        return ref.at[pl.ds(pl.multiple_of(dev * n, n), n), :]
    return ref.at[:, pl.ds(pl.multiple_of(dev * n, n), n)]


def _hbm(x):
    return pltpu.with_memory_space_constraint(x, pltpu.HBM)


_EFFECT = pltpu.SideEffectType.DATAFLOW_SIDE_EFFECTING


def _exchange_start(srcs, lands, specs, groups, name):
    n = len(srcs)
    n_g = len(groups)

    def body(*refs):
        src_refs, land_refs = refs[:n], refs[n:2 * n]
        sems = refs[2 * n:2 * n + 2 * n_g]
        token = refs[-1]
        place = _my_place()
        me = _dev_id(place)
        for g, units in enumerate(groups):
            for j, u in enumerate(units):
                mode, kind, cnt = specs[u]
                for k in range(N_DEV):
                    peer = _flip(place, k)
                    if mode == "gather":
                        src, dst = src_refs[u], _window(kind, land_refs[u], me, cnt)
                    else:
                        src, dst = _window(kind, src_refs[u], _dev_id(peer), cnt), land_refs[u].at[k]
                    pltpu.make_async_remote_copy(src_ref=src, dst_ref=dst, send_sem=sems[2 * g].at[j], recv_sem=sems[2 * g + 1].at[j],
                                                 device_id=peer, device_id_type=MESH).start()
        token[...] = jnp.zeros_like(token)

    hbm = pl.BlockSpec(memory_space=pltpu.HBM)
    sem = pl.BlockSpec(memory_space=pltpu.SEMAPHORE)
    out_shape = [pltpu.SemaphoreType.DMA((len(units),)) for units in groups for _ in range(2)]
    out_shape += [pltpu.HBM(x.shape, x.dtype) for x in lands] + [jax.ShapeDtypeStruct((8, LANES), F32)]
    outs = pl.pallas_call(
        body, name=name, out_shape=out_shape,
        in_specs=[hbm] * (2 * n), out_specs=[sem] * (2 * n_g) + [hbm] * n + [pl.BlockSpec(memory_space=pltpu.VMEM)],
        input_output_aliases={n + u: 2 * n_g + u for u in range(n)},
        compiler_params=pltpu.CompilerParams(has_side_effects=_EFFECT),
    )(*[_hbm(x) for x in srcs], *[_hbm(x) for x in lands])
    sem_pairs = [(outs[2 * g], outs[2 * g + 1]) for g in range(n_g)]
    return sem_pairs, list(outs[2 * n_g:2 * n_g + n]), outs[-1]


def _exchange_wait(srcs, lands, specs, sem_pair, after, name):
    n = len(lands)

    def body(*refs):
        land_refs = refs[n:2 * n]
        send_sems, recv_sems = refs[2 * n], refs[2 * n + 1]
        place = _my_place()
        for u in range(n):
            _wait_all(land_refs[u], send_sems.at[u], recv_sems.at[u], place)

    hbm = pl.BlockSpec(memory_space=pltpu.HBM)
    sem = pl.BlockSpec(memory_space=pltpu.SEMAPHORE)
    outs = pl.pallas_call(
        body, name=name, out_shape=[pltpu.HBM(x.shape, x.dtype) for x in lands],
        in_specs=[hbm] * (2 * n) + [sem, sem, pl.BlockSpec(memory_space=pl.ANY)], out_specs=[hbm] * n,
        input_output_aliases={n + u: u for u in range(n)},
        compiler_params=pltpu.CompilerParams(has_side_effects=_EFFECT),
    )(*[_hbm(x) for x in srcs], *lands, sem_pair[0], sem_pair[1], after)
    return list(outs)


def _sum_devices(parts, name):
    _, r, w = parts.shape

    def body(p_ref, out_ref):
        acc = p_ref[0]
        for d in range(1, N_DEV):
            acc = acc + p_ref[d]
        out_ref[...] = acc

    return pl.pallas_call(
        body, name=name, out_shape=jax.ShapeDtypeStruct((r, w), F32),
        in_specs=[pl.BlockSpec(memory_space=pltpu.VMEM)], out_specs=pl.BlockSpec(memory_space=pltpu.VMEM),
    )(parts)


def _adamw_math(w, m, v, g):
    m_new = ADAM_B1 * m + (1.0 - ADAM_B1) * g
    v_new = ADAM_B2 * v + (1.0 - ADAM_B2) * (g * g)
    m_hat = m_new / (1.0 - ADAM_B1 ** ADAM_STEP)
    v_hat = v_new / (1.0 - ADAM_B2 ** ADAM_STEP)
    return -ADAM_LR * (m_hat / (jnp.sqrt(v_hat) + ADAM_EPS) + ADAM_WD * w), m_new, v_new


def _adamw(w, m, v, g, name):
    rows = w.shape[0]
    tr = min(FLAT_ROWS, rows)
    assert rows % tr == 0, (name, rows)

    def body(w_ref, m_ref, v_ref, g_ref, d_out, m_out, v_out):
        d_out[...], m_out[...], v_out[...] = _adamw_math(w_ref[...], m_ref[...], v_ref[...], g_ref[...])

    flat = pl.BlockSpec((tr, LANES), lambda i: (i, 0))
    return pl.pallas_call(
        body, name=name, grid=(rows // tr,), in_specs=[flat] * 4, out_specs=[flat] * 3,
        out_shape=[jax.ShapeDtypeStruct((rows, LANES), F32)] * 3,
        compiler_params=_params("parallel"),
    )(w, m, v, g)


def _adamw_shard(w, m, v, recv, layer, prev, name):
    depth, a, b = w.shape
    ta = min(256, a)
    assert a % ta == 0

    def body(w_ref, m_ref, v_ref, r_ref, *rest):
        g_out, d_out, m_out, v_out = rest[-4:]
        g = r_ref[0].astype(F32)
        for k in range(1, N_DEV):
            g = g + r_ref[k].astype(F32)
        g_out[0] = g
        d_out[0], m_out[0], v_out[0] = _adamw_math(w_ref[0], m_ref[0], v_ref[0], g)

    lay = pl.BlockSpec((1, ta, b), lambda i: (layer, i, 0))
    in_specs = [lay] * 3 + [pl.BlockSpec((N_DEV, ta, b), lambda i: (0, i, 0))]
    args = [w, m, v, recv]
    aliases = {}
    if prev is not None:
        in_specs += [pl.BlockSpec(memory_space=pl.ANY)] * 4
        args += list(prev)
        aliases = {4 + i: i for i in range(4)}
    return pl.pallas_call(
        body, name=name, grid=(a // ta,), in_specs=in_specs, out_specs=[lay] * 4,
        out_shape=[jax.ShapeDtypeStruct(w.shape, F32)] * 4, input_output_aliases=aliases,
        compiler_params=_params("parallel"),
    )(*args)


def _round_up(n, mult):
    return (n + mult - 1) // mult * mult


def _flatten(parts, row_mult):
    flat = jnp.concatenate([p.reshape(-1) for p in parts])
    rows = _round_up(-(-flat.shape[0] // LANES), row_mult)
    return jnp.pad(flat, (0, rows * LANES - flat.shape[0])).reshape(rows, LANES)


def _unflatten(flat, shapes):
    flat = flat.reshape(-1)
    out, off = [], 0
    for shp in shapes:
        n = 1
        for dim in shp:
            n *= dim
        out.append(flat[off:off + n].reshape(shp))
        off += n
    return out


def kernel(x, norm1_g, w_in, b_f, q_norm_g, k_norm_g, conv_w, conv_b, conv_ln_g, conv_ln_b, w_o, norm2_g, w_mlp_in, w_mlp_out, loss_target, m_norm1_g, m_w_in, m_b_f, m_q_norm_g, m_k_norm_g, m_conv_w, m_conv_b, m_conv_ln_g, m_conv_ln_b, m_w_o, m_norm2_g, m_w_mlp_in, m_w_mlp_out, v_norm1_g, v_w_in, v_b_f, v_q_norm_g, v_k_norm_g, v_conv_w, v_conv_b, v_conv_ln_g, v_conv_ln_b, v_w_o, v_norm2_g, v_w_mlp_in, v_w_mlp_out):
    depth, d_model, n_in_loc = w_in.shape
    seq = x.shape[1]
    n_heads = b_f.shape[1]
    aw = n_heads * HEAD_DIM
    cc = conv_b.shape[1]
    n_in = n_in_loc * N_DEV
    o_f = 3 * aw
    n_all = 3 * aw + 2 * cc + LANES
    assert n_in == 3 * aw + n_heads + 2 * cc and aw + cc == d_model and n_heads % 2 == 0
    assert aw % LANES == 0 and cc % LANES == 0 and x.shape[0] == 1
    me = 4 * lax.axis_index("x") + 2 * lax.axis_index("y") + lax.axis_index("c")

    d_ff = w_mlp_in.shape[2] * N_DEV

    r_o, f_1, f_2 = w_o.shape[1], w_mlp_in.shape[2], w_mlp_out.shape[1]

    ag_src, ag_land, ag_spec = [], [], []
    for l in range(depth):
        ag_src += [w_in[l].astype(BF16), w_o[l].astype(BF16), w_mlp_in[l].astype(BF16), w_mlp_out[l].astype(BF16)]
        ag_land += [(N_DEV, d_model, n_in_loc), (N_DEV * r_o, d_model), (d_model, N_DEV * f_1), (N_DEV * f_2, d_model)]
        ag_spec += [("gather", "slot", 1), ("gather", "rows", r_o), ("gather", "cols", f_1), ("gather", "rows", f_2)]
    ag_src.append(jnp.stack(_split3(conv_w)))
    ag_land.append((N_DEV, 3) + conv_w.shape)
    ag_spec.append(("gather", "slot", 1))
    ag_groups = [grp for l in range(depth) for grp in ([4 * l] + ([4 * depth] if l == 0 else []), [4 * l + 1], [4 * l + 2, 4 * l + 3])]
    ag_land = [lax.empty(shp, BF16) for shp in ag_land]
    ag_sems, ag_land, ag_token = _exchange_start(ag_src, ag_land, ag_spec, ag_groups, "gather_start")

    def gathered(g, after):
        units = ag_groups[g]
        return _exchange_wait([ag_src[u] for u in units], [ag_land[u] for u in units], [ag_spec[u] for u in units],
                              ag_sems[g], after, f"gather_wait_{g}")

    def to_all(w):
        return jnp.concatenate([w[:, :o_f], w[:, o_f + n_heads:], w[:, o_f:o_f + n_heads],
                                jnp.zeros((w.shape[0], LANES - n_heads), w.dtype)], axis=1)

    def from_all(w):
        return jnp.concatenate([w[:, :o_f], w[:, n_all - LANES:n_all - LANES + n_heads], w[:, o_f:n_all - LANES]], axis=1)

    def whole_in(lin):
        return to_all(jnp.moveaxis(lin, 0, 1).reshape(d_model, n_in))

    def row(p, l, width=None):
        v = p[l].reshape(1, -1)
        return v if width is None else jnp.pad(v, ((0, 0), (0, width - v.shape[1])))

    a_col, g_col = 3 * aw // LANES, (3 * aw + cc) // LANES

    gq = [jnp.tile(row(q_norm_g, l), (1, n_heads)) for l in range(depth)]
    gk = [jnp.tile(row(k_norm_g, l), (1, n_heads)) for l in range(depth)]
    bfp = [row(b_f, l, LANES) for l in range(depth)]
    add_res = lambda acc, res: (acc + res,)
    w_all, w_out, w_ff1, w_ff2 = [None] * depth, [None] * depth, [None] * depth, [None] * depth

    h = x[0]
    saved = []
    for l in range(depth):
        if l == 0:
            lin, lc = gathered(0, h)
            lc = lc.astype(F32)
            conv_full = jnp.moveaxis(lc[:, 0] + lc[:, 1] + lc[:, 2], 0, 2).reshape(depth, CONV_TAPS, cc)
            w32 = [jnp.pad(conv_full[i], ((0, CONV_PAD - CONV_TAPS), (0, 0))) for i in range(depth)]
        else:
            lin, = gathered(3 * l, h)
        w_all[l] = whole_in(lin)
        u1, proj = _rms_matmul(h, row(norm1_g, l), w_all[l], out_dtypes=(F32,), name=f"mm_in_{l}")
        qa, ka, vb = _prep_fwd(proj, gq[l], gk[l], bfp[l], n_heads, f"prep_fwd_{l}")
        att, lse, mixed = _attn_fwd(qa, ka, vb, n_heads, aw + cc, f"attn_fwd_{l}")
        yc = _conv_fwd(proj, w32[l], row(conv_b, l), cc, a_col, g_col, f"conv_fwd_{l}")
        mixed = _ln_silu_fwd(yc, row(conv_ln_g, l), row(conv_ln_b, l), mixed, f"ln_silu_fwd_{l}")
        w_out[l], = gathered(3 * l + 1, mixed)
        x1 = _matmul(mixed, w_out[l], mode="nn", out_dtypes=(F32,), name=f"mm_o_{l}", epilogue=add_res, extras=(h,))
        w_ff1[l], w_ff2[l] = gathered(3 * l + 2, x1)
        u2, r, a = _rms_matmul(x1, row(norm2_g, l), w_ff1[l], out_dtypes=(BF16, BF16), name=f"mm_ff1_{l}", tm=256,
                               epilogue=lambda acc: (jnp.maximum(acc, 0.0), jnp.square(jnp.maximum(acc, 0.0))))
        if l < depth - 1:
            x2 = _matmul(a, w_ff2[l], mode="nn", out_dtypes=(F32,), name=f"mm_ff2_{l}", epilogue=add_res, extras=(x1,), tk=d_ff)
        else:
            x2 = None
            dh, dh16, sq = _matmul_loss(a, w_ff2[l], x1, loss_target[0], f"mm_ff2_loss_{l}")
        saved.append(dict(x_in=h, u1=u1, proj=proj, qa=qa, ka=ka, vb=vb, att=att, lse=lse, yc=yc, mixed=mixed,
                          x1=x1, u2=u2, r=r, a=a))
        h = x2

    loss = lax.psum(0.5 * jnp.sum(sq) / d_model, ("x", "y", "c"))

    g_in, g_o, g_1, g_2 = [None] * depth, [None] * depth, [None] * depth, [None] * depth
    gs = {n: [None] * depth for n in ("norm1", "bf", "qn", "kn", "convw", "convb", "lng", "lnb", "norm2")}
    scattering = {}

    def scatter_start(stage, l, srcs, specs, slabs):
        lands = [lax.empty((N_DEV,) + shp, BF16) for shp in slabs]
        sems, lands, token = _exchange_start(srcs, lands, specs, [list(range(len(srcs)))], f"scatter_start_{stage}_{l}")
        scattering[(stage, l)] = (srcs, lands, specs, sems[0])
        return token[0, 0]

    for l in reversed(range(depth)):
        sv = saved[l]
        dh1 = _matmul(dh16, w_ff2[l], mode="nt", out_dtypes=(BF16,), name=f"mm_dff2_{l}", tm=256, tn=d_ff,
                      epilogue=lambda acc, rr: (acc * (2.0 * rr.astype(F32)),), extras=(sv["r"],))
        g_2[l] = _matmul(sv["a"], dh16, mode="tn", out_dtypes=(BF16,), name=f"mm_dw2_{l}", tm=1024, tk=seq)
        g_1[l] = _matmul(sv["u2"], dh1, mode="tn", out_dtypes=(BF16,), name=f"mm_dw1_{l}", tm=d_model, tk=seq)
        tok = scatter_start("ff", l, [g_1[l], g_2[l]], [("scatter", "cols", f_1), ("scatter", "rows", f_2)],
                            [(d_model, f_1), (f_2, d_model)])
        dx1, dx16, gs["norm2"][l] = _matmul_rms_bwd(dh1, w_ff1[l], sv["x1"], row(norm2_g, l) + tok, dh, f"mm_du2_rms_{l}")

        dmixed = _matmul(dx16, w_out[l], mode="nt", out_dtypes=(F32,), name=f"mm_dmixed_{l}")
        g_o[l] = _matmul(sv["mixed"], dx16, mode="tn", out_dtypes=(BF16,), name=f"mm_dwo_{l}", tm=1024, tk=seq // 2)
        tok = scatter_start("o", l, [g_o[l]], [("scatter", "rows", r_o)], [(r_o, d_model)])
        dyc, gs["lng"][l], gs["lnb"][l] = _ln_silu_bwd(sv["yc"], row(conv_ln_g, l) + tok, row(conv_ln_b, l), dmixed, f"ln_silu_bwd_{l}")
        dpa, dpg, dw32 = _conv_bwd(sv["proj"], w32[l], dyc, cc, a_col, g_col, f"conv_bwd_{l}")
        gs["convw"][l], gs["convb"][l] = dw32[:CONV_TAPS], dw32[CONV_TAPS:CONV_TAPS + 1]
        dq, dka, dv, drow, dcol = _attn_bwd(sv["qa"], sv["ka"], sv["vb"], sv["att"], sv["lse"], dmixed, n_heads, f"attn_bwd_{l}")
        dpq, dpk, dpf, dgq, dgk, dbf = _prep_bwd(sv["proj"], dq, dka, drow, dcol, gq[l], gk[l], bfp[l], n_heads, f"prep_bwd_{l}")
        gs["qn"][l] = dgq.reshape(n_heads, HEAD_DIM).sum(axis=0)
        gs["kn"][l] = dgk.reshape(n_heads, HEAD_DIM).sum(axis=0)
        gs["bf"][l] = dbf[0, :n_heads]
        dproj = jnp.concatenate([dpq, dpk, dv.astype(BF16), dpa, dpg, dpf], axis=1)
        dwall = _matmul(sv["u1"], dproj, mode="tn", out_dtypes=(BF16,), name=f"mm_dwall_{l}", tm=d_model, tn=n_all // 3, tk=seq)
        g_in[l] = jnp.moveaxis(from_all(dwall).reshape(d_model, N_DEV, n_in_loc), 1, 0)
        tok = scatter_start("in", l, [g_in[l]], [("scatter", "slot", 1)], [(d_model, n_in_loc)])
        dh, dh16, gs["norm1"][l] = _matmul_rms_bwd(dproj, w_all[l], sv["x_in"], row(norm1_g, l) + tok, dx1, f"mm_du1_rms_{l}")
    grad_x = dh[None]

    small_g = [jnp.stack(gs[n]).reshape(shp) for n, shp in (
        ("norm1", norm1_g.shape), ("bf", b_f.shape), ("qn", q_norm_g.shape), ("kn", k_norm_g.shape),
        ("convw", (depth, CONV_TAPS, cc)), ("convb", conv_b.shape), ("lng", conv_ln_g.shape), ("lnb", conv_ln_b.shape),
        ("norm2", norm2_g.shape))]
    small_shapes = [g.shape for g in small_g]
    small_flat = _flatten(small_g, 8)
    small_spec = [("gather", "slot", 1)]
    small_sems, small_land, _ = _exchange_start([small_flat], [lax.empty((N_DEV,) + small_flat.shape, F32)], small_spec, [[0]],
                                                "small_grads_start")

    def landed(stage, l, after):
        srcs, lands, specs, sems = scattering[(stage, l)]
        return _exchange_wait(srcs, lands, specs, sems, after, f"scatter_wait_{stage}_{l}")

    def adamw_layers(kd, w, m, v, recv):
        outs = None
        for l in reversed(range(depth)):
            outs = _adamw_shard(w, m, v, recv[l], l, outs, f"adamw_{kd}_{l}")
        return outs

    recv_1, recv_2, recv_in, recv_o = [None] * depth, [None] * depth, [None] * depth, [None] * depth
    for l in reversed(range(depth)):
        recv_1[l], recv_2[l] = landed("ff", l, dh)
    out_1 = adamw_layers("1", w_mlp_in, m_w_mlp_in, v_w_mlp_in, recv_1)
    out_2 = adamw_layers("2", w_mlp_out, m_w_mlp_out, v_w_mlp_out, recv_2)
    for l in reversed(range(depth)):
        recv_o[l], = landed("o", l, out_2[1])
    out_o = adamw_layers("o", w_o, m_w_o, v_w_o, recv_o)
    for l in reversed(range(depth)):
        recv_in[l], = landed("in", l, out_o[1])
    out_in = adamw_layers("in", w_in, m_w_in, v_w_in, recv_in)
    big_out = [[outs[kind] for outs in (out_in, out_o, out_1, out_2)] for kind in range(4)]

    small_parts, = _exchange_wait([small_flat], small_land, small_spec, small_sems[0], out_in[1], "small_grads_wait")
    small_g = _unflatten(_sum_devices(small_parts, "small_grads_sum"), small_shapes)
    cw = conv_w.shape[2]
    small_g[4] = lax.dynamic_slice_in_dim(small_g[4], me * cw, cw, axis=2)
    small = (norm1_g, b_f, q_norm_g, k_norm_g, conv_w, conv_b, conv_ln_g, conv_ln_b, norm2_g)
    small_m = (m_norm1_g, m_b_f, m_q_norm_g, m_k_norm_g, m_conv_w, m_conv_b, m_conv_ln_g, m_conv_ln_b, m_norm2_g)
    small_v = (v_norm1_g, v_b_f, v_q_norm_g, v_k_norm_g, v_conv_w, v_conv_b, v_conv_ln_g, v_conv_ln_b, v_norm2_g)
    small_out = _adamw(_flatten(small, 8), _flatten(small_m, 8), _flatten(small_v, 8), _flatten(small_g, 8), "adamw_small")
    small_out = [small_g] + [_unflatten(o, [w.shape for w in small]) for o in small_out]

    def group(kind):
        s_, b_ = small_out[kind], big_out[kind]
        return [s_[0], b_[0], s_[1], s_[2], s_[3], s_[4], s_[5], s_[6], s_[7], b_[1], s_[8], b_[2], b_[3]]

    return (loss, grad_x, *group(0), *group(1), *group(2), *group(3))
```

```python
import functools

import jax
import jax.numpy as jnp
from jax import lax
from jax.experimental import pallas as pl
from jax.experimental.pallas import tpu as pltpu

F32 = jnp.float32
BF16 = jnp.bfloat16

EPS = 1e-6
HEAD_DIM = 64
LANES = 128
PAIR = 2 * LANES
N_DEV = 8
CONV_TAPS = 31
CONV_PAD = 32
NEG = -1e30

ADAM_LR = 0.001
ADAM_B1 = 0.9
ADAM_B2 = 0.999
ADAM_EPS = 1e-08
ADAM_WD = 0.01
ADAM_STEP = 10

TM = 512
TQ = 512
CONV_ROWS = 128
FLAT_ROWS = 1024
MESH = pl.DeviceIdType.MESH


def _params(*sem):
    return pltpu.CompilerParams(dimension_semantics=sem, vmem_limit_bytes=56 * 1024 * 1024)


def _split3(x):
    hi = x.astype(BF16)
    r1 = x - hi.astype(F32)
    mid = r1.astype(BF16)
    lo = (r1 - mid.astype(F32)).astype(BF16)
    return hi, mid, lo


def _dot(a, b):
    return jnp.dot(a, b, preferred_element_type=F32)


def _dot_nt(a, b):
    return lax.dot_general(a, b, (((1,), (1,)), ((), ())), preferred_element_type=F32)


def _dot_tn(a, b):
    return lax.dot_general(a, b, (((0,), (0,)), ((), ())), preferred_element_type=F32)


def _dot3(x, mat):
    hi, mid, lo = _split3(x)
    return _dot(hi, mat) + _dot(mid, mat) + _dot(lo, mat)


def _dot2(x, mat):
    hi = x.astype(BF16)
    lo = (x - hi.astype(F32)).astype(BF16)
    return _dot(hi, mat) + _dot(lo, mat)


def _dot3_r(mat, x):
    hi, mid, lo = _split3(x)
    return _dot(mat, hi) + _dot(mat, mid) + _dot(mat, lo)


def _iota(shape, dim):
    return lax.broadcasted_iota(jnp.int32, shape, dim)


def _sigmoid(x):
    return 1.0 / (1.0 + jnp.exp(-x))


def _matmul(a, b, *, mode, out_dtypes, name, epilogue=None, extras=(), tm=TM, tn=1024, tk=1024):
    if mode == "nn":
        (m, k), (k2, n) = a.shape, b.shape
    elif mode == "nt":
        (m, k), (n, k2) = a.shape, b.shape
    else:
        (k, m), (k2, n) = a.shape, b.shape
    assert k == k2, (name, a.shape, b.shape)
    tm, tn, tk = min(tm, m), min(tn, n), min(tk, k)
    assert m % tm == 0 and n % tn == 0 and k % tk == 0, (name, m, n, k, tm, tn, tk)
    nk = k // tk
    a_mode = dict(pipeline_mode=pl.Buffered(1)) if (m == tm and nk == 1) else {}
    b_mode = dict(pipeline_mode=pl.Buffered(1)) if (n == tn and nk == 1) else {}
    if mode == "tn":
        a_spec = pl.BlockSpec((tk, tm), lambda i, j, kk: (kk, i), **a_mode)
    else:
        a_spec = pl.BlockSpec((tm, tk), lambda i, j, kk: (i, kk), **a_mode)
    if mode == "nt":
        b_spec = pl.BlockSpec((tn, tk), lambda i, j, kk: (j, kk), **b_mode)
    else:
        b_spec = pl.BlockSpec((tk, tn), lambda i, j, kk: (kk, j), **b_mode)
    dot = {"nn": _dot, "nt": _dot_nt, "tn": _dot_tn}[mode]
    tile = pl.BlockSpec((tm, tn), lambda i, j, kk: (i, j))
    n_ex, n_out = len(extras), len(out_dtypes)
    acc_in_out = nk > 1 and epilogue is None and out_dtypes[0] == F32

    def body(a_ref, b_ref, *rest):
        ex_refs, out_refs = rest[:n_ex], rest[n_ex:n_ex + n_out]
        part = dot(a_ref[...].astype(BF16), b_ref[...].astype(BF16))

        def finish(acc):
            res = epilogue(acc, *[e[...] for e in ex_refs]) if epilogue is not None else (acc,) * n_out
            for o_ref, r in zip(out_refs, res):
                o_ref[...] = r.astype(o_ref.dtype)

        if nk == 1:
            finish(part)
        else:
            acc_ref = out_refs[0] if acc_in_out else rest[-1]
            kk = pl.program_id(2)

            @pl.when(kk == 0)
            def _():
                acc_ref[...] = part

            @pl.when(kk > 0)
            def _():
                acc_ref[...] += part

            @pl.when(kk == nk - 1)
            def _():
                if acc_in_out:
                    for o_ref in out_refs[1:]:
                        o_ref[...] = acc_ref[...].astype(o_ref.dtype)
                else:
                    finish(acc_ref[...])

    outs = pl.pallas_call(
        body,
        name=name,
        grid=(m // tm, n // tn, nk),
        in_specs=[a_spec, b_spec] + [tile] * n_ex,
        out_specs=[tile] * n_out,
        out_shape=[jax.ShapeDtypeStruct((m, n), dt) for dt in out_dtypes],
        scratch_shapes=[pltpu.VMEM((tm, tn), F32)] if nk > 1 and not acc_in_out else [],
        compiler_params=_params("parallel", "parallel", "arbitrary"),
    )(a, b, *extras)
    return outs if n_out > 1 else outs[0]


def _rms_matmul(x, g, b, *, out_dtypes, name, epilogue=None, tm=TM):
    s, d = x.shape
    n = b.shape[1]
    ts = min(tm, s)
    n_out = len(out_dtypes)

    def body(x_ref, g_ref, b_ref, u_ref, *out_refs):
        xv = x_ref[...]
        u = (xv * lax.rsqrt(jnp.mean(xv * xv, axis=-1, keepdims=True) + EPS) * g_ref[...]).astype(BF16)
        u_ref[...] = u
        acc = _dot(u, b_ref[...])
        res = epilogue(acc) if epilogue is not None else (acc,)
        for o_ref, r in zip(out_refs, res):
            o_ref[...] = r.astype(o_ref.dtype)

    row = lambda w: pl.BlockSpec((ts, w), lambda i: (i, 0))
    return pl.pallas_call(
        body, name=name, grid=(s // ts,),
        in_specs=[row(d), pl.BlockSpec((1, d), lambda i: (0, 0)), pl.BlockSpec((d, n), lambda i: (0, 0), pipeline_mode=pl.Buffered(1))],
        out_specs=[row(d)] + [row(n)] * n_out,
        out_shape=[jax.ShapeDtypeStruct((s, d), BF16)] + [jax.ShapeDtypeStruct((s, n), dt) for dt in out_dtypes],
        compiler_params=_params("parallel"),
    )(x, g, b)


def _matmul_rms_bwd(a, b, x, g, dres, name):
    s, k = a.shape
    d = b.shape[0]
    ts = min(TM, s)

    def body(a_ref, b_ref, x_ref, g_ref, dres_ref, dx_ref, dx16_ref, dg_ref):
        @pl.when(pl.program_id(0) == 0)
        def _():
            dg_ref[...] = jnp.zeros_like(dg_ref)

        duv = _dot_nt(a_ref[...], b_ref[...])
        xv = x_ref[...]
        r = lax.rsqrt(jnp.mean(xv * xv, axis=-1, keepdims=True) + EPS)
        xh = xv * r
        dxh = duv * g_ref[...]
        dx = dres_ref[...] + r * (dxh - xh * jnp.mean(dxh * xh, axis=-1, keepdims=True))
        dx_ref[...] = dx
        dx16_ref[...] = dx.astype(BF16)
        dg_ref[...] += jnp.sum(duv * xh, axis=0, keepdims=True)

    row = pl.BlockSpec((ts, d), lambda i: (i, 0))
    vec = pl.BlockSpec((1, d), lambda i: (0, 0))
    return pl.pallas_call(
        body, name=name, grid=(s // ts,),
        in_specs=[pl.BlockSpec((ts, k), lambda i: (i, 0)), pl.BlockSpec((d, k), lambda i: (0, 0), pipeline_mode=pl.Buffered(1)),
                  row, vec, row],
        out_specs=[row, row, vec],
        out_shape=[jax.ShapeDtypeStruct((s, d), F32), jax.ShapeDtypeStruct((s, d), BF16), jax.ShapeDtypeStruct((1, d), F32)],
        compiler_params=_params("arbitrary"),
    )(a, b, x, g, dres)


def _matmul_loss(a, b, res, target, name):
    s, k = a.shape
    d = b.shape[1]
    ts = min(TM, s)

    def body(a_ref, b_ref, res_ref, t_ref, dy_ref, dy16_ref, sq_ref):
        @pl.when(pl.program_id(0) == 0)
        def _():
            sq_ref[...] = jnp.zeros_like(sq_ref)

        err = _dot(a_ref[...], b_ref[...]) + res_ref[...] - t_ref[...]
        dy = err * (1.0 / d)
        dy_ref[...] = dy
        dy16_ref[...] = dy.astype(BF16)
        sq_ref[...] += jnp.sum(err * err, axis=0, keepdims=True)

    row = pl.BlockSpec((ts, d), lambda i: (i, 0))
    vec = pl.BlockSpec((1, d), lambda i: (0, 0))
    return pl.pallas_call(
        body, name=name, grid=(s // ts,),
        in_specs=[pl.BlockSpec((ts, k), lambda i: (i, 0)), pl.BlockSpec((k, d), lambda i: (0, 0), pipeline_mode=pl.Buffered(1)),
                  row, row],
        out_specs=[row, row, vec],
        out_shape=[jax.ShapeDtypeStruct((s, d), F32), jax.ShapeDtypeStruct((s, d), BF16), jax.ShapeDtypeStruct((1, d), F32)],
        compiler_params=_params("arbitrary"),
    )(a, b, res, target)


def _ln_silu_fwd(y, g, b, mixed, name):
    s, c = y.shape
    ts = min(TM, s)
    assert mixed.shape == (s, 2 * c)

    def body(y_ref, g_ref, b_ref, mixed_ref, h_ref):
        yv = y_ref[...]
        mu = jnp.mean(yv, axis=-1, keepdims=True)
        yc = yv - mu
        z = yc * lax.rsqrt(jnp.mean(yc * yc, axis=-1, keepdims=True) + EPS) * g_ref[...] + b_ref[...]
        h_ref[...] = (z * _sigmoid(z)).astype(BF16)

    row = pl.BlockSpec((ts, c), lambda i: (i, 0))
    vec = pl.BlockSpec((1, c), lambda i: (0, 0))
    return pl.pallas_call(
        body, name=name, grid=(s // ts,), in_specs=[row, vec, vec, pl.BlockSpec(memory_space=pl.ANY)],
        out_specs=pl.BlockSpec((ts, c), lambda i: (i, 1)),
        out_shape=jax.ShapeDtypeStruct((s, 2 * c), BF16), input_output_aliases={3: 0},
        compiler_params=_params("parallel"),
    )(y, g, b, mixed)


def _ln_silu_bwd(y, g, b, dmixed, name):
    s, c = y.shape
    ts = min(TM, s)

    def body(y_ref, g_ref, b_ref, dh_ref, dy_ref, dg_ref, db_ref):
        @pl.when(pl.program_id(0) == 0)
        def _():
            dg_ref[...] = jnp.zeros_like(dg_ref)
            db_ref[...] = jnp.zeros_like(db_ref)

        yv = y_ref[...]
        mu = jnp.mean(yv, axis=-1, keepdims=True)
        yc = yv - mu
        r = lax.rsqrt(jnp.mean(yc * yc, axis=-1, keepdims=True) + EPS)
        yh = yc * r
        z = yh * g_ref[...] + b_ref[...]
        sg = _sigmoid(z)
        dz = dh_ref[...] * (sg * (1.0 + z * (1.0 - sg)))
        dg_ref[...] += jnp.sum(dz * yh, axis=0, keepdims=True)
        db_ref[...] += jnp.sum(dz, axis=0, keepdims=True)
        dyh = dz * g_ref[...]
        dy_ref[...] = r * (dyh - jnp.mean(dyh, axis=-1, keepdims=True) - yh * jnp.mean(dyh * yh, axis=-1, keepdims=True))

    row = pl.BlockSpec((ts, c), lambda i: (i, 0))
    vec = pl.BlockSpec((1, c), lambda i: (0, 0))
    return pl.pallas_call(
        body, name=name, grid=(s // ts,),
        in_specs=[row, vec, vec, pl.BlockSpec((ts, c), lambda i: (i, 1))], out_specs=[row, vec, vec],
        out_shape=[jax.ShapeDtypeStruct((s, c), F32), jax.ShapeDtypeStruct((1, c), F32), jax.ShapeDtypeStruct((1, c), F32)],
        compiler_params=_params("arbitrary"),
    )(y, g, b, dmixed)


def _head_masks():
    lane2 = _iota((1, PAIR), 1)
    lane1 = _iota((1, LANES), 1)
    qa = (lane2 < HEAD_DIM) | ((lane2 >= LANES) & (lane2 < LANES + 3))
    qb = ((lane2 >= HEAD_DIM) & (lane2 < LANES)) | ((lane2 >= LANES + 3) & (lane2 < LANES + 6))
    return (qa, qb), (lane1 < HEAD_DIM, lane1 >= HEAD_DIM)


def _group_matrix(width):
    shift = HEAD_DIM.bit_length() - 1
    return ((_iota((width, width), 0) >> shift) == (_iota((width, width), 1) >> shift)).astype(BF16)


def _prep_fwd(proj, gq, gk, bf, n_heads, name):
    s = proj.shape[0]
    aw = n_heads * HEAD_DIM
    n_pairs = n_heads // 2
    ts = min(TM, s)
    f_col = (proj.shape[1] - LANES) // LANES

    def body(q_ref, k_ref, v_ref, f_ref, gq_ref, gk_ref, bf_ref, qa_ref, ka_ref, vb_ref, carry_ref):
        @pl.when(pl.program_id(0) == 0)
        def _():
            carry_ref[...] = jnp.zeros_like(carry_ref)

        gmat = _group_matrix(aw)

        def head_norm(xv, g):
            ms = _dot2(xv * xv, gmat) * (1.0 / HEAD_DIM)
            return xv * lax.rsqrt(ms + EPS) * g

        qn = head_norm(q_ref[...], gq_ref[...]) * (HEAD_DIM ** -0.5)
        kn = head_norm(k_ref[...], gk_ref[...])
        z = f_ref[...] + bf_ref[...]
        logf = jnp.minimum(z, 0.0) - jnp.log(1.0 + jnp.exp(-jnp.abs(z)))
        tri = (_iota((ts, ts), 0) >= _iota((ts, ts), 1)).astype(BF16)
        c = _dot3_r(tri, logf) + carry_ref[...]
        carry_ref[...] = c[ts - 1:ts, :]
        terms = _split3(-c)
        row, col = _iota((LANES, LANES), 0), _iota((LANES, LANES), 1)
        ones = jnp.where(_iota((ts, LANES), 1) < 6, 1.0, 0.0).astype(BF16)
        for p in range(n_pairs):
            extra = jnp.zeros((ts, LANES), F32)
            for t, term in enumerate(terms):
                sel = ((row == 2 * p) & (col == t)) | ((row == 2 * p + 1) & (col == 3 + t))
                extra += _dot(term, sel.astype(BF16))
            lo, hi = p * PAIR, p * PAIR + LANES
            ka_ref[:, lo:hi] = kn[:, p * LANES:(p + 1) * LANES].astype(BF16)
            ka_ref[:, hi:hi + LANES] = extra.astype(BF16)
            qa_ref[:, lo:hi] = qn[:, p * LANES:(p + 1) * LANES].astype(BF16)
            qa_ref[:, hi:hi + LANES] = ones
            vb_ref[:, lo:hi] = v_ref[:, p * LANES:(p + 1) * LANES].astype(BF16)
            vb_ref[:, hi:hi + LANES] = jnp.ones((ts, LANES), BF16)

    blk = lambda j: pl.BlockSpec((ts, aw), lambda i: (i, j))
    vec = lambda w: pl.BlockSpec((1, w), lambda i: (0, 0))
    return pl.pallas_call(
        body, name=name, grid=(s // ts,),
        in_specs=[blk(0), blk(1), blk(2), pl.BlockSpec((ts, LANES), lambda i: (i, f_col)), vec(aw), vec(aw), vec(LANES)],
        out_specs=[pl.BlockSpec((ts, n_pairs * PAIR), lambda i: (i, 0))] * 3,
        out_shape=[jax.ShapeDtypeStruct((s, n_pairs * PAIR), BF16)] * 3,
        scratch_shapes=[pltpu.VMEM((1, LANES), F32)],
        compiler_params=_params("arbitrary"),
    )(proj, proj, proj, proj, gq, gk, bf)


def _prep_bwd(proj, dq, dka, drow, dcol, gq, gk, bf, n_heads, name):
    s = proj.shape[0]
    aw = n_heads * HEAD_DIM
    n_pairs = n_heads // 2
    ts = min(TM, s)
    nt = s // ts
    f_col = (proj.shape[1] - LANES) // LANES
    shift = HEAD_DIM.bit_length() - 1

    def body(q_ref, k_ref, f_ref, dq_ref, dka_ref, drow_ref, dcol_ref, gq_ref, gk_ref, bf_ref,
             dpq_ref, dpk_ref, dpf_ref, dgq_ref, dgk_ref, dbf_ref, carry_ref):
        @pl.when(pl.program_id(0) == 0)
        def _():
            carry_ref[...] = jnp.zeros_like(carry_ref)
            dgq_ref[...] = jnp.zeros_like(dgq_ref)
            dgk_ref[...] = jnp.zeros_like(dgk_ref)
            dbf_ref[...] = jnp.zeros_like(dbf_ref)

        gmat = _group_matrix(aw)

        def head_norm_bwd(xv, g, dn):
            r = lax.rsqrt(_dot2(xv * xv, gmat) * (1.0 / HEAD_DIM) + EPS)
            xh = xv * r
            dxh = dn * g
            dx = r * (dxh - xh * (_dot2(dxh * xh, gmat) * (1.0 / HEAD_DIM)))
            return dx, jnp.sum(dn * xh, axis=0, keepdims=True)

        dkav = dka_ref[...]
        dx, dg = head_norm_bwd(q_ref[...], gq_ref[...], dq_ref[...] * (HEAD_DIM ** -0.5))
        dpq_ref[...] = dx.astype(BF16)
        dgq_ref[...] += dg
        dkn = jnp.concatenate([dkav[:, p * PAIR:p * PAIR + LANES] for p in range(n_pairs)], axis=1)
        dx, dg = head_norm_bwd(k_ref[...], gk_ref[...], dkn)
        dpk_ref[...] = dx.astype(BF16)
        dgk_ref[...] += dg

        pick = (_iota((aw, LANES), 0) == (_iota((aw, LANES), 1) << shift)).astype(BF16)
        dc = _dot3(drow_ref[...], pick)
        r16, c16 = _iota((16, LANES), 0), _iota((16, LANES), 1)
        for p in range(n_pairs):
            place = ((r16 < 2) & (c16 == 2 * p + r16)).astype(BF16)
            for term in _split3(dcol_ref[p]):
                dc -= _dot_tn(term, place)
        triu = (_iota((ts, ts), 0) <= _iota((ts, ts), 1)).astype(BF16)
        dlogf = _dot3_r(triu, dc) + carry_ref[...]
        carry_ref[...] = dlogf[0:1, :]
        z = f_ref[...] + bf_ref[...]
        dz = dlogf * (1.0 / (1.0 + jnp.exp(z)))
        dpf_ref[...] = dz.astype(BF16)
        dbf_ref[...] += jnp.sum(dz, axis=0, keepdims=True)

    rev = lambda w, j: pl.BlockSpec((ts, w), lambda i: (nt - 1 - i, j))
    vec = lambda w: pl.BlockSpec((1, w), lambda i: (0, 0))
    return pl.pallas_call(
        body, name=name, grid=(nt,),
        in_specs=[rev(aw, 0), rev(aw, 1), rev(LANES, f_col), rev(aw, 0), rev(n_pairs * PAIR, 0), rev(aw, 0),
                  pl.BlockSpec((n_pairs, 16, ts), lambda i: (0, 0, nt - 1 - i)), vec(aw), vec(aw), vec(LANES)],
        out_specs=[rev(aw, 0), rev(aw, 0), rev(LANES, 0), vec(aw), vec(aw), vec(LANES)],
        out_shape=[jax.ShapeDtypeStruct((s, aw), BF16), jax.ShapeDtypeStruct((s, aw), BF16), jax.ShapeDtypeStruct((s, LANES), BF16),
                   jax.ShapeDtypeStruct((1, aw), F32), jax.ShapeDtypeStruct((1, aw), F32), jax.ShapeDtypeStruct((1, LANES), F32)],
        scratch_shapes=[pltpu.VMEM((1, LANES), F32)],
        compiler_params=_params("arbitrary"),
    )(proj, proj, proj, dq, dka, drow, dcol, gq, gk, bf)


def _attn_fwd(qa, ka, vb, n_heads, mix_width, name):
    s = qa.shape[0]
    aw = n_heads * HEAD_DIM
    n_pairs = n_heads // 2
    tq = min(TQ, s)

    def body(q_ref, k_ref, v_ref, o_ref, lse_ref, o16_ref):
        i = pl.program_id(1)
        qmasks, omasks = _head_masks()
        qv = q_ref[...]
        causal = _iota((tq, tq), 1) <= _iota((tq, tq), 0)
        qhs = [jnp.where(qmasks[h], qv, jnp.zeros_like(qv)) for h in range(2)]

        def scores(j):
            kv = k_ref[pl.ds(pl.multiple_of(j * tq, tq), tq), :]
            return tuple(_dot_nt(qhs[h], kv) for h in range(2))

        def update(j, state, scs, masked):
            vv = v_ref[pl.ds(pl.multiple_of(j * tq, tq), tq), :]
            out = []
            for h in range(2):
                m, acc = state[h]
                sc = jnp.where(causal, scs[h], NEG) if masked else scs[h]
                m_new = jnp.maximum(m, jnp.max(sc, axis=1, keepdims=True))
                p = jnp.exp(sc - m_new).astype(BF16)
                out.append((m_new, jnp.exp(m - m_new) * acc + _dot(p, vv)))
            return tuple(out)

        def body(j, state):
            return update(j, state, scores(j), False)

        init = ((jnp.full((tq, 1), NEG, F32), jnp.zeros((tq, PAIR), F32)),) * 2
        state = lax.fori_loop(0, i, body, init)
        res = []
        for m, acc in update(i, state, scores(i), True):
            l = acc[:, LANES:LANES + 1]
            res.append((acc[:, :LANES] * (1.0 / l), m + jnp.log(l)))
        out = jnp.where(omasks[0], res[0][0], res[1][0])
        o_ref[...] = out
        o16_ref[...] = out.astype(BF16)
        lse_ref[...] = jnp.where(omasks[0], res[0][1], res[1][1])

    return pl.pallas_call(
        body, name=name, grid=(n_pairs, s // tq),
        in_specs=[pl.BlockSpec((tq, PAIR), lambda p, i: (i, p)), pl.BlockSpec((s, PAIR), lambda p, i: (0, p)),
                  pl.BlockSpec((s, PAIR), lambda p, i: (0, p))],
        out_specs=[pl.BlockSpec((tq, LANES), lambda p, i: (i, p))] * 3,
        out_shape=[jax.ShapeDtypeStruct((s, aw), F32)] * 2 + [jax.ShapeDtypeStruct((s, mix_width), BF16)],
        compiler_params=_params("parallel", "parallel"),
    )(qa, ka, vb)


def _attn_bwd(qa, ka, vb, o, lse, dmixed, n_heads, name):
    s = qa.shape[0]
    aw = n_heads * HEAD_DIM
    n_pairs = n_heads // 2
    tq = min(TQ, s)
    nq = s // tq

    dot_tt = lambda a, b: lax.dot_general(a, b, (((0,), (1,)), ((), ())), preferred_element_type=F32)

    def body(q_ref, k_ref, v_ref, o_ref, lse_ref, do_ref, dq_ref, dka_ref, dv_ref, drow_ref, dcol_ref,
             delta_ref, dqt_ref, dkat_ref, dvt_ref):
        j = pl.program_id(1)
        qmasks, omasks = _head_masks()
        rowmasks = (_iota((LANES, 1), 0) < HEAD_DIM, _iota((LANES, 1), 0) >= HEAD_DIM)

        @pl.when(j == 0)
        def _():
            dqt_ref[...] = jnp.zeros_like(dqt_ref)
            drow_ref[...] = jnp.zeros_like(drow_ref)
            for c in range(nq):
                rows = slice(c * tq, (c + 1) * tq)
                prod = do_ref[rows, :] * o_ref[rows, :]
                da = jnp.sum(jnp.where(omasks[0], prod, 0.0), axis=1, keepdims=True)
                db = jnp.sum(jnp.where(omasks[1], prod, 0.0), axis=1, keepdims=True)
                delta_ref[rows, :] = jnp.where(omasks[0], da, db)

        dkat_ref[...] = jnp.zeros_like(dkat_ref)
        dvt_ref[...] = jnp.zeros_like(dvt_ref)
        dcol_ref[...] = jnp.zeros_like(dcol_ref)
        kv = k_ref[...]
        kk = kv[:, :LANES]
        vv = v_ref[...]
        causal = _iota((tq, tq), 1) <= _iota((tq, tq), 0)

        def step(i, masked):
            off = pl.multiple_of(i * tq, tq)
            qv = q_ref[pl.ds(off, tq), :]
            dov = do_ref[pl.ds(off, tq), :]
            lsev = lse_ref[pl.ds(off, tq), :]
            dlv = delta_ref[pl.ds(off, tq), :]
            for h in range(2):
                qh = jnp.where(qmasks[h], qv, jnp.zeros_like(qv))
                doh = jnp.where(omasks[h], dov, 0.0).astype(BF16)
                lane = h * HEAD_DIM
                sc = _dot_nt(qh, kv)
                if masked:
                    sc = jnp.where(causal, sc, NEG)
                p = jnp.exp(sc - lsev[:, lane:lane + 1])
                dvt_ref[...] += _dot_tn(doh, p.astype(BF16))
                dp = _dot_nt(doh, vv)
                dsf = p * (dp - dlv[:, lane:lane + 1])
                drow_ref[pl.ds(off, tq), :] += jnp.where(omasks[h], jnp.sum(dsf, axis=1, keepdims=True), 0.0)
                dcol_ref[0, h:h + 1, :] += jnp.sum(dsf, axis=0, keepdims=True)
                ds = dsf.astype(BF16)
                dkat_ref[...] += _dot_tn(qh, ds)
                dqt_ref[i] += jnp.where(rowmasks[h], dot_tt(kk, ds), 0.0)

        step(j, True)

        def loop_body(i, carry):
            step(i, False)
            return carry

        lax.fori_loop(j + 1, nq, loop_body, 0)
        dv_ref[...] = dvt_ref[...].T.astype(BF16)
        dka_ref[...] = dkat_ref[...].T

        @pl.when(j == nq - 1)
        def _():
            for c in range(nq):
                dq_ref[c * tq:(c + 1) * tq, :] = dqt_ref[c].T

    full = lambda w: pl.BlockSpec((s, w), lambda p, j: (0, p))
    blk = lambda w: pl.BlockSpec((tq, w), lambda p, j: (j, p))
    return pl.pallas_call(
        body, name=name, grid=(n_pairs, nq),
        in_specs=[full(PAIR), blk(PAIR), pl.BlockSpec((tq, LANES), lambda p, j: (j, 2 * p)), full(LANES), full(LANES), full(LANES)],
        out_specs=[full(LANES), blk(PAIR), blk(LANES), full(LANES), pl.BlockSpec((1, 16, tq), lambda p, j: (p, 0, j))],
        out_shape=[jax.ShapeDtypeStruct((s, aw), F32), jax.ShapeDtypeStruct((s, n_pairs * PAIR), F32),
                   jax.ShapeDtypeStruct((s, aw), BF16), jax.ShapeDtypeStruct((s, aw), F32),
                   jax.ShapeDtypeStruct((n_pairs, 16, s), F32)],
        scratch_shapes=[pltpu.VMEM((s, LANES), F32), pltpu.VMEM((nq, LANES, tq), F32), pltpu.VMEM((PAIR, tq), F32),
                        pltpu.VMEM((LANES, tq), F32)],
        compiler_params=_params("parallel", "arbitrary"),
    )(qa, ka, vb, o, lse, dmixed)


def _conv_fwd(proj, w32, bias, n_ch, a_col, g_col, name):
    s = proj.shape[0]
    rows = min(CONV_ROWS, s)

    def body(a_ref, g_ref, w_ref, b_ref, y_ref, pad_ref):
        pad_ref[0:CONV_PAD, :] = jnp.zeros((CONV_PAD, LANES), F32)
        pad_ref[CONV_PAD:CONV_PAD + s, :] = a_ref[...] * _sigmoid(g_ref[...])
        wv = w_ref[...]
        for c in range(s // rows):
            acc = jnp.broadcast_to(b_ref[...], (rows, LANES))
            for t in range(CONV_TAPS):
                start = c * rows + CONV_PAD - (CONV_TAPS - 1) + t
                acc = acc + wv[t:t + 1, :] * pad_ref[start:start + rows, :]
            y_ref[c * rows:(c + 1) * rows, :] = acc

    col = lambda j0: pl.BlockSpec((s, LANES), lambda c: (0, j0 + c))
    return pl.pallas_call(
        body, name=name, grid=(n_ch // LANES,),
        in_specs=[col(a_col), col(g_col), pl.BlockSpec((CONV_PAD, LANES), lambda c: (0, c)), pl.BlockSpec((1, LANES), lambda c: (0, c))],
        out_specs=pl.BlockSpec((s, LANES), lambda c: (0, c)),
        out_shape=jax.ShapeDtypeStruct((s, n_ch), F32),
        scratch_shapes=[pltpu.VMEM((s + CONV_PAD, LANES), F32)],
        compiler_params=_params("parallel"),
    )(proj, proj, w32, bias)


def _conv_bwd(proj, w32, dy, n_ch, a_col, g_col, name):
    s = proj.shape[0]
    rows = min(CONV_ROWS, s)
    sub = 8

    def fold(x):
        acc = x[0:sub, :]
        for r in range(1, rows // sub):
            acc = acc + x[r * sub:(r + 1) * sub, :]
        return acc

    def body(a_ref, g_ref, w_ref, dy_ref, da_ref, dg_ref, dw_ref, padh_ref, padd_ref):
        sg = _sigmoid(g_ref[...])
        padh_ref[0:CONV_PAD, :] = jnp.zeros((CONV_PAD, LANES), F32)
        padh_ref[CONV_PAD:CONV_PAD + s, :] = a_ref[...] * sg
        padd_ref[0:s, :] = dy_ref[...]
        padd_ref[s:s + CONV_PAD, :] = jnp.zeros((CONV_PAD, LANES), F32)
        wv = w_ref[...]
        dw = [jnp.zeros((sub, LANES), F32) for _ in range(CONV_TAPS + 1)]
        for c in range(s // rows):
            r0 = c * rows
            acc = jnp.zeros((rows, LANES), F32)
            dyc = dy_ref[r0:r0 + rows, :]
            for t in range(CONV_TAPS):
                back = r0 + (CONV_TAPS - 1) - t
                acc = acc + wv[t:t + 1, :] * padd_ref[back:back + rows, :]
                start = r0 + CONV_PAD - (CONV_TAPS - 1) + t
                dw[t] = dw[t] + fold(dyc * padh_ref[start:start + rows, :])
            dw[CONV_TAPS] = dw[CONV_TAPS] + fold(dyc)
            av = a_ref[r0:r0 + rows, :]
            sgc = _sigmoid(g_ref[r0:r0 + rows, :])
            da_ref[r0:r0 + rows, :] = (acc * sgc).astype(BF16)
            dg_ref[r0:r0 + rows, :] = (acc * av * sgc * (1.0 - sgc)).astype(BF16)
        for t in range(CONV_TAPS + 1):
            dw_ref[t:t + 1, :] = jnp.sum(dw[t], axis=0, keepdims=True)

    col = lambda j0: pl.BlockSpec((s, LANES), lambda c: (0, j0 + c))
    wspec = pl.BlockSpec((CONV_PAD, LANES), lambda c: (0, c))
    return pl.pallas_call(
        body, name=name, grid=(n_ch // LANES,),
        in_specs=[col(a_col), col(g_col), wspec, col(0)],
        out_specs=[col(0), col(0), wspec],
        out_shape=[jax.ShapeDtypeStruct((s, n_ch), BF16), jax.ShapeDtypeStruct((s, n_ch), BF16),
                   jax.ShapeDtypeStruct((CONV_PAD, n_ch), F32)],
        scratch_shapes=[pltpu.VMEM((s + CONV_PAD, LANES), F32), pltpu.VMEM((s + CONV_PAD, LANES), F32)],
        compiler_params=_params("parallel"),
    )(proj, proj, w32, dy)


def _my_place():
    return lax.axis_index("x"), lax.axis_index("y"), lax.axis_index("c")


def _flip(place, k):
    x, y, c = place
    return (1 - x if k & 4 else x, 1 - y if k & 2 else y, 1 - c if k & 1 else c)


def _dev_id(place):
    return 4 * place[0] + 2 * place[1] + place[2]


def _wait_all(ref, send_sem, recv_sem, place):
    pltpu.make_async_remote_copy(src_ref=ref, dst_ref=ref, send_sem=send_sem, recv_sem=recv_sem,
                                 device_id=place, device_id_type=MESH).wait()


def _window(kind, ref, dev, n):
    if kind == "slot":
        return ref.at[dev]
    if kind == "rows":
        return ref.at[pl.ds(pl.multiple_of(dev * n, n), n), :]
    return ref.at[:, pl.ds(pl.multiple_of(dev * n, n), n)]


def _hbm(x):
    return pltpu.with_memory_space_constraint(x, pltpu.HBM)


_EFFECT = pltpu.SideEffectType.DATAFLOW_SIDE_EFFECTING


def _exchange_start(srcs, lands, specs, groups, name):
    n = len(srcs)
    n_g = len(groups)

    def body(*refs):
        src_refs, land_refs = refs[:n], refs[n:2 * n]
        sems = refs[2 * n:2 * n + 2 * n_g]
        token = refs[-1]
        place = _my_place()
        me = _dev_id(place)
        for g, units in enumerate(groups):
            for j, u in enumerate(units):
                mode, kind, cnt = specs[u]
                for k in range(N_DEV):
                    peer = _flip(place, k)
                    if mode == "gather":
                        src, dst = src_refs[u], _window(kind, land_refs[u], me, cnt)
                    else:
                        src, dst = _window(kind, src_refs[u], _dev_id(peer), cnt), land_refs[u].at[k]
                    pltpu.make_async_remote_copy(src_ref=src, dst_ref=dst, send_sem=sems[2 * g].at[j], recv_sem=sems[2 * g + 1].at[j],
                                                 device_id=peer, device_id_type=MESH).start()
        token[...] = jnp.zeros_like(token)

    hbm = pl.BlockSpec(memory_space=pltpu.HBM)
    sem = pl.BlockSpec(memory_space=pltpu.SEMAPHORE)
    out_shape = [pltpu.SemaphoreType.DMA((len(units),)) for units in groups for _ in range(2)]
    out_shape += [pltpu.HBM(x.shape, x.dtype) for x in lands] + [jax.ShapeDtypeStruct((8, LANES), F32)]
    outs = pl.pallas_call(
        body, name=name, out_shape=out_shape,
        in_specs=[hbm] * (2 * n), out_specs=[sem] * (2 * n_g) + [hbm] * n + [pl.BlockSpec(memory_space=pltpu.VMEM)],
        input_output_aliases={n + u: 2 * n_g + u for u in range(n)},
        compiler_params=pltpu.CompilerParams(has_side_effects=_EFFECT),
    )(*[_hbm(x) for x in srcs], *[_hbm(x) for x in lands])
    sem_pairs = [(outs[2 * g], outs[2 * g + 1]) for g in range(n_g)]
    return sem_pairs, list(outs[2 * n_g:2 * n_g + n]), outs[-1]


def _exchange_wait(srcs, lands, specs, sem_pair, after, name):
    n = len(lands)

    def body(*refs):
        land_refs = refs[n:2 * n]
        send_sems, recv_sems = refs[2 * n], refs[2 * n + 1]
        place = _my_place()
        for u in range(n):
            _wait_all(land_refs[u], send_sems.at[u], recv_sems.at[u], place)

    hbm = pl.BlockSpec(memory_space=pltpu.HBM)
    sem = pl.BlockSpec(memory_space=pltpu.SEMAPHORE)
    outs = pl.pallas_call(
        body, name=name, out_shape=[pltpu.HBM(x.shape, x.dtype) for x in lands],
        in_specs=[hbm] * (2 * n) + [sem, sem, pl.BlockSpec(memory_space=pl.ANY)], out_specs=[hbm] * n,
        input_output_aliases={n + u: u for u in range(n)},
        compiler_params=pltpu.CompilerParams(has_side_effects=_EFFECT),
    )(*[_hbm(x) for x in srcs], *lands, sem_pair[0], sem_pair[1], after)
    return list(outs)


def _sum_devices(parts, name):
    _, r, w = parts.shape

    def body(p_ref, out_ref):
        acc = p_ref[0]
        for d in range(1, N_DEV):
            acc = acc + p_ref[d]
        out_ref[...] = acc

    return pl.pallas_call(
        body, name=name, out_shape=jax.ShapeDtypeStruct((r, w), F32),
        in_specs=[pl.BlockSpec(memory_space=pltpu.VMEM)], out_specs=pl.BlockSpec(memory_space=pltpu.VMEM),
    )(parts)


def _adamw_math(w, m, v, g):
    m_new = ADAM_B1 * m + (1.0 - ADAM_B1) * g
    v_new = ADAM_B2 * v + (1.0 - ADAM_B2) * (g * g)
    m_hat = m_new / (1.0 - ADAM_B1 ** ADAM_STEP)
    v_hat = v_new / (1.0 - ADAM_B2 ** ADAM_STEP)
    return -ADAM_LR * (m_hat / (jnp.sqrt(v_hat) + ADAM_EPS) + ADAM_WD * w), m_new, v_new


def _adamw(w, m, v, g, name):
    rows = w.shape[0]
    tr = min(FLAT_ROWS, rows)
    assert rows % tr == 0, (name, rows)

    def body(w_ref, m_ref, v_ref, g_ref, d_out, m_out, v_out):
        d_out[...], m_out[...], v_out[...] = _adamw_math(w_ref[...], m_ref[...], v_ref[...], g_ref[...])

    flat = pl.BlockSpec((tr, LANES), lambda i: (i, 0))
    return pl.pallas_call(
        body, name=name, grid=(rows // tr,), in_specs=[flat] * 4, out_specs=[flat] * 3,
        out_shape=[jax.ShapeDtypeStruct((rows, LANES), F32)] * 3,
        compiler_params=_params("parallel"),
    )(w, m, v, g)


def _adamw_shard(w, m, v, recv, layer, prev, name):
    depth, a, b = w.shape
    ta = min(256, a)
    assert a % ta == 0

    def body(w_ref, m_ref, v_ref, r_ref, *rest):
        g_out, d_out, m_out, v_out = rest[-4:]
        g = r_ref[0].astype(F32)
        for k in range(1, N_DEV):
            g = g + r_ref[k].astype(F32)
        g_out[0] = g
        d_out[0], m_out[0], v_out[0] = _adamw_math(w_ref[0], m_ref[0], v_ref[0], g)

    lay = pl.BlockSpec((1, ta, b), lambda i: (layer, i, 0))
    in_specs = [lay] * 3 + [pl.BlockSpec((N_DEV, ta, b), lambda i: (0, i, 0))]
    args = [w, m, v, recv]
    aliases = {}
    if prev is not None:
        in_specs += [pl.BlockSpec(memory_space=pl.ANY)] * 4
        args += list(prev)
        aliases = {4 + i: i for i in range(4)}
    return pl.pallas_call(
        body, name=name, grid=(a // ta,), in_specs=in_specs, out_specs=[lay] * 4,
        out_shape=[jax.ShapeDtypeStruct(w.shape, F32)] * 4, input_output_aliases=aliases,
        compiler_params=_params("parallel"),
    )(*args)


def _round_up(n, mult):
    return (n + mult - 1) // mult * mult


def _flatten(parts, row_mult):
    flat = jnp.concatenate([p.reshape(-1) for p in parts])
    rows = _round_up(-(-flat.shape[0] // LANES), row_mult)
    return jnp.pad(flat, (0, rows * LANES - flat.shape[0])).reshape(rows, LANES)


def _unflatten(flat, shapes):
    flat = flat.reshape(-1)
    out, off = [], 0
    for shp in shapes:
        n = 1
        for dim in shp:
            n *= dim
        out.append(flat[off:off + n].reshape(shp))
        off += n
    return out


def kernel(x, norm1_g, w_in, b_f, q_norm_g, k_norm_g, conv_w, conv_b, conv_ln_g, conv_ln_b, w_o, norm2_g, w_mlp_in, w_mlp_out, loss_target, m_norm1_g, m_w_in, m_b_f, m_q_norm_g, m_k_norm_g, m_conv_w, m_conv_b, m_conv_ln_g, m_conv_ln_b, m_w_o, m_norm2_g, m_w_mlp_in, m_w_mlp_out, v_norm1_g, v_w_in, v_b_f, v_q_norm_g, v_k_norm_g, v_conv_w, v_conv_b, v_conv_ln_g, v_conv_ln_b, v_w_o, v_norm2_g, v_w_mlp_in, v_w_mlp_out):
    depth, d_model, n_in_loc = w_in.shape
    seq = x.shape[1]
    n_heads = b_f.shape[1]
    aw = n_heads * HEAD_DIM
    cc = conv_b.shape[1]
    n_in = n_in_loc * N_DEV
    o_f = 3 * aw
    n_all = 3 * aw + 2 * cc + LANES
    assert n_in == 3 * aw + n_heads + 2 * cc and aw + cc == d_model and n_heads % 2 == 0
    assert aw % LANES == 0 and cc % LANES == 0 and x.shape[0] == 1
    me = 4 * lax.axis_index("x") + 2 * lax.axis_index("y") + lax.axis_index("c")

    d_ff = w_mlp_in.shape[2] * N_DEV

    r_o, f_1, f_2 = w_o.shape[1], w_mlp_in.shape[2], w_mlp_out.shape[1]

    ag_src, ag_land, ag_spec = [], [], []
    for l in range(depth):
        ag_src += [w_in[l].astype(BF16), w_o[l].astype(BF16), w_mlp_in[l].astype(BF16), w_mlp_out[l].astype(BF16)]
        ag_land += [(N_DEV, d_model, n_in_loc), (N_DEV * r_o, d_model), (d_model, N_DEV * f_1), (N_DEV * f_2, d_model)]
        ag_spec += [("gather", "slot", 1), ("gather", "rows", r_o), ("gather", "cols", f_1), ("gather", "rows", f_2)]
    ag_src.append(jnp.stack(_split3(conv_w)))
    ag_land.append((N_DEV, 3) + conv_w.shape)
    ag_spec.append(("gather", "slot", 1))
    ag_groups = [grp for l in range(depth) for grp in ([4 * l] + ([4 * depth] if l == 0 else []), [4 * l + 1], [4 * l + 2, 4 * l + 3])]
    ag_land = [lax.empty(shp, BF16) for shp in ag_land]
    ag_sems, ag_land, ag_token = _exchange_start(ag_src, ag_land, ag_spec, ag_groups, "gather_start")

    def gathered(g, after):
        units = ag_groups[g]
        return _exchange_wait([ag_src[u] for u in units], [ag_land[u] for u in units], [ag_spec[u] for u in units],
                              ag_sems[g], after, f"gather_wait_{g}")

    def to_all(w):
        return jnp.concatenate([w[:, :o_f], w[:, o_f + n_heads:], w[:, o_f:o_f + n_heads],
                                jnp.zeros((w.shape[0], LANES - n_heads), w.dtype)], axis=1)

    def from_all(w):
        return jnp.concatenate([w[:, :o_f], w[:, n_all - LANES:n_all - LANES + n_heads], w[:, o_f:n_all - LANES]], axis=1)

    def whole_in(lin):
        return to_all(jnp.moveaxis(lin, 0, 1).reshape(d_model, n_in))

    def row(p, l, width=None):
        v = p[l].reshape(1, -1)
        return v if width is None else jnp.pad(v, ((0, 0), (0, width - v.shape[1])))

    a_col, g_col = 3 * aw // LANES, (3 * aw + cc) // LANES

    gq = [jnp.tile(row(q_norm_g, l), (1, n_heads)) for l in range(depth)]
    gk = [jnp.tile(row(k_norm_g, l), (1, n_heads)) for l in range(depth)]
    bfp = [row(b_f, l, LANES) for l in range(depth)]
    add_res = lambda acc, res: (acc + res,)
    w_all, w_out, w_ff1, w_ff2 = [None] * depth, [None] * depth, [None] * depth, [None] * depth

    h = x[0]
    saved = []
    for l in range(depth):
        if l == 0:
            lin, lc = gathered(0, h)
            lc = lc.astype(F32)
            conv_full = jnp.moveaxis(lc[:, 0] + lc[:, 1] + lc[:, 2], 0, 2).reshape(depth, CONV_TAPS, cc)
            w32 = [jnp.pad(conv_full[i], ((0, CONV_PAD - CONV_TAPS), (0, 0))) for i in range(depth)]
        else:
            lin, = gathered(3 * l, h)
        w_all[l] = whole_in(lin)
        u1, proj = _rms_matmul(h, row(norm1_g, l), w_all[l], out_dtypes=(F32,), name=f"mm_in_{l}")
        qa, ka, vb = _prep_fwd(proj, gq[l], gk[l], bfp[l], n_heads, f"prep_fwd_{l}")
        att, lse, mixed = _attn_fwd(qa, ka, vb, n_heads, aw + cc, f"attn_fwd_{l}")
        yc = _conv_fwd(proj, w32[l], row(conv_b, l), cc, a_col, g_col, f"conv_fwd_{l}")
        mixed = _ln_silu_fwd(yc, row(conv_ln_g, l), row(conv_ln_b, l), mixed, f"ln_silu_fwd_{l}")
        w_out[l], = gathered(3 * l + 1, mixed)
        x1 = _matmul(mixed, w_out[l], mode="nn", out_dtypes=(F32,), name=f"mm_o_{l}", epilogue=add_res, extras=(h,))
        w_ff1[l], w_ff2[l] = gathered(3 * l + 2, x1)
        u2, r, a = _rms_matmul(x1, row(norm2_g, l), w_ff1[l], out_dtypes=(BF16, BF16), name=f"mm_ff1_{l}", tm=256,
                               epilogue=lambda acc: (jnp.maximum(acc, 0.0), jnp.square(jnp.maximum(acc, 0.0))))
        if l < depth - 1:
            x2 = _matmul(a, w_ff2[l], mode="nn", out_dtypes=(F32,), name=f"mm_ff2_{l}", epilogue=add_res, extras=(x1,), tk=d_ff)
        else:
            x2 = None
            dh, dh16, sq = _matmul_loss(a, w_ff2[l], x1, loss_target[0], f"mm_ff2_loss_{l}")
        saved.append(dict(x_in=h, u1=u1, proj=proj, qa=qa, ka=ka, vb=vb, att=att, lse=lse, yc=yc, mixed=mixed,
                          x1=x1, u2=u2, r=r, a=a))
        h = x2

    loss = lax.psum(0.5 * jnp.sum(sq) / d_model, ("x", "y", "c"))

    g_in, g_o, g_1, g_2 = [None] * depth, [None] * depth, [None] * depth, [None] * depth
    gs = {n: [None] * depth for n in ("norm1", "bf", "qn", "kn", "convw", "convb", "lng", "lnb", "norm2")}
    scattering = {}

    def scatter_start(stage, l, srcs, specs, slabs):
        lands = [lax.empty((N_DEV,) + shp, BF16) for shp in slabs]
        sems, lands, token = _exchange_start(srcs, lands, specs, [list(range(len(srcs)))], f"scatter_start_{stage}_{l}")
        scattering[(stage, l)] = (srcs, lands, specs, sems[0])
        return token[0, 0]

    for l in reversed(range(depth)):
        sv = saved[l]
        dh1 = _matmul(dh16, w_ff2[l], mode="nt", out_dtypes=(BF16,), name=f"mm_dff2_{l}", tm=256, tn=d_ff,
                      epilogue=lambda acc, rr: (acc * (2.0 * rr.astype(F32)),), extras=(sv["r"],))
        g_2[l] = _matmul(sv["a"], dh16, mode="tn", out_dtypes=(BF16,), name=f"mm_dw2_{l}", tm=1024, tk=seq)
        g_1[l] = _matmul(sv["u2"], dh1, mode="tn", out_dtypes=(BF16,), name=f"mm_dw1_{l}", tm=d_model, tk=seq)
        tok = scatter_start("ff", l, [g_1[l], g_2[l]], [("scatter", "cols", f_1), ("scatter", "rows", f_2)],
                            [(d_model, f_1), (f_2, d_model)])
        dx1, dx16, gs["norm2"][l] = _matmul_rms_bwd(dh1, w_ff1[l], sv["x1"], row(norm2_g, l) + tok, dh, f"mm_du2_rms_{l}")

        dmixed = _matmul(dx16, w_out[l], mode="nt", out_dtypes=(F32,), name=f"mm_dmixed_{l}")
        g_o[l] = _matmul(sv["mixed"], dx16, mode="tn", out_dtypes=(BF16,), name=f"mm_dwo_{l}", tm=1024, tk=seq // 2)
        tok = scatter_start("o", l, [g_o[l]], [("scatter", "rows", r_o)], [(r_o, d_model)])
        dyc, gs["lng"][l], gs["lnb"][l] = _ln_silu_bwd(sv["yc"], row(conv_ln_g, l) + tok, row(conv_ln_b, l), dmixed, f"ln_silu_bwd_{l}")
        dpa, dpg, dw32 = _conv_bwd(sv["proj"], w32[l], dyc, cc, a_col, g_col, f"conv_bwd_{l}")
        gs["convw"][l], gs["convb"][l] = dw32[:CONV_TAPS], dw32[CONV_TAPS:CONV_TAPS + 1]
        dq, dka, dv, drow, dcol = _attn_bwd(sv["qa"], sv["ka"], sv["vb"], sv["att"], sv["lse"], dmixed, n_heads, f"attn_bwd_{l}")
        dpq, dpk, dpf, dgq, dgk, dbf = _prep_bwd(sv["proj"], dq, dka, drow, dcol, gq[l], gk[l], bfp[l], n_heads, f"prep_bwd_{l}")
        gs["qn"][l] = dgq.reshape(n_heads, HEAD_DIM).sum(axis=0)
        gs["kn"][l] = dgk.reshape(n_heads, HEAD_DIM).sum(axis=0)
        gs["bf"][l] = dbf[0, :n_heads]
        dproj = jnp.concatenate([dpq, dpk, dv, dpa, dpg, dpf], axis=1)
        dwall = _matmul(sv["u1"], dproj, mode="tn", out_dtypes=(BF16,), name=f"mm_dwall_{l}", tm=d_model, tn=n_all // 3, tk=seq)
        g_in[l] = jnp.moveaxis(from_all(dwall).reshape(d_model, N_DEV, n_in_loc), 1, 0)
        tok = scatter_start("in", l, [g_in[l]], [("scatter", "slot", 1)], [(d_model, n_in_loc)])
        dh, dh16, gs["norm1"][l] = _matmul_rms_bwd(dproj, w_all[l], sv["x_in"], row(norm1_g, l) + tok, dx1, f"mm_du1_rms_{l}")
    grad_x = dh[None]

    small_g = [jnp.stack(gs[n]).reshape(shp) for n, shp in (
        ("norm1", norm1_g.shape), ("bf", b_f.shape), ("qn", q_norm_g.shape), ("kn", k_norm_g.shape),
        ("convw", (depth, CONV_TAPS, cc)), ("convb", conv_b.shape), ("lng", conv_ln_g.shape), ("lnb", conv_ln_b.shape),
        ("norm2", norm2_g.shape))]
    small_shapes = [g.shape for g in small_g]
    small_flat = _flatten(small_g, 8)
    small_spec = [("gather", "slot", 1)]
    small_sems, small_land, _ = _exchange_start([small_flat], [lax.empty((N_DEV,) + small_flat.shape, F32)], small_spec, [[0]],
                                                "small_grads_start")

    def landed(stage, l, after):
        srcs, lands, specs, sems = scattering[(stage, l)]
        return _exchange_wait(srcs, lands, specs, sems, after, f"scatter_wait_{stage}_{l}")

    def adamw_layers(kd, w, m, v, recv):
        outs = None
        for l in reversed(range(depth)):
            outs = _adamw_shard(w, m, v, recv[l], l, outs, f"adamw_{kd}_{l}")
        return outs

    recv_1, recv_2, recv_in, recv_o = [None] * depth, [None] * depth, [None] * depth, [None] * depth
    for l in reversed(range(depth)):
        recv_1[l], recv_2[l] = landed("ff", l, dh)
    out_1 = adamw_layers("1", w_mlp_in, m_w_mlp_in, v_w_mlp_in, recv_1)
    out_2 = adamw_layers("2", w_mlp_out, m_w_mlp_out, v_w_mlp_out, recv_2)
    for l in reversed(range(depth)):
        recv_o[l], = landed("o", l, out_2[1])
    out_o = adamw_layers("o", w_o, m_w_o, v_w_o, recv_o)
    for l in reversed(range(depth)):
        recv_in[l], = landed("in", l, out_o[1])
    out_in = adamw_layers("in", w_in, m_w_in, v_w_in, recv_in)
    big_out = [[outs[kind] for outs in (out_in, out_o, out_1, out_2)] for kind in range(4)]

    small_parts, = _exchange_wait([small_flat], small_land, small_spec, small_sems[0], out_in[1], "small_grads_wait")
    small_g = _unflatten(_sum_devices(small_parts, "small_grads_sum"), small_shapes)
    cw = conv_w.shape[2]
    small_g[4] = lax.dynamic_slice_in_dim(small_g[4], me * cw, cw, axis=2)
    small = (norm1_g, b_f, q_norm_g, k_norm_g, conv_w, conv_b, conv_ln_g, conv_ln_b, norm2_g)
    small_m = (m_norm1_g, m_b_f, m_q_norm_g, m_k_norm_g, m_conv_w, m_conv_b, m_conv_ln_g, m_conv_ln_b, m_norm2_g)
    small_v = (v_norm1_g, v_b_f, v_q_norm_g, v_k_norm_g, v_conv_w, v_conv_b, v_conv_ln_g, v_conv_ln_b, v_norm2_g)
    small_out = _adamw(_flatten(small, 8), _flatten(small_m, 8), _flatten(small_v, 8), _flatten(small_g, 8), "adamw_small")
    small_out = [small_g] + [_unflatten(o, [w.shape for w in small]) for o in small_out]

    def group(kind):
        s_, b_ = small_out[kind], big_out[kind]
        return [s_[0], b_[0], s_[1], s_[2], s_[3], s_[4], s_[5], s_[6], s_[7], b_[1], s_[8], b_[2], b_[3]]

    return (loss, grad_x, *group(0), *group(1), *group(2), *group(3))
```

```python
import functools

import jax
import jax.numpy as jnp
from jax import lax
from jax.experimental import pallas as pl
from jax.experimental.pallas import tpu as pltpu

F32 = jnp.float32
BF16 = jnp.bfloat16

EPS = 1e-6
HEAD_DIM = 64
LANES = 128
PAIR = 2 * LANES
N_DEV = 8
CONV_TAPS = 31
CONV_PAD = 32
NEG = -1e30

ADAM_LR = 0.001
ADAM_B1 = 0.9
ADAM_B2 = 0.999
ADAM_EPS = 1e-08
ADAM_WD = 0.01
ADAM_STEP = 10

TM = 512
TQ = 512
CONV_ROWS = 128
FLAT_ROWS = 1024
MESH = pl.DeviceIdType.MESH


def _params(*sem):
    return pltpu.CompilerParams(dimension_semantics=sem, vmem_limit_bytes=56 * 1024 * 1024)


def _split3(x):
    hi = x.astype(BF16)
    r1 = x - hi.astype(F32)
    mid = r1.astype(BF16)
    lo = (r1 - mid.astype(F32)).astype(BF16)
    return hi, mid, lo


def _dot(a, b):
    return jnp.dot(a, b, preferred_element_type=F32)


def _dot_nt(a, b):
    return lax.dot_general(a, b, (((1,), (1,)), ((), ())), preferred_element_type=F32)


def _dot_tn(a, b):
    return lax.dot_general(a, b, (((0,), (0,)), ((), ())), preferred_element_type=F32)


def _dot3(x, mat):
    hi, mid, lo = _split3(x)
    return _dot(hi, mat) + _dot(mid, mat) + _dot(lo, mat)


def _dot2(x, mat):
    hi = x.astype(BF16)
    lo = (x - hi.astype(F32)).astype(BF16)
    return _dot(hi, mat) + _dot(lo, mat)


def _dot3_r(mat, x):
    hi, mid, lo = _split3(x)
    return _dot(mat, hi) + _dot(mat, mid) + _dot(mat, lo)


def _iota(shape, dim):
    return lax.broadcasted_iota(jnp.int32, shape, dim)


def _sigmoid(x):
    return 1.0 / (1.0 + jnp.exp(-x))


def _matmul(a, b, *, mode, out_dtypes, name, epilogue=None, extras=(), tm=TM, tn=1024, tk=1024):
    if mode == "nn":
        (m, k), (k2, n) = a.shape, b.shape
    elif mode == "nt":
        (m, k), (n, k2) = a.shape, b.shape
    else:
        (k, m), (k2, n) = a.shape, b.shape
    assert k == k2, (name, a.shape, b.shape)
    tm, tn, tk = min(tm, m), min(tn, n), min(tk, k)
    assert m % tm == 0 and n % tn == 0 and k % tk == 0, (name, m, n, k, tm, tn, tk)
    nk = k // tk
    a_mode = dict(pipeline_mode=pl.Buffered(1)) if (m == tm and nk == 1) else {}
    b_mode = dict(pipeline_mode=pl.Buffered(1)) if (n == tn and nk == 1) else {}
    if mode == "tn":
        a_spec = pl.BlockSpec((tk, tm), lambda i, j, kk: (kk, i), **a_mode)
    else:
        a_spec = pl.BlockSpec((tm, tk), lambda i, j, kk: (i, kk), **a_mode)
    if mode == "nt":
        b_spec = pl.BlockSpec((tn, tk), lambda i, j, kk: (j, kk), **b_mode)
    else:
        b_spec = pl.BlockSpec((tk, tn), lambda i, j, kk: (kk, j), **b_mode)
    dot = {"nn": _dot, "nt": _dot_nt, "tn": _dot_tn}[mode]
    tile = pl.BlockSpec((tm, tn), lambda i, j, kk: (i, j))
    n_ex, n_out = len(extras), len(out_dtypes)
    acc_in_out = nk > 1 and epilogue is None and out_dtypes[0] == F32

    def body(a_ref, b_ref, *rest):
        ex_refs, out_refs = rest[:n_ex], rest[n_ex:n_ex + n_out]
        part = dot(a_ref[...].astype(BF16), b_ref[...].astype(BF16))

        def finish(acc):
            res = epilogue(acc, *[e[...] for e in ex_refs]) if epilogue is not None else (acc,) * n_out
            for o_ref, r in zip(out_refs, res):
                o_ref[...] = r.astype(o_ref.dtype)

        if nk == 1:
            finish(part)
        else:
            acc_ref = out_refs[0] if acc_in_out else rest[-1]
            kk = pl.program_id(2)

            @pl.when(kk == 0)
            def _():
                acc_ref[...] = part

            @pl.when(kk > 0)
            def _():
                acc_ref[...] += part

            @pl.when(kk == nk - 1)
            def _():
                if acc_in_out:
                    for o_ref in out_refs[1:]:
                        o_ref[...] = acc_ref[...].astype(o_ref.dtype)
                else:
                    finish(acc_ref[...])

    outs = pl.pallas_call(
        body,
        name=name,
        grid=(m // tm, n // tn, nk),
        in_specs=[a_spec, b_spec] + [tile] * n_ex,
        out_specs=[tile] * n_out,
        out_shape=[jax.ShapeDtypeStruct((m, n), dt) for dt in out_dtypes],
        scratch_shapes=[pltpu.VMEM((tm, tn), F32)] if nk > 1 and not acc_in_out else [],
        compiler_params=_params("parallel", "parallel", "arbitrary"),
    )(a, b, *extras)
    return outs if n_out > 1 else outs[0]


def _rms_matmul(x, g, b, *, out_dtypes, name, epilogue=None, tm=TM):
    s, d = x.shape
    n = b.shape[1]
    ts = min(tm, s)
    n_out = len(out_dtypes)

    def body(x_ref, g_ref, b_ref, u_ref, *out_refs):
        xv = x_ref[...]
        u = (xv * lax.rsqrt(jnp.mean(xv * xv, axis=-1, keepdims=True) + EPS) * g_ref[...]).astype(BF16)
        u_ref[...] = u
        acc = _dot(u, b_ref[...])
        res = epilogue(acc) if epilogue is not None else (acc,)
        for o_ref, r in zip(out_refs, res):
            o_ref[...] = r.astype(o_ref.dtype)

    row = lambda w: pl.BlockSpec((ts, w), lambda i: (i, 0))
    return pl.pallas_call(
        body, name=name, grid=(s // ts,),
        in_specs=[row(d), pl.BlockSpec((1, d), lambda i: (0, 0)), pl.BlockSpec((d, n), lambda i: (0, 0), pipeline_mode=pl.Buffered(1))],
        out_specs=[row(d)] + [row(n)] * n_out,
        out_shape=[jax.ShapeDtypeStruct((s, d), BF16)] + [jax.ShapeDtypeStruct((s, n), dt) for dt in out_dtypes],
        compiler_params=_params("parallel"),
    )(x, g, b)


def _matmul_rms_bwd(a, b, x, g, dres, name):
    s, k = a.shape
    d = b.shape[0]
    ts = min(TM, s)

    def body(a_ref, b_ref, x_ref, g_ref, dres_ref, dx_ref, dx16_ref, dg_ref):
        @pl.when(pl.program_id(0) == 0)
        def _():
            dg_ref[...] = jnp.zeros_like(dg_ref)

        duv = _dot_nt(a_ref[...], b_ref[...])
        xv = x_ref[...]
        r = lax.rsqrt(jnp.mean(xv * xv, axis=-1, keepdims=True) + EPS)
        xh = xv * r
        dxh = duv * g_ref[...]
        dx = dres_ref[...] + r * (dxh - xh * jnp.mean(dxh * xh, axis=-1, keepdims=True))
        dx_ref[...] = dx
        dx16_ref[...] = dx.astype(BF16)
        dg_ref[...] += jnp.sum(duv * xh, axis=0, keepdims=True)

    row = pl.BlockSpec((ts, d), lambda i: (i, 0))
    vec = pl.BlockSpec((1, d), lambda i: (0, 0))
    return pl.pallas_call(
        body, name=name, grid=(s // ts,),
        in_specs=[pl.BlockSpec((ts, k), lambda i: (i, 0)), pl.BlockSpec((d, k), lambda i: (0, 0), pipeline_mode=pl.Buffered(1)),
                  row, vec, row],
        out_specs=[row, row, vec],
        out_shape=[jax.ShapeDtypeStruct((s, d), F32), jax.ShapeDtypeStruct((s, d), BF16), jax.ShapeDtypeStruct((1, d), F32)],
        compiler_params=_params("arbitrary"),
    )(a, b, x, g, dres)


def _matmul_loss(a, b, res, target, name):
    s, k = a.shape
    d = b.shape[1]
    ts = min(TM, s)

    def body(a_ref, b_ref, res_ref, t_ref, dy_ref, dy16_ref, sq_ref):
        @pl.when(pl.program_id(0) == 0)
        def _():
            sq_ref[...] = jnp.zeros_like(sq_ref)

        err = _dot(a_ref[...], b_ref[...]) + res_ref[...] - t_ref[...]
        dy = err * (1.0 / d)
        dy_ref[...] = dy
        dy16_ref[...] = dy.astype(BF16)
        sq_ref[...] += jnp.sum(err * err, axis=0, keepdims=True)

    row = pl.BlockSpec((ts, d), lambda i: (i, 0))
    vec = pl.BlockSpec((1, d), lambda i: (0, 0))
    return pl.pallas_call(
        body, name=name, grid=(s // ts,),
        in_specs=[pl.BlockSpec((ts, k), lambda i: (i, 0)), pl.BlockSpec((k, d), lambda i: (0, 0), pipeline_mode=pl.Buffered(1)),
                  row, row],
        out_specs=[row, row, vec],
        out_shape=[jax.ShapeDtypeStruct((s, d), F32), jax.ShapeDtypeStruct((s, d), BF16), jax.ShapeDtypeStruct((1, d), F32)],
        compiler_params=_params("arbitrary"),
    )(a, b, res, target)


def _ln_silu_fwd(y, g, b, mixed, name):
    s, c = y.shape
    ts = min(TM, s)
    assert mixed.shape == (s, 2 * c)

    def body(y_ref, g_ref, b_ref, mixed_ref, h_ref):
        yv = y_ref[...]
        mu = jnp.mean(yv, axis=-1, keepdims=True)
        yc = yv - mu
        z = yc * lax.rsqrt(jnp.mean(yc * yc, axis=-1, keepdims=True) + EPS) * g_ref[...] + b_ref[...]
        h_ref[...] = (z * _sigmoid(z)).astype(BF16)

    row = pl.BlockSpec((ts, c), lambda i: (i, 0))
    vec = pl.BlockSpec((1, c), lambda i: (0, 0))
    return pl.pallas_call(
        body, name=name, grid=(s // ts,), in_specs=[row, vec, vec, pl.BlockSpec(memory_space=pl.ANY)],
        out_specs=pl.BlockSpec((ts, c), lambda i: (i, 1)),
        out_shape=jax.ShapeDtypeStruct((s, 2 * c), BF16), input_output_aliases={3: 0},
        compiler_params=_params("parallel"),
    )(y, g, b, mixed)


def _ln_silu_bwd(y, g, b, dmixed, name):
    s, c = y.shape
    ts = min(TM, s)

    def body(y_ref, g_ref, b_ref, dh_ref, dy_ref, dg_ref, db_ref):
        @pl.when(pl.program_id(0) == 0)
        def _():
            dg_ref[...] = jnp.zeros_like(dg_ref)
            db_ref[...] = jnp.zeros_like(db_ref)

        yv = y_ref[...]
        mu = jnp.mean(yv, axis=-1, keepdims=True)
        yc = yv - mu
        r = lax.rsqrt(jnp.mean(yc * yc, axis=-1, keepdims=True) + EPS)
        yh = yc * r
        z = yh * g_ref[...] + b_ref[...]
        sg = _sigmoid(z)
        dz = dh_ref[...] * (sg * (1.0 + z * (1.0 - sg)))
        dg_ref[...] += jnp.sum(dz * yh, axis=0, keepdims=True)
        db_ref[...] += jnp.sum(dz, axis=0, keepdims=True)
        dyh = dz * g_ref[...]
        dy_ref[...] = r * (dyh - jnp.mean(dyh, axis=-1, keepdims=True) - yh * jnp.mean(dyh * yh, axis=-1, keepdims=True))

    row = pl.BlockSpec((ts, c), lambda i: (i, 0))
    vec = pl.BlockSpec((1, c), lambda i: (0, 0))
    return pl.pallas_call(
        body, name=name, grid=(s // ts,),
        in_specs=[row, vec, vec, pl.BlockSpec((ts, c), lambda i: (i, 1))], out_specs=[row, vec, vec],
        out_shape=[jax.ShapeDtypeStruct((s, c), F32), jax.ShapeDtypeStruct((1, c), F32), jax.ShapeDtypeStruct((1, c), F32)],
        compiler_params=_params("arbitrary"),
    )(y, g, b, dmixed)


def _head_masks():
    lane2 = _iota((1, PAIR), 1)
    lane1 = _iota((1, LANES), 1)
    qa = (lane2 < HEAD_DIM) | ((lane2 >= LANES) & (lane2 < LANES + 3))
    qb = ((lane2 >= HEAD_DIM) & (lane2 < LANES)) | ((lane2 >= LANES + 3) & (lane2 < LANES + 6))
    return (qa, qb), (lane1 < HEAD_DIM, lane1 >= HEAD_DIM)


def _group_matrix(width):
    shift = HEAD_DIM.bit_length() - 1
    return ((_iota((width, width), 0) >> shift) == (_iota((width, width), 1) >> shift)).astype(BF16)


def _prep_fwd(proj, gq, gk, bf, n_heads, name):
    s = proj.shape[0]
    aw = n_heads * HEAD_DIM
    n_pairs = n_heads // 2
    ts = min(TM, s)
    f_col = (proj.shape[1] - LANES) // LANES

    def body(q_ref, k_ref, v_ref, f_ref, gq_ref, gk_ref, bf_ref, qa_ref, ka_ref, vb_ref, carry_ref):
        @pl.when(pl.program_id(0) == 0)
        def _():
            carry_ref[...] = jnp.zeros_like(carry_ref)

        gmat = _group_matrix(aw)

        def head_norm(xv, g):
            ms = _dot2(xv * xv, gmat) * (1.0 / HEAD_DIM)
            return xv * lax.rsqrt(ms + EPS) * g

        qn = head_norm(q_ref[...], gq_ref[...]) * (HEAD_DIM ** -0.5)
        kn = head_norm(k_ref[...], gk_ref[...])
        z = f_ref[...] + bf_ref[...]
        logf = jnp.minimum(z, 0.0) - jnp.log(1.0 + jnp.exp(-jnp.abs(z)))
        tri = (_iota((ts, ts), 0) >= _iota((ts, ts), 1)).astype(BF16)
        c = _dot3_r(tri, logf) + carry_ref[...]
        carry_ref[...] = c[ts - 1:ts, :]
        terms = _split3(-c)
        row, col = _iota((LANES, LANES), 0), _iota((LANES, LANES), 1)
        ones = jnp.where(_iota((ts, LANES), 1) < 6, 1.0, 0.0).astype(BF16)
        for p in range(n_pairs):
            extra = jnp.zeros((ts, LANES), F32)
            for t, term in enumerate(terms):
                sel = ((row == 2 * p) & (col == t)) | ((row == 2 * p + 1) & (col == 3 + t))
                extra += _dot(term, sel.astype(BF16))
            lo, hi = p * PAIR, p * PAIR + LANES
            ka_ref[:, lo:hi] = kn[:, p * LANES:(p + 1) * LANES].astype(BF16)
            ka_ref[:, hi:hi + LANES] = extra.astype(BF16)
            qa_ref[:, lo:hi] = qn[:, p * LANES:(p + 1) * LANES].astype(BF16)
            qa_ref[:, hi:hi + LANES] = ones
            vb_ref[:, lo:hi] = v_ref[:, p * LANES:(p + 1) * LANES].astype(BF16)
            vb_ref[:, hi:hi + LANES] = jnp.ones((ts, LANES), BF16)

    blk = lambda j: pl.BlockSpec((ts, aw), lambda i: (i, j))
    vec = lambda w: pl.BlockSpec((1, w), lambda i: (0, 0))
    return pl.pallas_call(
        body, name=name, grid=(s // ts,),
        in_specs=[blk(0), blk(1), blk(2), pl.BlockSpec((ts, LANES), lambda i: (i, f_col)), vec(aw), vec(aw), vec(LANES)],
        out_specs=[pl.BlockSpec((ts, n_pairs * PAIR), lambda i: (i, 0))] * 3,
        out_shape=[jax.ShapeDtypeStruct((s, n_pairs * PAIR), BF16)] * 3,
        scratch_shapes=[pltpu.VMEM((1, LANES), F32)],
        compiler_params=_params("arbitrary"),
    )(proj, proj, proj, proj, gq, gk, bf)


def _prep_bwd(proj, dq, dka, drow, dcol, gq, gk, bf, n_heads, name):
    s = proj.shape[0]
    aw = n_heads * HEAD_DIM
    n_pairs = n_heads // 2
    ts = min(TM, s)
    nt = s // ts
    f_col = (proj.shape[1] - LANES) // LANES
    shift = HEAD_DIM.bit_length() - 1

    def body(q_ref, k_ref, f_ref, dq_ref, dka_ref, drow_ref, dcol_ref, gq_ref, gk_ref, bf_ref,
             dpq_ref, dpk_ref, dpf_ref, dgq_ref, dgk_ref, dbf_ref, carry_ref):
        @pl.when(pl.program_id(0) == 0)
        def _():
            carry_ref[...] = jnp.zeros_like(carry_ref)
            dgq_ref[...] = jnp.zeros_like(dgq_ref)
            dgk_ref[...] = jnp.zeros_like(dgk_ref)
            dbf_ref[...] = jnp.zeros_like(dbf_ref)

        gmat = _group_matrix(aw)

        def head_norm_bwd(xv, g, dn):
            r = lax.rsqrt(_dot2(xv * xv, gmat) * (1.0 / HEAD_DIM) + EPS)
            xh = xv * r
            dxh = dn * g
            dx = r * (dxh - xh * (_dot2(dxh * xh, gmat) * (1.0 / HEAD_DIM)))
            return dx, jnp.sum(dn * xh, axis=0, keepdims=True)

        dkav = dka_ref[...]
        dx, dg = head_norm_bwd(q_ref[...], gq_ref[...], dq_ref[...] * (HEAD_DIM ** -0.5))
        dpq_ref[...] = dx.astype(BF16)
        dgq_ref[...] += dg
        dkn = jnp.concatenate([dkav[:, p * PAIR:p * PAIR + LANES] for p in range(n_pairs)], axis=1)
        dx, dg = head_norm_bwd(k_ref[...], gk_ref[...], dkn)
        dpk_ref[...] = dx.astype(BF16)
        dgk_ref[...] += dg

        pick = (_iota((aw, LANES), 0) == (_iota((aw, LANES), 1) << shift)).astype(BF16)
        dc = _dot3(drow_ref[...], pick)
        r16, c16 = _iota((16, LANES), 0), _iota((16, LANES), 1)
        for p in range(n_pairs):
            place = ((r16 < 2) & (c16 == 2 * p + r16)).astype(BF16)
            for term in _split3(dcol_ref[p]):
                dc -= _dot_tn(term, place)
        triu = (_iota((ts, ts), 0) <= _iota((ts, ts), 1)).astype(BF16)
        dlogf = _dot3_r(triu, dc) + carry_ref[...]
        carry_ref[...] = dlogf[0:1, :]
        z = f_ref[...] + bf_ref[...]
        dz = dlogf * (1.0 / (1.0 + jnp.exp(z)))
        dpf_ref[...] = dz.astype(BF16)
        dbf_ref[...] += jnp.sum(dz, axis=0, keepdims=True)

    rev = lambda w, j: pl.BlockSpec((ts, w), lambda i: (nt - 1 - i, j))
    vec = lambda w: pl.BlockSpec((1, w), lambda i: (0, 0))
    return pl.pallas_call(
        body, name=name, grid=(nt,),
        in_specs=[rev(aw, 0), rev(aw, 1), rev(LANES, f_col), rev(aw, 0), rev(n_pairs * PAIR, 0), rev(aw, 0),
                  pl.BlockSpec((n_pairs, 16, ts), lambda i: (0, 0, nt - 1 - i)), vec(aw), vec(aw), vec(LANES)],
        out_specs=[rev(aw, 0), rev(aw, 0), rev(LANES, 0), vec(aw), vec(aw), vec(LANES)],
        out_shape=[jax.ShapeDtypeStruct((s, aw), BF16), jax.ShapeDtypeStruct((s, aw), BF16), jax.ShapeDtypeStruct((s, LANES), BF16),
                   jax.ShapeDtypeStruct((1, aw), F32), jax.ShapeDtypeStruct((1, aw), F32), jax.ShapeDtypeStruct((1, LANES), F32)],
        scratch_shapes=[pltpu.VMEM((1, LANES), F32)],
        compiler_params=_params("arbitrary"),
    )(proj, proj, proj, dq, dka, drow, dcol, gq, gk, bf)


def _attn_fwd(qa, ka, vb, n_heads, mix_width, name):
    s = qa.shape[0]
    aw = n_heads * HEAD_DIM
    n_pairs = n_heads // 2
    tq = min(2 * TQ, s)

    def body(q_ref, k_ref, v_ref, o_ref, lse_ref, o16_ref):
        i = pl.program_id(1)
        qmasks, omasks = _head_masks()
        qv = q_ref[...]
        causal = _iota((tq, tq), 1) <= _iota((tq, tq), 0)
        qhs = [jnp.where(qmasks[h], qv, jnp.zeros_like(qv)) for h in range(2)]

        def scores(j):
            kv = k_ref[pl.ds(pl.multiple_of(j * tq, tq), tq), :]
            return tuple(_dot_nt(qhs[h], kv) for h in range(2))

        def update(j, state, scs, masked):
            vv = v_ref[pl.ds(pl.multiple_of(j * tq, tq), tq), :]
            out = []
            for h in range(2):
                m, acc = state[h]
                sc = jnp.where(causal, scs[h], NEG) if masked else scs[h]
                m_new = jnp.maximum(m, jnp.max(sc, axis=1, keepdims=True))
                p = jnp.exp(sc - m_new).astype(BF16)
                out.append((m_new, jnp.exp(m - m_new) * acc + _dot(p, vv)))
            return tuple(out)

        def body(j, state):
            return update(j, state, scores(j), False)

        init = ((jnp.full((tq, 1), NEG, F32), jnp.zeros((tq, PAIR), F32)),) * 2
        state = lax.fori_loop(0, i, body, init)
        res = []
        for m, acc in update(i, state, scores(i), True):
            l = acc[:, LANES:LANES + 1]
            res.append((acc[:, :LANES] * (1.0 / l), m + jnp.log(l)))
        out = jnp.where(omasks[0], res[0][0], res[1][0])
        o_ref[...] = out
        o16_ref[...] = out.astype(BF16)
        lse_ref[...] = jnp.where(omasks[0], res[0][1], res[1][1])

    return pl.pallas_call(
        body, name=name, grid=(n_pairs, s // tq),
        in_specs=[pl.BlockSpec((tq, PAIR), lambda p, i: (i, p)), pl.BlockSpec((s, PAIR), lambda p, i: (0, p)),
                  pl.BlockSpec((s, PAIR), lambda p, i: (0, p))],
        out_specs=[pl.BlockSpec((tq, LANES), lambda p, i: (i, p))] * 3,
        out_shape=[jax.ShapeDtypeStruct((s, aw), F32)] * 2 + [jax.ShapeDtypeStruct((s, mix_width), BF16)],
        compiler_params=_params("parallel", "parallel"),
    )(qa, ka, vb)


def _attn_bwd(qa, ka, vb, o, lse, dmixed, n_heads, name):
    s = qa.shape[0]
    aw = n_heads * HEAD_DIM
    n_pairs = n_heads // 2
    tq = min(2 * TQ, s)
    nq = s // tq

    def body(q_ref, k_ref, v_ref, o_ref, lse_ref, do_ref, dq_ref, dka_ref, dv_ref, drow_ref, dcol_ref, delta_ref):
        j = pl.program_id(1)
        qmasks, omasks = _head_masks()

        @pl.when(j == 0)
        def _():
            dq_ref[...] = jnp.zeros_like(dq_ref)
            drow_ref[...] = jnp.zeros_like(drow_ref)
            for c in range(nq):
                rows = slice(c * tq, (c + 1) * tq)
                prod = do_ref[rows, :] * o_ref[rows, :]
                da = jnp.sum(jnp.where(omasks[0], prod, 0.0), axis=1, keepdims=True)
                db = jnp.sum(jnp.where(omasks[1], prod, 0.0), axis=1, keepdims=True)
                delta_ref[rows, :] = jnp.where(omasks[0], da, db)

        dka_ref[...] = jnp.zeros_like(dka_ref)
        dv_ref[...] = jnp.zeros_like(dv_ref)
        dcol_ref[...] = jnp.zeros_like(dcol_ref)
        kv = k_ref[...]
        kk = kv[:, :LANES]
        vv = v_ref[...]
        causal = _iota((tq, tq), 1) <= _iota((tq, tq), 0)

        def step(i, masked):
            off = pl.multiple_of(i * tq, tq)
            qv = q_ref[pl.ds(off, tq), :]
            dov = do_ref[pl.ds(off, tq), :]
            lsev = lse_ref[pl.ds(off, tq), :]
            dlv = delta_ref[pl.ds(off, tq), :]
            for h in range(2):
                qh = jnp.where(qmasks[h], qv, jnp.zeros_like(qv))
                doh = jnp.where(omasks[h], dov, 0.0).astype(BF16)
                lane = h * HEAD_DIM
                sc = _dot_nt(qh, kv)
                if masked:
                    sc = jnp.where(causal, sc, NEG)
                p = jnp.exp(sc - lsev[:, lane:lane + 1])
                dv_ref[...] += _dot_tn(p.astype(BF16), doh)
                dp = _dot_nt(doh, vv)
                dsf = p * (dp - dlv[:, lane:lane + 1])
                drow_ref[pl.ds(off, tq), :] += jnp.where(omasks[h], jnp.sum(dsf, axis=1, keepdims=True), 0.0)
                dcol_ref[0, h:h + 1, :] += jnp.sum(dsf, axis=0, keepdims=True)
                ds = dsf.astype(BF16)
                dka_ref[...] += _dot_tn(ds, qh)
                dq_ref[pl.ds(off, tq), :] += jnp.where(omasks[h], _dot(ds, kk), 0.0)

        step(j, True)

        def loop_body(i, carry):
            step(i, False)
            return carry

        lax.fori_loop(j + 1, nq, loop_body, 0)

    full = lambda w: pl.BlockSpec((s, w), lambda p, j: (0, p))
    blk = lambda w: pl.BlockSpec((tq, w), lambda p, j: (j, p))
    return pl.pallas_call(
        body, name=name, grid=(n_pairs, nq),
        in_specs=[full(PAIR), blk(PAIR), pl.BlockSpec((tq, LANES), lambda p, j: (j, 2 * p)), full(LANES), full(LANES), full(LANES)],
        out_specs=[full(LANES), blk(PAIR), blk(LANES), full(LANES), pl.BlockSpec((1, 16, tq), lambda p, j: (p, 0, j))],
        out_shape=[jax.ShapeDtypeStruct((s, aw), F32), jax.ShapeDtypeStruct((s, n_pairs * PAIR), F32),
                   jax.ShapeDtypeStruct((s, aw), F32), jax.ShapeDtypeStruct((s, aw), F32),
                   jax.ShapeDtypeStruct((n_pairs, 16, s), F32)],
        scratch_shapes=[pltpu.VMEM((s, LANES), F32)],
        compiler_params=_params("parallel", "arbitrary"),
    )(qa, ka, vb, o, lse, dmixed)


def _conv_fwd(proj, w32, bias, n_ch, a_col, g_col, name):
    s = proj.shape[0]
    rows = min(CONV_ROWS, s)

    def body(a_ref, g_ref, w_ref, b_ref, y_ref, pad_ref):
        pad_ref[0:CONV_PAD, :] = jnp.zeros((CONV_PAD, LANES), F32)
        pad_ref[CONV_PAD:CONV_PAD + s, :] = a_ref[...] * _sigmoid(g_ref[...])
        wv = w_ref[...]
        for c in range(s // rows):
            acc = jnp.broadcast_to(b_ref[...], (rows, LANES))
            for t in range(CONV_TAPS):
                start = c * rows + CONV_PAD - (CONV_TAPS - 1) + t
                acc = acc + wv[t:t + 1, :] * pad_ref[start:start + rows, :]
            y_ref[c * rows:(c + 1) * rows, :] = acc

    col = lambda j0: pl.BlockSpec((s, LANES), lambda c: (0, j0 + c))
    return pl.pallas_call(
        body, name=name, grid=(n_ch // LANES,),
        in_specs=[col(a_col), col(g_col), pl.BlockSpec((CONV_PAD, LANES), lambda c: (0, c)), pl.BlockSpec((1, LANES), lambda c: (0, c))],
        out_specs=pl.BlockSpec((s, LANES), lambda c: (0, c)),
        out_shape=jax.ShapeDtypeStruct((s, n_ch), F32),
        scratch_shapes=[pltpu.VMEM((s + CONV_PAD, LANES), F32)],
        compiler_params=_params("parallel"),
    )(proj, proj, w32, bias)


def _conv_bwd(proj, w32, dy, n_ch, a_col, g_col, name):
    s = proj.shape[0]
    rows = min(CONV_ROWS, s)
    sub = 8

    def fold(x):
        acc = x[0:sub, :]
        for r in range(1, rows // sub):
            acc = acc + x[r * sub:(r + 1) * sub, :]
        return acc

    def body(a_ref, g_ref, w_ref, dy_ref, da_ref, dg_ref, dw_ref, padh_ref, padd_ref):
        sg = _sigmoid(g_ref[...])
        padh_ref[0:CONV_PAD, :] = jnp.zeros((CONV_PAD, LANES), F32)
        padh_ref[CONV_PAD:CONV_PAD + s, :] = a_ref[...] * sg
        padd_ref[0:s, :] = dy_ref[...]
        padd_ref[s:s + CONV_PAD, :] = jnp.zeros((CONV_PAD, LANES), F32)
        wv = w_ref[...]
        dw = [jnp.zeros((sub, LANES), F32) for _ in range(CONV_TAPS + 1)]
        for c in range(s // rows):
            r0 = c * rows
            acc = jnp.zeros((rows, LANES), F32)
            dyc = dy_ref[r0:r0 + rows, :]
            for t in range(CONV_TAPS):
                back = r0 + (CONV_TAPS - 1) - t
                acc = acc + wv[t:t + 1, :] * padd_ref[back:back + rows, :]
                start = r0 + CONV_PAD - (CONV_TAPS - 1) + t
                dw[t] = dw[t] + fold(dyc * padh_ref[start:start + rows, :])
            dw[CONV_TAPS] = dw[CONV_TAPS] + fold(dyc)
            av = a_ref[r0:r0 + rows, :]
            sgc = _sigmoid(g_ref[r0:r0 + rows, :])
            da_ref[r0:r0 + rows, :] = (acc * sgc).astype(BF16)
            dg_ref[r0:r0 + rows, :] = (acc * av * sgc * (1.0 - sgc)).astype(BF16)
        for t in range(CONV_TAPS + 1):
            dw_ref[t:t + 1, :] = jnp.sum(dw[t], axis=0, keepdims=True)

    col = lambda j0: pl.BlockSpec((s, LANES), lambda c: (0, j0 + c))
    wspec = pl.BlockSpec((CONV_PAD, LANES), lambda c: (0, c))
    return pl.pallas_call(
        body, name=name, grid=(n_ch // LANES,),
        in_specs=[col(a_col), col(g_col), wspec, col(0)],
        out_specs=[col(0), col(0), wspec],
        out_shape=[jax.ShapeDtypeStruct((s, n_ch), BF16), jax.ShapeDtypeStruct((s, n_ch), BF16),
                   jax.ShapeDtypeStruct((CONV_PAD, n_ch), F32)],
        scratch_shapes=[pltpu.VMEM((s + CONV_PAD, LANES), F32), pltpu.VMEM((s + CONV_PAD, LANES), F32)],
        compiler_params=_params("parallel"),
    )(proj, proj, w32, dy)


def _my_place():
    return lax.axis_index("x"), lax.axis_index("y"), lax.axis_index("c")


def _flip(place, k):
    x, y, c = place
    return (1 - x if k & 4 else x, 1 - y if k & 2 else y, 1 - c if k & 1 else c)


def _dev_id(place):
    return 4 * place[0] + 2 * place[1] + place[2]


def _wait_all(ref, send_sem, recv_sem, place):
    pltpu.make_async_remote_copy(src_ref=ref, dst_ref=ref, send_sem=send_sem, recv_sem=recv_sem,
                                 device_id=place, device_id_type=MESH).wait()


def _window(kind, ref, dev, n):
    if kind == "slot":
        return ref.at[dev]
    if kind == "rows":
        return ref.at[pl.ds(pl.multiple_of(dev * n, n), n), :]
    return ref.at[:, pl.ds(pl.multiple_of(dev * n, n), n)]


def _hbm(x):
    return pltpu.with_memory_space_constraint(x, pltpu.HBM)


_EFFECT = pltpu.SideEffectType.DATAFLOW_SIDE_EFFECTING


def _exchange_start(srcs, lands, specs, groups, name):
    n = len(srcs)
    n_g = len(groups)

    def body(*refs):
        src_refs, land_refs = refs[:n], refs[n:2 * n]
        sems = refs[2 * n:2 * n + 2 * n_g]
        token = refs[-1]
        place = _my_place()
        me = _dev_id(place)
        for g, units in enumerate(groups):
            for j, u in enumerate(units):
                mode, kind, cnt = specs[u]
                for k in range(N_DEV):
                    peer = _flip(place, k)
                    if mode == "gather":
                        src, dst = src_refs[u], _window(kind, land_refs[u], me, cnt)
                    else:
                        src, dst = _window(kind, src_refs[u], _dev_id(peer), cnt), land_refs[u].at[k]
                    pltpu.make_async_remote_copy(src_ref=src, dst_ref=dst, send_sem=sems[2 * g].at[j], recv_sem=sems[2 * g + 1].at[j],
                                                 device_id=peer, device_id_type=MESH).start()
        token[...] = jnp.zeros_like(token)

    hbm = pl.BlockSpec(memory_space=pltpu.HBM)
    sem = pl.BlockSpec(memory_space=pltpu.SEMAPHORE)
    out_shape = [pltpu.SemaphoreType.DMA((len(units),)) for units in groups for _ in range(2)]
    out_shape += [pltpu.HBM(x.shape, x.dtype) for x in lands] + [jax.ShapeDtypeStruct((8, LANES), F32)]
    outs = pl.pallas_call(
        body, name=name, out_shape=out_shape,
        in_specs=[hbm] * (2 * n), out_specs=[sem] * (2 * n_g) + [hbm] * n + [pl.BlockSpec(memory_space=pltpu.VMEM)],
        input_output_aliases={n + u: 2 * n_g + u for u in range(n)},
        compiler_params=pltpu.CompilerParams(has_side_effects=_EFFECT),
    )(*[_hbm(x) for x in srcs], *[_hbm(x) for x in lands])
    sem_pairs = [(outs[2 * g], outs[2 * g + 1]) for g in range(n_g)]
    return sem_pairs, list(outs[2 * n_g:2 * n_g + n]), outs[-1]


def _exchange_wait(srcs, lands, specs, sem_pair, after, name):
    n = len(lands)

    def body(*refs):
        land_refs = refs[n:2 * n]
        send_sems, recv_sems = refs[2 * n], refs[2 * n + 1]
        place = _my_place()
        for u in range(n):
            _wait_all(land_refs[u], send_sems.at[u], recv_sems.at[u], place)

    hbm = pl.BlockSpec(memory_space=pltpu.HBM)
    sem = pl.BlockSpec(memory_space=pltpu.SEMAPHORE)
    outs = pl.pallas_call(
        body, name=name, out_shape=[pltpu.HBM(x.shape, x.dtype) for x in lands],
        in_specs=[hbm] * (2 * n) + [sem, sem, pl.BlockSpec(memory_space=pl.ANY)], out_specs=[hbm] * n,
        input_output_aliases={n + u: u for u in range(n)},
        compiler_params=pltpu.CompilerParams(has_side_effects=_EFFECT),
    )(*[_hbm(x) for x in srcs], *lands, sem_pair[0], sem_pair[1], after)
    return list(outs)


def _sum_devices(parts, name):
    _, r, w = parts.shape

    def body(p_ref, out_ref):
        acc = p_ref[0]
        for d in range(1, N_DEV):
            acc = acc + p_ref[d]
        out_ref[...] = acc

    return pl.pallas_call(
        body, name=name, out_shape=jax.ShapeDtypeStruct((r, w), F32),
        in_specs=[pl.BlockSpec(memory_space=pltpu.VMEM)], out_specs=pl.BlockSpec(memory_space=pltpu.VMEM),
    )(parts)


def _adamw_math(w, m, v, g):
    m_new = ADAM_B1 * m + (1.0 - ADAM_B1) * g
    v_new = ADAM_B2 * v + (1.0 - ADAM_B2) * (g * g)
    m_hat = m_new / (1.0 - ADAM_B1 ** ADAM_STEP)
    v_hat = v_new / (1.0 - ADAM_B2 ** ADAM_STEP)
    return -ADAM_LR * (m_hat / (jnp.sqrt(v_hat) + ADAM_EPS) + ADAM_WD * w), m_new, v_new


def _adamw(w, m, v, g, name):
    rows = w.shape[0]
    tr = min(FLAT_ROWS, rows)
    assert rows % tr == 0, (name, rows)

    def body(w_ref, m_ref, v_ref, g_ref, d_out, m_out, v_out):
        d_out[...], m_out[...], v_out[...] = _adamw_math(w_ref[...], m_ref[...], v_ref[...], g_ref[...])

    flat = pl.BlockSpec((tr, LANES), lambda i: (i, 0))
    return pl.pallas_call(
        body, name=name, grid=(rows // tr,), in_specs=[flat] * 4, out_specs=[flat] * 3,
        out_shape=[jax.ShapeDtypeStruct((rows, LANES), F32)] * 3,
        compiler_params=_params("parallel"),
    )(w, m, v, g)


def _adamw_shard(w, m, v, recv, layer, prev, name):
    depth, a, b = w.shape
    ta = min(256, a)
    assert a % ta == 0

    def body(w_ref, m_ref, v_ref, r_ref, *rest):
        g_out, d_out, m_out, v_out = rest[-4:]
        g = r_ref[0].astype(F32)
        for k in range(1, N_DEV):
            g = g + r_ref[k].astype(F32)
        g_out[0] = g
        d_out[0], m_out[0], v_out[0] = _adamw_math(w_ref[0], m_ref[0], v_ref[0], g)

    lay = pl.BlockSpec((1, ta, b), lambda i: (layer, i, 0))
    in_specs = [lay] * 3 + [pl.BlockSpec((N_DEV, ta, b), lambda i: (0, i, 0))]
    args = [w, m, v, recv]
    aliases = {}
    if prev is not None:
        in_specs += [pl.BlockSpec(memory_space=pl.ANY)] * 4
        args += list(prev)
        aliases = {4 + i: i for i in range(4)}
    return pl.pallas_call(
        body, name=name, grid=(a // ta,), in_specs=in_specs, out_specs=[lay] * 4,
        out_shape=[jax.ShapeDtypeStruct(w.shape, F32)] * 4, input_output_aliases=aliases,
        compiler_params=_params("parallel"),
    )(*args)


def _round_up(n, mult):
    return (n + mult - 1) // mult * mult


def _flatten(parts, row_mult):
    flat = jnp.concatenate([p.reshape(-1) for p in parts])
    rows = _round_up(-(-flat.shape[0] // LANES), row_mult)
    return jnp.pad(flat, (0, rows * LANES - flat.shape[0])).reshape(rows, LANES)


def _unflatten(flat, shapes):
    flat = flat.reshape(-1)
    out, off = [], 0
    for shp in shapes:
        n = 1
        for dim in shp:
            n *= dim
        out.append(flat[off:off + n].reshape(shp))
        off += n
    return out


def kernel(x, norm1_g, w_in, b_f, q_norm_g, k_norm_g, conv_w, conv_b, conv_ln_g, conv_ln_b, w_o, norm2_g, w_mlp_in, w_mlp_out, loss_target, m_norm1_g, m_w_in, m_b_f, m_q_norm_g, m_k_norm_g, m_conv_w, m_conv_b, m_conv_ln_g, m_conv_ln_b, m_w_o, m_norm2_g, m_w_mlp_in, m_w_mlp_out, v_norm1_g, v_w_in, v_b_f, v_q_norm_g, v_k_norm_g, v_conv_w, v_conv_b, v_conv_ln_g, v_conv_ln_b, v_w_o, v_norm2_g, v_w_mlp_in, v_w_mlp_out):
    depth, d_model, n_in_loc = w_in.shape
    seq = x.shape[1]
    n_heads = b_f.shape[1]
    aw = n_heads * HEAD_DIM
    cc = conv_b.shape[1]
    n_in = n_in_loc * N_DEV
    o_f = 3 * aw
    n_all = 3 * aw + 2 * cc + LANES
    assert n_in == 3 * aw + n_heads + 2 * cc and aw + cc == d_model and n_heads % 2 == 0
    assert aw % LANES == 0 and cc % LANES == 0 and x.shape[0] == 1
    me = 4 * lax.axis_index("x") + 2 * lax.axis_index("y") + lax.axis_index("c")

    d_ff = w_mlp_in.shape[2] * N_DEV

    r_o, f_1, f_2 = w_o.shape[1], w_mlp_in.shape[2], w_mlp_out.shape[1]

    ag_src, ag_land, ag_spec = [], [], []
    for l in range(depth):
        ag_src += [w_in[l].astype(BF16), w_o[l].astype(BF16), w_mlp_in[l].astype(BF16), w_mlp_out[l].astype(BF16)]
        ag_land += [(N_DEV, d_model, n_in_loc), (N_DEV * r_o, d_model), (d_model, N_DEV * f_1), (N_DEV * f_2, d_model)]
        ag_spec += [("gather", "slot", 1), ("gather", "rows", r_o), ("gather", "cols", f_1), ("gather", "rows", f_2)]
    ag_src.append(jnp.stack(_split3(conv_w)))
    ag_land.append((N_DEV, 3) + conv_w.shape)
    ag_spec.append(("gather", "slot", 1))
    ag_groups = [grp for l in range(depth) for grp in ([4 * l] + ([4 * depth] if l == 0 else []), [4 * l + 1], [4 * l + 2, 4 * l + 3])]
    ag_land = [lax.empty(shp, BF16) for shp in ag_land]
    ag_sems, ag_land, ag_token = _exchange_start(ag_src, ag_land, ag_spec, ag_groups, "gather_start")

    def gathered(g, after):
        units = ag_groups[g]
        return _exchange_wait([ag_src[u] for u in units], [ag_land[u] for u in units], [ag_spec[u] for u in units],
                              ag_sems[g], after, f"gather_wait_{g}")

    def to_all(w):
        return jnp.concatenate([w[:, :o_f], w[:, o_f + n_heads:], w[:, o_f:o_f + n_heads],
                                jnp.zeros((w.shape[0], LANES - n_heads), w.dtype)], axis=1)

    def from_all(w):
        return jnp.concatenate([w[:, :o_f], w[:, n_all - LANES:n_all - LANES + n_heads], w[:, o_f:n_all - LANES]], axis=1)

    def whole_in(lin):
        return to_all(jnp.moveaxis(lin, 0, 1).reshape(d_model, n_in))

    def row(p, l, width=None):
        v = p[l].reshape(1, -1)
        return v if width is None else jnp.pad(v, ((0, 0), (0, width - v.shape[1])))

    a_col, g_col = 3 * aw // LANES, (3 * aw + cc) // LANES

    gq = [jnp.tile(row(q_norm_g, l), (1, n_heads)) for l in range(depth)]
    gk = [jnp.tile(row(k_norm_g, l), (1, n_heads)) for l in range(depth)]
    bfp = [row(b_f, l, LANES) for l in range(depth)]
    add_res = lambda acc, res: (acc + res,)
    w_all, w_out, w_ff1, w_ff2 = [None] * depth, [None] * depth, [None] * depth, [None] * depth

    h = x[0]
    saved = []
    for l in range(depth):
        if l == 0:
            lin, lc = gathered(0, h)
            lc = lc.astype(F32)
            conv_full = jnp.moveaxis(lc[:, 0] + lc[:, 1] + lc[:, 2], 0, 2).reshape(depth, CONV_TAPS, cc)
            w32 = [jnp.pad(conv_full[i], ((0, CONV_PAD - CONV_TAPS), (0, 0))) for i in range(depth)]
        else:
            lin, = gathered(3 * l, h)
        w_all[l] = whole_in(lin)
        u1, proj = _rms_matmul(h, row(norm1_g, l), w_all[l], out_dtypes=(F32,), name=f"mm_in_{l}")
        qa, ka, vb = _prep_fwd(proj, gq[l], gk[l], bfp[l], n_heads, f"prep_fwd_{l}")
        att, lse, mixed = _attn_fwd(qa, ka, vb, n_heads, aw + cc, f"attn_fwd_{l}")
        yc = _conv_fwd(proj, w32[l], row(conv_b, l), cc, a_col, g_col, f"conv_fwd_{l}")
        mixed = _ln_silu_fwd(yc, row(conv_ln_g, l), row(conv_ln_b, l), mixed, f"ln_silu_fwd_{l}")
        w_out[l], = gathered(3 * l + 1, mixed)
        x1 = _matmul(mixed, w_out[l], mode="nn", out_dtypes=(F32,), name=f"mm_o_{l}", epilogue=add_res, extras=(h,))
        w_ff1[l], w_ff2[l] = gathered(3 * l + 2, x1)
        u2, r, a = _rms_matmul(x1, row(norm2_g, l), w_ff1[l], out_dtypes=(BF16, BF16), name=f"mm_ff1_{l}", tm=256,
                               epilogue=lambda acc: (jnp.maximum(acc, 0.0), jnp.square(jnp.maximum(acc, 0.0))))
        if l < depth - 1:
            x2 = _matmul(a, w_ff2[l], mode="nn", out_dtypes=(F32,), name=f"mm_ff2_{l}", epilogue=add_res, extras=(x1,), tk=d_ff)
        else:
            x2 = None
            dh, dh16, sq = _matmul_loss(a, w_ff2[l], x1, loss_target[0], f"mm_ff2_loss_{l}")
        saved.append(dict(x_in=h, u1=u1, proj=proj, qa=qa, ka=ka, vb=vb, att=att, lse=lse, yc=yc, mixed=mixed,
                          x1=x1, u2=u2, r=r, a=a))
        h = x2

    loss = lax.psum(0.5 * jnp.sum(sq) / d_model, ("x", "y", "c"))

    g_in, g_o, g_1, g_2 = [None] * depth, [None] * depth, [None] * depth, [None] * depth
    gs = {n: [None] * depth for n in ("norm1", "bf", "qn", "kn", "convw", "convb", "lng", "lnb", "norm2")}
    scattering = {}

    def scatter_start(stage, l, srcs, specs, slabs):
        lands = [lax.empty((N_DEV,) + shp, BF16) for shp in slabs]
        sems, lands, token = _exchange_start(srcs, lands, specs, [list(range(len(srcs)))], f"scatter_start_{stage}_{l}")
        scattering[(stage, l)] = (srcs, lands, specs, sems[0])
        return token[0, 0]

    for l in reversed(range(depth)):
        sv = saved[l]
        dh1 = _matmul(dh16, w_ff2[l], mode="nt", out_dtypes=(BF16,), name=f"mm_dff2_{l}", tm=256, tn=d_ff,
                      epilogue=lambda acc, rr: (acc * (2.0 * rr.astype(F32)),), extras=(sv["r"],))
        g_2[l] = _matmul(sv["a"], dh16, mode="tn", out_dtypes=(BF16,), name=f"mm_dw2_{l}", tm=1024, tk=seq)
        g_1[l] = _matmul(sv["u2"], dh1, mode="tn", out_dtypes=(BF16,), name=f"mm_dw1_{l}", tm=d_model, tk=seq)
        tok = scatter_start("ff", l, [g_1[l], g_2[l]], [("scatter", "cols", f_1), ("scatter", "rows", f_2)],
                            [(d_model, f_1), (f_2, d_model)])
        dx1, dx16, gs["norm2"][l] = _matmul_rms_bwd(dh1, w_ff1[l], sv["x1"], row(norm2_g, l) + tok, dh, f"mm_du2_rms_{l}")

        dmixed = _matmul(dx16, w_out[l], mode="nt", out_dtypes=(F32,), name=f"mm_dmixed_{l}")
        g_o[l] = _matmul(sv["mixed"], dx16, mode="tn", out_dtypes=(BF16,), name=f"mm_dwo_{l}", tm=1024, tk=seq // 2)
        tok = scatter_start("o", l, [g_o[l]], [("scatter", "rows", r_o)], [(r_o, d_model)])
        dyc, gs["lng"][l], gs["lnb"][l] = _ln_silu_bwd(sv["yc"], row(conv_ln_g, l) + tok, row(conv_ln_b, l), dmixed, f"ln_silu_bwd_{l}")
        dpa, dpg, dw32 = _conv_bwd(sv["proj"], w32[l], dyc, cc, a_col, g_col, f"conv_bwd_{l}")
        gs["convw"][l], gs["convb"][l] = dw32[:CONV_TAPS], dw32[CONV_TAPS:CONV_TAPS + 1]
        dq, dka, dv, drow, dcol = _attn_bwd(sv["qa"], sv["ka"], sv["vb"], sv["att"], sv["lse"], dmixed, n_heads, f"attn_bwd_{l}")
        dpq, dpk, dpf, dgq, dgk, dbf = _prep_bwd(sv["proj"], dq, dka, drow, dcol, gq[l], gk[l], bfp[l], n_heads, f"prep_bwd_{l}")
        gs["qn"][l] = dgq.reshape(n_heads, HEAD_DIM).sum(axis=0)
        gs["kn"][l] = dgk.reshape(n_heads, HEAD_DIM).sum(axis=0)
        gs["bf"][l] = dbf[0, :n_heads]
        dproj = jnp.concatenate([dpq, dpk, dv.astype(BF16), dpa, dpg, dpf], axis=1)
        dwall = _matmul(sv["u1"], dproj, mode="tn", out_dtypes=(BF16,), name=f"mm_dwall_{l}", tm=d_model, tn=n_all // 3, tk=seq)
        g_in[l] = jnp.moveaxis(from_all(dwall).reshape(d_model, N_DEV, n_in_loc), 1, 0)
        tok = scatter_start("in", l, [g_in[l]], [("scatter", "slot", 1)], [(d_model, n_in_loc)])
        dh, dh16, gs["norm1"][l] = _matmul_rms_bwd(dproj, w_all[l], sv["x_in"], row(norm1_g, l) + tok, dx1, f"mm_du1_rms_{l}")
    grad_x = dh[None]

    small_g = [jnp.stack(gs[n]).reshape(shp) for n, shp in (
        ("norm1", norm1_g.shape), ("bf", b_f.shape), ("qn", q_norm_g.shape), ("kn", k_norm_g.shape),
        ("convw", (depth, CONV_TAPS, cc)), ("convb", conv_b.shape), ("lng", conv_ln_g.shape), ("lnb", conv_ln_b.shape),
        ("norm2", norm2_g.shape))]
    small_shapes = [g.shape for g in small_g]
    small_flat = _flatten(small_g, 8)
    small_spec = [("gather", "slot", 1)]
    small_sems, small_land, _ = _exchange_start([small_flat], [lax.empty((N_DEV,) + small_flat.shape, F32)], small_spec, [[0]],
                                                "small_grads_start")

    def landed(stage, l, after):
        srcs, lands, specs, sems = scattering[(stage, l)]
        return _exchange_wait(srcs, lands, specs, sems, after, f"scatter_wait_{stage}_{l}")

    def adamw_layers(kd, w, m, v, recv):
        outs = None
        for l in reversed(range(depth)):
            outs = _adamw_shard(w, m, v, recv[l], l, outs, f"adamw_{kd}_{l}")
        return outs

    recv_1, recv_2, recv_in, recv_o = [None] * depth, [None] * depth, [None] * depth, [None] * depth
    for l in reversed(range(depth)):
        recv_1[l], recv_2[l] = landed("ff", l, dh)
    out_1 = adamw_layers("1", w_mlp_in, m_w_mlp_in, v_w_mlp_in, recv_1)
    out_2 = adamw_layers("2", w_mlp_out, m_w_mlp_out, v_w_mlp_out, recv_2)
    for l in reversed(range(depth)):
        recv_o[l], = landed("o", l, out_2[1])
    out_o = adamw_layers("o", w_o, m_w_o, v_w_o, recv_o)
    for l in reversed(range(depth)):
        recv_in[l], = landed("in", l, out_o[1])
    out_in = adamw_layers("in", w_in, m_w_in, v_w_in, recv_in)
    big_out = [[outs[kind] for outs in (out_in, out_o, out_1, out_2)] for kind in range(4)]

    small_parts, = _exchange_wait([small_flat], small_land, small_spec, small_sems[0], out_in[1], "small_grads_wait")
    small_g = _unflatten(_sum_devices(small_parts, "small_grads_sum"), small_shapes)
    cw = conv_w.shape[2]
    small_g[4] = lax.dynamic_slice_in_dim(small_g[4], me * cw, cw, axis=2)
    small = (norm1_g, b_f, q_norm_g, k_norm_g, conv_w, conv_b, conv_ln_g, conv_ln_b, norm2_g)
    small_m = (m_norm1_g, m_b_f, m_q_norm_g, m_k_norm_g, m_conv_w, m_conv_b, m_conv_ln_g, m_conv_ln_b, m_norm2_g)
    small_v = (v_norm1_g, v_b_f, v_q_norm_g, v_k_norm_g, v_conv_w, v_conv_b, v_conv_ln_g, v_conv_ln_b, v_norm2_g)
    small_out = _adamw(_flatten(small, 8), _flatten(small_m, 8), _flatten(small_v, 8), _flatten(small_g, 8), "adamw_small")
    small_out = [small_g] + [_unflatten(o, [w.shape for w in small]) for o in small_out]

    def group(kind):
        s_, b_ = small_out[kind], big_out[kind]
        return [s_[0], b_[0], s_[1], s_[2], s_[3], s_[4], s_[5], s_[6], s_[7], b_[1], s_[8], b_[2], b_[3]]

    return (loss, grad_x, *group(0), *group(1), *group(2), *group(3))
```

```python
import functools

import jax
import jax.numpy as jnp
from jax import lax
from jax.experimental import pallas as pl
from jax.experimental.pallas import tpu as pltpu

F32 = jnp.float32
BF16 = jnp.bfloat16

EPS = 1e-6
HEAD_DIM = 64
LANES = 128
PAIR = 2 * LANES
N_DEV = 8
CONV_TAPS = 31
CONV_PAD = 32
NEG = -1e30

ADAM_LR = 0.001
ADAM_B1 = 0.9
ADAM_B2 = 0.999
ADAM_EPS = 1e-08
ADAM_WD = 0.01
ADAM_STEP = 10

TM = 512
TQ = 512
CONV_ROWS = 128
FLAT_ROWS = 1024
MESH = pl.DeviceIdType.MESH


def _params(*sem):
    return pltpu.CompilerParams(dimension_semantics=sem, vmem_limit_bytes=56 * 1024 * 1024)


def _split3(x):
    hi = x.astype(BF16)
    r1 = x - hi.astype(F32)
    mid = r1.astype(BF16)
    lo = (r1 - mid.astype(F32)).astype(BF16)
    return hi, mid, lo


def _dot(a, b):
    return jnp.dot(a, b, preferred_element_type=F32)


def _dot_nt(a, b):
    return lax.dot_general(a, b, (((1,), (1,)), ((), ())), preferred_element_type=F32)


def _dot_tn(a, b):
    return lax.dot_general(a, b, (((0,), (0,)), ((), ())), preferred_element_type=F32)


def _dot3(x, mat):
    hi, mid, lo = _split3(x)
    return _dot(hi, mat) + _dot(mid, mat) + _dot(lo, mat)


def _dot2(x, mat):
    hi = x.astype(BF16)
    lo = (x - hi.astype(F32)).astype(BF16)
    return _dot(hi, mat) + _dot(lo, mat)


def _dot3_r(mat, x):
    hi, mid, lo = _split3(x)
    return _dot(mat, hi) + _dot(mat, mid) + _dot(mat, lo)


def _iota(shape, dim):
    return lax.broadcasted_iota(jnp.int32, shape, dim)


def _sigmoid(x):
    return 1.0 / (1.0 + jnp.exp(-x))


def _matmul(a, b, *, mode, out_dtypes, name, epilogue=None, extras=(), tm=TM, tn=1024, tk=1024):
    if mode == "nn":
        (m, k), (k2, n) = a.shape, b.shape
    elif mode == "nt":
        (m, k), (n, k2) = a.shape, b.shape
    else:
        (k, m), (k2, n) = a.shape, b.shape
    assert k == k2, (name, a.shape, b.shape)
    tm, tn, tk = min(tm, m), min(tn, n), min(tk, k)
    assert m % tm == 0 and n % tn == 0 and k % tk == 0, (name, m, n, k, tm, tn, tk)
    nk = k // tk
    a_mode = dict(pipeline_mode=pl.Buffered(1)) if (m == tm and nk == 1) else {}
    b_mode = dict(pipeline_mode=pl.Buffered(1)) if (n == tn and nk == 1) else {}
    if mode == "tn":
        a_spec = pl.BlockSpec((tk, tm), lambda i, j, kk: (kk, i), **a_mode)
    else:
        a_spec = pl.BlockSpec((tm, tk), lambda i, j, kk: (i, kk), **a_mode)
    if mode == "nt":
        b_spec = pl.BlockSpec((tn, tk), lambda i, j, kk: (j, kk), **b_mode)
    else:
        b_spec = pl.BlockSpec((tk, tn), lambda i, j, kk: (kk, j), **b_mode)
    dot = {"nn": _dot, "nt": _dot_nt, "tn": _dot_tn}[mode]
    tile = pl.BlockSpec((tm, tn), lambda i, j, kk: (i, j))
    n_ex, n_out = len(extras), len(out_dtypes)
    acc_in_out = nk > 1 and epilogue is None and out_dtypes[0] == F32

    def body(a_ref, b_ref, *rest):
        ex_refs, out_refs = rest[:n_ex], rest[n_ex:n_ex + n_out]
        part = dot(a_ref[...].astype(BF16), b_ref[...].astype(BF16))

        def finish(acc):
            res = epilogue(acc, *[e[...] for e in ex_refs]) if epilogue is not None else (acc,) * n_out
            for o_ref, r in zip(out_refs, res):
                o_ref[...] = r.astype(o_ref.dtype)

        if nk == 1:
            finish(part)
        else:
            acc_ref = out_refs[0] if acc_in_out else rest[-1]
            kk = pl.program_id(2)

            @pl.when(kk == 0)
            def _():
                acc_ref[...] = part

            @pl.when(kk > 0)
            def _():
                acc_ref[...] += part

            @pl.when(kk == nk - 1)
            def _():
                if acc_in_out:
                    for o_ref in out_refs[1:]:
                        o_ref[...] = acc_ref[...].astype(o_ref.dtype)
                else:
                    finish(acc_ref[...])

    outs = pl.pallas_call(
        body,
        name=name,
        grid=(m // tm, n // tn, nk),
        in_specs=[a_spec, b_spec] + [tile] * n_ex,
        out_specs=[tile] * n_out,
        out_shape=[jax.ShapeDtypeStruct((m, n), dt) for dt in out_dtypes],
        scratch_shapes=[pltpu.VMEM((tm, tn), F32)] if nk > 1 and not acc_in_out else [],
        compiler_params=_params("parallel", "parallel", "arbitrary"),
    )(a, b, *extras)
    return outs if n_out > 1 else outs[0]


def _rms_matmul(x, g, b, *, out_dtypes, name, epilogue=None, tm=TM):
    s, d = x.shape
    n = b.shape[1]
    ts = min(tm, s)
    n_out = len(out_dtypes)

    def body(x_ref, g_ref, b_ref, u_ref, *out_refs):
        xv = x_ref[...]
        u = (xv * lax.rsqrt(jnp.mean(xv * xv, axis=-1, keepdims=True) + EPS) * g_ref[...]).astype(BF16)
        u_ref[...] = u
        acc = _dot(u, b_ref[...])
        res = epilogue(acc) if epilogue is not None else (acc,)
        for o_ref, r in zip(out_refs, res):
            o_ref[...] = r.astype(o_ref.dtype)

    row = lambda w: pl.BlockSpec((ts, w), lambda i: (i, 0))
    return pl.pallas_call(
        body, name=name, grid=(s // ts,),
        in_specs=[row(d), pl.BlockSpec((1, d), lambda i: (0, 0)), pl.BlockSpec((d, n), lambda i: (0, 0), pipeline_mode=pl.Buffered(1))],
        out_specs=[row(d)] + [row(n)] * n_out,
        out_shape=[jax.ShapeDtypeStruct((s, d), BF16)] + [jax.ShapeDtypeStruct((s, n), dt) for dt in out_dtypes],
        compiler_params=_params("parallel"),
    )(x, g, b)


def _matmul_rms_bwd(a, b, x, g, dres, name):
    s, k = a.shape
    d = b.shape[0]
    ts = min(TM, s)

    def body(a_ref, b_ref, x_ref, g_ref, dres_ref, dx_ref, dx16_ref, dg_ref):
        @pl.when(pl.program_id(0) == 0)
        def _():
            dg_ref[...] = jnp.zeros_like(dg_ref)

        duv = _dot_nt(a_ref[...], b_ref[...])
        xv = x_ref[...]
        r = lax.rsqrt(jnp.mean(xv * xv, axis=-1, keepdims=True) + EPS)
        xh = xv * r
        dxh = duv * g_ref[...]
        dx = dres_ref[...] + r * (dxh - xh * jnp.mean(dxh * xh, axis=-1, keepdims=True))
        dx_ref[...] = dx
        dx16_ref[...] = dx.astype(BF16)
        dg_ref[...] += jnp.sum(duv * xh, axis=0, keepdims=True)

    row = pl.BlockSpec((ts, d), lambda i: (i, 0))
    vec = pl.BlockSpec((1, d), lambda i: (0, 0))
    return pl.pallas_call(
        body, name=name, grid=(s // ts,),
        in_specs=[pl.BlockSpec((ts, k), lambda i: (i, 0)), pl.BlockSpec((d, k), lambda i: (0, 0), pipeline_mode=pl.Buffered(1)),
                  row, vec, row],
        out_specs=[row, row, vec],
        out_shape=[jax.ShapeDtypeStruct((s, d), F32), jax.ShapeDtypeStruct((s, d), BF16), jax.ShapeDtypeStruct((1, d), F32)],
        compiler_params=_params("arbitrary"),
    )(a, b, x, g, dres)


def _matmul_loss(a, b, res, target, name):
    s, k = a.shape
    d = b.shape[1]
    ts = min(TM, s)

    def body(a_ref, b_ref, res_ref, t_ref, dy_ref, dy16_ref, sq_ref):
        @pl.when(pl.program_id(0) == 0)
        def _():
            sq_ref[...] = jnp.zeros_like(sq_ref)

        err = _dot(a_ref[...], b_ref[...]) + res_ref[...] - t_ref[...]
        dy = err * (1.0 / d)
        dy_ref[...] = dy
        dy16_ref[...] = dy.astype(BF16)
        sq_ref[...] += jnp.sum(err * err, axis=0, keepdims=True)

    row = pl.BlockSpec((ts, d), lambda i: (i, 0))
    vec = pl.BlockSpec((1, d), lambda i: (0, 0))
    return pl.pallas_call(
        body, name=name, grid=(s // ts,),
        in_specs=[pl.BlockSpec((ts, k), lambda i: (i, 0)), pl.BlockSpec((k, d), lambda i: (0, 0), pipeline_mode=pl.Buffered(1)),
                  row, row],
        out_specs=[row, row, vec],
        out_shape=[jax.ShapeDtypeStruct((s, d), F32), jax.ShapeDtypeStruct((s, d), BF16), jax.ShapeDtypeStruct((1, d), F32)],
        compiler_params=_params("arbitrary"),
    )(a, b, res, target)


def _ln_silu_fwd(y, g, b, mixed, name):
    s, c = y.shape
    ts = min(TM, s)
    assert mixed.shape == (s, 2 * c)

    def body(y_ref, g_ref, b_ref, mixed_ref, h_ref):
        yv = y_ref[...]
        mu = jnp.mean(yv, axis=-1, keepdims=True)
        yc = yv - mu
        z = yc * lax.rsqrt(jnp.mean(yc * yc, axis=-1, keepdims=True) + EPS) * g_ref[...] + b_ref[...]
        h_ref[...] = (z * _sigmoid(z)).astype(BF16)

    row = pl.BlockSpec((ts, c), lambda i: (i, 0))
    vec = pl.BlockSpec((1, c), lambda i: (0, 0))
    return pl.pallas_call(
        body, name=name, grid=(s // ts,), in_specs=[row, vec, vec, pl.BlockSpec(memory_space=pl.ANY)],
        out_specs=pl.BlockSpec((ts, c), lambda i: (i, 1)),
        out_shape=jax.ShapeDtypeStruct((s, 2 * c), BF16), input_output_aliases={3: 0},
        compiler_params=_params("parallel"),
    )(y, g, b, mixed)


def _ln_silu_bwd(y, g, b, dmixed, name):
    s, c = y.shape
    ts = min(TM, s)

    def body(y_ref, g_ref, b_ref, dh_ref, dy_ref, dg_ref, db_ref):
        @pl.when(pl.program_id(0) == 0)
        def _():
            dg_ref[...] = jnp.zeros_like(dg_ref)
            db_ref[...] = jnp.zeros_like(db_ref)

        yv = y_ref[...]
        mu = jnp.mean(yv, axis=-1, keepdims=True)
        yc = yv - mu
        r = lax.rsqrt(jnp.mean(yc * yc, axis=-1, keepdims=True) + EPS)
        yh = yc * r
        z = yh * g_ref[...] + b_ref[...]
        sg = _sigmoid(z)
        dz = dh_ref[...] * (sg * (1.0 + z * (1.0 - sg)))
        dg_ref[...] += jnp.sum(dz * yh, axis=0, keepdims=True)
        db_ref[...] += jnp.sum(dz, axis=0, keepdims=True)
        dyh = dz * g_ref[...]
        dy_ref[...] = r * (dyh - jnp.mean(dyh, axis=-1, keepdims=True) - yh * jnp.mean(dyh * yh, axis=-1, keepdims=True))

    row = pl.BlockSpec((ts, c), lambda i: (i, 0))
    vec = pl.BlockSpec((1, c), lambda i: (0, 0))
    return pl.pallas_call(
        body, name=name, grid=(s // ts,),
        in_specs=[row, vec, vec, pl.BlockSpec((ts, c), lambda i: (i, 1))], out_specs=[row, vec, vec],
        out_shape=[jax.ShapeDtypeStruct((s, c), F32), jax.ShapeDtypeStruct((1, c), F32), jax.ShapeDtypeStruct((1, c), F32)],
        compiler_params=_params("arbitrary"),
    )(y, g, b, dmixed)


def _head_masks():
    lane2 = _iota((1, PAIR), 1)
    lane1 = _iota((1, LANES), 1)
    qa = (lane2 < HEAD_DIM) | ((lane2 >= LANES) & (lane2 < LANES + 3))
    qb = ((lane2 >= HEAD_DIM) & (lane2 < LANES)) | ((lane2 >= LANES + 3) & (lane2 < LANES + 6))
    return (qa, qb), (lane1 < HEAD_DIM, lane1 >= HEAD_DIM)


def _group_matrix(width):
    shift = HEAD_DIM.bit_length() - 1
    return ((_iota((width, width), 0) >> shift) == (_iota((width, width), 1) >> shift)).astype(BF16)


def _prep_fwd(proj, gq, gk, bf, n_heads, name):
    s = proj.shape[0]
    aw = n_heads * HEAD_DIM
    n_pairs = n_heads // 2
    ts = min(TM, s)
    f_col = (proj.shape[1] - LANES) // LANES

    def body(q_ref, k_ref, v_ref, f_ref, gq_ref, gk_ref, bf_ref, qa_ref, ka_ref, vb_ref, carry_ref):
        @pl.when(pl.program_id(0) == 0)
        def _():
            carry_ref[...] = jnp.zeros_like(carry_ref)

        gmat = _group_matrix(aw)

        def head_norm(xv, g):
            ms = _dot2(xv * xv, gmat) * (1.0 / HEAD_DIM)
            return xv * lax.rsqrt(ms + EPS) * g

        qn = head_norm(q_ref[...], gq_ref[...]) * (HEAD_DIM ** -0.5)
        kn = head_norm(k_ref[...], gk_ref[...])
        z = f_ref[...] + bf_ref[...]
        logf = jnp.minimum(z, 0.0) - jnp.log(1.0 + jnp.exp(-jnp.abs(z)))
        tri = (_iota((ts, ts), 0) >= _iota((ts, ts), 1)).astype(BF16)
        c = _dot3_r(tri, logf) + carry_ref[...]
        carry_ref[...] = c[ts - 1:ts, :]
        terms = _split3(-c)
        row, col = _iota((LANES, LANES), 0), _iota((LANES, LANES), 1)
        ones = jnp.where(_iota((ts, LANES), 1) < 6, 1.0, 0.0).astype(BF16)
        for p in range(n_pairs):
            extra = jnp.zeros((ts, LANES), F32)
            for t, term in enumerate(terms):
                sel = ((row == 2 * p) & (col == t)) | ((row == 2 * p + 1) & (col == 3 + t))
                extra += _dot(term, sel.astype(BF16))
            lo, hi = p * PAIR, p * PAIR + LANES
            ka_ref[:, lo:hi] = kn[:, p * LANES:(p + 1) * LANES].astype(BF16)
            ka_ref[:, hi:hi + LANES] = extra.astype(BF16)
            qa_ref[:, lo:hi] = qn[:, p * LANES:(p + 1) * LANES].astype(BF16)
            qa_ref[:, hi:hi + LANES] = ones
            vb_ref[:, lo:hi] = v_ref[:, p * LANES:(p + 1) * LANES].astype(BF16)
            vb_ref[:, hi:hi + LANES] = jnp.ones((ts, LANES), BF16)

    blk = lambda j: pl.BlockSpec((ts, aw), lambda i: (i, j))
    vec = lambda w: pl.BlockSpec((1, w), lambda i: (0, 0))
    return pl.pallas_call(
        body, name=name, grid=(s // ts,),
        in_specs=[blk(0), blk(1), blk(2), pl.BlockSpec((ts, LANES), lambda i: (i, f_col)), vec(aw), vec(aw), vec(LANES)],
        out_specs=[pl.BlockSpec((ts, n_pairs * PAIR), lambda i: (i, 0))] * 3,
        out_shape=[jax.ShapeDtypeStruct((s, n_pairs * PAIR), BF16)] * 3,
        scratch_shapes=[pltpu.VMEM((1, LANES), F32)],
        compiler_params=_params("arbitrary"),
    )(proj, proj, proj, proj, gq, gk, bf)


def _prep_bwd(proj, dq, dka, drow, dcol, gq, gk, bf, n_heads, name):
    s = proj.shape[0]
    aw = n_heads * HEAD_DIM
    n_pairs = n_heads // 2
    ts = min(TM, s)
    nt = s // ts
    f_col = (proj.shape[1] - LANES) // LANES
    shift = HEAD_DIM.bit_length() - 1

    def body(q_ref, k_ref, f_ref, dq_ref, dka_ref, drow_ref, dcol_ref, gq_ref, gk_ref, bf_ref,
             dpq_ref, dpk_ref, dpf_ref, dgq_ref, dgk_ref, dbf_ref, carry_ref):
        @pl.when(pl.program_id(0) == 0)
        def _():
            carry_ref[...] = jnp.zeros_like(carry_ref)
            dgq_ref[...] = jnp.zeros_like(dgq_ref)
            dgk_ref[...] = jnp.zeros_like(dgk_ref)
            dbf_ref[...] = jnp.zeros_like(dbf_ref)

        gmat = _group_matrix(aw)

        def head_norm_bwd(xv, g, dn):
            r = lax.rsqrt(_dot2(xv * xv, gmat) * (1.0 / HEAD_DIM) + EPS)
            xh = xv * r
            dxh = dn * g
            dx = r * (dxh - xh * (_dot2(dxh * xh, gmat) * (1.0 / HEAD_DIM)))
            return dx, jnp.sum(dn * xh, axis=0, keepdims=True)

        dkav = dka_ref[...]
        dx, dg = head_norm_bwd(q_ref[...], gq_ref[...], dq_ref[...] * (HEAD_DIM ** -0.5))
        dpq_ref[...] = dx.astype(BF16)
        dgq_ref[...] += dg
        dkn = jnp.concatenate([dkav[:, p * PAIR:p * PAIR + LANES] for p in range(n_pairs)], axis=1)
        dx, dg = head_norm_bwd(k_ref[...], gk_ref[...], dkn)
        dpk_ref[...] = dx.astype(BF16)
        dgk_ref[...] += dg

        pick = (_iota((aw, LANES), 0) == (_iota((aw, LANES), 1) << shift)).astype(BF16)
        dc = _dot3(drow_ref[...], pick)
        r16, c16 = _iota((16, LANES), 0), _iota((16, LANES), 1)
        for p in range(n_pairs):
            place = ((r16 < 2) & (c16 == 2 * p + r16)).astype(BF16)
            for term in _split3(dcol_ref[p]):
                dc -= _dot_tn(term, place)
        triu = (_iota((ts, ts), 0) <= _iota((ts, ts), 1)).astype(BF16)
        dlogf = _dot3_r(triu, dc) + carry_ref[...]
        carry_ref[...] = dlogf[0:1, :]
        z = f_ref[...] + bf_ref[...]
        dz = dlogf * (1.0 / (1.0 + jnp.exp(z)))
        dpf_ref[...] = dz.astype(BF16)
        dbf_ref[...] += jnp.sum(dz, axis=0, keepdims=True)

    rev = lambda w, j: pl.BlockSpec((ts, w), lambda i: (nt - 1 - i, j))
    vec = lambda w: pl.BlockSpec((1, w), lambda i: (0, 0))
    return pl.pallas_call(
        body, name=name, grid=(nt,),
        in_specs=[rev(aw, 0), rev(aw, 1), rev(LANES, f_col), rev(aw, 0), rev(n_pairs * PAIR, 0), rev(aw, 0),
                  pl.BlockSpec((n_pairs, 16, ts), lambda i: (0, 0, nt - 1 - i)), vec(aw), vec(aw), vec(LANES)],
        out_specs=[rev(aw, 0), rev(aw, 0), rev(LANES, 0), vec(aw), vec(aw), vec(LANES)],
        out_shape=[jax.ShapeDtypeStruct((s, aw), BF16), jax.ShapeDtypeStruct((s, aw), BF16), jax.ShapeDtypeStruct((s, LANES), BF16),
                   jax.ShapeDtypeStruct((1, aw), F32), jax.ShapeDtypeStruct((1, aw), F32), jax.ShapeDtypeStruct((1, LANES), F32)],
        scratch_shapes=[pltpu.VMEM((1, LANES), F32)],
        compiler_params=_params("arbitrary"),
    )(proj, proj, proj, dq, dka, drow, dcol, gq, gk, bf)


def _attn_fwd(qa, ka, vb, n_heads, mix_width, name):
    s = qa.shape[0]
    aw = n_heads * HEAD_DIM
    n_pairs = n_heads // 2
    tq = min(2 * TQ, s)

    def body(q_ref, k_ref, v_ref, o_ref, lse_ref, o16_ref):
        i = pl.program_id(1)
        qmasks, omasks = _head_masks()
        qv = q_ref[...]
        causal = _iota((tq, tq), 1) <= _iota((tq, tq), 0)
        qhs = [jnp.where(qmasks[h], qv, jnp.zeros_like(qv)) for h in range(2)]

        def scores(j):
            kv = k_ref[pl.ds(pl.multiple_of(j * tq, tq), tq), :]
            return tuple(_dot_nt(qhs[h], kv) for h in range(2))

        def update(j, state, scs, masked):
            vv = v_ref[pl.ds(pl.multiple_of(j * tq, tq), tq), :]
            out = []
            for h in range(2):
                m, acc = state[h]
                sc = jnp.where(causal, scs[h], NEG) if masked else scs[h]
                m_new = jnp.maximum(m, jnp.max(sc, axis=1, keepdims=True))
                p = jnp.exp(sc - m_new).astype(BF16)
                out.append((m_new, jnp.exp(m - m_new) * acc + _dot(p, vv)))
            return tuple(out)

        def body(j, state):
            return update(j, state, scores(j), False)

        init = ((jnp.full((tq, 1), NEG, F32), jnp.zeros((tq, PAIR), F32)),) * 2
        state = lax.fori_loop(0, i, body, init)
        res = []
        for m, acc in update(i, state, scores(i), True):
            l = acc[:, LANES:LANES + 1]
            res.append((acc[:, :LANES] * (1.0 / l), m + jnp.log(l)))
        out = jnp.where(omasks[0], res[0][0], res[1][0])
        o_ref[...] = out
        o16_ref[...] = out.astype(BF16)
        lse_ref[...] = jnp.where(omasks[0], res[0][1], res[1][1])

    return pl.pallas_call(
        body, name=name, grid=(n_pairs, s // tq),
        in_specs=[pl.BlockSpec((tq, PAIR), lambda p, i: (i, p)), pl.BlockSpec((s, PAIR), lambda p, i: (0, p)),
                  pl.BlockSpec((s, PAIR), lambda p, i: (0, p))],
        out_specs=[pl.BlockSpec((tq, LANES), lambda p, i: (i, p))] * 3,
        out_shape=[jax.ShapeDtypeStruct((s, aw), F32)] * 2 + [jax.ShapeDtypeStruct((s, mix_width), BF16)],
        compiler_params=_params("parallel", "parallel"),
    )(qa, ka, vb)


def _attn_bwd(qa, ka, vb, o, lse, dmixed, n_heads, name):
    s = qa.shape[0]
    aw = n_heads * HEAD_DIM
    n_pairs = n_heads // 2
    tq = min(2 * TQ, s)
    nq = s // tq

    def body(q_ref, k_ref, v_ref, o_ref, lse_ref, do_ref, dq_ref, dka_ref, dv_ref, drow_ref, dcol_ref, delta_ref):
        j = pl.program_id(1)
        qmasks, omasks = _head_masks()

        @pl.when(j == 0)
        def _():
            dq_ref[...] = jnp.zeros_like(dq_ref)
            drow_ref[...] = jnp.zeros_like(drow_ref)
            for c in range(nq):
                rows = slice(c * tq, (c + 1) * tq)
                prod = do_ref[rows, :] * o_ref[rows, :]
                da = jnp.sum(jnp.where(omasks[0], prod, 0.0), axis=1, keepdims=True)
                db = jnp.sum(jnp.where(omasks[1], prod, 0.0), axis=1, keepdims=True)
                delta_ref[rows, :] = jnp.where(omasks[0], da, db)

        dka_ref[...] = jnp.zeros_like(dka_ref)
        dv_ref[...] = jnp.zeros_like(dv_ref)
        dcol_ref[...] = jnp.zeros_like(dcol_ref)
        kv = k_ref[...]
        kk = kv[:, :LANES]
        vv = v_ref[...]
        causal = _iota((tq, tq), 1) <= _iota((tq, tq), 0)

        def step(i, masked):
            off = pl.multiple_of(i * tq, tq)
            qv = q_ref[pl.ds(off, tq), :]
            dov = do_ref[pl.ds(off, tq), :]
            lsev = lse_ref[pl.ds(off, tq), :]
            dlv = delta_ref[pl.ds(off, tq), :]
            for h in range(2):
                qh = jnp.where(qmasks[h], qv, jnp.zeros_like(qv))
                doh = jnp.where(omasks[h], dov, 0.0).astype(BF16)
                lane = h * HEAD_DIM
                sc = _dot_nt(qh, kv)
                if masked:
                    sc = jnp.where(causal, sc, NEG)
                p = jnp.exp(sc - lsev[:, lane:lane + 1])
                dv_ref[...] += _dot_tn(p.astype(BF16), doh)
                dp = _dot_nt(doh, vv)
                dsf = p * (dp - dlv[:, lane:lane + 1])
                drow_ref[pl.ds(off, tq), :] += jnp.where(omasks[h], jnp.sum(dsf, axis=1, keepdims=True), 0.0)
                dcol_ref[0, h:h + 1, :] += jnp.sum(dsf, axis=0, keepdims=True)
                ds = dsf.astype(BF16)
                dka_ref[...] += _dot_tn(ds, qh)
                dq_ref[pl.ds(off, tq), :] += jnp.where(omasks[h], _dot(ds, kk), 0.0)

        step(j, True)

        def loop_body(i, carry):
            step(i, False)
            return carry

        lax.fori_loop(j + 1, nq, loop_body, 0)

    full = lambda w: pl.BlockSpec((s, w), lambda p, j: (0, p))
    blk = lambda w: pl.BlockSpec((tq, w), lambda p, j: (j, p))
    return pl.pallas_call(
        body, name=name, grid=(n_pairs, nq),
        in_specs=[full(PAIR), blk(PAIR), pl.BlockSpec((tq, LANES), lambda p, j: (j, 2 * p)), full(LANES), full(LANES), full(LANES)],
        out_specs=[full(LANES), blk(PAIR), blk(LANES), full(LANES), pl.BlockSpec((1, 16, tq), lambda p, j: (p, 0, j))],
        out_shape=[jax.ShapeDtypeStruct((s, aw), F32), jax.ShapeDtypeStruct((s, n_pairs * PAIR), F32),
                   jax.ShapeDtypeStruct((s, aw), F32), jax.ShapeDtypeStruct((s, aw), F32),
                   jax.ShapeDtypeStruct((n_pairs, 16, s), F32)],
        scratch_shapes=[pltpu.VMEM((s, LANES), F32)],
        compiler_params=_params("parallel", "arbitrary"),
    )(qa, ka, vb, o, lse, dmixed)


def _conv_fwd(proj, w32, bias, n_ch, a_col, g_col, name):
    s = proj.shape[0]
    rows = min(CONV_ROWS, s)

    def body(a_ref, g_ref, w_ref, b_ref, y_ref, pad_ref):
        pad_ref[0:CONV_PAD, :] = jnp.zeros((CONV_PAD, LANES), F32)
        pad_ref[CONV_PAD:CONV_PAD + s, :] = a_ref[...] * _sigmoid(g_ref[...])
        wv = w_ref[...]
        for c in range(s // rows):
            acc = jnp.broadcast_to(b_ref[...], (rows, LANES))
            for t in range(CONV_TAPS):
                start = c * rows + CONV_PAD - (CONV_TAPS - 1) + t
                acc = acc + wv[t:t + 1, :] * pad_ref[start:start + rows, :]
            y_ref[c * rows:(c + 1) * rows, :] = acc

    col = lambda j0: pl.BlockSpec((s, LANES), lambda c: (0, j0 + c))
    return pl.pallas_call(
        body, name=name, grid=(n_ch // LANES,),
        in_specs=[col(a_col), col(g_col), pl.BlockSpec((CONV_PAD, LANES), lambda c: (0, c)), pl.BlockSpec((1, LANES), lambda c: (0, c))],
        out_specs=pl.BlockSpec((s, LANES), lambda c: (0, c)),
        out_shape=jax.ShapeDtypeStruct((s, n_ch), F32),
        scratch_shapes=[pltpu.VMEM((s + CONV_PAD, LANES), F32)],
        compiler_params=_params("parallel"),
    )(proj, proj, w32, bias)


def _conv_bwd(proj, w32, dy, n_ch, a_col, g_col, name):
    s = proj.shape[0]
    rows = min(CONV_ROWS, s)
    sub = 8

    def fold(x):
        acc = x[0:sub, :]
        for r in range(1, rows // sub):
            acc = acc + x[r * sub:(r + 1) * sub, :]
        return acc

    def body(a_ref, g_ref, w_ref, dy_ref, da_ref, dg_ref, dw_ref, padh_ref, padd_ref):
        sg = _sigmoid(g_ref[...])
        padh_ref[0:CONV_PAD, :] = jnp.zeros((CONV_PAD, LANES), F32)
        padh_ref[CONV_PAD:CONV_PAD + s, :] = a_ref[...] * sg
        padd_ref[0:s, :] = dy_ref[...]
        padd_ref[s:s + CONV_PAD, :] = jnp.zeros((CONV_PAD, LANES), F32)
        wv = w_ref[...]
        dw = [jnp.zeros((sub, LANES), F32) for _ in range(CONV_TAPS + 1)]
        for c in range(s // rows):
            r0 = c * rows
            acc = jnp.zeros((rows, LANES), F32)
            dyc = dy_ref[r0:r0 + rows, :]
            for t in range(CONV_TAPS):
                back = r0 + (CONV_TAPS - 1) - t
                acc = acc + wv[t:t + 1, :] * padd_ref[back:back + rows, :]
                start = r0 + CONV_PAD - (CONV_TAPS - 1) + t
                dw[t] = dw[t] + fold(dyc * padh_ref[start:start + rows, :])
            dw[CONV_TAPS] = dw[CONV_TAPS] + fold(dyc)
            av = a_ref[r0:r0 + rows, :]
            sgc = _sigmoid(g_ref[r0:r0 + rows, :])
            da_ref[r0:r0 + rows, :] = (acc * sgc).astype(BF16)
            dg_ref[r0:r0 + rows, :] = (acc * av * sgc * (1.0 - sgc)).astype(BF16)
        for t in range(CONV_TAPS + 1):
            dw_ref[t:t + 1, :] = jnp.sum(dw[t], axis=0, keepdims=True)

    col = lambda j0: pl.BlockSpec((s, LANES), lambda c: (0, j0 + c))
    wspec = pl.BlockSpec((CONV_PAD, LANES), lambda c: (0, c))
    return pl.pallas_call(
        body, name=name, grid=(n_ch // LANES,),
        in_specs=[col(a_col), col(g_col), wspec, col(0)],
        out_specs=[col(0), col(0), wspec],
        out_shape=[jax.ShapeDtypeStruct((s, n_ch), BF16), jax.ShapeDtypeStruct((s, n_ch), BF16),
                   jax.ShapeDtypeStruct((CONV_PAD, n_ch), F32)],
        scratch_shapes=[pltpu.VMEM((s + CONV_PAD, LANES), F32), pltpu.VMEM((s + CONV_PAD, LANES), F32)],
        compiler_params=_params("parallel"),
    )(proj, proj, w32, dy)


def _my_place():
    return lax.axis_index("x"), lax.axis_index("y"), lax.axis_index("c")


def _flip(place, k):
    x, y, c = place
    return (1 - x if k & 4 else x, 1 - y if k & 2 else y, 1 - c if k & 1 else c)


def _dev_id(place):
    return 4 * place[0] + 2 * place[1] + place[2]


def _wait_all(ref, send_sem, recv_sem, place):
    pltpu.make_async_remote_copy(src_ref=ref, dst_ref=ref, send_sem=send_sem, recv_sem=recv_sem,
                                 device_id=place, device_id_type=MESH).wait()


def _window(kind, ref, dev, n):
    if kind == "slot":
        return ref.at[dev]
    if kind == "rows":
        return ref.at[pl.ds(pl.multiple_of(dev * n, n), n), :]
    return ref.at[:, pl.ds(pl.multiple_of(dev * n, n), n)]


def _hbm(x):
    return pltpu.with_memory_space_constraint(x, pltpu.HBM)


_EFFECT = pltpu.SideEffectType.DATAFLOW_SIDE_EFFECTING


def _exchange_start(srcs, lands, specs, groups, name):
    n = len(srcs)
    n_g = len(groups)

    def body(*refs):
        src_refs, land_refs = refs[:n], refs[n:2 * n]
        sems = refs[2 * n:2 * n + 2 * n_g]
        token = refs[-1]
        place = _my_place()
        me = _dev_id(place)
        for g, units in enumerate(groups):
            for j, u in enumerate(units):
                mode, kind, cnt = specs[u]
                for k in range(N_DEV):
                    peer = _flip(place, k)
                    if mode == "gather":
                        src, dst = src_refs[u], _window(kind, land_refs[u], me, cnt)
                    else:
                        src, dst = _window(kind, src_refs[u], _dev_id(peer), cnt), land_refs[u].at[k]
                    pltpu.make_async_remote_copy(src_ref=src, dst_ref=dst, send_sem=sems[2 * g].at[j], recv_sem=sems[2 * g + 1].at[j],
                                                 device_id=peer, device_id_type=MESH).start()
        token[...] = jnp.zeros_like(token)

    hbm = pl.BlockSpec(memory_space=pltpu.HBM)
    sem = pl.BlockSpec(memory_space=pltpu.SEMAPHORE)
    out_shape = [pltpu.SemaphoreType.DMA((len(units),)) for units in groups for _ in range(2)]
    out_shape += [pltpu.HBM(x.shape, x.dtype) for x in lands] + [jax.ShapeDtypeStruct((8, LANES), F32)]
    outs = pl.pallas_call(
        body, name=name, out_shape=out_shape,
        in_specs=[hbm] * (2 * n), out_specs=[sem] * (2 * n_g) + [hbm] * n + [pl.BlockSpec(memory_space=pltpu.VMEM)],
        input_output_aliases={n + u: 2 * n_g + u for u in range(n)},
        compiler_params=pltpu.CompilerParams(has_side_effects=_EFFECT),
    )(*[_hbm(x) for x in srcs], *[_hbm(x) for x in lands])
    sem_pairs = [(outs[2 * g], outs[2 * g + 1]) for g in range(n_g)]
    return sem_pairs, list(outs[2 * n_g:2 * n_g + n]), outs[-1]


def _exchange_wait(srcs, lands, specs, sem_pair, after, name):
    n = len(lands)

    def body(*refs):
        land_refs = refs[n:2 * n]
        send_sems, recv_sems = refs[2 * n], refs[2 * n + 1]
        place = _my_place()
        for u in range(n):
            _wait_all(land_refs[u], send_sems.at[u], recv_sems.at[u], place)

    hbm = pl.BlockSpec(memory_space=pltpu.HBM)
    sem = pl.BlockSpec(memory_space=pltpu.SEMAPHORE)
    outs = pl.pallas_call(
        body, name=name, out_shape=[pltpu.HBM(x.shape, x.dtype) for x in lands],
        in_specs=[hbm] * (2 * n) + [sem, sem, pl.BlockSpec(memory_space=pl.ANY)], out_specs=[hbm] * n,
        input_output_aliases={n + u: u for u in range(n)},
        compiler_params=pltpu.CompilerParams(has_side_effects=_EFFECT),
    )(*[_hbm(x) for x in srcs], *lands, sem_pair[0], sem_pair[1], after)
    return list(outs)


def _sum_devices(parts, name):
    _, r, w = parts.shape

    def body(p_ref, out_ref):
        acc = p_ref[0]
        for d in range(1, N_DEV):
            acc = acc + p_ref[d]
        out_ref[...] = acc

    return pl.pallas_call(
        body, name=name, out_shape=jax.ShapeDtypeStruct((r, w), F32),
        in_specs=[pl.BlockSpec(memory_space=pltpu.VMEM)], out_specs=pl.BlockSpec(memory_space=pltpu.VMEM),
    )(parts)


def _adamw_math(w, m, v, g):
    m_new = ADAM_B1 * m + (1.0 - ADAM_B1) * g
    v_new = ADAM_B2 * v + (1.0 - ADAM_B2) * (g * g)
    m_hat = m_new / (1.0 - ADAM_B1 ** ADAM_STEP)
    v_hat = v_new / (1.0 - ADAM_B2 ** ADAM_STEP)
    return -ADAM_LR * (m_hat / (jnp.sqrt(v_hat) + ADAM_EPS) + ADAM_WD * w), m_new, v_new


def _adamw(w, m, v, g, name):
    rows = w.shape[0]
    tr = min(FLAT_ROWS, rows)
    assert rows % tr == 0, (name, rows)

    def body(w_ref, m_ref, v_ref, g_ref, d_out, m_out, v_out):
        d_out[...], m_out[...], v_out[...] = _adamw_math(w_ref[...], m_ref[...], v_ref[...], g_ref[...])

    flat = pl.BlockSpec((tr, LANES), lambda i: (i, 0))
    return pl.pallas_call(
        body, name=name, grid=(rows // tr,), in_specs=[flat] * 4, out_specs=[flat] * 3,
        out_shape=[jax.ShapeDtypeStruct((rows, LANES), F32)] * 3,
        compiler_params=_params("parallel"),
    )(w, m, v, g)


def _adamw_shard(w, m, v, recv, layer, prev, name):
    depth, a, b = w.shape
    ta = min(256, a)
    assert a % ta == 0

    def body(w_ref, m_ref, v_ref, r_ref, *rest):
        g_out, d_out, m_out, v_out = rest[-4:]
        g = r_ref[0].astype(F32)
        for k in range(1, N_DEV):
            g = g + r_ref[k].astype(F32)
        g_out[0] = g
        d_out[0], m_out[0], v_out[0] = _adamw_math(w_ref[0], m_ref[0], v_ref[0], g)

    lay = pl.BlockSpec((1, ta, b), lambda i: (layer, i, 0))
    in_specs = [lay] * 3 + [pl.BlockSpec((N_DEV, ta, b), lambda i: (0, i, 0))]
    args = [w, m, v, recv]
    aliases = {}
    if prev is not None:
        in_specs += [pl.BlockSpec(memory_space=pl.ANY)] * 4
        args += list(prev)
        aliases = {4 + i: i for i in range(4)}
    return pl.pallas_call(
        body, name=name, grid=(a // ta,), in_specs=in_specs, out_specs=[lay] * 4,
        out_shape=[jax.ShapeDtypeStruct(w.shape, F32)] * 4, input_output_aliases=aliases,
        compiler_params=_params("parallel"),
    )(*args)


def _round_up(n, mult):
    return (n + mult - 1) // mult * mult


def _flatten(parts, row_mult):
    flat = jnp.concatenate([p.reshape(-1) for p in parts])
    rows = _round_up(-(-flat.shape[0] // LANES), row_mult)
    return jnp.pad(flat, (0, rows * LANES - flat.shape[0])).reshape(rows, LANES)


def _unflatten(flat, shapes):
    flat = flat.reshape(-1)
    out, off = [], 0
    for shp in shapes:
        n = 1
        for dim in shp:
            n *= dim
        out.append(flat[off:off + n].reshape(shp))
        off += n
    return out


def kernel(x, norm1_g, w_in, b_f, q_norm_g, k_norm_g, conv_w, conv_b, conv_ln_g, conv_ln_b, w_o, norm2_g, w_mlp_in, w_mlp_out, loss_target, m_norm1_g, m_w_in, m_b_f, m_q_norm_g, m_k_norm_g, m_conv_w, m_conv_b, m_conv_ln_g, m_conv_ln_b, m_w_o, m_norm2_g, m_w_mlp_in, m_w_mlp_out, v_norm1_g, v_w_in, v_b_f, v_q_norm_g, v_k_norm_g, v_conv_w, v_conv_b, v_conv_ln_g, v_conv_ln_b, v_w_o, v_norm2_g, v_w_mlp_in, v_w_mlp_out):
    depth, d_model, n_in_loc = w_in.shape
    seq = x.shape[1]
    n_heads = b_f.shape[1]
    aw = n_heads * HEAD_DIM
    cc = conv_b.shape[1]
    n_in = n_in_loc * N_DEV
    o_f = 3 * aw
    n_all = 3 * aw + 2 * cc + LANES
    assert n_in == 3 * aw + n_heads + 2 * cc and aw + cc == d_model and n_heads % 2 == 0
    assert aw % LANES == 0 and cc % LANES == 0 and x.shape[0] == 1
    me = 4 * lax.axis_index("x") + 2 * lax.axis_index("y") + lax.axis_index("c")

    d_ff = w_mlp_in.shape[2] * N_DEV

    r_o, f_1, f_2 = w_o.shape[1], w_mlp_in.shape[2], w_mlp_out.shape[1]

    ag_src, ag_land, ag_spec = [], [], []
    for l in range(depth):
        ag_src += [w_in[l].astype(BF16), w_o[l].astype(BF16), w_mlp_in[l].astype(BF16), w_mlp_out[l].astype(BF16)]
        ag_land += [(N_DEV, d_model, n_in_loc), (N_DEV * r_o, d_model), (d_model, N_DEV * f_1), (N_DEV * f_2, d_model)]
        ag_spec += [("gather", "slot", 1), ("gather", "rows", r_o), ("gather", "cols", f_1), ("gather", "rows", f_2)]
    ag_src.append(jnp.stack(_split3(conv_w)))
    ag_land.append((N_DEV, 3) + conv_w.shape)
    ag_spec.append(("gather", "slot", 1))
    ag_groups = [grp for l in range(depth) for grp in ([4 * l] + ([4 * depth] if l == 0 else []), [4 * l + 1], [4 * l + 2, 4 * l + 3])]
    ag_land = [lax.empty(shp, BF16) for shp in ag_land]
    ag_sems, ag_land, ag_token = _exchange_start(ag_src, ag_land, ag_spec, ag_groups, "gather_start")

    def gathered(g, after):
        units = ag_groups[g]
        return _exchange_wait([ag_src[u] for u in units], [ag_land[u] for u in units], [ag_spec[u] for u in units],
                              ag_sems[g], after, f"gather_wait_{g}")

    f_all = n_all - LANES
    in_segments = ((0, o_f, 0), (o_f + n_heads, n_in, o_f), (o_f, o_f + n_heads, f_all))

    def whole_in(lin):
        pieces = []
        for c0, c1, _ in in_segments:
            while c0 < c1:
                j, off = divmod(c0, n_in_loc)
                stop = min(c1, (j + 1) * n_in_loc)
                pieces.append(lin[j][:, off:off + stop - c0])
                c0 = stop
        pieces.append(jnp.zeros((d_model, LANES - n_heads), lin.dtype))
        return jnp.concatenate(pieces, axis=1)

    def shards_of(w):
        slabs = []
        for j in range(N_DEV):
            pieces = []
            for c0, c1, a0 in sorted(in_segments):
                lo, hi = max(c0, j * n_in_loc), min(c1, (j + 1) * n_in_loc)
                if lo < hi:
                    pieces.append(w[:, a0 + lo - c0:a0 + hi - c0])
            slabs.append(jnp.concatenate(pieces, axis=1))
        return jnp.stack(slabs)

    def row(p, l, width=None):
        v = p[l].reshape(1, -1)
        return v if width is None else jnp.pad(v, ((0, 0), (0, width - v.shape[1])))

    a_col, g_col = 3 * aw // LANES, (3 * aw + cc) // LANES

    gq = [jnp.tile(row(q_norm_g, l), (1, n_heads)) for l in range(depth)]
    gk = [jnp.tile(row(k_norm_g, l), (1, n_heads)) for l in range(depth)]
    bfp = [row(b_f, l, LANES) for l in range(depth)]
    add_res = lambda acc, res: (acc + res,)
    w_all, w_out, w_ff1, w_ff2 = [None] * depth, [None] * depth, [None] * depth, [None] * depth

    h = x[0]
    saved = []
    for l in range(depth):
        if l == 0:
            lin, lc = gathered(0, h)
            lc = lc.astype(F32)
            conv_full = jnp.moveaxis(lc[:, 0] + lc[:, 1] + lc[:, 2], 0, 2).reshape(depth, CONV_TAPS, cc)
            w32 = [jnp.pad(conv_full[i], ((0, CONV_PAD - CONV_TAPS), (0, 0))) for i in range(depth)]
        else:
            lin, = gathered(3 * l, h)
        w_all[l] = whole_in(lin)
        u1, proj = _rms_matmul(h, row(norm1_g, l), w_all[l], out_dtypes=(F32,), name=f"mm_in_{l}")
        qa, ka, vb = _prep_fwd(proj, gq[l], gk[l], bfp[l], n_heads, f"prep_fwd_{l}")
        att, lse, mixed = _attn_fwd(qa, ka, vb, n_heads, aw + cc, f"attn_fwd_{l}")
        yc = _conv_fwd(proj, w32[l], row(conv_b, l), cc, a_col, g_col, f"conv_fwd_{l}")
        mixed = _ln_silu_fwd(yc, row(conv_ln_g, l), row(conv_ln_b, l), mixed, f"ln_silu_fwd_{l}")
        w_out[l], = gathered(3 * l + 1, mixed)
        x1 = _matmul(mixed, w_out[l], mode="nn", out_dtypes=(F32,), name=f"mm_o_{l}", epilogue=add_res, extras=(h,))
        w_ff1[l], w_ff2[l] = gathered(3 * l + 2, x1)
        u2, r, a = _rms_matmul(x1, row(norm2_g, l), w_ff1[l], out_dtypes=(BF16, BF16), name=f"mm_ff1_{l}", tm=256,
                               epilogue=lambda acc: (jnp.maximum(acc, 0.0), jnp.square(jnp.maximum(acc, 0.0))))
        if l < depth - 1:
            x2 = _matmul(a, w_ff2[l], mode="nn", out_dtypes=(F32,), name=f"mm_ff2_{l}", epilogue=add_res, extras=(x1,), tk=d_ff)
        else:
            x2 = None
            dh, dh16, sq = _matmul_loss(a, w_ff2[l], x1, loss_target[0], f"mm_ff2_loss_{l}")
        saved.append(dict(x_in=h, u1=u1, proj=proj, qa=qa, ka=ka, vb=vb, att=att, lse=lse, yc=yc, mixed=mixed,
                          x1=x1, u2=u2, r=r, a=a))
        h = x2

    loss = lax.psum(0.5 * jnp.sum(sq) / d_model, ("x", "y", "c"))

    g_in, g_o, g_1, g_2 = [None] * depth, [None] * depth, [None] * depth, [None] * depth
    gs = {n: [None] * depth for n in ("norm1", "bf", "qn", "kn", "convw", "convb", "lng", "lnb", "norm2")}
    scattering = {}

    def scatter_start(stage, l, srcs, specs, slabs):
        lands = [lax.empty((N_DEV,) + shp, BF16) for shp in slabs]
        sems, lands, token = _exchange_start(srcs, lands, specs, [list(range(len(srcs)))], f"scatter_start_{stage}_{l}")
        scattering[(stage, l)] = (srcs, lands, specs, sems[0])
        return token[0, 0]

    for l in reversed(range(depth)):
        sv = saved[l]
        dh1 = _matmul(dh16, w_ff2[l], mode="nt", out_dtypes=(BF16,), name=f"mm_dff2_{l}", tm=256, tn=d_ff,
                      epilogue=lambda acc, rr: (acc * (2.0 * rr.astype(F32)),), extras=(sv["r"],))
        g_2[l] = _matmul(sv["a"], dh16, mode="tn", out_dtypes=(BF16,), name=f"mm_dw2_{l}", tm=1024, tk=seq)
        g_1[l] = _matmul(sv["u2"], dh1, mode="tn", out_dtypes=(BF16,), name=f"mm_dw1_{l}", tm=d_model, tk=seq)
        tok = scatter_start("ff", l, [g_1[l], g_2[l]], [("scatter", "cols", f_1), ("scatter", "rows", f_2)],
                            [(d_model, f_1), (f_2, d_model)])
        dx1, dx16, gs["norm2"][l] = _matmul_rms_bwd(dh1, w_ff1[l], sv["x1"], row(norm2_g, l) + tok, dh, f"mm_du2_rms_{l}")

        dmixed = _matmul(dx16, w_out[l], mode="nt", out_dtypes=(F32,), name=f"mm_dmixed_{l}")
        g_o[l] = _matmul(sv["mixed"], dx16, mode="tn", out_dtypes=(BF16,), name=f"mm_dwo_{l}", tm=1024, tk=seq // 2)
        tok = scatter_start("o", l, [g_o[l]], [("scatter", "rows", r_o)], [(r_o, d_model)])
        dyc, gs["lng"][l], gs["lnb"][l] = _ln_silu_bwd(sv["yc"], row(conv_ln_g, l) + tok, row(conv_ln_b, l), dmixed, f"ln_silu_bwd_{l}")
        dpa, dpg, dw32 = _conv_bwd(sv["proj"], w32[l], dyc, cc, a_col, g_col, f"conv_bwd_{l}")
        gs["convw"][l], gs["convb"][l] = dw32[:CONV_TAPS], dw32[CONV_TAPS:CONV_TAPS + 1]
        dq, dka, dv, drow, dcol = _attn_bwd(sv["qa"], sv["ka"], sv["vb"], sv["att"], sv["lse"], dmixed, n_heads, f"attn_bwd_{l}")
        dpq, dpk, dpf, dgq, dgk, dbf = _prep_bwd(sv["proj"], dq, dka, drow, dcol, gq[l], gk[l], bfp[l], n_heads, f"prep_bwd_{l}")
        gs["qn"][l] = dgq.reshape(n_heads, HEAD_DIM).sum(axis=0)
        gs["kn"][l] = dgk.reshape(n_heads, HEAD_DIM).sum(axis=0)
        gs["bf"][l] = dbf[0, :n_heads]
        dproj = jnp.concatenate([dpq, dpk, dv.astype(BF16), dpa, dpg, dpf], axis=1)
        dwall = _matmul(sv["u1"], dproj, mode="tn", out_dtypes=(BF16,), name=f"mm_dwall_{l}", tm=d_model, tn=n_all // 3, tk=seq)
        g_in[l] = shards_of(dwall)
        tok = scatter_start("in", l, [g_in[l]], [("scatter", "slot", 1)], [(d_model, n_in_loc)])
        dh, dh16, gs["norm1"][l] = _matmul_rms_bwd(dproj, w_all[l], sv["x_in"], row(norm1_g, l) + tok, dx1, f"mm_du1_rms_{l}")
    grad_x = dh[None]

    small_g = [jnp.stack(gs[n]).reshape(shp) for n, shp in (
        ("norm1", norm1_g.shape), ("bf", b_f.shape), ("qn", q_norm_g.shape), ("kn", k_norm_g.shape),
        ("convw", (depth, CONV_TAPS, cc)), ("convb", conv_b.shape), ("lng", conv_ln_g.shape), ("lnb", conv_ln_b.shape),
        ("norm2", norm2_g.shape))]
    small_shapes = [g.shape for g in small_g]
    small_flat = _flatten(small_g, 8)
    small_spec = [("gather", "slot", 1)]
    small_sems, small_land, _ = _exchange_start([small_flat], [lax.empty((N_DEV,) + small_flat.shape, F32)], small_spec, [[0]],
                                                "small_grads_start")

    def landed(stage, l, after):
        srcs, lands, specs, sems = scattering[(stage, l)]
        return _exchange_wait(srcs, lands, specs, sems, after, f"scatter_wait_{stage}_{l}")

    def adamw_layers(kd, w, m, v, recv):
        outs = None
        for l in reversed(range(depth)):
            outs = _adamw_shard(w, m, v, recv[l], l, outs, f"adamw_{kd}_{l}")
        return outs

    recv_1, recv_2, recv_in, recv_o = [None] * depth, [None] * depth, [None] * depth, [None] * depth
    for l in reversed(range(depth)):
        recv_1[l], recv_2[l] = landed("ff", l, dh)
    out_1 = adamw_layers("1", w_mlp_in, m_w_mlp_in, v_w_mlp_in, recv_1)
    out_2 = adamw_layers("2", w_mlp_out, m_w_mlp_out, v_w_mlp_out, recv_2)
    for l in reversed(range(depth)):
        recv_o[l], = landed("o", l, out_2[1])
    out_o = adamw_layers("o", w_o, m_w_o, v_w_o, recv_o)
    for l in reversed(range(depth)):
        recv_in[l], = landed("in", l, out_o[1])
    out_in = adamw_layers("in", w_in, m_w_in, v_w_in, recv_in)
    big_out = [[outs[kind] for outs in (out_in, out_o, out_1, out_2)] for kind in range(4)]

    small_parts, = _exchange_wait([small_flat], small_land, small_spec, small_sems[0], out_in[1], "small_grads_wait")
    small_g = _unflatten(_sum_devices(small_parts, "small_grads_sum"), small_shapes)
    cw = conv_w.shape[2]
    small_g[4] = lax.dynamic_slice_in_dim(small_g[4], me * cw, cw, axis=2)
    small = (norm1_g, b_f, q_norm_g, k_norm_g, conv_w, conv_b, conv_ln_g, conv_ln_b, norm2_g)
    small_m = (m_norm1_g, m_b_f, m_q_norm_g, m_k_norm_g, m_conv_w, m_conv_b, m_conv_ln_g, m_conv_ln_b, m_norm2_g)
    small_v = (v_norm1_g, v_b_f, v_q_norm_g, v_k_norm_g, v_conv_w, v_conv_b, v_conv_ln_g, v_conv_ln_b, v_norm2_g)
    small_out = _adamw(_flatten(small, 8), _flatten(small_m, 8), _flatten(small_v, 8), _flatten(small_g, 8), "adamw_small")
    small_out = [small_g] + [_unflatten(o, [w.shape for w in small]) for o in small_out]

    def group(kind):
        s_, b_ = small_out[kind], big_out[kind]
        return [s_[0], b_[0], s_[1], s_[2], s_[3], s_[4], s_[5], s_[6], s_[7], b_[1], s_[8], b_[2], b_[3]]

    return (loss, grad_x, *group(0), *group(1), *group(2), *group(3))
```

```python
import functools

import jax
import jax.numpy as jnp
from jax import lax
from jax.experimental import pallas as pl
from jax.experimental.pallas import tpu as pltpu

F32 = jnp.float32
BF16 = jnp.bfloat16

EPS = 1e-6
HEAD_DIM = 64
LANES = 128
PAIR = 2 * LANES
N_DEV = 8
CONV_TAPS = 31
CONV_PAD = 32
NEG = -1e30

ADAM_LR = 0.001
ADAM_B1 = 0.9
ADAM_B2 = 0.999
ADAM_EPS = 1e-08
ADAM_WD = 0.01
ADAM_STEP = 10

TM = 512
TQ = 512
CONV_ROWS = 128
FLAT_ROWS = 1024
MESH = pl.DeviceIdType.MESH


def _params(*sem):
    return pltpu.CompilerParams(dimension_semantics=sem, vmem_limit_bytes=56 * 1024 * 1024)


def _split3(x):
    hi = x.astype(BF16)
    r1 = x - hi.astype(F32)
    mid = r1.astype(BF16)
    lo = (r1 - mid.astype(F32)).astype(BF16)
    return hi, mid, lo


def _dot(a, b):
    return jnp.dot(a, b, preferred_element_type=F32)


def _dot_nt(a, b):
    return lax.dot_general(a, b, (((1,), (1,)), ((), ())), preferred_element_type=F32)


def _dot_tn(a, b):
    return lax.dot_general(a, b, (((0,), (0,)), ((), ())), preferred_element_type=F32)


def _dot3(x, mat):
    hi, mid, lo = _split3(x)
    return _dot(hi, mat) + _dot(mid, mat) + _dot(lo, mat)


def _dot2(x, mat):
    hi = x.astype(BF16)
    lo = (x - hi.astype(F32)).astype(BF16)
    return _dot(hi, mat) + _dot(lo, mat)


def _dot3_r(mat, x):
    hi, mid, lo = _split3(x)
    return _dot(mat, hi) + _dot(mat, mid) + _dot(mat, lo)


def _iota(shape, dim):
    return lax.broadcasted_iota(jnp.int32, shape, dim)


def _sigmoid(x):
    return 1.0 / (1.0 + jnp.exp(-x))


def _matmul(a, b, *, mode, out_dtypes, name, epilogue=None, extras=(), tm=TM, tn=1024, tk=1024):
    if mode == "nn":
        (m, k), (k2, n) = a.shape, b.shape
    elif mode == "nt":
        (m, k), (n, k2) = a.shape, b.shape
    else:
        (k, m), (k2, n) = a.shape, b.shape
    assert k == k2, (name, a.shape, b.shape)
    tm, tn, tk = min(tm, m), min(tn, n), min(tk, k)
    assert m % tm == 0 and n % tn == 0 and k % tk == 0, (name, m, n, k, tm, tn, tk)
    nk = k // tk
    a_mode = dict(pipeline_mode=pl.Buffered(1)) if (m == tm and nk == 1) else {}
    b_mode = dict(pipeline_mode=pl.Buffered(1)) if (n == tn and nk == 1) else {}
    if mode == "tn":
        a_spec = pl.BlockSpec((tk, tm), lambda i, j, kk: (kk, i), **a_mode)
    else:
        a_spec = pl.BlockSpec((tm, tk), lambda i, j, kk: (i, kk), **a_mode)
    if mode == "nt":
        b_spec = pl.BlockSpec((tn, tk), lambda i, j, kk: (j, kk), **b_mode)
    else:
        b_spec = pl.BlockSpec((tk, tn), lambda i, j, kk: (kk, j), **b_mode)
    dot = {"nn": _dot, "nt": _dot_nt, "tn": _dot_tn}[mode]
    tile = pl.BlockSpec((tm, tn), lambda i, j, kk: (i, j))
    n_ex, n_out = len(extras), len(out_dtypes)
    acc_in_out = nk > 1 and epilogue is None and out_dtypes[0] == F32

    def body(a_ref, b_ref, *rest):
        ex_refs, out_refs = rest[:n_ex], rest[n_ex:n_ex + n_out]
        part = dot(a_ref[...].astype(BF16), b_ref[...].astype(BF16))

        def finish(acc):
            res = epilogue(acc, *[e[...] for e in ex_refs]) if epilogue is not None else (acc,) * n_out
            for o_ref, r in zip(out_refs, res):
                o_ref[...] = r.astype(o_ref.dtype)

        if nk == 1:
            finish(part)
        else:
            acc_ref = out_refs[0] if acc_in_out else rest[-1]
            kk = pl.program_id(2)

            @pl.when(kk == 0)
            def _():
                acc_ref[...] = part

            @pl.when(kk > 0)
            def _():
                acc_ref[...] += part

            @pl.when(kk == nk - 1)
            def _():
                if acc_in_out:
                    for o_ref in out_refs[1:]:
                        o_ref[...] = acc_ref[...].astype(o_ref.dtype)
                else:
                    finish(acc_ref[...])

    outs = pl.pallas_call(
        body,
        name=name,
        grid=(m // tm, n // tn, nk),
        in_specs=[a_spec, b_spec] + [tile] * n_ex,
        out_specs=[tile] * n_out,
        out_shape=[jax.ShapeDtypeStruct((m, n), dt) for dt in out_dtypes],
        scratch_shapes=[pltpu.VMEM((tm, tn), F32)] if nk > 1 and not acc_in_out else [],
        compiler_params=_params("parallel", "parallel", "arbitrary"),
    )(a, b, *extras)
    return outs if n_out > 1 else outs[0]


def _rms_matmul(x, g, b, *, out_dtypes, name, epilogue=None, tm=TM):
    s, d = x.shape
    n = b.shape[1]
    ts = min(tm, s)
    n_out = len(out_dtypes)

    def body(x_ref, g_ref, b_ref, u_ref, *out_refs):
        xv = x_ref[...]
        u = (xv * lax.rsqrt(jnp.mean(xv * xv, axis=-1, keepdims=True) + EPS) * g_ref[...]).astype(BF16)
        u_ref[...] = u
        acc = _dot(u, b_ref[...])
        res = epilogue(acc) if epilogue is not None else (acc,)
        for o_ref, r in zip(out_refs, res):
            o_ref[...] = r.astype(o_ref.dtype)

    row = lambda w: pl.BlockSpec((ts, w), lambda i: (i, 0))
    return pl.pallas_call(
        body, name=name, grid=(s // ts,),
        in_specs=[row(d), pl.BlockSpec((1, d), lambda i: (0, 0)), pl.BlockSpec((d, n), lambda i: (0, 0), pipeline_mode=pl.Buffered(1))],
        out_specs=[row(d)] + [row(n)] * n_out,
        out_shape=[jax.ShapeDtypeStruct((s, d), BF16)] + [jax.ShapeDtypeStruct((s, n), dt) for dt in out_dtypes],
        compiler_params=_params("parallel"),
    )(x, g, b)


def _matmul_rms_bwd(a, b, x, g, dres, name):
    s, k = a.shape
    d = b.shape[0]
    ts = min(TM, s)

    def body(a_ref, b_ref, x_ref, g_ref, dres_ref, dx_ref, dx16_ref, dg_ref):
        @pl.when(pl.program_id(0) == 0)
        def _():
            dg_ref[...] = jnp.zeros_like(dg_ref)

        duv = _dot_nt(a_ref[...], b_ref[...])
        xv = x_ref[...]
        r = lax.rsqrt(jnp.mean(xv * xv, axis=-1, keepdims=True) + EPS)
        xh = xv * r
        dxh = duv * g_ref[...]
        dx = dres_ref[...] + r * (dxh - xh * jnp.mean(dxh * xh, axis=-1, keepdims=True))
        dx_ref[...] = dx
        dx16_ref[...] = dx.astype(BF16)
        dg_ref[...] += jnp.sum(duv * xh, axis=0, keepdims=True)

    row = pl.BlockSpec((ts, d), lambda i: (i, 0))
    vec = pl.BlockSpec((1, d), lambda i: (0, 0))
    return pl.pallas_call(
        body, name=name, grid=(s // ts,),
        in_specs=[pl.BlockSpec((ts, k), lambda i: (i, 0)), pl.BlockSpec((d, k), lambda i: (0, 0), pipeline_mode=pl.Buffered(1)),
                  row, vec, row],
        out_specs=[row, row, vec],
        out_shape=[jax.ShapeDtypeStruct((s, d), F32), jax.ShapeDtypeStruct((s, d), BF16), jax.ShapeDtypeStruct((1, d), F32)],
        compiler_params=_params("arbitrary"),
    )(a, b, x, g, dres)


def _matmul_loss(a, b, res, target, name):
    s, k = a.shape
    d = b.shape[1]
    ts = min(TM, s)

    def body(a_ref, b_ref, res_ref, t_ref, dy_ref, dy16_ref, sq_ref):
        @pl.when(pl.program_id(0) == 0)
        def _():
            sq_ref[...] = jnp.zeros_like(sq_ref)

        err = _dot(a_ref[...], b_ref[...]) + res_ref[...] - t_ref[...]
        dy = err * (1.0 / d)
        dy_ref[...] = dy
        dy16_ref[...] = dy.astype(BF16)
        sq_ref[...] += jnp.sum(err * err, axis=0, keepdims=True)

    row = pl.BlockSpec((ts, d), lambda i: (i, 0))
    vec = pl.BlockSpec((1, d), lambda i: (0, 0))
    return pl.pallas_call(
        body, name=name, grid=(s // ts,),
        in_specs=[pl.BlockSpec((ts, k), lambda i: (i, 0)), pl.BlockSpec((k, d), lambda i: (0, 0), pipeline_mode=pl.Buffered(1)),
                  row, row],
        out_specs=[row, row, vec],
        out_shape=[jax.ShapeDtypeStruct((s, d), F32), jax.ShapeDtypeStruct((s, d), BF16), jax.ShapeDtypeStruct((1, d), F32)],
        compiler_params=_params("arbitrary"),
    )(a, b, res, target)


def _ln_silu_fwd(y, g, b, mixed, name):
    s, c = y.shape
    ts = min(TM, s)
    assert mixed.shape == (s, 2 * c)

    def body(y_ref, g_ref, b_ref, mixed_ref, h_ref):
        yv = y_ref[...]
        mu = jnp.mean(yv, axis=-1, keepdims=True)
        yc = yv - mu
        z = yc * lax.rsqrt(jnp.mean(yc * yc, axis=-1, keepdims=True) + EPS) * g_ref[...] + b_ref[...]
        h_ref[...] = (z * _sigmoid(z)).astype(BF16)

    row = pl.BlockSpec((ts, c), lambda i: (i, 0))
    vec = pl.BlockSpec((1, c), lambda i: (0, 0))
    return pl.pallas_call(
        body, name=name, grid=(s // ts,), in_specs=[row, vec, vec, pl.BlockSpec(memory_space=pl.ANY)],
        out_specs=pl.BlockSpec((ts, c), lambda i: (i, 1)),
        out_shape=jax.ShapeDtypeStruct((s, 2 * c), BF16), input_output_aliases={3: 0},
        compiler_params=_params("parallel"),
    )(y, g, b, mixed)


def _ln_silu_bwd(y, g, b, dmixed, name):
    s, c = y.shape
    ts = min(TM, s)

    def body(y_ref, g_ref, b_ref, dh_ref, dy_ref, dg_ref, db_ref):
        @pl.when(pl.program_id(0) == 0)
        def _():
            dg_ref[...] = jnp.zeros_like(dg_ref)
            db_ref[...] = jnp.zeros_like(db_ref)

        yv = y_ref[...]
        mu = jnp.mean(yv, axis=-1, keepdims=True)
        yc = yv - mu
        r = lax.rsqrt(jnp.mean(yc * yc, axis=-1, keepdims=True) + EPS)
        yh = yc * r
        z = yh * g_ref[...] + b_ref[...]
        sg = _sigmoid(z)
        dz = dh_ref[...] * (sg * (1.0 + z * (1.0 - sg)))
        dg_ref[...] += jnp.sum(dz * yh, axis=0, keepdims=True)
        db_ref[...] += jnp.sum(dz, axis=0, keepdims=True)
        dyh = dz * g_ref[...]
        dy_ref[...] = r * (dyh - jnp.mean(dyh, axis=-1, keepdims=True) - yh * jnp.mean(dyh * yh, axis=-1, keepdims=True))

    row = pl.BlockSpec((ts, c), lambda i: (i, 0))
    vec = pl.BlockSpec((1, c), lambda i: (0, 0))
    return pl.pallas_call(
        body, name=name, grid=(s // ts,),
        in_specs=[row, vec, vec, pl.BlockSpec((ts, c), lambda i: (i, 1))], out_specs=[row, vec, vec],
        out_shape=[jax.ShapeDtypeStruct((s, c), F32), jax.ShapeDtypeStruct((1, c), F32), jax.ShapeDtypeStruct((1, c), F32)],
        compiler_params=_params("arbitrary"),
    )(y, g, b, dmixed)


def _head_masks():
    lane2 = _iota((1, PAIR), 1)
    lane1 = _iota((1, LANES), 1)
    qa = (lane2 < HEAD_DIM) | ((lane2 >= LANES) & (lane2 < LANES + 3))
    qb = ((lane2 >= HEAD_DIM) & (lane2 < LANES)) | ((lane2 >= LANES + 3) & (lane2 < LANES + 6))
    return (qa, qb), (lane1 < HEAD_DIM, lane1 >= HEAD_DIM)


def _group_matrix(width):
    shift = HEAD_DIM.bit_length() - 1
    return ((_iota((width, width), 0) >> shift) == (_iota((width, width), 1) >> shift)).astype(BF16)


def _prep_fwd(proj, gq, gk, bf, n_heads, name):
    s = proj.shape[0]
    aw = n_heads * HEAD_DIM
    n_pairs = n_heads // 2
    ts = min(TM, s)
    f_col = (proj.shape[1] - LANES) // LANES

    def body(q_ref, k_ref, v_ref, f_ref, gq_ref, gk_ref, bf_ref, qa_ref, ka_ref, vb_ref, carry_ref):
        @pl.when(pl.program_id(0) == 0)
        def _():
            carry_ref[...] = jnp.zeros_like(carry_ref)

        gmat = _group_matrix(aw)

        def head_norm(xv, g):
            ms = _dot2(xv * xv, gmat) * (1.0 / HEAD_DIM)
            return xv * lax.rsqrt(ms + EPS) * g

        qn = head_norm(q_ref[...], gq_ref[...]) * (HEAD_DIM ** -0.5)
        kn = head_norm(k_ref[...], gk_ref[...])
        z = f_ref[...] + bf_ref[...]
        logf = jnp.minimum(z, 0.0) - jnp.log(1.0 + jnp.exp(-jnp.abs(z)))
        tri = (_iota((ts, ts), 0) >= _iota((ts, ts), 1)).astype(BF16)
        c = _dot3_r(tri, logf) + carry_ref[...]
        carry_ref[...] = c[ts - 1:ts, :]
        terms = _split3(-c)
        row, col = _iota((LANES, LANES), 0), _iota((LANES, LANES), 1)
        ones = jnp.where(_iota((ts, LANES), 1) < 6, 1.0, 0.0).astype(BF16)
        for p in range(n_pairs):
            extra = jnp.zeros((ts, LANES), F32)
            for t, term in enumerate(terms):
                sel = ((row == 2 * p) & (col == t)) | ((row == 2 * p + 1) & (col == 3 + t))
                extra += _dot(term, sel.astype(BF16))
            lo, hi = p * PAIR, p * PAIR + LANES
            ka_ref[:, lo:hi] = kn[:, p * LANES:(p + 1) * LANES].astype(BF16)
            ka_ref[:, hi:hi + LANES] = extra.astype(BF16)
            qa_ref[:, lo:hi] = qn[:, p * LANES:(p + 1) * LANES].astype(BF16)
            qa_ref[:, hi:hi + LANES] = ones
            vb_ref[:, lo:hi] = v_ref[:, p * LANES:(p + 1) * LANES].astype(BF16)
            vb_ref[:, hi:hi + LANES] = jnp.ones((ts, LANES), BF16)

    blk = lambda j: pl.BlockSpec((ts, aw), lambda i: (i, j))
    vec = lambda w: pl.BlockSpec((1, w), lambda i: (0, 0))
    return pl.pallas_call(
        body, name=name, grid=(s // ts,),
        in_specs=[blk(0), blk(1), blk(2), pl.BlockSpec((ts, LANES), lambda i: (i, f_col)), vec(aw), vec(aw), vec(LANES)],
        out_specs=[pl.BlockSpec((ts, n_pairs * PAIR), lambda i: (i, 0))] * 3,
        out_shape=[jax.ShapeDtypeStruct((s, n_pairs * PAIR), BF16)] * 3,
        scratch_shapes=[pltpu.VMEM((1, LANES), F32)],
        compiler_params=_params("arbitrary"),
    )(proj, proj, proj, proj, gq, gk, bf)


def _prep_bwd(proj, dq, dka, drow, dcol, gq, gk, bf, n_heads, name):
    s = proj.shape[0]
    aw = n_heads * HEAD_DIM
    n_pairs = n_heads // 2
    ts = min(TM, s)
    nt = s // ts
    f_col = (proj.shape[1] - LANES) // LANES
    shift = HEAD_DIM.bit_length() - 1

    def body(q_ref, k_ref, f_ref, dq_ref, dka_ref, drow_ref, dcol_ref, gq_ref, gk_ref, bf_ref,
             dpq_ref, dpk_ref, dpf_ref, dgq_ref, dgk_ref, dbf_ref, carry_ref):
        @pl.when(pl.program_id(0) == 0)
        def _():
            carry_ref[...] = jnp.zeros_like(carry_ref)
            dgq_ref[...] = jnp.zeros_like(dgq_ref)
            dgk_ref[...] = jnp.zeros_like(dgk_ref)
            dbf_ref[...] = jnp.zeros_like(dbf_ref)

        gmat = _group_matrix(aw)

        def head_norm_bwd(xv, g, dn):
            r = lax.rsqrt(_dot2(xv * xv, gmat) * (1.0 / HEAD_DIM) + EPS)
            xh = xv * r
            dxh = dn * g
            dx = r * (dxh - xh * (_dot2(dxh * xh, gmat) * (1.0 / HEAD_DIM)))
            return dx, jnp.sum(dn * xh, axis=0, keepdims=True)

        dkav = dka_ref[...]
        dx, dg = head_norm_bwd(q_ref[...], gq_ref[...], dq_ref[...] * (HEAD_DIM ** -0.5))
        dpq_ref[...] = dx.astype(BF16)
        dgq_ref[...] += dg
        dkn = jnp.concatenate([dkav[:, p * PAIR:p * PAIR + LANES] for p in range(n_pairs)], axis=1)
        dx, dg = head_norm_bwd(k_ref[...], gk_ref[...], dkn)
        dpk_ref[...] = dx.astype(BF16)
        dgk_ref[...] += dg

        pick = (_iota((aw, LANES), 0) == (_iota((aw, LANES), 1) << shift)).astype(BF16)
        dc = _dot3(drow_ref[...], pick)
        r16, c16 = _iota((16, LANES), 0), _iota((16, LANES), 1)
        for p in range(n_pairs):
            place = ((r16 < 2) & (c16 == 2 * p + r16)).astype(BF16)
            for term in _split3(dcol_ref[p]):
                dc -= _dot_tn(term, place)
        triu = (_iota((ts, ts), 0) <= _iota((ts, ts), 1)).astype(BF16)
        dlogf = _dot3_r(triu, dc) + carry_ref[...]
        carry_ref[...] = dlogf[0:1, :]
        z = f_ref[...] + bf_ref[...]
        dz = dlogf * (1.0 / (1.0 + jnp.exp(z)))
        dpf_ref[...] = dz.astype(BF16)
        dbf_ref[...] += jnp.sum(dz, axis=0, keepdims=True)

    rev = lambda w, j: pl.BlockSpec((ts, w), lambda i: (nt - 1 - i, j))
    vec = lambda w: pl.BlockSpec((1, w), lambda i: (0, 0))
    return pl.pallas_call(
        body, name=name, grid=(nt,),
        in_specs=[rev(aw, 0), rev(aw, 1), rev(LANES, f_col), rev(aw, 0), rev(n_pairs * PAIR, 0), rev(aw, 0),
                  pl.BlockSpec((n_pairs, 16, ts), lambda i: (0, 0, nt - 1 - i)), vec(aw), vec(aw), vec(LANES)],
        out_specs=[rev(aw, 0), rev(aw, 0), rev(LANES, 0), vec(aw), vec(aw), vec(LANES)],
        out_shape=[jax.ShapeDtypeStruct((s, aw), BF16), jax.ShapeDtypeStruct((s, aw), BF16), jax.ShapeDtypeStruct((s, LANES), BF16),
                   jax.ShapeDtypeStruct((1, aw), F32), jax.ShapeDtypeStruct((1, aw), F32), jax.ShapeDtypeStruct((1, LANES), F32)],
        scratch_shapes=[pltpu.VMEM((1, LANES), F32)],
        compiler_params=_params("arbitrary"),
    )(proj, proj, proj, dq, dka, drow, dcol, gq, gk, bf)


def _attn_fwd(qa, ka, vb, n_heads, mix_width, name):
    s = qa.shape[0]
    aw = n_heads * HEAD_DIM
    n_pairs = n_heads // 2
    tq = min(2 * TQ, s)

    def body(q_ref, k_ref, v_ref, o_ref, lse_ref, o16_ref):
        i = pl.program_id(1)
        qmasks, omasks = _head_masks()
        qv = q_ref[...]
        causal = _iota((tq, tq), 1) <= _iota((tq, tq), 0)
        qhs = [jnp.where(qmasks[h], qv, jnp.zeros_like(qv)) for h in range(2)]

        def scores(j):
            kv = k_ref[pl.ds(pl.multiple_of(j * tq, tq), tq), :]
            return tuple(_dot_nt(qhs[h], kv) for h in range(2))

        def update(j, state, scs, masked):
            vv = v_ref[pl.ds(pl.multiple_of(j * tq, tq), tq), :]
            out = []
            for h in range(2):
                m, acc = state[h]
                sc = jnp.where(causal, scs[h], NEG) if masked else scs[h]
                m_new = jnp.maximum(m, jnp.max(sc, axis=1, keepdims=True))
                p = jnp.exp(sc - m_new).astype(BF16)
                out.append((m_new, jnp.exp(m - m_new) * acc + _dot(p, vv)))
            return tuple(out)

        def body(j, state):
            return update(j, state, scores(j), False)

        init = ((jnp.full((tq, 1), NEG, F32), jnp.zeros((tq, PAIR), F32)),) * 2
        state = lax.fori_loop(0, i, body, init)
        res = []
        for m, acc in update(i, state, scores(i), True):
            l = acc[:, LANES:LANES + 1]
            res.append((acc[:, :LANES] * (1.0 / l), m + jnp.log(l)))
        out = jnp.where(omasks[0], res[0][0], res[1][0])
        o_ref[...] = out
        o16_ref[...] = out.astype(BF16)
        lse_ref[...] = jnp.where(omasks[0], res[0][1], res[1][1])

    return pl.pallas_call(
        body, name=name, grid=(n_pairs, s // tq),
        in_specs=[pl.BlockSpec((tq, PAIR), lambda p, i: (i, p)), pl.BlockSpec((s, PAIR), lambda p, i: (0, p)),
                  pl.BlockSpec((s, PAIR), lambda p, i: (0, p))],
        out_specs=[pl.BlockSpec((tq, LANES), lambda p, i: (i, p))] * 3,
        out_shape=[jax.ShapeDtypeStruct((s, aw), F32)] * 2 + [jax.ShapeDtypeStruct((s, mix_width), BF16)],
        compiler_params=_params("parallel", "parallel"),
    )(qa, ka, vb)


def _attn_bwd(qa, ka, vb, o, lse, dmixed, n_heads, name):
    s = qa.shape[0]
    aw = n_heads * HEAD_DIM
    n_pairs = n_heads // 2
    tq = min(2 * TQ, s)
    nq = s // tq

    def body(q_ref, k_ref, v_ref, o_ref, lse_ref, do_ref, dq_ref, dka_ref, dv_ref, drow_ref, dcol_ref, delta_ref):
        j = pl.program_id(1)
        qmasks, omasks = _head_masks()

        @pl.when(j == 0)
        def _():
            dq_ref[...] = jnp.zeros_like(dq_ref)
            drow_ref[...] = jnp.zeros_like(drow_ref)
            for c in range(nq):
                rows = slice(c * tq, (c + 1) * tq)
                prod = do_ref[rows, :] * o_ref[rows, :]
                da = jnp.sum(jnp.where(omasks[0], prod, 0.0), axis=1, keepdims=True)
                db = jnp.sum(jnp.where(omasks[1], prod, 0.0), axis=1, keepdims=True)
                delta_ref[rows, :] = jnp.where(omasks[0], da, db)

        dka_ref[...] = jnp.zeros_like(dka_ref)
        dv_ref[...] = jnp.zeros_like(dv_ref)
        dcol_ref[...] = jnp.zeros_like(dcol_ref)
        kv = k_ref[...]
        kk = kv[:, :LANES]
        vv = v_ref[...]
        causal = _iota((tq, tq), 1) <= _iota((tq, tq), 0)

        def step(i, masked):
            off = pl.multiple_of(i * tq, tq)
            qv = q_ref[pl.ds(off, tq), :]
            dov = do_ref[pl.ds(off, tq), :]
            lsev = lse_ref[pl.ds(off, tq), :]
            dlv = delta_ref[pl.ds(off, tq), :]
            for h in range(2):
                qh = jnp.where(qmasks[h], qv, jnp.zeros_like(qv))
                doh = jnp.where(omasks[h], dov, 0.0).astype(BF16)
                lane = h * HEAD_DIM
                sc = _dot_nt(qh, kv)
                if masked:
                    sc = jnp.where(causal, sc, NEG)
                p = jnp.exp(sc - lsev[:, lane:lane + 1])
                dv_ref[...] += _dot_tn(p.astype(BF16), doh)
                dp = _dot_nt(doh, vv)
                dsf = p * (dp - dlv[:, lane:lane + 1])
                drow_ref[pl.ds(off, tq), :] += jnp.where(omasks[h], jnp.sum(dsf, axis=1, keepdims=True), 0.0)
                dcol_ref[0, h:h + 1, :] += jnp.sum(dsf, axis=0, keepdims=True)
                ds = dsf.astype(BF16)
                dka_ref[...] += _dot_tn(ds, qh)
                dq_ref[pl.ds(off, tq), :] += jnp.where(omasks[h], _dot(ds, kk), 0.0)

        step(j, True)

        def loop_body(i, carry):
            step(i, False)
            return carry

        lax.fori_loop(j + 1, nq, loop_body, 0)

    full = lambda w: pl.BlockSpec((s, w), lambda p, j: (0, p))
    blk = lambda w: pl.BlockSpec((tq, w), lambda p, j: (j, p))
    return pl.pallas_call(
        body, name=name, grid=(n_pairs, nq),
        in_specs=[full(PAIR), blk(PAIR), pl.BlockSpec((tq, LANES), lambda p, j: (j, 2 * p)), full(LANES), full(LANES), full(LANES)],
        out_specs=[full(LANES), blk(PAIR), blk(LANES), full(LANES), pl.BlockSpec((1, 16, tq), lambda p, j: (p, 0, j))],
        out_shape=[jax.ShapeDtypeStruct((s, aw), F32), jax.ShapeDtypeStruct((s, n_pairs * PAIR), F32),
                   jax.ShapeDtypeStruct((s, aw), F32), jax.ShapeDtypeStruct((s, aw), F32),
                   jax.ShapeDtypeStruct((n_pairs, 16, s), F32)],
        scratch_shapes=[pltpu.VMEM((s, LANES), F32)],
        compiler_params=_params("parallel", "arbitrary"),
    )(qa, ka, vb, o, lse, dmixed)


def _conv_fwd(proj, w32, bias, n_ch, a_col, g_col, name):
    s = proj.shape[0]
    rows = min(CONV_ROWS, s)

    def body(a_ref, g_ref, w_ref, b_ref, y_ref, pad_ref):
        pad_ref[0:CONV_PAD, :] = jnp.zeros((CONV_PAD, LANES), F32)
        pad_ref[CONV_PAD:CONV_PAD + s, :] = a_ref[...] * _sigmoid(g_ref[...])
        wv = w_ref[...]
        for c in range(s // rows):
            acc = jnp.broadcast_to(b_ref[...], (rows, LANES))
            for t in range(CONV_TAPS):
                start = c * rows + CONV_PAD - (CONV_TAPS - 1) + t
                acc = acc + wv[t:t + 1, :] * pad_ref[start:start + rows, :]
            y_ref[c * rows:(c + 1) * rows, :] = acc

    col = lambda j0: pl.BlockSpec((s, LANES), lambda c: (0, j0 + c))
    return pl.pallas_call(
        body, name=name, grid=(n_ch // LANES,),
        in_specs=[col(a_col), col(g_col), pl.BlockSpec((CONV_PAD, LANES), lambda c: (0, c)), pl.BlockSpec((1, LANES), lambda c: (0, c))],
        out_specs=pl.BlockSpec((s, LANES), lambda c: (0, c)),
        out_shape=jax.ShapeDtypeStruct((s, n_ch), F32),
        scratch_shapes=[pltpu.VMEM((s + CONV_PAD, LANES), F32)],
        compiler_params=_params("parallel"),
    )(proj, proj, w32, bias)


def _conv_bwd(proj, w32, dy, n_ch, a_col, g_col, name):
    s = proj.shape[0]
    rows = min(CONV_ROWS, s)
    sub = 8

    def fold(x):
        acc = x[0:sub, :]
        for r in range(1, rows // sub):
            acc = acc + x[r * sub:(r + 1) * sub, :]
        return acc

    def body(a_ref, g_ref, w_ref, dy_ref, da_ref, dg_ref, dw_ref, padh_ref, padd_ref):
        sg = _sigmoid(g_ref[...])
        padh_ref[0:CONV_PAD, :] = jnp.zeros((CONV_PAD, LANES), F32)
        padh_ref[CONV_PAD:CONV_PAD + s, :] = a_ref[...] * sg
        padd_ref[0:s, :] = dy_ref[...]
        padd_ref[s:s + CONV_PAD, :] = jnp.zeros((CONV_PAD, LANES), F32)
        wv = w_ref[...]
        dw = [jnp.zeros((sub, LANES), F32) for _ in range(CONV_TAPS + 1)]
        for c in range(s // rows):
            r0 = c * rows
            acc = jnp.zeros((rows, LANES), F32)
            dyc = dy_ref[r0:r0 + rows, :]
            for t in range(CONV_TAPS):
                back = r0 + (CONV_TAPS - 1) - t
                acc = acc + wv[t:t + 1, :] * padd_ref[back:back + rows, :]
                start = r0 + CONV_PAD - (CONV_TAPS - 1) + t
                dw[t] = dw[t] + fold(dyc * padh_ref[start:start + rows, :])
            dw[CONV_TAPS] = dw[CONV_TAPS] + fold(dyc)
            av = a_ref[r0:r0 + rows, :]
            sgc = _sigmoid(g_ref[r0:r0 + rows, :])
            da_ref[r0:r0 + rows, :] = (acc * sgc).astype(BF16)
            dg_ref[r0:r0 + rows, :] = (acc * av * sgc * (1.0 - sgc)).astype(BF16)
        for t in range(CONV_TAPS + 1):
            dw_ref[t:t + 1, :] = jnp.sum(dw[t], axis=0, keepdims=True)

    col = lambda j0: pl.BlockSpec((s, LANES), lambda c: (0, j0 + c))
    wspec = pl.BlockSpec((CONV_PAD, LANES), lambda c: (0, c))
    return pl.pallas_call(
        body, name=name, grid=(n_ch // LANES,),
        in_specs=[col(a_col), col(g_col), wspec, col(0)],
        out_specs=[col(0), col(0), wspec],
        out_shape=[jax.ShapeDtypeStruct((s, n_ch), BF16), jax.ShapeDtypeStruct((s, n_ch), BF16),
                   jax.ShapeDtypeStruct((CONV_PAD, n_ch), F32)],
        scratch_shapes=[pltpu.VMEM((s + CONV_PAD, LANES), F32), pltpu.VMEM((s + CONV_PAD, LANES), F32)],
        compiler_params=_params("parallel"),
    )(proj, proj, w32, dy)


def _my_place():
    return lax.axis_index("x"), lax.axis_index("y"), lax.axis_index("c")


def _flip(place, k):
    x, y, c = place
    return (1 - x if k & 4 else x, 1 - y if k & 2 else y, 1 - c if k & 1 else c)


def _dev_id(place):
    return 4 * place[0] + 2 * place[1] + place[2]


def _wait_all(ref, send_sem, recv_sem, place):
    pltpu.make_async_remote_copy(src_ref=ref, dst_ref=ref, send_sem=send_sem, recv_sem=recv_sem,
                                 device_id=place, device_id_type=MESH).wait()


def _window(kind, ref, dev, n):
    if kind == "slot":
        return ref.at[dev]
    if kind == "rows":
        return ref.at[pl.ds(pl.multiple_of(dev * n, n), n), :]
    return ref.at[:, pl.ds(pl.multiple_of(dev * n, n), n)]


def _hbm(x):
    return pltpu.with_memory_space_constraint(x, pltpu.HBM)


_EFFECT = pltpu.SideEffectType.DATAFLOW_SIDE_EFFECTING


def _exchange_start(srcs, lands, specs, groups, name):
    n = len(srcs)
    n_g = len(groups)

    def body(*refs):
        src_refs, land_refs = refs[:n], refs[n:2 * n]
        sems = refs[2 * n:2 * n + 2 * n_g]
        token = refs[-1]
        place = _my_place()
        me = _dev_id(place)
        for g, units in enumerate(groups):
            for j, u in enumerate(units):
                mode, kind, cnt = specs[u]
                for k in list(range(1, N_DEV)) + [0]:
                    peer = _flip(place, k)
                    if mode == "gather":
                        src, dst = src_refs[u], _window(kind, land_refs[u], me, cnt)
                    else:
                        src, dst = _window(kind, src_refs[u], _dev_id(peer), cnt), land_refs[u].at[k]
                    pltpu.make_async_remote_copy(src_ref=src, dst_ref=dst, send_sem=sems[2 * g].at[j], recv_sem=sems[2 * g + 1].at[j],
                                                 device_id=peer, device_id_type=MESH).start()
        token[...] = jnp.zeros_like(token)

    hbm = pl.BlockSpec(memory_space=pltpu.HBM)
    sem = pl.BlockSpec(memory_space=pltpu.SEMAPHORE)
    out_shape = [pltpu.SemaphoreType.DMA((len(units),)) for units in groups for _ in range(2)]
    out_shape += [pltpu.HBM(x.shape, x.dtype) for x in lands] + [jax.ShapeDtypeStruct((8, LANES), F32)]
    outs = pl.pallas_call(
        body, name=name, out_shape=out_shape,
        in_specs=[hbm] * (2 * n), out_specs=[sem] * (2 * n_g) + [hbm] * n + [pl.BlockSpec(memory_space=pltpu.VMEM)],
        input_output_aliases={n + u: 2 * n_g + u for u in range(n)},
        compiler_params=pltpu.CompilerParams(has_side_effects=_EFFECT),
    )(*[_hbm(x) for x in srcs], *[_hbm(x) for x in lands])
    sem_pairs = [(outs[2 * g], outs[2 * g + 1]) for g in range(n_g)]
    return sem_pairs, list(outs[2 * n_g:2 * n_g + n]), outs[-1]


def _exchange_wait(srcs, lands, specs, sem_pair, after, name):
    n = len(lands)

    def body(*refs):
        land_refs = refs[n:2 * n]
        send_sems, recv_sems = refs[2 * n], refs[2 * n + 1]
        place = _my_place()
        for u in range(n):
            _wait_all(land_refs[u], send_sems.at[u], recv_sems.at[u], place)

    hbm = pl.BlockSpec(memory_space=pltpu.HBM)
    sem = pl.BlockSpec(memory_space=pltpu.SEMAPHORE)
    outs = pl.pallas_call(
        body, name=name, out_shape=[pltpu.HBM(x.shape, x.dtype) for x in lands],
        in_specs=[hbm] * (2 * n) + [sem, sem, pl.BlockSpec(memory_space=pl.ANY)], out_specs=[hbm] * n,
        input_output_aliases={n + u: u for u in range(n)},
        compiler_params=pltpu.CompilerParams(has_side_effects=_EFFECT),
    )(*[_hbm(x) for x in srcs], *lands, sem_pair[0], sem_pair[1], after)
    return list(outs)


def _sum_devices(parts, name):
    _, r, w = parts.shape

    def body(p_ref, out_ref):
        acc = p_ref[0]
        for d in range(1, N_DEV):
            acc = acc + p_ref[d]
        out_ref[...] = acc

    return pl.pallas_call(
        body, name=name, out_shape=jax.ShapeDtypeStruct((r, w), F32),
        in_specs=[pl.BlockSpec(memory_space=pltpu.VMEM)], out_specs=pl.BlockSpec(memory_space=pltpu.VMEM),
    )(parts)


def _adamw_math(w, m, v, g):
    m_new = ADAM_B1 * m + (1.0 - ADAM_B1) * g
    v_new = ADAM_B2 * v + (1.0 - ADAM_B2) * (g * g)
    m_hat = m_new / (1.0 - ADAM_B1 ** ADAM_STEP)
    v_hat = v_new / (1.0 - ADAM_B2 ** ADAM_STEP)
    return -ADAM_LR * (m_hat / (jnp.sqrt(v_hat) + ADAM_EPS) + ADAM_WD * w), m_new, v_new


def _adamw(w, m, v, g, name):
    rows = w.shape[0]
    tr = min(FLAT_ROWS, rows)
    assert rows % tr == 0, (name, rows)

    def body(w_ref, m_ref, v_ref, g_ref, d_out, m_out, v_out):
        d_out[...], m_out[...], v_out[...] = _adamw_math(w_ref[...], m_ref[...], v_ref[...], g_ref[...])

    flat = pl.BlockSpec((tr, LANES), lambda i: (i, 0))
    return pl.pallas_call(
        body, name=name, grid=(rows // tr,), in_specs=[flat] * 4, out_specs=[flat] * 3,
        out_shape=[jax.ShapeDtypeStruct((rows, LANES), F32)] * 3,
        compiler_params=_params("parallel"),
    )(w, m, v, g)


def _adamw_shard(w, m, v, recv, layer, prev, name):
    depth, a, b = w.shape
    ta = min(512, a)
    assert a % ta == 0

    def body(w_ref, m_ref, v_ref, r_ref, *rest):
        g_out, d_out, m_out, v_out = rest[-4:]
        g = r_ref[0].astype(F32)
        for k in range(1, N_DEV):
            g = g + r_ref[k].astype(F32)
        g_out[0] = g
        d_out[0], m_out[0], v_out[0] = _adamw_math(w_ref[0], m_ref[0], v_ref[0], g)

    lay = pl.BlockSpec((1, ta, b), lambda i: (layer, i, 0))
    in_specs = [lay] * 3 + [pl.BlockSpec((N_DEV, ta, b), lambda i: (0, i, 0))]
    args = [w, m, v, recv]
    aliases = {}
    if prev is not None:
        in_specs += [pl.BlockSpec(memory_space=pl.ANY)] * 4
        args += list(prev)
        aliases = {4 + i: i for i in range(4)}
    return pl.pallas_call(
        body, name=name, grid=(a // ta,), in_specs=in_specs, out_specs=[lay] * 4,
        out_shape=[jax.ShapeDtypeStruct(w.shape, F32)] * 4, input_output_aliases=aliases,
        compiler_params=_params("parallel"),
    )(*args)


def _round_up(n, mult):
    return (n + mult - 1) // mult * mult


def _flatten(parts, row_mult):
    flat = jnp.concatenate([p.reshape(-1) for p in parts])
    rows = _round_up(-(-flat.shape[0] // LANES), row_mult)
    return jnp.pad(flat, (0, rows * LANES - flat.shape[0])).reshape(rows, LANES)


def _unflatten(flat, shapes):
    flat = flat.reshape(-1)
    out, off = [], 0
    for shp in shapes:
        n = 1
        for dim in shp:
            n *= dim
        out.append(flat[off:off + n].reshape(shp))
        off += n
    return out


def kernel(x, norm1_g, w_in, b_f, q_norm_g, k_norm_g, conv_w, conv_b, conv_ln_g, conv_ln_b, w_o, norm2_g, w_mlp_in, w_mlp_out, loss_target, m_norm1_g, m_w_in, m_b_f, m_q_norm_g, m_k_norm_g, m_conv_w, m_conv_b, m_conv_ln_g, m_conv_ln_b, m_w_o, m_norm2_g, m_w_mlp_in, m_w_mlp_out, v_norm1_g, v_w_in, v_b_f, v_q_norm_g, v_k_norm_g, v_conv_w, v_conv_b, v_conv_ln_g, v_conv_ln_b, v_w_o, v_norm2_g, v_w_mlp_in, v_w_mlp_out):
    depth, d_model, n_in_loc = w_in.shape
    seq = x.shape[1]
    n_heads = b_f.shape[1]
    aw = n_heads * HEAD_DIM
    cc = conv_b.shape[1]
    n_in = n_in_loc * N_DEV
    o_f = 3 * aw
    n_all = 3 * aw + 2 * cc + LANES
    assert n_in == 3 * aw + n_heads + 2 * cc and aw + cc == d_model and n_heads % 2 == 0
    assert aw % LANES == 0 and cc % LANES == 0 and x.shape[0] == 1
    me = 4 * lax.axis_index("x") + 2 * lax.axis_index("y") + lax.axis_index("c")

    d_ff = w_mlp_in.shape[2] * N_DEV

    r_o, f_1, f_2 = w_o.shape[1], w_mlp_in.shape[2], w_mlp_out.shape[1]

    ag_src, ag_land, ag_spec = [], [], []
    for l in range(depth):
        ag_src += [w_in[l].astype(BF16), w_o[l].astype(BF16), w_mlp_in[l].astype(BF16), w_mlp_out[l].astype(BF16)]
        ag_land += [(N_DEV, d_model, n_in_loc), (N_DEV * r_o, d_model), (d_model, N_DEV * f_1), (N_DEV * f_2, d_model)]
        ag_spec += [("gather", "slot", 1), ("gather", "rows", r_o), ("gather", "cols", f_1), ("gather", "rows", f_2)]
    ag_src.append(jnp.stack(_split3(conv_w)))
    ag_land.append((N_DEV, 3) + conv_w.shape)
    ag_spec.append(("gather", "slot", 1))
    ag_groups = [grp for l in range(depth) for grp in ([4 * l] + ([4 * depth] if l == 0 else []), [4 * l + 1], [4 * l + 2, 4 * l + 3])]
    ag_land = [lax.empty(shp, BF16) for shp in ag_land]
    ag_sems, ag_land, ag_token = _exchange_start(ag_src, ag_land, ag_spec, ag_groups, "gather_start")

    def gathered(g, after):
        units = ag_groups[g]
        return _exchange_wait([ag_src[u] for u in units], [ag_land[u] for u in units], [ag_spec[u] for u in units],
                              ag_sems[g], after, f"gather_wait_{g}")

    f_all = n_all - LANES
    in_segments = ((0, o_f, 0), (o_f + n_heads, n_in, o_f), (o_f, o_f + n_heads, f_all))

    def whole_in(lin):
        pieces = []
        for c0, c1, _ in in_segments:
            while c0 < c1:
                j, off = divmod(c0, n_in_loc)
                stop = min(c1, (j + 1) * n_in_loc)
                pieces.append(lin[j][:, off:off + stop - c0])
                c0 = stop
        pieces.append(jnp.zeros((d_model, LANES - n_heads), lin.dtype))
        return jnp.concatenate(pieces, axis=1)

    def shards_of(w):
        slabs = []
        for j in range(N_DEV):
            pieces = []
            for c0, c1, a0 in sorted(in_segments):
                lo, hi = max(c0, j * n_in_loc), min(c1, (j + 1) * n_in_loc)
                if lo < hi:
                    pieces.append(w[:, a0 + lo - c0:a0 + hi - c0])
            slabs.append(jnp.concatenate(pieces, axis=1))
        return jnp.stack(slabs)

    def row(p, l, width=None):
        v = p[l].reshape(1, -1)
        return v if width is None else jnp.pad(v, ((0, 0), (0, width - v.shape[1])))

    a_col, g_col = 3 * aw // LANES, (3 * aw + cc) // LANES

    gq = [jnp.tile(row(q_norm_g, l), (1, n_heads)) for l in range(depth)]
    gk = [jnp.tile(row(k_norm_g, l), (1, n_heads)) for l in range(depth)]
    bfp = [row(b_f, l, LANES) for l in range(depth)]
    add_res = lambda acc, res: (acc + res,)
    w_all, w_out, w_ff1, w_ff2 = [None] * depth, [None] * depth, [None] * depth, [None] * depth

    h = x[0]
    saved = []
    for l in range(depth):
        if l == 0:
            lin, lc = gathered(0, h)
            lc = lc.astype(F32)
            conv_full = jnp.moveaxis(lc[:, 0] + lc[:, 1] + lc[:, 2], 0, 2).reshape(depth, CONV_TAPS, cc)
            w32 = [jnp.pad(conv_full[i], ((0, CONV_PAD - CONV_TAPS), (0, 0))) for i in range(depth)]
        else:
            lin, = gathered(3 * l, h)
        w_all[l] = whole_in(lin)
        u1, proj = _rms_matmul(h, row(norm1_g, l), w_all[l], out_dtypes=(F32,), name=f"mm_in_{l}")
        qa, ka, vb = _prep_fwd(proj, gq[l], gk[l], bfp[l], n_heads, f"prep_fwd_{l}")
        att, lse, mixed = _attn_fwd(qa, ka, vb, n_heads, aw + cc, f"attn_fwd_{l}")
        yc = _conv_fwd(proj, w32[l], row(conv_b, l), cc, a_col, g_col, f"conv_fwd_{l}")
        mixed = _ln_silu_fwd(yc, row(conv_ln_g, l), row(conv_ln_b, l), mixed, f"ln_silu_fwd_{l}")
        w_out[l], = gathered(3 * l + 1, mixed)
        x1 = _matmul(mixed, w_out[l], mode="nn", out_dtypes=(F32,), name=f"mm_o_{l}", epilogue=add_res, extras=(h,))
        w_ff1[l], w_ff2[l] = gathered(3 * l + 2, x1)
        u2, r, a = _rms_matmul(x1, row(norm2_g, l), w_ff1[l], out_dtypes=(BF16, BF16), name=f"mm_ff1_{l}", tm=256,
                               epilogue=lambda acc: (jnp.maximum(acc, 0.0), jnp.square(jnp.maximum(acc, 0.0))))
        if l < depth - 1:
            x2 = _matmul(a, w_ff2[l], mode="nn", out_dtypes=(F32,), name=f"mm_ff2_{l}", epilogue=add_res, extras=(x1,), tk=d_ff)
        else:
            x2 = None
            dh, dh16, sq = _matmul_loss(a, w_ff2[l], x1, loss_target[0], f"mm_ff2_loss_{l}")
        saved.append(dict(x_in=h, u1=u1, proj=proj, qa=qa, ka=ka, vb=vb, att=att, lse=lse, yc=yc, mixed=mixed,
                          x1=x1, u2=u2, r=r, a=a))
        h = x2

    loss = lax.psum(0.5 * jnp.sum(sq) / d_model, ("x", "y", "c"))

    g_in, g_o, g_1, g_2 = [None] * depth, [None] * depth, [None] * depth, [None] * depth
    gs = {n: [None] * depth for n in ("norm1", "bf", "qn", "kn", "convw", "convb", "lng", "lnb", "norm2")}
    scattering = {}

    def scatter_start(stage, l, srcs, specs, slabs):
        lands = [lax.empty((N_DEV,) + shp, BF16) for shp in slabs]
        sems, lands, token = _exchange_start(srcs, lands, specs, [list(range(len(srcs)))], f"scatter_start_{stage}_{l}")
        scattering[(stage, l)] = (srcs, lands, specs, sems[0])
        return token[0, 0]

    for l in reversed(range(depth)):
        sv = saved[l]
        dh1 = _matmul(dh16, w_ff2[l], mode="nt", out_dtypes=(BF16,), name=f"mm_dff2_{l}", tm=256, tn=d_ff,
                      epilogue=lambda acc, rr: (acc * (2.0 * rr.astype(F32)),), extras=(sv["r"],))
        g_2[l] = _matmul(sv["a"], dh16, mode="tn", out_dtypes=(BF16,), name=f"mm_dw2_{l}", tm=1024, tk=seq)
        g_1[l] = _matmul(sv["u2"], dh1, mode="tn", out_dtypes=(BF16,), name=f"mm_dw1_{l}", tm=d_model, tk=seq)
        tok = scatter_start("ff", l, [g_1[l], g_2[l]], [("scatter", "cols", f_1), ("scatter", "rows", f_2)],
                            [(d_model, f_1), (f_2, d_model)])
        dx1, dx16, gs["norm2"][l] = _matmul_rms_bwd(dh1, w_ff1[l], sv["x1"], row(norm2_g, l) + tok, dh, f"mm_du2_rms_{l}")

        dmixed = _matmul(dx16, w_out[l], mode="nt", out_dtypes=(F32,), name=f"mm_dmixed_{l}")
        g_o[l] = _matmul(sv["mixed"], dx16, mode="tn", out_dtypes=(BF16,), name=f"mm_dwo_{l}", tm=1024, tk=seq // 2)
        tok = scatter_start("o", l, [g_o[l]], [("scatter", "rows", r_o)], [(r_o, d_model)])
        dyc, gs["lng"][l], gs["lnb"][l] = _ln_silu_bwd(sv["yc"], row(conv_ln_g, l) + tok, row(conv_ln_b, l), dmixed, f"ln_silu_bwd_{l}")
        dpa, dpg, dw32 = _conv_bwd(sv["proj"], w32[l], dyc, cc, a_col, g_col, f"conv_bwd_{l}")
        gs["convw"][l], gs["convb"][l] = dw32[:CONV_TAPS], dw32[CONV_TAPS:CONV_TAPS + 1]
        dq, dka, dv, drow, dcol = _attn_bwd(sv["qa"], sv["ka"], sv["vb"], sv["att"], sv["lse"], dmixed, n_heads, f"attn_bwd_{l}")
        dpq, dpk, dpf, dgq, dgk, dbf = _prep_bwd(sv["proj"], dq, dka, drow, dcol, gq[l], gk[l], bfp[l], n_heads, f"prep_bwd_{l}")
        gs["qn"][l] = dgq.reshape(n_heads, HEAD_DIM).sum(axis=0)
        gs["kn"][l] = dgk.reshape(n_heads, HEAD_DIM).sum(axis=0)
        gs["bf"][l] = dbf[0, :n_heads]
        dproj = jnp.concatenate([dpq, dpk, dv.astype(BF16), dpa, dpg, dpf], axis=1)
        dwall = _matmul(sv["u1"], dproj, mode="tn", out_dtypes=(BF16,), name=f"mm_dwall_{l}", tm=d_model, tn=n_all // 3, tk=seq)
        g_in[l] = shards_of(dwall)
        tok = scatter_start("in", l, [g_in[l]], [("scatter", "slot", 1)], [(d_model, n_in_loc)])
        dh, dh16, gs["norm1"][l] = _matmul_rms_bwd(dproj, w_all[l], sv["x_in"], row(norm1_g, l) + tok, dx1, f"mm_du1_rms_{l}")
    grad_x = dh[None]

    small_g = [jnp.stack(gs[n]).reshape(shp) for n, shp in (
        ("norm1", norm1_g.shape), ("bf", b_f.shape), ("qn", q_norm_g.shape), ("kn", k_norm_g.shape),
        ("convw", (depth, CONV_TAPS, cc)), ("convb", conv_b.shape), ("lng", conv_ln_g.shape), ("lnb", conv_ln_b.shape),
        ("norm2", norm2_g.shape))]
    small_shapes = [g.shape for g in small_g]
    small_flat = _flatten(small_g, 8)
    small_spec = [("gather", "slot", 1)]
    small_sems, small_land, _ = _exchange_start([small_flat], [lax.empty((N_DEV,) + small_flat.shape, F32)], small_spec, [[0]],
                                                "small_grads_start")

    def landed(stage, l, after):
        srcs, lands, specs, sems = scattering[(stage, l)]
        return _exchange_wait(srcs, lands, specs, sems, after, f"scatter_wait_{stage}_{l}")

    def adamw_layers(kd, w, m, v, recv):
        outs = None
        for l in reversed(range(depth)):
            outs = _adamw_shard(w, m, v, recv[l], l, outs, f"adamw_{kd}_{l}")
        return outs

    recv_1, recv_2, recv_in, recv_o = [None] * depth, [None] * depth, [None] * depth, [None] * depth
    for l in reversed(range(depth)):
        recv_1[l], recv_2[l] = landed("ff", l, dh)
    out_1 = adamw_layers("1", w_mlp_in, m_w_mlp_in, v_w_mlp_in, recv_1)
    out_2 = adamw_layers("2", w_mlp_out, m_w_mlp_out, v_w_mlp_out, recv_2)
    for l in reversed(range(depth)):
        recv_o[l], = landed("o", l, out_2[1])
    out_o = adamw_layers("o", w_o, m_w_o, v_w_o, recv_o)
    for l in reversed(range(depth)):
        recv_in[l], = landed("in", l, out_o[1])
    out_in = adamw_layers("in", w_in, m_w_in, v_w_in, recv_in)
    big_out = [[outs[kind] for outs in (out_in, out_o, out_1, out_2)] for kind in range(4)]

    small_parts, = _exchange_wait([small_flat], small_land, small_spec, small_sems[0], out_in[1], "small_grads_wait")
    small_g = _unflatten(_sum_devices(small_parts, "small_grads_sum"), small_shapes)
    cw = conv_w.shape[2]
    small_g[4] = lax.dynamic_slice_in_dim(small_g[4], me * cw, cw, axis=2)
    small = (norm1_g, b_f, q_norm_g, k_norm_g, conv_w, conv_b, conv_ln_g, conv_ln_b, norm2_g)
    small_m = (m_norm1_g, m_b_f, m_q_norm_g, m_k_norm_g, m_conv_w, m_conv_b, m_conv_ln_g, m_conv_ln_b, m_norm2_g)
    small_v = (v_norm1_g, v_b_f, v_q_norm_g, v_k_norm_g, v_conv_w, v_conv_b, v_conv_ln_g, v_conv_ln_b, v_norm2_g)
    small_out = _adamw(_flatten(small, 8), _flatten(small_m, 8), _flatten(small_v, 8), _flatten(small_g, 8), "adamw_small")
    small_out = [small_g] + [_unflatten(o, [w.shape for w in small]) for o in small_out]

    def group(kind):
        s_, b_ = small_out[kind], big_out[kind]
        return [s_[0], b_[0], s_[1], s_[2], s_[3], s_[4], s_[5], s_[6], s_[7], b_[1], s_[8], b_[2], b_[3]]

    return (loss, grad_x, *group(0), *group(1), *group(2), *group(3))
```

```python
import jax
import jax.numpy as jnp
from jax import lax
from jax.experimental import pallas as pl
from jax.experimental.pallas import tpu as pltpu

F32 = jnp.float32
BF16 = jnp.bfloat16

EPS = 1e-6
HEAD_DIM = 64
LANES = 128
PAIR = 2 * LANES
N_DEV = 8
CONV_TAPS = 31
CONV_PAD = 32
NEG = -1e30

ADAM_LR = 0.001
ADAM_B1 = 0.9
ADAM_B2 = 0.999
ADAM_EPS = 1e-08
ADAM_WD = 0.01
ADAM_STEP = 10

TM = 512
TQ = 512
CONV_ROWS = 128
FLAT_ROWS = 1024
MESH = pl.DeviceIdType.MESH


V7X_VMEM_BYTES = 64 * 1024 * 1024
VMEM_LIMIT_BYTES = V7X_VMEM_BYTES * 7 // 8


def _params(*sem):
    return pltpu.CompilerParams(dimension_semantics=sem, vmem_limit_bytes=VMEM_LIMIT_BYTES)


def _split3(x):
    hi = x.astype(BF16)
    r1 = x - hi.astype(F32)
    mid = r1.astype(BF16)
    lo = (r1 - mid.astype(F32)).astype(BF16)
    return hi, mid, lo


def _dot(a, b):
    return jnp.dot(a, b, preferred_element_type=F32)


def _dot_nt(a, b):
    return lax.dot_general(a, b, (((1,), (1,)), ((), ())), preferred_element_type=F32)


def _dot_tn(a, b):
    return lax.dot_general(a, b, (((0,), (0,)), ((), ())), preferred_element_type=F32)


def _dot3(x, mat):
    hi, mid, lo = _split3(x)
    return _dot(hi, mat) + _dot(mid, mat) + _dot(lo, mat)


def _dot2(x, mat):
    hi = x.astype(BF16)
    lo = (x - hi.astype(F32)).astype(BF16)
    return _dot(hi, mat) + _dot(lo, mat)


def _dot3_r(mat, x):
    hi, mid, lo = _split3(x)
    return _dot(mat, hi) + _dot(mat, mid) + _dot(mat, lo)


def _iota(shape, dim):
    return lax.broadcasted_iota(jnp.int32, shape, dim)


def _sigmoid(x):
    return 1.0 / (1.0 + jnp.exp(-x))


def _matmul(a, b, *, mode, out_dtypes, name, epilogue=None, extras=(), tm=TM, tn=1024, tk=1024):
    if mode == "nn":
        (m, k), (k2, n) = a.shape, b.shape
    elif mode == "nt":
        (m, k), (n, k2) = a.shape, b.shape
    else:
        (k, m), (k2, n) = a.shape, b.shape
    assert k == k2, (name, a.shape, b.shape)
    tm, tn, tk = min(tm, m), min(tn, n), min(tk, k)
    assert m % tm == 0 and n % tn == 0 and k % tk == 0, (name, m, n, k, tm, tn, tk)
    nk = k // tk
    a_mode = dict(pipeline_mode=pl.Buffered(1)) if (m == tm and nk == 1) else {}
    b_mode = dict(pipeline_mode=pl.Buffered(1)) if (n == tn and nk == 1) else {}
    if mode == "tn":
        a_spec = pl.BlockSpec((tk, tm), lambda i, j, kk: (kk, i), **a_mode)
    else:
        a_spec = pl.BlockSpec((tm, tk), lambda i, j, kk: (i, kk), **a_mode)
    if mode == "nt":
        b_spec = pl.BlockSpec((tn, tk), lambda i, j, kk: (j, kk), **b_mode)
    else:
        b_spec = pl.BlockSpec((tk, tn), lambda i, j, kk: (kk, j), **b_mode)
    dot = {"nn": _dot, "nt": _dot_nt, "tn": _dot_tn}[mode]
    tile = pl.BlockSpec((tm, tn), lambda i, j, kk: (i, j))
    n_ex, n_out = len(extras), len(out_dtypes)
    acc_in_out = nk > 1 and epilogue is None and out_dtypes[0] == F32

    def body(a_ref, b_ref, *rest):
        ex_refs, out_refs = rest[:n_ex], rest[n_ex:n_ex + n_out]
        part = dot(a_ref[...].astype(BF16), b_ref[...].astype(BF16))

        def finish(acc):
            res = epilogue(acc, *[e[...] for e in ex_refs]) if epilogue is not None else (acc,) * n_out
            for o_ref, r in zip(out_refs, res):
                o_ref[...] = r.astype(o_ref.dtype)

        if nk == 1:
            finish(part)
        else:
            acc_ref = out_refs[0] if acc_in_out else rest[-1]
            kk = pl.program_id(2)

            @pl.when(kk == 0)
            def _():
                acc_ref[...] = part

            @pl.when(kk > 0)
            def _():
                acc_ref[...] += part

            @pl.when(kk == nk - 1)
            def _():
                if acc_in_out:
                    for o_ref in out_refs[1:]:
                        o_ref[...] = acc_ref[...].astype(o_ref.dtype)
                else:
                    finish(acc_ref[...])

    outs = pl.pallas_call(
        body,
        name=name,
        grid=(m // tm, n // tn, nk),
        in_specs=[a_spec, b_spec] + [tile] * n_ex,
        out_specs=[tile] * n_out,
        out_shape=[jax.ShapeDtypeStruct((m, n), dt) for dt in out_dtypes],
        scratch_shapes=[pltpu.VMEM((tm, tn), F32)] if nk > 1 and not acc_in_out else [],
        compiler_params=_params("parallel", "parallel", "arbitrary"),
    )(a, b, *extras)
    return outs if n_out > 1 else outs[0]


def _rms_matmul(x, g, b, *, out_dtypes, name, epilogue=None, tm=TM):
    s, d = x.shape
    n = b.shape[1]
    ts = min(tm, s)
    n_out = len(out_dtypes)

    def body(x_ref, g_ref, b_ref, u_ref, *out_refs):
        xv = x_ref[...]
        u = (xv * lax.rsqrt(jnp.mean(xv * xv, axis=-1, keepdims=True) + EPS) * g_ref[...]).astype(BF16)
        u_ref[...] = u
        acc = _dot(u, b_ref[...])
        res = epilogue(acc) if epilogue is not None else (acc,)
        for o_ref, r in zip(out_refs, res):
            o_ref[...] = r.astype(o_ref.dtype)

    row = lambda w: pl.BlockSpec((ts, w), lambda i: (i, 0))
    return pl.pallas_call(
        body, name=name, grid=(s // ts,),
        in_specs=[row(d), pl.BlockSpec((1, d), lambda i: (0, 0)), pl.BlockSpec((d, n), lambda i: (0, 0), pipeline_mode=pl.Buffered(1))],
        out_specs=[row(d)] + [row(n)] * n_out,
        out_shape=[jax.ShapeDtypeStruct((s, d), BF16)] + [jax.ShapeDtypeStruct((s, n), dt) for dt in out_dtypes],
        compiler_params=_params("parallel"),
    )(x, g, b)


def _matmul_rms_bwd(a, b, x, g, dres, name):
    s, k = a.shape
    d = b.shape[0]
    ts = min(TM, s)

    def body(a_ref, b_ref, x_ref, g_ref, dres_ref, dx_ref, dx16_ref, dg_ref):
        @pl.when(pl.program_id(0) == 0)
        def _():
            dg_ref[...] = jnp.zeros_like(dg_ref)

        duv = _dot_nt(a_ref[...], b_ref[...])
        xv = x_ref[...]
        r = lax.rsqrt(jnp.mean(xv * xv, axis=-1, keepdims=True) + EPS)
        xh = xv * r
        dxh = duv * g_ref[...]
        dx = dres_ref[...] + r * (dxh - xh * jnp.mean(dxh * xh, axis=-1, keepdims=True))
        dx_ref[...] = dx
        dx16_ref[...] = dx.astype(BF16)
        dg_ref[...] += jnp.sum(duv * xh, axis=0, keepdims=True)

    row = pl.BlockSpec((ts, d), lambda i: (i, 0))
    vec = pl.BlockSpec((1, d), lambda i: (0, 0))
    return pl.pallas_call(
        body, name=name, grid=(s // ts,),
        in_specs=[pl.BlockSpec((ts, k), lambda i: (i, 0)), pl.BlockSpec((d, k), lambda i: (0, 0), pipeline_mode=pl.Buffered(1)),
                  row, vec, row],
        out_specs=[row, row, vec],
        out_shape=[jax.ShapeDtypeStruct((s, d), F32), jax.ShapeDtypeStruct((s, d), BF16), jax.ShapeDtypeStruct((1, d), F32)],
        compiler_params=_params("arbitrary"),
    )(a, b, x, g, dres)


def _matmul_loss(a, b, res, target, name):
    s, k = a.shape
    d = b.shape[1]
    ts = min(TM, s)

    def body(a_ref, b_ref, res_ref, t_ref, dy_ref, dy16_ref, sq_ref):
        @pl.when(pl.program_id(0) == 0)
        def _():
            sq_ref[...] = jnp.zeros_like(sq_ref)

        err = _dot(a_ref[...], b_ref[...]) + res_ref[...] - t_ref[...]
        dy = err * (1.0 / d)
        dy_ref[...] = dy
        dy16_ref[...] = dy.astype(BF16)
        sq_ref[...] += jnp.sum(err * err, axis=0, keepdims=True)

    row = pl.BlockSpec((ts, d), lambda i: (i, 0))
    vec = pl.BlockSpec((1, d), lambda i: (0, 0))
    return pl.pallas_call(
        body, name=name, grid=(s // ts,),
        in_specs=[pl.BlockSpec((ts, k), lambda i: (i, 0)), pl.BlockSpec((k, d), lambda i: (0, 0), pipeline_mode=pl.Buffered(1)),
                  row, row],
        out_specs=[row, row, vec],
        out_shape=[jax.ShapeDtypeStruct((s, d), F32), jax.ShapeDtypeStruct((s, d), BF16), jax.ShapeDtypeStruct((1, d), F32)],
        compiler_params=_params("arbitrary"),
    )(a, b, res, target)


def _ln_silu_fwd(y, g, b, mixed, name):
    s, c = y.shape
    ts = min(TM, s)
    assert mixed.shape == (s, 2 * c)

    def body(y_ref, g_ref, b_ref, mixed_ref, h_ref):
        yv = y_ref[...]
        mu = jnp.mean(yv, axis=-1, keepdims=True)
        yc = yv - mu
        z = yc * lax.rsqrt(jnp.mean(yc * yc, axis=-1, keepdims=True) + EPS) * g_ref[...] + b_ref[...]
        h_ref[...] = (z * _sigmoid(z)).astype(BF16)

    row = pl.BlockSpec((ts, c), lambda i: (i, 0))
    vec = pl.BlockSpec((1, c), lambda i: (0, 0))
    return pl.pallas_call(
        body, name=name, grid=(s // ts,), in_specs=[row, vec, vec, pl.BlockSpec(memory_space=pl.ANY)],
        out_specs=pl.BlockSpec((ts, c), lambda i: (i, 1)),
        out_shape=jax.ShapeDtypeStruct((s, 2 * c), BF16), input_output_aliases={3: 0},
        compiler_params=_params("parallel"),
    )(y, g, b, mixed)


def _ln_silu_bwd(y, g, b, dmixed, name):
    s, c = y.shape
    ts = min(TM, s)

    def body(y_ref, g_ref, b_ref, dh_ref, dy_ref, dg_ref, db_ref):
        @pl.when(pl.program_id(0) == 0)
        def _():
            dg_ref[...] = jnp.zeros_like(dg_ref)
            db_ref[...] = jnp.zeros_like(db_ref)

        yv = y_ref[...]
        mu = jnp.mean(yv, axis=-1, keepdims=True)
        yc = yv - mu
        r = lax.rsqrt(jnp.mean(yc * yc, axis=-1, keepdims=True) + EPS)
        yh = yc * r
        z = yh * g_ref[...] + b_ref[...]
        sg = _sigmoid(z)
        dz = dh_ref[...] * (sg * (1.0 + z * (1.0 - sg)))
        dg_ref[...] += jnp.sum(dz * yh, axis=0, keepdims=True)
        db_ref[...] += jnp.sum(dz, axis=0, keepdims=True)
        dyh = dz * g_ref[...]
        dy_ref[...] = r * (dyh - jnp.mean(dyh, axis=-1, keepdims=True) - yh * jnp.mean(dyh * yh, axis=-1, keepdims=True))

    row = pl.BlockSpec((ts, c), lambda i: (i, 0))
    vec = pl.BlockSpec((1, c), lambda i: (0, 0))
    return pl.pallas_call(
        body, name=name, grid=(s // ts,),
        in_specs=[row, vec, vec, pl.BlockSpec((ts, c), lambda i: (i, 1))], out_specs=[row, vec, vec],
        out_shape=[jax.ShapeDtypeStruct((s, c), F32), jax.ShapeDtypeStruct((1, c), F32), jax.ShapeDtypeStruct((1, c), F32)],
        compiler_params=_params("arbitrary"),
    )(y, g, b, dmixed)


def _head_masks():
    lane2 = _iota((1, PAIR), 1)
    lane1 = _iota((1, LANES), 1)
    qa = (lane2 < HEAD_DIM) | ((lane2 >= LANES) & (lane2 < LANES + 3))
    qb = ((lane2 >= HEAD_DIM) & (lane2 < LANES)) | ((lane2 >= LANES + 3) & (lane2 < LANES + 6))
    return (qa, qb), (lane1 < HEAD_DIM, lane1 >= HEAD_DIM)


def _group_matrix(width):
    shift = HEAD_DIM.bit_length() - 1
    return ((_iota((width, width), 0) >> shift) == (_iota((width, width), 1) >> shift)).astype(BF16)


def _prep_fwd(proj, gq, gk, bf, n_heads, name):
    s = proj.shape[0]
    aw = n_heads * HEAD_DIM
    n_pairs = n_heads // 2
    ts = min(TM, s)
    f_col = (proj.shape[1] - LANES) // LANES

    def body(q_ref, k_ref, v_ref, f_ref, gq_ref, gk_ref, bf_ref, qa_ref, ka_ref, vb_ref, carry_ref):
        @pl.when(pl.program_id(0) == 0)
        def _():
            carry_ref[...] = jnp.zeros_like(carry_ref)

        gmat = _group_matrix(aw)

        def head_norm(xv, g):
            ms = _dot2(xv * xv, gmat) * (1.0 / HEAD_DIM)
            return xv * lax.rsqrt(ms + EPS) * g

        qn = head_norm(q_ref[...], gq_ref[...]) * (HEAD_DIM ** -0.5)
        kn = head_norm(k_ref[...], gk_ref[...])
        z = f_ref[...] + bf_ref[...]
        logf = jnp.minimum(z, 0.0) - jnp.log(1.0 + jnp.exp(-jnp.abs(z)))
        tri = (_iota((ts, ts), 0) >= _iota((ts, ts), 1)).astype(BF16)
        c = _dot3_r(tri, logf) + carry_ref[...]
        carry_ref[...] = c[ts - 1:ts, :]
        terms = _split3(-c)
        row, col = _iota((LANES, LANES), 0), _iota((LANES, LANES), 1)
        ones = jnp.where(_iota((ts, LANES), 1) < 6, 1.0, 0.0).astype(BF16)
        for p in range(n_pairs):
            extra = jnp.zeros((ts, LANES), F32)
            for t, term in enumerate(terms):
                sel = ((row == 2 * p) & (col == t)) | ((row == 2 * p + 1) & (col == 3 + t))
                extra += _dot(term, sel.astype(BF16))
            lo, hi = p * PAIR, p * PAIR + LANES
            ka_ref[:, lo:hi] = kn[:, p * LANES:(p + 1) * LANES].astype(BF16)
            ka_ref[:, hi:hi + LANES] = extra.astype(BF16)
            qa_ref[:, lo:hi] = qn[:, p * LANES:(p + 1) * LANES].astype(BF16)
            qa_ref[:, hi:hi + LANES] = ones
            vb_ref[:, lo:hi] = v_ref[:, p * LANES:(p + 1) * LANES].astype(BF16)
            vb_ref[:, hi:hi + LANES] = jnp.ones((ts, LANES), BF16)

    blk = lambda j: pl.BlockSpec((ts, aw), lambda i: (i, j))
    vec = lambda w: pl.BlockSpec((1, w), lambda i: (0, 0))
    return pl.pallas_call(
        body, name=name, grid=(s // ts,),
        in_specs=[blk(0), blk(1), blk(2), pl.BlockSpec((ts, LANES), lambda i: (i, f_col)), vec(aw), vec(aw), vec(LANES)],
        out_specs=[pl.BlockSpec((ts, n_pairs * PAIR), lambda i: (i, 0))] * 3,
        out_shape=[jax.ShapeDtypeStruct((s, n_pairs * PAIR), BF16)] * 3,
        scratch_shapes=[pltpu.VMEM((1, LANES), F32)],
        compiler_params=_params("arbitrary"),
    )(proj, proj, proj, proj, gq, gk, bf)


def _prep_bwd(proj, dq, dka, drow, dcol, gq, gk, bf, n_heads, name):
    s = proj.shape[0]
    aw = n_heads * HEAD_DIM
    n_pairs = n_heads // 2
    ts = min(TM, s)
    nt = s // ts
    f_col = (proj.shape[1] - LANES) // LANES
    shift = HEAD_DIM.bit_length() - 1

    def body(q_ref, k_ref, f_ref, dq_ref, dka_ref, drow_ref, dcol_ref, gq_ref, gk_ref, bf_ref,
             dpq_ref, dpk_ref, dpf_ref, dgq_ref, dgk_ref, dbf_ref, carry_ref):
        @pl.when(pl.program_id(0) == 0)
        def _():
            carry_ref[...] = jnp.zeros_like(carry_ref)
            dgq_ref[...] = jnp.zeros_like(dgq_ref)
            dgk_ref[...] = jnp.zeros_like(dgk_ref)
            dbf_ref[...] = jnp.zeros_like(dbf_ref)

        gmat = _group_matrix(aw)

        def head_norm_bwd(xv, g, dn):
            r = lax.rsqrt(_dot2(xv * xv, gmat) * (1.0 / HEAD_DIM) + EPS)
            xh = xv * r
            dxh = dn * g
            dx = r * (dxh - xh * (_dot2(dxh * xh, gmat) * (1.0 / HEAD_DIM)))
            return dx, jnp.sum(dn * xh, axis=0, keepdims=True)

        dkav = dka_ref[...]
        dx, dg = head_norm_bwd(q_ref[...], gq_ref[...], dq_ref[...] * (HEAD_DIM ** -0.5))
        dpq_ref[...] = dx.astype(BF16)
        dgq_ref[...] += dg
        dkn = jnp.concatenate([dkav[:, p * PAIR:p * PAIR + LANES] for p in range(n_pairs)], axis=1)
        dx, dg = head_norm_bwd(k_ref[...], gk_ref[...], dkn)
        dpk_ref[...] = dx.astype(BF16)
        dgk_ref[...] += dg

        pick = (_iota((aw, LANES), 0) == (_iota((aw, LANES), 1) << shift)).astype(BF16)
        dc = _dot3(drow_ref[...], pick)
        r16, c16 = _iota((16, LANES), 0), _iota((16, LANES), 1)
        for p in range(n_pairs):
            place = ((r16 < 2) & (c16 == 2 * p + r16)).astype(BF16)
            for term in _split3(dcol_ref[p]):
                dc -= _dot_tn(term, place)
        triu = (_iota((ts, ts), 0) <= _iota((ts, ts), 1)).astype(BF16)
        dlogf = _dot3_r(triu, dc) + carry_ref[...]
        carry_ref[...] = dlogf[0:1, :]
        z = f_ref[...] + bf_ref[...]
        dz = dlogf * (1.0 / (1.0 + jnp.exp(z)))
        dpf_ref[...] = dz.astype(BF16)
        dbf_ref[...] += jnp.sum(dz, axis=0, keepdims=True)

    rev = lambda w, j: pl.BlockSpec((ts, w), lambda i: (nt - 1 - i, j))
    vec = lambda w: pl.BlockSpec((1, w), lambda i: (0, 0))
    return pl.pallas_call(
        body, name=name, grid=(nt,),
        in_specs=[rev(aw, 0), rev(aw, 1), rev(LANES, f_col), rev(aw, 0), rev(n_pairs * PAIR, 0), rev(aw, 0),
                  pl.BlockSpec((n_pairs, 16, ts), lambda i: (0, 0, nt - 1 - i)), vec(aw), vec(aw), vec(LANES)],
        out_specs=[rev(aw, 0), rev(aw, 0), rev(LANES, 0), vec(aw), vec(aw), vec(LANES)],
        out_shape=[jax.ShapeDtypeStruct((s, aw), BF16), jax.ShapeDtypeStruct((s, aw), BF16), jax.ShapeDtypeStruct((s, LANES), BF16),
                   jax.ShapeDtypeStruct((1, aw), F32), jax.ShapeDtypeStruct((1, aw), F32), jax.ShapeDtypeStruct((1, LANES), F32)],
        scratch_shapes=[pltpu.VMEM((1, LANES), F32)],
        compiler_params=_params("arbitrary"),
    )(proj, proj, proj, dq, dka, drow, dcol, gq, gk, bf)


def _attn_fwd(qa, ka, vb, n_heads, mix_width, name):
    s = qa.shape[0]
    aw = n_heads * HEAD_DIM
    n_pairs = n_heads // 2
    tq = min(2 * TQ, s)

    def body(q_ref, k_ref, v_ref, o_ref, lse_ref, o16_ref):
        i = pl.program_id(1)
        qmasks, omasks = _head_masks()
        qv = q_ref[...]
        causal = _iota((tq, tq), 1) <= _iota((tq, tq), 0)
        qhs = [jnp.where(qmasks[h], qv, jnp.zeros_like(qv)) for h in range(2)]

        def scores(j):
            kv = k_ref[pl.ds(pl.multiple_of(j * tq, tq), tq), :]
            return tuple(_dot_nt(qhs[h], kv) for h in range(2))

        def update(j, state, scs, masked):
            vv = v_ref[pl.ds(pl.multiple_of(j * tq, tq), tq), :]
            out = []
            for h in range(2):
                m, acc = state[h]
                sc = jnp.where(causal, scs[h], NEG) if masked else scs[h]
                m_new = jnp.maximum(m, jnp.max(sc, axis=1, keepdims=True))
                p = jnp.exp(sc - m_new).astype(BF16)
                out.append((m_new, jnp.exp(m - m_new) * acc + _dot(p, vv)))
            return tuple(out)

        def body(j, state):
            return update(j, state, scores(j), False)

        init = ((jnp.full((tq, 1), NEG, F32), jnp.zeros((tq, PAIR), F32)),) * 2
        state = lax.fori_loop(0, i, body, init)
        res = []
        for m, acc in update(i, state, scores(i), True):
            l = acc[:, LANES:LANES + 1]
            res.append((acc[:, :LANES] * (1.0 / l), m + jnp.log(l)))
        out = jnp.where(omasks[0], res[0][0], res[1][0])
        o_ref[...] = out
        o16_ref[...] = out.astype(BF16)
        lse_ref[...] = jnp.where(omasks[0], res[0][1], res[1][1])

    return pl.pallas_call(
        body, name=name, grid=(n_pairs, s // tq),
        in_specs=[pl.BlockSpec((tq, PAIR), lambda p, i: (i, p)), pl.BlockSpec((s, PAIR), lambda p, i: (0, p)),
                  pl.BlockSpec((s, PAIR), lambda p, i: (0, p))],
        out_specs=[pl.BlockSpec((tq, LANES), lambda p, i: (i, p))] * 3,
        out_shape=[jax.ShapeDtypeStruct((s, aw), F32)] * 2 + [jax.ShapeDtypeStruct((s, mix_width), BF16)],
        compiler_params=_params("parallel", "parallel"),
    )(qa, ka, vb)


def _attn_bwd(qa, ka, vb, o, lse, dmixed, n_heads, name):
    s = qa.shape[0]
    aw = n_heads * HEAD_DIM
    n_pairs = n_heads // 2
    tq = min(2 * TQ, s)
    nq = s // tq

    def body(q_ref, k_ref, v_ref, o_ref, lse_ref, do_ref, dq_ref, dka_ref, dv_ref, drow_ref, dcol_ref, delta_ref):
        j = pl.program_id(1)
        qmasks, omasks = _head_masks()

        @pl.when(j == 0)
        def _():
            dq_ref[...] = jnp.zeros_like(dq_ref)
            drow_ref[...] = jnp.zeros_like(drow_ref)
            for c in range(nq):
                rows = slice(c * tq, (c + 1) * tq)
                prod = do_ref[rows, :] * o_ref[rows, :]
                da = jnp.sum(jnp.where(omasks[0], prod, 0.0), axis=1, keepdims=True)
                db = jnp.sum(jnp.where(omasks[1], prod, 0.0), axis=1, keepdims=True)
                delta_ref[rows, :] = jnp.where(omasks[0], da, db)

        dka_ref[...] = jnp.zeros_like(dka_ref)
        dv_ref[...] = jnp.zeros_like(dv_ref)
        dcol_ref[...] = jnp.zeros_like(dcol_ref)
        kv = k_ref[...]
        kk = kv[:, :LANES]
        vv = v_ref[...]
        causal = _iota((tq, tq), 1) <= _iota((tq, tq), 0)

        def step(i, masked):
            off = pl.multiple_of(i * tq, tq)
            qv = q_ref[pl.ds(off, tq), :]
            dov = do_ref[pl.ds(off, tq), :]
            lsev = lse_ref[pl.ds(off, tq), :]
            dlv = delta_ref[pl.ds(off, tq), :]
            for h in range(2):
                qh = jnp.where(qmasks[h], qv, jnp.zeros_like(qv))
                doh = jnp.where(omasks[h], dov, 0.0).astype(BF16)
                lane = h * HEAD_DIM
                sc = _dot_nt(qh, kv)
                if masked:
                    sc = jnp.where(causal, sc, NEG)
                p = jnp.exp(sc - lsev[:, lane:lane + 1])
                dv_ref[...] += _dot_tn(p.astype(BF16), doh)
                dp = _dot_nt(doh, vv)
                dsf = p * (dp - dlv[:, lane:lane + 1])
                drow_ref[pl.ds(off, tq), :] += jnp.where(omasks[h], jnp.sum(dsf, axis=1, keepdims=True), 0.0)
                dcol_ref[0, h:h + 1, :] += jnp.sum(dsf, axis=0, keepdims=True)
                ds = dsf.astype(BF16)
                dka_ref[...] += _dot_tn(ds, qh)
                dq_ref[pl.ds(off, tq), :] += jnp.where(omasks[h], _dot(ds, kk), 0.0)

        step(j, True)

        def loop_body(i, carry):
            step(i, False)
            return carry

        lax.fori_loop(j + 1, nq, loop_body, 0)

    full = lambda w: pl.BlockSpec((s, w), lambda p, j: (0, p))
    blk = lambda w: pl.BlockSpec((tq, w), lambda p, j: (j, p))
    return pl.pallas_call(
        body, name=name, grid=(n_pairs, nq),
        in_specs=[full(PAIR), blk(PAIR), pl.BlockSpec((tq, LANES), lambda p, j: (j, 2 * p)), full(LANES), full(LANES), full(LANES)],
        out_specs=[full(LANES), blk(PAIR), blk(LANES), full(LANES), pl.BlockSpec((1, 16, tq), lambda p, j: (p, 0, j))],
        out_shape=[jax.ShapeDtypeStruct((s, aw), F32), jax.ShapeDtypeStruct((s, n_pairs * PAIR), F32),
                   jax.ShapeDtypeStruct((s, aw), F32), jax.ShapeDtypeStruct((s, aw), F32),
                   jax.ShapeDtypeStruct((n_pairs, 16, s), F32)],
        scratch_shapes=[pltpu.VMEM((s, LANES), F32)],
        compiler_params=_params("parallel", "arbitrary"),
    )(qa, ka, vb, o, lse, dmixed)


def _conv_fwd(proj, w32, bias, n_ch, a_col, g_col, name):
    s = proj.shape[0]
    rows = min(CONV_ROWS, s)

    def body(a_ref, g_ref, w_ref, b_ref, y_ref, pad_ref):
        pad_ref[0:CONV_PAD, :] = jnp.zeros((CONV_PAD, LANES), F32)
        pad_ref[CONV_PAD:CONV_PAD + s, :] = a_ref[...] * _sigmoid(g_ref[...])
        wv = w_ref[...]
        for c in range(s // rows):
            acc = jnp.broadcast_to(b_ref[...], (rows, LANES))
            for t in range(CONV_TAPS):
                start = c * rows + CONV_PAD - (CONV_TAPS - 1) + t
                acc = acc + wv[t:t + 1, :] * pad_ref[start:start + rows, :]
            y_ref[c * rows:(c + 1) * rows, :] = acc

    col = lambda j0: pl.BlockSpec((s, LANES), lambda c: (0, j0 + c))
    return pl.pallas_call(
        body, name=name, grid=(n_ch // LANES,),
        in_specs=[col(a_col), col(g_col), pl.BlockSpec((CONV_PAD, LANES), lambda c: (0, c)), pl.BlockSpec((1, LANES), lambda c: (0, c))],
        out_specs=pl.BlockSpec((s, LANES), lambda c: (0, c)),
        out_shape=jax.ShapeDtypeStruct((s, n_ch), F32),
        scratch_shapes=[pltpu.VMEM((s + CONV_PAD, LANES), F32)],
        compiler_params=_params("parallel"),
    )(proj, proj, w32, bias)


def _conv_bwd(proj, w32, dy, n_ch, a_col, g_col, name):
    s = proj.shape[0]
    rows = min(CONV_ROWS, s)
    sub = 8

    def fold(x):
        acc = x[0:sub, :]
        for r in range(1, rows // sub):
            acc = acc + x[r * sub:(r + 1) * sub, :]
        return acc

    def body(a_ref, g_ref, w_ref, dy_ref, da_ref, dg_ref, dw_ref, padh_ref, padd_ref):
        sg = _sigmoid(g_ref[...])
        padh_ref[0:CONV_PAD, :] = jnp.zeros((CONV_PAD, LANES), F32)
        padh_ref[CONV_PAD:CONV_PAD + s, :] = a_ref[...] * sg
        padd_ref[0:s, :] = dy_ref[...]
        padd_ref[s:s + CONV_PAD, :] = jnp.zeros((CONV_PAD, LANES), F32)
        wv = w_ref[...]
        dw = [jnp.zeros((sub, LANES), F32) for _ in range(CONV_TAPS + 1)]
        for c in range(s // rows):
            r0 = c * rows
            acc = jnp.zeros((rows, LANES), F32)
            dyc = dy_ref[r0:r0 + rows, :]
            for t in range(CONV_TAPS):
                back = r0 + (CONV_TAPS - 1) - t
                acc = acc + wv[t:t + 1, :] * padd_ref[back:back + rows, :]
                start = r0 + CONV_PAD - (CONV_TAPS - 1) + t
                dw[t] = dw[t] + fold(dyc * padh_ref[start:start + rows, :])
            dw[CONV_TAPS] = dw[CONV_TAPS] + fold(dyc)
            av = a_ref[r0:r0 + rows, :]
            sgc = _sigmoid(g_ref[r0:r0 + rows, :])
            da_ref[r0:r0 + rows, :] = (acc * sgc).astype(BF16)
            dg_ref[r0:r0 + rows, :] = (acc * av * sgc * (1.0 - sgc)).astype(BF16)
        for t in range(CONV_TAPS + 1):
            dw_ref[t:t + 1, :] = jnp.sum(dw[t], axis=0, keepdims=True)

    col = lambda j0: pl.BlockSpec((s, LANES), lambda c: (0, j0 + c))
    wspec = pl.BlockSpec((CONV_PAD, LANES), lambda c: (0, c))
    return pl.pallas_call(
        body, name=name, grid=(n_ch // LANES,),
        in_specs=[col(a_col), col(g_col), wspec, col(0)],
        out_specs=[col(0), col(0), wspec],
        out_shape=[jax.ShapeDtypeStruct((s, n_ch), BF16), jax.ShapeDtypeStruct((s, n_ch), BF16),
                   jax.ShapeDtypeStruct((CONV_PAD, n_ch), F32)],
        scratch_shapes=[pltpu.VMEM((s + CONV_PAD, LANES), F32), pltpu.VMEM((s + CONV_PAD, LANES), F32)],
        compiler_params=_params("parallel"),
    )(proj, proj, w32, dy)


def _my_place():
    return lax.axis_index("x"), lax.axis_index("y"), lax.axis_index("c")


def _flip(place, k):
    x, y, c = place
    return (1 - x if k & 4 else x, 1 - y if k & 2 else y, 1 - c if k & 1 else c)


def _dev_id(place):
    return 4 * place[0] + 2 * place[1] + place[2]


def _wait_all(ref, send_sem, recv_sem, place):
    pltpu.make_async_remote_copy(src_ref=ref, dst_ref=ref, send_sem=send_sem, recv_sem=recv_sem,
                                 device_id=place, device_id_type=MESH).wait()


def _window(kind, ref, dev, n):
    if kind == "slot":
        return ref.at[dev]
    if kind == "rows":
        return ref.at[pl.ds(pl.multiple_of(dev * n, n), n), :]
    return ref.at[:, pl.ds(pl.multiple_of(dev * n, n), n)]


def _hbm(x):
    return pltpu.with_memory_space_constraint(x, pltpu.HBM)


_EFFECT = pltpu.SideEffectType.DATAFLOW_SIDE_EFFECTING


def _exchange_start(srcs, lands, specs, groups, name):
    n = len(srcs)
    n_g = len(groups)

    def body(*refs):
        src_refs, land_refs = refs[:n], refs[n:2 * n]
        sems = refs[2 * n:2 * n + 2 * n_g]
        token = refs[-1]
        place = _my_place()
        me = _dev_id(place)
        for g, units in enumerate(groups):
            for j, u in enumerate(units):
                mode, kind, cnt = specs[u]
                for k in range(N_DEV):
                    peer = _flip(place, k)
                    if mode == "gather":
                        src, dst = src_refs[u], _window(kind, land_refs[u], me, cnt)
                    else:
                        src, dst = _window(kind, src_refs[u], _dev_id(peer), cnt), land_refs[u].at[k]
                    pltpu.make_async_remote_copy(src_ref=src, dst_ref=dst, send_sem=sems[2 * g].at[j], recv_sem=sems[2 * g + 1].at[j],
                                                 device_id=peer, device_id_type=MESH).start()
        token[...] = jnp.zeros_like(token)

    hbm = pl.BlockSpec(memory_space=pltpu.HBM)
    sem = pl.BlockSpec(memory_space=pltpu.SEMAPHORE)
    out_shape = [pltpu.SemaphoreType.DMA((len(units),)) for units in groups for _ in range(2)]
    out_shape += [pltpu.HBM(x.shape, x.dtype) for x in lands] + [jax.ShapeDtypeStruct((8, LANES), F32)]
    outs = pl.pallas_call(
        body, name=name, out_shape=out_shape,
        in_specs=[hbm] * (2 * n), out_specs=[sem] * (2 * n_g) + [hbm] * n + [pl.BlockSpec(memory_space=pltpu.VMEM)],
        input_output_aliases={n + u: 2 * n_g + u for u in range(n)},
        compiler_params=pltpu.CompilerParams(has_side_effects=_EFFECT),
    )(*[_hbm(x) for x in srcs], *[_hbm(x) for x in lands])
    sem_pairs = [(outs[2 * g], outs[2 * g + 1]) for g in range(n_g)]
    return sem_pairs, list(outs[2 * n_g:2 * n_g + n]), outs[-1]


def _exchange_wait(srcs, lands, specs, sem_pair, after, name):
    n = len(lands)

    def body(*refs):
        land_refs = refs[n:2 * n]
        send_sems, recv_sems = refs[2 * n], refs[2 * n + 1]
        place = _my_place()
        for u in range(n):
            _wait_all(land_refs[u], send_sems.at[u], recv_sems.at[u], place)

    hbm = pl.BlockSpec(memory_space=pltpu.HBM)
    sem = pl.BlockSpec(memory_space=pltpu.SEMAPHORE)
    outs = pl.pallas_call(
        body, name=name, out_shape=[pltpu.HBM(x.shape, x.dtype) for x in lands],
        in_specs=[hbm] * (2 * n) + [sem, sem, pl.BlockSpec(memory_space=pl.ANY)], out_specs=[hbm] * n,
        input_output_aliases={n + u: u for u in range(n)},
        compiler_params=pltpu.CompilerParams(has_side_effects=_EFFECT),
    )(*[_hbm(x) for x in srcs], *lands, sem_pair[0], sem_pair[1], after)
    return list(outs)


def _sum_devices(parts, name):
    _, r, w = parts.shape

    def body(p_ref, out_ref):
        acc = p_ref[0]
        for d in range(1, N_DEV):
            acc = acc + p_ref[d]
        out_ref[...] = acc

    return pl.pallas_call(
        body, name=name, out_shape=jax.ShapeDtypeStruct((r, w), F32),
        in_specs=[pl.BlockSpec(memory_space=pltpu.VMEM)], out_specs=pl.BlockSpec(memory_space=pltpu.VMEM),
    )(parts)


def _adamw_math(w, m, v, g):
    m_new = ADAM_B1 * m + (1.0 - ADAM_B1) * g
    v_new = ADAM_B2 * v + (1.0 - ADAM_B2) * (g * g)
    m_hat = m_new / (1.0 - ADAM_B1 ** ADAM_STEP)
    v_hat = v_new / (1.0 - ADAM_B2 ** ADAM_STEP)
    return -ADAM_LR * (m_hat / (jnp.sqrt(v_hat) + ADAM_EPS) + ADAM_WD * w), m_new, v_new


def _adamw(w, m, v, g, name):
    rows = w.shape[0]
    tr = min(FLAT_ROWS, rows)
    assert rows % tr == 0, (name, rows)

    def body(w_ref, m_ref, v_ref, g_ref, d_out, m_out, v_out):
        d_out[...], m_out[...], v_out[...] = _adamw_math(w_ref[...], m_ref[...], v_ref[...], g_ref[...])

    flat = pl.BlockSpec((tr, LANES), lambda i: (i, 0))
    return pl.pallas_call(
        body, name=name, grid=(rows // tr,), in_specs=[flat] * 4, out_specs=[flat] * 3,
        out_shape=[jax.ShapeDtypeStruct((rows, LANES), F32)] * 3,
        compiler_params=_params("parallel"),
    )(w, m, v, g)


def _adamw_shard(w, m, v, recv, layer, prev, name):
    depth, a, b = w.shape
    ta = min(256, a)
    assert a % ta == 0

    def body(w_ref, m_ref, v_ref, r_ref, *rest):
        g_out, d_out, m_out, v_out = rest[-4:]
        g = r_ref[0].astype(F32)
        for k in range(1, N_DEV):
            g = g + r_ref[k].astype(F32)
        g_out[0] = g
        d_out[0], m_out[0], v_out[0] = _adamw_math(w_ref[0], m_ref[0], v_ref[0], g)

    lay = pl.BlockSpec((1, ta, b), lambda i: (layer, i, 0))
    in_specs = [lay] * 3 + [pl.BlockSpec((N_DEV, ta, b), lambda i: (0, i, 0))]
    args = [w, m, v, recv]
    aliases = {}
    if prev is not None:
        in_specs += [pl.BlockSpec(memory_space=pl.ANY)] * 4
        args += list(prev)
        aliases = {4 + i: i for i in range(4)}
    return pl.pallas_call(
        body, name=name, grid=(a // ta,), in_specs=in_specs, out_specs=[lay] * 4,
        out_shape=[jax.ShapeDtypeStruct(w.shape, F32)] * 4, input_output_aliases=aliases,
        compiler_params=_params("parallel"),
    )(*args)


def _round_up(n, mult):
    return (n + mult - 1) // mult * mult


def _flatten(parts, row_mult):
    n = sum(p.size for p in parts)
    rows = _round_up(-(-n // LANES), row_mult)
    tail = [jnp.zeros((rows * LANES - n,), parts[0].dtype)] if rows * LANES > n else []
    return jnp.concatenate([p.reshape(-1) for p in parts] + tail).reshape(rows, LANES)


def _unflatten(flat, shapes):
    flat = flat.reshape(-1)
    out, off = [], 0
    for shp in shapes:
        n = 1
        for dim in shp:
            n *= dim
        out.append(flat[off:off + n].reshape(shp))
        off += n
    return out


def kernel(x, norm1_g, w_in, b_f, q_norm_g, k_norm_g, conv_w, conv_b, conv_ln_g, conv_ln_b, w_o, norm2_g, w_mlp_in, w_mlp_out, loss_target, m_norm1_g, m_w_in, m_b_f, m_q_norm_g, m_k_norm_g, m_conv_w, m_conv_b, m_conv_ln_g, m_conv_ln_b, m_w_o, m_norm2_g, m_w_mlp_in, m_w_mlp_out, v_norm1_g, v_w_in, v_b_f, v_q_norm_g, v_k_norm_g, v_conv_w, v_conv_b, v_conv_ln_g, v_conv_ln_b, v_w_o, v_norm2_g, v_w_mlp_in, v_w_mlp_out):
    depth, d_model, n_in_loc = w_in.shape
    seq = x.shape[1]
    n_heads = b_f.shape[1]
    aw = n_heads * HEAD_DIM
    cc = conv_b.shape[1]
    n_in = n_in_loc * N_DEV
    o_f = 3 * aw
    n_all = 3 * aw + 2 * cc + LANES
    assert n_in == 3 * aw + n_heads + 2 * cc and aw + cc == d_model and n_heads % 2 == 0
    assert aw % LANES == 0 and cc % LANES == 0 and x.shape[0] == 1
    me = 4 * lax.axis_index("x") + 2 * lax.axis_index("y") + lax.axis_index("c")

    d_ff = w_mlp_in.shape[2] * N_DEV

    r_o, f_1, f_2 = w_o.shape[1], w_mlp_in.shape[2], w_mlp_out.shape[1]

    ag_src, ag_land, ag_spec = [], [], []
    for l in range(depth):
        ag_src += [w_in[l].astype(BF16), w_o[l].astype(BF16), w_mlp_in[l].astype(BF16), w_mlp_out[l].astype(BF16)]
        ag_land += [(N_DEV, d_model, n_in_loc), (N_DEV * r_o, d_model), (d_model, N_DEV * f_1), (N_DEV * f_2, d_model)]
        ag_spec += [("gather", "slot", 1), ("gather", "rows", r_o), ("gather", "cols", f_1), ("gather", "rows", f_2)]
    ag_src.append(jnp.stack(_split3(conv_w)))
    ag_land.append((N_DEV, 3) + conv_w.shape)
    ag_spec.append(("gather", "slot", 1))
    ag_groups = [grp for l in range(depth) for grp in ([4 * l] + ([4 * depth] if l == 0 else []), [4 * l + 1], [4 * l + 2, 4 * l + 3])]
    ag_land = [lax.empty(shp, BF16) for shp in ag_land]
    ag_sems, ag_land, _ = _exchange_start(ag_src, ag_land, ag_spec, ag_groups, "gather_start")

    def gathered(g, after):
        units = ag_groups[g]
        return _exchange_wait([ag_src[u] for u in units], [ag_land[u] for u in units], [ag_spec[u] for u in units],
                              ag_sems[g], after, f"gather_wait_{g}")

    f_all = n_all - LANES
    in_segments = ((0, o_f, 0), (o_f + n_heads, n_in, o_f), (o_f, o_f + n_heads, f_all))

    def whole_in(lin):
        pieces = []
        for c0, c1, _ in in_segments:
            while c0 < c1:
                j, off = divmod(c0, n_in_loc)
                stop = min(c1, (j + 1) * n_in_loc)
                pieces.append(lin[j][:, off:off + stop - c0])
                c0 = stop
        pieces.append(jnp.zeros((d_model, LANES - n_heads), lin.dtype))
        return jnp.concatenate(pieces, axis=1)

    def shards_of(w):
        slabs = []
        for j in range(N_DEV):
            pieces = []
            for c0, c1, a0 in sorted(in_segments):
                lo, hi = max(c0, j * n_in_loc), min(c1, (j + 1) * n_in_loc)
                if lo < hi:
                    pieces.append(w[:, a0 + lo - c0:a0 + hi - c0])
            slabs.append(jnp.concatenate(pieces, axis=1))
        return jnp.stack(slabs)

    def row(p, l, width=None):
        v = p[l].reshape(1, -1)
        return v if width is None else jnp.concatenate([v, jnp.zeros((1, width - v.shape[1]), v.dtype)], axis=1)

    a_col, g_col = 3 * aw // LANES, (3 * aw + cc) // LANES

    gq = [jnp.tile(row(q_norm_g, l), (1, n_heads)) for l in range(depth)]
    gk = [jnp.tile(row(k_norm_g, l), (1, n_heads)) for l in range(depth)]
    bfp = [row(b_f, l, LANES) for l in range(depth)]
    add_res = lambda acc, res: (acc + res,)
    w_all, w_out, w_ff1, w_ff2 = [None] * depth, [None] * depth, [None] * depth, [None] * depth

    h = x[0]
    saved = []
    for l in range(depth):
        if l == 0:
            lin, lc = gathered(0, h)
            lc = lc.astype(F32)
            conv_full = jnp.moveaxis(lc[:, 0] + lc[:, 1] + lc[:, 2], 0, 2).reshape(depth, CONV_TAPS, cc)
            w32 = [jnp.concatenate([conv_full[i], jnp.zeros((CONV_PAD - CONV_TAPS, cc), F32)]) for i in range(depth)]
        else:
            lin, = gathered(3 * l, h)
        w_all[l] = whole_in(lin)
        u1, proj = _rms_matmul(h, row(norm1_g, l), w_all[l], out_dtypes=(F32,), name=f"mm_in_{l}")
        qa, ka, vb = _prep_fwd(proj, gq[l], gk[l], bfp[l], n_heads, f"prep_fwd_{l}")
        att, lse, mixed = _attn_fwd(qa, ka, vb, n_heads, aw + cc, f"attn_fwd_{l}")
        yc = _conv_fwd(proj, w32[l], row(conv_b, l), cc, a_col, g_col, f"conv_fwd_{l}")
        mixed = _ln_silu_fwd(yc, row(conv_ln_g, l), row(conv_ln_b, l), mixed, f"ln_silu_fwd_{l}")
        w_out[l], = gathered(3 * l + 1, mixed)
        x1 = _matmul(mixed, w_out[l], mode="nn", out_dtypes=(F32,), name=f"mm_o_{l}", epilogue=add_res, extras=(h,))
        w_ff1[l], w_ff2[l] = gathered(3 * l + 2, x1)
        u2, r, a = _rms_matmul(x1, row(norm2_g, l), w_ff1[l], out_dtypes=(BF16, BF16), name=f"mm_ff1_{l}", tm=256,
                               epilogue=lambda acc: (jnp.maximum(acc, 0.0), jnp.square(jnp.maximum(acc, 0.0))))
        if l < depth - 1:
            x2 = _matmul(a, w_ff2[l], mode="nn", out_dtypes=(F32,), name=f"mm_ff2_{l}", epilogue=add_res, extras=(x1,), tk=d_ff)
        else:
            x2 = None
            dh, dh16, sq = _matmul_loss(a, w_ff2[l], x1, loss_target[0], f"mm_ff2_loss_{l}")
        saved.append(dict(x_in=h, u1=u1, proj=proj, qa=qa, ka=ka, vb=vb, att=att, lse=lse, yc=yc, mixed=mixed,
                          x1=x1, u2=u2, r=r, a=a))
        h = x2

    loss = lax.psum(0.5 * jnp.sum(sq) / d_model, ("x", "y", "c"))

    g_in, g_o, g_1, g_2 = [None] * depth, [None] * depth, [None] * depth, [None] * depth
    gs = {n: [None] * depth for n in ("norm1", "bf", "qn", "kn", "convw", "convb", "lng", "lnb", "norm2")}
    scattering = {}

    def scatter_start(stage, l, srcs, specs, slabs):
        lands = [lax.empty((N_DEV,) + shp, BF16) for shp in slabs]
        sems, lands, token = _exchange_start(srcs, lands, specs, [list(range(len(srcs)))], f"scatter_start_{stage}_{l}")
        scattering[(stage, l)] = (srcs, lands, specs, sems[0])
        return token[0, 0]

    for l in reversed(range(depth)):
        sv = saved[l]
        dh1 = _matmul(dh16, w_ff2[l], mode="nt", out_dtypes=(BF16,), name=f"mm_dff2_{l}", tm=256, tn=d_ff,
                      epilogue=lambda acc, rr: (acc * (2.0 * rr.astype(F32)),), extras=(sv["r"],))
        g_2[l] = _matmul(sv["a"], dh16, mode="tn", out_dtypes=(BF16,), name=f"mm_dw2_{l}", tm=1024, tk=seq)
        g_1[l] = _matmul(sv["u2"], dh1, mode="tn", out_dtypes=(BF16,), name=f"mm_dw1_{l}", tm=d_model, tk=seq)
        tok = scatter_start("ff", l, [g_1[l], g_2[l]], [("scatter", "cols", f_1), ("scatter", "rows", f_2)],
                            [(d_model, f_1), (f_2, d_model)])
        dx1, dx16, gs["norm2"][l] = _matmul_rms_bwd(dh1, w_ff1[l], sv["x1"], row(norm2_g, l) + tok, dh, f"mm_du2_rms_{l}")

        dmixed = _matmul(dx16, w_out[l], mode="nt", out_dtypes=(F32,), name=f"mm_dmixed_{l}")
        g_o[l] = _matmul(sv["mixed"], dx16, mode="tn", out_dtypes=(BF16,), name=f"mm_dwo_{l}", tm=1024, tk=seq // 2)
        tok = scatter_start("o", l, [g_o[l]], [("scatter", "rows", r_o)], [(r_o, d_model)])
        dyc, gs["lng"][l], gs["lnb"][l] = _ln_silu_bwd(sv["yc"], row(conv_ln_g, l) + tok, row(conv_ln_b, l), dmixed, f"ln_silu_bwd_{l}")
        dpa, dpg, dw32 = _conv_bwd(sv["proj"], w32[l], dyc, cc, a_col, g_col, f"conv_bwd_{l}")
        gs["convw"][l], gs["convb"][l] = dw32[:CONV_TAPS], dw32[CONV_TAPS:CONV_TAPS + 1]
        dq, dka, dv, drow, dcol = _attn_bwd(sv["qa"], sv["ka"], sv["vb"], sv["att"], sv["lse"], dmixed, n_heads, f"attn_bwd_{l}")
        dpq, dpk, dpf, dgq, dgk, dbf = _prep_bwd(sv["proj"], dq, dka, drow, dcol, gq[l], gk[l], bfp[l], n_heads, f"prep_bwd_{l}")
        gs["qn"][l] = dgq.reshape(n_heads, HEAD_DIM).sum(axis=0)
        gs["kn"][l] = dgk.reshape(n_heads, HEAD_DIM).sum(axis=0)
        gs["bf"][l] = dbf[0, :n_heads]
        dproj = jnp.concatenate([dpq, dpk, dv.astype(BF16), dpa, dpg, dpf], axis=1)
        dwall = _matmul(sv["u1"], dproj, mode="tn", out_dtypes=(BF16,), name=f"mm_dwall_{l}", tm=d_model, tn=n_all // 3, tk=seq)
        g_in[l] = shards_of(dwall)
        tok = scatter_start("in", l, [g_in[l]], [("scatter", "slot", 1)], [(d_model, n_in_loc)])
        dh, dh16, gs["norm1"][l] = _matmul_rms_bwd(dproj, w_all[l], sv["x_in"], row(norm1_g, l) + tok, dx1, f"mm_du1_rms_{l}")
    grad_x = dh[None]

    small_g = [jnp.stack(gs[n]).reshape(shp) for n, shp in (
        ("norm1", norm1_g.shape), ("bf", b_f.shape), ("qn", q_norm_g.shape), ("kn", k_norm_g.shape),
        ("convw", (depth, CONV_TAPS, cc)), ("convb", conv_b.shape), ("lng", conv_ln_g.shape), ("lnb", conv_ln_b.shape),
        ("norm2", norm2_g.shape))]
    small_shapes = [g.shape for g in small_g]
    small_flat = _flatten(small_g, 8)
    small_spec = [("gather", "slot", 1)]
    small_sems, small_land, _ = _exchange_start([small_flat], [lax.empty((N_DEV,) + small_flat.shape, F32)], small_spec, [[0]],
                                                "small_grads_start")

    def landed(stage, l, after):
        srcs, lands, specs, sems = scattering[(stage, l)]
        return _exchange_wait(srcs, lands, specs, sems, after, f"scatter_wait_{stage}_{l}")

    def adamw_layers(kd, w, m, v, recv):
        outs = None
        for l in reversed(range(depth)):
            outs = _adamw_shard(w, m, v, recv[l], l, outs, f"adamw_{kd}_{l}")
        return outs

    recv_1, recv_2, recv_in, recv_o = [None] * depth, [None] * depth, [None] * depth, [None] * depth
    for l in reversed(range(depth)):
        recv_1[l], recv_2[l] = landed("ff", l, dh)
    out_1 = adamw_layers("1", w_mlp_in, m_w_mlp_in, v_w_mlp_in, recv_1)
    out_2 = adamw_layers("2", w_mlp_out, m_w_mlp_out, v_w_mlp_out, recv_2)
    for l in reversed(range(depth)):
        recv_o[l], = landed("o", l, out_2[1])
    out_o = adamw_layers("o", w_o, m_w_o, v_w_o, recv_o)
    for l in reversed(range(depth)):
        recv_in[l], = landed("in", l, out_o[1])
    out_in = adamw_layers("in", w_in, m_w_in, v_w_in, recv_in)
    big_out = [[outs[kind] for outs in (out_in, out_o, out_1, out_2)] for kind in range(4)]

    small_parts, = _exchange_wait([small_flat], small_land, small_spec, small_sems[0], out_in[1], "small_grads_wait")
    small_g = _unflatten(_sum_devices(small_parts, "small_grads_sum"), small_shapes)
    cw = conv_w.shape[2]
    small_g[4] = lax.dynamic_slice_in_dim(small_g[4], me * cw, cw, axis=2)
    small = (norm1_g, b_f, q_norm_g, k_norm_g, conv_w, conv_b, conv_ln_g, conv_ln_b, norm2_g)
    small_m = (m_norm1_g, m_b_f, m_q_norm_g, m_k_norm_g, m_conv_w, m_conv_b, m_conv_ln_g, m_conv_ln_b, m_norm2_g)
    small_v = (v_norm1_g, v_b_f, v_q_norm_g, v_k_norm_g, v_conv_w, v_conv_b, v_conv_ln_g, v_conv_ln_b, v_norm2_g)
    small_out = _adamw(_flatten(small, 8), _flatten(small_m, 8), _flatten(small_v, 8), _flatten(small_g, 8), "adamw_small")
    small_out = [small_g] + [_unflatten(o, [w.shape for w in small]) for o in small_out]

    def group(kind):
        s_, b_ = small_out[kind], big_out[kind]
        return [s_[0], b_[0], s_[1], s_[2], s_[3], s_[4], s_[5], s_[6], s_[7], b_[1], s_[8], b_[2], b_[3]]

    return (loss, grad_x, *group(0), *group(1), *group(2), *group(3))
```
